```python
import jax, jax.numpy as jnp
from jax import lax
import numpy as np

D_MODEL = 1024
BATCH = 8
SEQ = 4096
DEPTH = 2

N_META = 16
D_FF = 2816
EPS = 1e-6
POOL_WINDOWS = (2, 4, 8, 16)
POOL_GROUP = D_MODEL // 16
D_POOL = POOL_GROUP * len(POOL_WINDOWS)
HG_HEAD_K = 128
HG_HEAD_V = 128
D_HGRN = D_MODEL - D_POOL
HG_HEADS = D_HGRN // HG_HEAD_K
HG_CHUNK = 64
D_IN_EVEN = D_POOL + 4 * D_HGRN
D_CONV = D_MODEL // 2
CONV_WIDTH = 31
CONV_GROUPS = 4
D_LRU = D_MODEL // 2
LRU_HEADS = 8
LRU_HEAD = D_LRU // LRU_HEADS
LRU_CONV = 4
LRU_C = 8.0
D_IN_ODD = 2 * D_CONV + 2 * D_LRU
N_EVEN = (DEPTH + 1) // 2
N_ODD = DEPTH // 2

kernel_name = "hybrid_pool_hgrn2_conv_rglru_macaron"


def rms_norm(x, g):
    xf = x.astype(jnp.float32)
    y = xf * lax.rsqrt(jnp.mean(xf * xf, axis=-1, keepdims=True) + EPS)
    return (y * g.astype(jnp.float32)).astype(x.dtype)


def swiglu_ffn(x, wg, wu, wd):
    return (jax.nn.silu(x @ wg) * (x @ wu)) @ wd


def causal_depthwise_conv(x, w, b):
    width = w.shape[0]
    xp = jnp.pad(x, ((0, 0), (width - 1, 0), (0, 0)))
    y = lax.conv_general_dilated(xp, w.astype(x.dtype)[:, None, :], window_strides=(1,), padding='VALID',
                                 dimension_numbers=('NWC', 'WIO', 'NWC'), feature_group_count=x.shape[-1])
    return y + b.astype(x.dtype)


def multiscale_pool(u, w_grp, scale):
    bn, L, _ = u.shape
    ug = u.astype(jnp.float32).reshape(bn, L, len(POOL_WINDOWS), POOL_GROUP)
    c = jnp.cumsum(ug, axis=1)
    t = jnp.arange(1, L + 1, dtype=jnp.float32)[None, :, None]
    pooled = []
    for gi, w in enumerate(POOL_WINDOWS):
        cg = c[:, :, gi]
        lagged = jnp.pad(cg, ((0, 0), (w, 0), (0, 0)))[:, :L]
        pooled.append((cg - lagged) / jnp.minimum(t, float(w)))
    mixed = jnp.stack(pooled, axis=2) - ug
    y = jnp.einsum('blgc,gcd->blgd', mixed, w_grp.astype(jnp.float32))
    return (y.reshape(bn, L, D_POOL) * scale.astype(jnp.float32)).astype(u.dtype)


def hgrn2_mixer(q_raw, f_raw, i_raw, g_raw, lb, gnorm):
    f32 = jnp.float32
    bn, L, _ = q_raw.shape
    q = jax.nn.silu(q_raw.astype(f32))
    z = f_raw.astype(f32)
    lbf = lb.astype(f32)
    log_f = jnp.logaddexp(jnp.log(lbf), jnp.log1p(-lbf) + jax.nn.log_sigmoid(z))
    k = (1.0 - lbf) * jax.nn.sigmoid(-z)
    v = i_raw.astype(f32)
    pad = (-N_META) % HG_CHUNK
    n_chunks = (L + pad) // HG_CHUNK

    def to_chunks(a):
        a = jnp.pad(a, ((0, 0), (pad, 0), (0, 0)))
        return a.reshape(bn, n_chunks, HG_CHUNK, HG_HEADS, -1).transpose(1, 0, 3, 2, 4)

    causal = jnp.tril(jnp.ones((HG_CHUNK, HG_CHUNK), dtype=bool))[:, :, None]

    def step(S, inp):
        qc, kc, lfc, vc = inp
        b = jnp.cumsum(lfc, axis=2)
        o_inter = jnp.einsum('bhtk,bhkv->bhtv', qc * jnp.exp(b), S)
        diff = b[:, :, :, None, :] - b[:, :, None, :, :]
        decay = jnp.exp(jnp.where(causal, diff, -jnp.inf))
        A = jnp.einsum('bhtsk,bhsk->bhts', qc[:, :, :, None, :] * decay, kc)
        o = o_inter + jnp.einsum('bhts,bhsv->bhtv', A, vc)
        b_last = b[:, :, -1:, :]
        S_new = jnp.exp(b_last[:, :, 0, :])[..., None] * S + jnp.einsum('bhsk,bhsv->bhkv', kc * jnp.exp(b_last - b), vc)
        return S_new, o

    S0 = jnp.zeros((bn, HG_HEADS, HG_HEAD_K, HG_HEAD_V), f32)
    _, o = lax.scan(step, S0, (to_chunks(q), to_chunks(k), to_chunks(log_f), to_chunks(v)))
    o = o.transpose(1, 0, 3, 2, 4).reshape(bn, n_chunks * HG_CHUNK, HG_HEADS, HG_HEAD_V)[:, pad:]
    o = o * lax.rsqrt(jnp.mean(o * o, axis=-1, keepdims=True) + EPS) * gnorm.astype(f32)
    o = o * jax.nn.silu(g_raw.astype(f32)).reshape(bn, L, HG_HEADS, HG_HEAD_V)
    return o.reshape(bn, L, D_HGRN).astype(q_raw.dtype)


def conformer_conv_module(a, b, w, bias, ln_g, ln_b):
    f32 = jnp.float32
    bn, L, _ = a.shape
    u = a * jax.nn.sigmoid(b)
    u = causal_depthwise_conv(u, w, bias).astype(f32).reshape(bn, L, CONV_GROUPS, D_CONV // CONV_GROUPS)
    mu = jnp.mean(u, axis=-1, keepdims=True)
    var = jnp.mean(jnp.square(u - mu), axis=-1, keepdims=True)
    un = ((u - mu) * lax.rsqrt(var + EPS)).reshape(bn, L, D_CONV) * ln_g.astype(f32) + ln_b.astype(f32)
    return jax.nn.silu(un).astype(a.dtype)


def rglru_block(xb, gate, conv_w, conv_b, wa, ba, wx, bx, lam):
    f32 = jnp.float32
    u = causal_depthwise_conv(xb, conv_w, conv_b).astype(f32)
    bn, L, _ = u.shape
    uh = u.reshape(bn, L, LRU_HEADS, LRU_HEAD)
    r = jax.nn.sigmoid(jnp.einsum('blhi,hij->blhj', uh, wa.astype(f32)).reshape(bn, L, D_LRU) + ba.astype(f32))
    i = jax.nn.sigmoid(jnp.einsum('blhi,hij->blhj', uh, wx.astype(f32)).reshape(bn, L, D_LRU) + bx.astype(f32))
    log_a = -LRU_C * r * jax.nn.softplus(-lam.astype(f32))
    a = jnp.exp(log_a)
    mult = jnp.sqrt(-jnp.expm1(2.0 * log_a))
    reset = (jnp.arange(L) == 0)[None, :, None]
    bterm = jnp.where(reset, 1.0, mult) * (i * u)

    def combine(c1, c2):
        a1, b1 = c1
        a2, b2 = c2
        return a1 * a2, a2 * b1 + b2

    _, h = lax.associative_scan(combine, (a, bterm), axis=1)
    return (jax.nn.gelu(gate.astype(f32)) * h).astype(xb.dtype)


def _fwd_setup_inputs(seed: int = 0) -> dict:
    key = jax.random.key(seed)
    ks = iter(jax.random.split(key, 48))
    f32 = jnp.float32

    def nrm(shape, scale):
        return jax.random.normal(next(ks), shape, f32) * scale

    def gain(shape):
        return 1.0 + 0.05 * jax.random.normal(next(ks), shape, f32)

    a0 = jax.random.uniform(next(ks), (N_ODD, D_LRU), f32, 0.9, 0.999)
    s = a0 ** (1.0 / LRU_C)
    lam = jnp.log(s) - jnp.log1p(-s)
    return {
        "x": nrm((BATCH, SEQ, D_MODEL), 1.0),
        "meta_tokens": nrm((N_META, D_MODEL), 1.0),
        "ffn1_norm": gain((DEPTH, D_MODEL)),
        "ffn1_wg": nrm((DEPTH, D_MODEL, D_FF), D_MODEL ** -0.5),
        "ffn1_wu": nrm((DEPTH, D_MODEL, D_FF), D_MODEL ** -0.5),
        "ffn1_wd": nrm((DEPTH, D_FF, D_MODEL), D_FF ** -0.5),
        "mix_norm": gain((DEPTH, D_MODEL)),
        "ffn2_norm": gain((DEPTH, D_MODEL)),
        "ffn2_wg": nrm((DEPTH, D_MODEL, D_FF), D_MODEL ** -0.5),
        "ffn2_wu": nrm((DEPTH, D_MODEL, D_FF), D_MODEL ** -0.5),
        "ffn2_wd": nrm((DEPTH, D_FF, D_MODEL), D_FF ** -0.5),
        "w_in_even": nrm((N_EVEN, D_MODEL, D_IN_EVEN), D_MODEL ** -0.5),
        "pool_w": nrm((N_EVEN, len(POOL_WINDOWS), POOL_GROUP, POOL_GROUP), POOL_GROUP ** -0.5),
        "pool_scale": gain((N_EVEN, D_POOL)),
        "hgrn_lb_logits": nrm((N_EVEN + 1, D_HGRN), 0.5),
        "hgrn_gnorm": gain((N_EVEN, HG_HEAD_V)),
        "w_out_even": nrm((N_EVEN, D_POOL + D_HGRN, D_MODEL), (D_POOL + D_HGRN) ** -0.5),
        "w_in_odd": nrm((N_ODD, D_MODEL, D_IN_ODD), D_MODEL ** -0.5),
        "conv_w": nrm((N_ODD, CONV_WIDTH, D_CONV), CONV_WIDTH ** -0.5),
        "conv_b": nrm((N_ODD, D_CONV), 0.02),
        "conv_ln_g": gain((N_ODD, D_CONV)),
        "conv_ln_b": nrm((N_ODD, D_CONV), 0.02),
        "lru_conv_w": nrm((N_ODD, LRU_CONV, D_LRU), LRU_CONV ** -0.5),
        "lru_conv_b": nrm((N_ODD, D_LRU), 0.02),
        "lru_wa": nrm((N_ODD, LRU_HEADS, LRU_HEAD, LRU_HEAD), LRU_HEAD ** -0.5),
        "lru_ba": nrm((N_ODD, D_LRU), 0.02),
        "lru_wx": nrm((N_ODD, LRU_HEADS, LRU_HEAD, LRU_HEAD), LRU_HEAD ** -0.5),
        "lru_bx": nrm((N_ODD, D_LRU), 0.02),
        "lru_lambda": lam,
        "w_out_odd": nrm((N_ODD, D_CONV + D_LRU, D_MODEL), (D_CONV + D_LRU) ** -0.5),
        "final_norm": gain((D_MODEL,)),
    }


def _fwd_reference(x, meta_tokens, ffn1_norm, ffn1_wg, ffn1_wu, ffn1_wd, mix_norm, ffn2_norm, ffn2_wg, ffn2_wu,
              ffn2_wd, w_in_even, pool_w, pool_scale, hgrn_lb_logits, hgrn_gnorm, w_out_even, w_in_odd,
              conv_w, conv_b, conv_ln_g, conv_ln_b, lru_conv_w, lru_conv_b, lru_wa, lru_ba, lru_wx, lru_bx,
              lru_lambda, w_out_odd, final_norm):
    bn = x.shape[0]
    meta = jnp.broadcast_to(meta_tokens.astype(x.dtype)[None], (bn, N_META, D_MODEL))
    h = jnp.concatenate([meta, x], axis=1)
    lbs = jnp.cumsum(jax.nn.softmax(hgrn_lb_logits.astype(jnp.float32), axis=0), axis=0)
    for l in range(DEPTH):
        j = l // 2
        h = h + 0.5 * swiglu_ffn(rms_norm(h, ffn1_norm[l]), ffn1_wg[l], ffn1_wu[l], ffn1_wd[l])
        u = rms_norm(h, mix_norm[l])
        if l % 2 == 0:
            p = u @ w_in_even[j]
            p_pool, q_r, f_r, i_r, g_r = jnp.split(
                p, [D_POOL, D_POOL + D_HGRN, D_POOL + 2 * D_HGRN, D_POOL + 3 * D_HGRN], axis=-1)
            ya = multiscale_pool(p_pool, pool_w[j], pool_scale[j])
            yb = hgrn2_mixer(q_r, f_r, i_r, g_r, lbs[j], hgrn_gnorm[j])
            y = jnp.concatenate([ya, yb], axis=-1) @ w_out_even[j]
        else:
            p = u @ w_in_odd[j]
            c_a, c_b, d_x, d_g = jnp.split(p, [D_CONV, 2 * D_CONV, 2 * D_CONV + D_LRU], axis=-1)
            yc = conformer_conv_module(c_a, c_b, conv_w[j], conv_b[j], conv_ln_g[j], conv_ln_b[j])
            yd = rglru_block(d_x, d_g, lru_conv_w[j], lru_conv_b[j], lru_wa[j], lru_ba[j], lru_wx[j],
                             lru_bx[j], lru_lambda[j])
            y = jnp.concatenate([yc, yd], axis=-1) @ w_out_odd[j]
        h = h + y
        h = h + 0.5 * swiglu_ffn(rms_norm(h, ffn2_norm[l]), ffn2_wg[l], ffn2_wu[l], ffn2_wd[l])
    h = rms_norm(h, final_norm)
    return h[:, N_META:]


import jax as _jax
import jax.numpy as _jnp

TWIN_FORMAT = 'train_step'
FWD_PARAMS = ['x', 'meta_tokens', 'ffn1_norm', 'ffn1_wg', 'ffn1_wu', 'ffn1_wd', 'mix_norm', 'ffn2_norm', 'ffn2_wg', 'ffn2_wu', 'ffn2_wd', 'w_in_even', 'pool_w', 'pool_scale', 'hgrn_lb_logits', 'hgrn_gnorm', 'w_out_even', 'w_in_odd', 'conv_w', 'conv_b', 'conv_ln_g', 'conv_ln_b', 'lru_conv_w', 'lru_conv_b', 'lru_wa', 'lru_ba', 'lru_wx', 'lru_bx', 'lru_lambda', 'w_out_odd', 'final_norm']
TWIN_WEIGHTS = ['meta_tokens', 'ffn1_norm', 'ffn1_wg', 'ffn1_wu', 'ffn1_wd', 'mix_norm', 'ffn2_norm', 'ffn2_wg', 'ffn2_wu', 'ffn2_wd', 'w_in_even', 'pool_w', 'pool_scale', 'hgrn_lb_logits', 'hgrn_gnorm', 'w_out_even', 'w_in_odd', 'conv_w', 'conv_b', 'conv_ln_g', 'conv_ln_b', 'lru_conv_w', 'lru_conv_b', 'lru_wa', 'lru_ba', 'lru_wx', 'lru_bx', 'lru_lambda', 'w_out_odd', 'final_norm']
TWIN_DIFF_INPUT = 'x'
TWIN_INPUTS = ['x', 'meta_tokens', 'ffn1_norm', 'ffn1_wg', 'ffn1_wu', 'ffn1_wd', 'mix_norm', 'ffn2_norm', 'ffn2_wg', 'ffn2_wu', 'ffn2_wd', 'w_in_even', 'pool_w', 'pool_scale', 'hgrn_lb_logits', 'hgrn_gnorm', 'w_out_even', 'w_in_odd', 'conv_w', 'conv_b', 'conv_ln_g', 'conv_ln_b', 'lru_conv_w', 'lru_conv_b', 'lru_wa', 'lru_ba', 'lru_wx', 'lru_bx', 'lru_lambda', 'w_out_odd', 'final_norm', 'loss_target', 'm_meta_tokens', 'm_ffn1_norm', 'm_ffn1_wg', 'm_ffn1_wu', 'm_ffn1_wd', 'm_mix_norm', 'm_ffn2_norm', 'm_ffn2_wg', 'm_ffn2_wu', 'm_ffn2_wd', 'm_w_in_even', 'm_pool_w', 'm_pool_scale', 'm_hgrn_lb_logits', 'm_hgrn_gnorm', 'm_w_out_even', 'm_w_in_odd', 'm_conv_w', 'm_conv_b', 'm_conv_ln_g', 'm_conv_ln_b', 'm_lru_conv_w', 'm_lru_conv_b', 'm_lru_wa', 'm_lru_ba', 'm_lru_wx', 'm_lru_bx', 'm_lru_lambda', 'm_w_out_odd', 'm_final_norm', 'v_meta_tokens', 'v_ffn1_norm', 'v_ffn1_wg', 'v_ffn1_wu', 'v_ffn1_wd', 'v_mix_norm', 'v_ffn2_norm', 'v_ffn2_wg', 'v_ffn2_wu', 'v_ffn2_wd', 'v_w_in_even', 'v_pool_w', 'v_pool_scale', 'v_hgrn_lb_logits', 'v_hgrn_gnorm', 'v_w_out_even', 'v_w_in_odd', 'v_conv_w', 'v_conv_b', 'v_conv_ln_g', 'v_conv_ln_b', 'v_lru_conv_w', 'v_lru_conv_b', 'v_lru_wa', 'v_lru_ba', 'v_lru_wx', 'v_lru_bx', 'v_lru_lambda', 'v_w_out_odd', 'v_final_norm']
TWIN_OUTPUTS = ['loss', 'grad_x', 'grad_meta_tokens', 'grad_ffn1_norm', 'grad_ffn1_wg', 'grad_ffn1_wu', 'grad_ffn1_wd', 'grad_mix_norm', 'grad_ffn2_norm', 'grad_ffn2_wg', 'grad_ffn2_wu', 'grad_ffn2_wd', 'grad_w_in_even', 'grad_pool_w', 'grad_pool_scale', 'grad_hgrn_lb_logits', 'grad_hgrn_gnorm', 'grad_w_out_even', 'grad_w_in_odd', 'grad_conv_w', 'grad_conv_b', 'grad_conv_ln_g', 'grad_conv_ln_b', 'grad_lru_conv_w', 'grad_lru_conv_b', 'grad_lru_wa', 'grad_lru_ba', 'grad_lru_wx', 'grad_lru_bx', 'grad_lru_lambda', 'grad_w_out_odd', 'grad_final_norm', 'delta_meta_tokens', 'delta_ffn1_norm', 'delta_ffn1_wg', 'delta_ffn1_wu', 'delta_ffn1_wd', 'delta_mix_norm', 'delta_ffn2_norm', 'delta_ffn2_wg', 'delta_ffn2_wu', 'delta_ffn2_wd', 'delta_w_in_even', 'delta_pool_w', 'delta_pool_scale', 'delta_hgrn_lb_logits', 'delta_hgrn_gnorm', 'delta_w_out_even', 'delta_w_in_odd', 'delta_conv_w', 'delta_conv_b', 'delta_conv_ln_g', 'delta_conv_ln_b', 'delta_lru_conv_w', 'delta_lru_conv_b', 'delta_lru_wa', 'delta_lru_ba', 'delta_lru_wx', 'delta_lru_bx', 'delta_lru_lambda', 'delta_w_out_odd', 'delta_final_norm', 'new_m_meta_tokens', 'new_m_ffn1_norm', 'new_m_ffn1_wg', 'new_m_ffn1_wu', 'new_m_ffn1_wd', 'new_m_mix_norm', 'new_m_ffn2_norm', 'new_m_ffn2_wg', 'new_m_ffn2_wu', 'new_m_ffn2_wd', 'new_m_w_in_even', 'new_m_pool_w', 'new_m_pool_scale', 'new_m_hgrn_lb_logits', 'new_m_hgrn_gnorm', 'new_m_w_out_even', 'new_m_w_in_odd', 'new_m_conv_w', 'new_m_conv_b', 'new_m_conv_ln_g', 'new_m_conv_ln_b', 'new_m_lru_conv_w', 'new_m_lru_conv_b', 'new_m_lru_wa', 'new_m_lru_ba', 'new_m_lru_wx', 'new_m_lru_bx', 'new_m_lru_lambda', 'new_m_w_out_odd', 'new_m_final_norm', 'new_v_meta_tokens', 'new_v_ffn1_norm', 'new_v_ffn1_wg', 'new_v_ffn1_wu', 'new_v_ffn1_wd', 'new_v_mix_norm', 'new_v_ffn2_norm', 'new_v_ffn2_wg', 'new_v_ffn2_wu', 'new_v_ffn2_wd', 'new_v_w_in_even', 'new_v_pool_w', 'new_v_pool_scale', 'new_v_hgrn_lb_logits', 'new_v_hgrn_gnorm', 'new_v_w_out_even', 'new_v_w_in_odd', 'new_v_conv_w', 'new_v_conv_b', 'new_v_conv_ln_g', 'new_v_conv_ln_b', 'new_v_lru_conv_w', 'new_v_lru_conv_b', 'new_v_lru_wa', 'new_v_lru_ba', 'new_v_lru_wx', 'new_v_lru_bx', 'new_v_lru_lambda', 'new_v_w_out_odd', 'new_v_final_norm']
TWIN_LEAF_KINDS = {'loss': 'loss', 'grad_x': 'grad_x', 'grad_meta_tokens': 'grad_w', 'grad_ffn1_norm': 'grad_w', 'grad_ffn1_wg': 'grad_w', 'grad_ffn1_wu': 'grad_w', 'grad_ffn1_wd': 'grad_w', 'grad_mix_norm': 'grad_w', 'grad_ffn2_norm': 'grad_w', 'grad_ffn2_wg': 'grad_w', 'grad_ffn2_wu': 'grad_w', 'grad_ffn2_wd': 'grad_w', 'grad_w_in_even': 'grad_w', 'grad_pool_w': 'grad_w', 'grad_pool_scale': 'grad_w', 'grad_hgrn_lb_logits': 'grad_w', 'grad_hgrn_gnorm': 'grad_w', 'grad_w_out_even': 'grad_w', 'grad_w_in_odd': 'grad_w', 'grad_conv_w': 'grad_w', 'grad_conv_b': 'grad_w', 'grad_conv_ln_g': 'grad_w', 'grad_conv_ln_b': 'grad_w', 'grad_lru_conv_w': 'grad_w', 'grad_lru_conv_b': 'grad_w', 'grad_lru_wa': 'grad_w', 'grad_lru_ba': 'grad_w', 'grad_lru_wx': 'grad_w', 'grad_lru_bx': 'grad_w', 'grad_lru_lambda': 'grad_w', 'grad_w_out_odd': 'grad_w', 'grad_final_norm': 'grad_w', 'delta_meta_tokens': 'delta_w', 'delta_ffn1_norm': 'delta_w', 'delta_ffn1_wg': 'delta_w', 'delta_ffn1_wu': 'delta_w', 'delta_ffn1_wd': 'delta_w', 'delta_mix_norm': 'delta_w', 'delta_ffn2_norm': 'delta_w', 'delta_ffn2_wg': 'delta_w', 'delta_ffn2_wu': 'delta_w', 'delta_ffn2_wd': 'delta_w', 'delta_w_in_even': 'delta_w', 'delta_pool_w': 'delta_w', 'delta_pool_scale': 'delta_w', 'delta_hgrn_lb_logits': 'delta_w', 'delta_hgrn_gnorm': 'delta_w', 'delta_w_out_even': 'delta_w', 'delta_w_in_odd': 'delta_w', 'delta_conv_w': 'delta_w', 'delta_conv_b': 'delta_w', 'delta_conv_ln_g': 'delta_w', 'delta_conv_ln_b': 'delta_w', 'delta_lru_conv_w': 'delta_w', 'delta_lru_conv_b': 'delta_w', 'delta_lru_wa': 'delta_w', 'delta_lru_ba': 'delta_w', 'delta_lru_wx': 'delta_w', 'delta_lru_bx': 'delta_w', 'delta_lru_lambda': 'delta_w', 'delta_w_out_odd': 'delta_w', 'delta_final_norm': 'delta_w', 'new_m_meta_tokens': 'new_m', 'new_m_ffn1_norm': 'new_m', 'new_m_ffn1_wg': 'new_m', 'new_m_ffn1_wu': 'new_m', 'new_m_ffn1_wd': 'new_m', 'new_m_mix_norm': 'new_m', 'new_m_ffn2_norm': 'new_m', 'new_m_ffn2_wg': 'new_m', 'new_m_ffn2_wu': 'new_m', 'new_m_ffn2_wd': 'new_m', 'new_m_w_in_even': 'new_m', 'new_m_pool_w': 'new_m', 'new_m_pool_scale': 'new_m', 'new_m_hgrn_lb_logits': 'new_m', 'new_m_hgrn_gnorm': 'new_m', 'new_m_w_out_even': 'new_m', 'new_m_w_in_odd': 'new_m', 'new_m_conv_w': 'new_m', 'new_m_conv_b': 'new_m', 'new_m_conv_ln_g': 'new_m', 'new_m_conv_ln_b': 'new_m', 'new_m_lru_conv_w': 'new_m', 'new_m_lru_conv_b': 'new_m', 'new_m_lru_wa': 'new_m', 'new_m_lru_ba': 'new_m', 'new_m_lru_wx': 'new_m', 'new_m_lru_bx': 'new_m', 'new_m_lru_lambda': 'new_m', 'new_m_w_out_odd': 'new_m', 'new_m_final_norm': 'new_m', 'new_v_meta_tokens': 'new_v', 'new_v_ffn1_norm': 'new_v', 'new_v_ffn1_wg': 'new_v', 'new_v_ffn1_wu': 'new_v', 'new_v_ffn1_wd': 'new_v', 'new_v_mix_norm': 'new_v', 'new_v_ffn2_norm': 'new_v', 'new_v_ffn2_wg': 'new_v', 'new_v_ffn2_wu': 'new_v', 'new_v_ffn2_wd': 'new_v', 'new_v_w_in_even': 'new_v', 'new_v_pool_w': 'new_v', 'new_v_pool_scale': 'new_v', 'new_v_hgrn_lb_logits': 'new_v', 'new_v_hgrn_gnorm': 'new_v', 'new_v_w_out_even': 'new_v', 'new_v_w_in_odd': 'new_v', 'new_v_conv_w': 'new_v', 'new_v_conv_b': 'new_v', 'new_v_conv_ln_g': 'new_v', 'new_v_conv_ln_b': 'new_v', 'new_v_lru_conv_w': 'new_v', 'new_v_lru_conv_b': 'new_v', 'new_v_lru_wa': 'new_v', 'new_v_lru_ba': 'new_v', 'new_v_lru_wx': 'new_v', 'new_v_lru_bx': 'new_v', 'new_v_lru_lambda': 'new_v', 'new_v_w_out_odd': 'new_v', 'new_v_final_norm': 'new_v'}


def _forward(args):
    return _fwd_reference(*[args[k] for k in FWD_PARAMS])


def _output_shape():
    out = _jax.eval_shape(lambda: _forward(_fwd_setup_inputs(0)))
    return out.shape, out.dtype

N_MICROBATCH = 1
ADAM_LR = 0.001
ADAM_B1 = 0.9
ADAM_B2 = 0.999
ADAM_EPS = 1e-08
ADAM_WD = 0.01
ADAM_STEP = 10
PER_EXAMPLE_BATCH_AXIS = {'x': 0, 'loss_target': 0}
SHARED_INPUTS = []
_WEIGHT_DTYPES = {'meta_tokens': _jnp.float32, 'ffn1_norm': _jnp.float32, 'ffn1_wg': _jnp.float32, 'ffn1_wu': _jnp.float32, 'ffn1_wd': _jnp.float32, 'mix_norm': _jnp.float32, 'ffn2_norm': _jnp.float32, 'ffn2_wg': _jnp.float32, 'ffn2_wu': _jnp.float32, 'ffn2_wd': _jnp.float32, 'w_in_even': _jnp.float32, 'pool_w': _jnp.float32, 'pool_scale': _jnp.float32, 'hgrn_lb_logits': _jnp.float32, 'hgrn_gnorm': _jnp.float32, 'w_out_even': _jnp.float32, 'w_in_odd': _jnp.float32, 'conv_w': _jnp.float32, 'conv_b': _jnp.float32, 'conv_ln_g': _jnp.float32, 'conv_ln_b': _jnp.float32, 'lru_conv_w': _jnp.float32, 'lru_conv_b': _jnp.float32, 'lru_wa': _jnp.float32, 'lru_ba': _jnp.float32, 'lru_wx': _jnp.float32, 'lru_bx': _jnp.float32, 'lru_lambda': _jnp.float32, 'w_out_odd': _jnp.float32, 'final_norm': _jnp.float32}
MOMENT_SCALE = {'meta_tokens': 5.236740e-03, 'ffn1_norm': 7.715786e-02, 'ffn1_wg': 3.314748e-02, 'ffn1_wu': 3.211318e-02, 'ffn1_wd': 5.329837e-02, 'mix_norm': 1.142310e-01, 'ffn2_norm': 6.090410e-02, 'ffn2_wg': 2.567843e-02, 'ffn2_wu': 2.491596e-02, 'ffn2_wd': 4.135360e-02, 'w_in_even': 7.702891e-02, 'pool_w': 1.420009e-01, 'pool_scale': 1.514798e-01, 'hgrn_lb_logits': 8.787967e-03, 'hgrn_gnorm': 2.574172e-01, 'w_out_even': 1.094627e-01, 'w_in_odd': 5.857810e-02, 'conv_w': 8.131707e-02, 'conv_b': 1.839107e-01, 'conv_ln_g': 1.000127e-01, 'conv_ln_b': 9.850500e-02, 'lru_conv_w': 6.099614e-02, 'lru_conv_b': 6.627910e-01, 'lru_wa': 1.965653e-02, 'lru_ba': 1.540587e-02, 'lru_wx': 3.500110e-02, 'lru_bx': 2.274995e-02, 'lru_lambda': 3.194337e-02, 'w_out_odd': 7.227282e-02, 'final_norm': 3.197356e+01}


def _to_microbatches(a, axis):
    t = _jnp.moveaxis(a, axis, 0)
    t = t.reshape((N_MICROBATCH, t.shape[0] // N_MICROBATCH) + t.shape[1:])
    return _jnp.moveaxis(t, 1, axis + 1)


def setup_inputs(seed: int = 0) -> dict:
    inp = _fwd_setup_inputs(seed)
    key = _jax.random.fold_in(_jax.random.key(seed), 7919)
    shape, _ = _output_shape()
    out = dict(inp)
    out["loss_target"] = _jax.random.normal(_jax.random.fold_in(key, 0), shape, _jnp.float32)
    for i, name in enumerate(TWIN_WEIGHTS):
        w = inp[name].astype(_jnp.float32)
        if MOMENT_SCALE is None:
            s = _jnp.sqrt(_jnp.mean(_jnp.square(w)) + 1e-30)
        else:
            s = MOMENT_SCALE[name]
        km, kv = _jax.random.split(_jax.random.fold_in(key, i + 1))
        out[name] = w
        out["m_" + name] = s * _jax.random.normal(km, w.shape, _jnp.float32)
        out["v_" + name] = (s * s) * _jax.random.uniform(kv, w.shape, _jnp.float32, 0.5, 1.5)
    if N_MICROBATCH > 1:
        for name, axis in PER_EXAMPLE_BATCH_AXIS.items():
            out[name] = _to_microbatches(out[name], axis)
    return {'x': out['x'], 'meta_tokens': out['meta_tokens'], 'ffn1_norm': out['ffn1_norm'], 'ffn1_wg': out['ffn1_wg'], 'ffn1_wu': out['ffn1_wu'], 'ffn1_wd': out['ffn1_wd'], 'mix_norm': out['mix_norm'], 'ffn2_norm': out['ffn2_norm'], 'ffn2_wg': out['ffn2_wg'], 'ffn2_wu': out['ffn2_wu'], 'ffn2_wd': out['ffn2_wd'], 'w_in_even': out['w_in_even'], 'pool_w': out['pool_w'], 'pool_scale': out['pool_scale'], 'hgrn_lb_logits': out['hgrn_lb_logits'], 'hgrn_gnorm': out['hgrn_gnorm'], 'w_out_even': out['w_out_even'], 'w_in_odd': out['w_in_odd'], 'conv_w': out['conv_w'], 'conv_b': out['conv_b'], 'conv_ln_g': out['conv_ln_g'], 'conv_ln_b': out['conv_ln_b'], 'lru_conv_w': out['lru_conv_w'], 'lru_conv_b': out['lru_conv_b'], 'lru_wa': out['lru_wa'], 'lru_ba': out['lru_ba'], 'lru_wx': out['lru_wx'], 'lru_bx': out['lru_bx'], 'lru_lambda': out['lru_lambda'], 'w_out_odd': out['w_out_odd'], 'final_norm': out['final_norm'], 'loss_target': out['loss_target'], 'm_meta_tokens': out['m_meta_tokens'], 'm_ffn1_norm': out['m_ffn1_norm'], 'm_ffn1_wg': out['m_ffn1_wg'], 'm_ffn1_wu': out['m_ffn1_wu'], 'm_ffn1_wd': out['m_ffn1_wd'], 'm_mix_norm': out['m_mix_norm'], 'm_ffn2_norm': out['m_ffn2_norm'], 'm_ffn2_wg': out['m_ffn2_wg'], 'm_ffn2_wu': out['m_ffn2_wu'], 'm_ffn2_wd': out['m_ffn2_wd'], 'm_w_in_even': out['m_w_in_even'], 'm_pool_w': out['m_pool_w'], 'm_pool_scale': out['m_pool_scale'], 'm_hgrn_lb_logits': out['m_hgrn_lb_logits'], 'm_hgrn_gnorm': out['m_hgrn_gnorm'], 'm_w_out_even': out['m_w_out_even'], 'm_w_in_odd': out['m_w_in_odd'], 'm_conv_w': out['m_conv_w'], 'm_conv_b': out['m_conv_b'], 'm_conv_ln_g': out['m_conv_ln_g'], 'm_conv_ln_b': out['m_conv_ln_b'], 'm_lru_conv_w': out['m_lru_conv_w'], 'm_lru_conv_b': out['m_lru_conv_b'], 'm_lru_wa': out['m_lru_wa'], 'm_lru_ba': out['m_lru_ba'], 'm_lru_wx': out['m_lru_wx'], 'm_lru_bx': out['m_lru_bx'], 'm_lru_lambda': out['m_lru_lambda'], 'm_w_out_odd': out['m_w_out_odd'], 'm_final_norm': out['m_final_norm'], 'v_meta_tokens': out['v_meta_tokens'], 'v_ffn1_norm': out['v_ffn1_norm'], 'v_ffn1_wg': out['v_ffn1_wg'], 'v_ffn1_wu': out['v_ffn1_wu'], 'v_ffn1_wd': out['v_ffn1_wd'], 'v_mix_norm': out['v_mix_norm'], 'v_ffn2_norm': out['v_ffn2_norm'], 'v_ffn2_wg': out['v_ffn2_wg'], 'v_ffn2_wu': out['v_ffn2_wu'], 'v_ffn2_wd': out['v_ffn2_wd'], 'v_w_in_even': out['v_w_in_even'], 'v_pool_w': out['v_pool_w'], 'v_pool_scale': out['v_pool_scale'], 'v_hgrn_lb_logits': out['v_hgrn_lb_logits'], 'v_hgrn_gnorm': out['v_hgrn_gnorm'], 'v_w_out_even': out['v_w_out_even'], 'v_w_in_odd': out['v_w_in_odd'], 'v_conv_w': out['v_conv_w'], 'v_conv_b': out['v_conv_b'], 'v_conv_ln_g': out['v_conv_ln_g'], 'v_conv_ln_b': out['v_conv_ln_b'], 'v_lru_conv_w': out['v_lru_conv_w'], 'v_lru_conv_b': out['v_lru_conv_b'], 'v_lru_wa': out['v_lru_wa'], 'v_lru_ba': out['v_lru_ba'], 'v_lru_wx': out['v_lru_wx'], 'v_lru_bx': out['v_lru_bx'], 'v_lru_lambda': out['v_lru_lambda'], 'v_w_out_odd': out['v_w_out_odd'], 'v_final_norm': out['v_final_norm']}


def _loss(weights, diff, rest, loss_target):
    with _jax.named_scope("forward"):
        args = {**rest, TWIN_DIFF_INPUT: diff, **{k: w.astype(_WEIGHT_DTYPES[k]) for k, w in weights.items()}}
        y = _forward(args)
    with _jax.named_scope("loss_head"):
        err = _jnp.square(y.astype(_jnp.float32) - loss_target)
        return 0.5 * _jnp.sum(_jnp.mean(err, axis=-1)) if err.ndim else 0.5 * err


def _adamw(w, g, m, v):
    m = ADAM_B1 * m + (1.0 - ADAM_B1) * g
    v = ADAM_B2 * v + (1.0 - ADAM_B2) * _jnp.square(g)
    m_hat = m / (1.0 - ADAM_B1 ** ADAM_STEP)
    v_hat = v / (1.0 - ADAM_B2 ** ADAM_STEP)
    delta = -ADAM_LR * (m_hat / (_jnp.sqrt(v_hat) + ADAM_EPS) + ADAM_WD * w)
    return delta, m, v


def reference(x, meta_tokens, ffn1_norm, ffn1_wg, ffn1_wu, ffn1_wd, mix_norm, ffn2_norm, ffn2_wg, ffn2_wu, ffn2_wd, w_in_even, pool_w, pool_scale, hgrn_lb_logits, hgrn_gnorm, w_out_even, w_in_odd, conv_w, conv_b, conv_ln_g, conv_ln_b, lru_conv_w, lru_conv_b, lru_wa, lru_ba, lru_wx, lru_bx, lru_lambda, w_out_odd, final_norm, loss_target, m_meta_tokens, m_ffn1_norm, m_ffn1_wg, m_ffn1_wu, m_ffn1_wd, m_mix_norm, m_ffn2_norm, m_ffn2_wg, m_ffn2_wu, m_ffn2_wd, m_w_in_even, m_pool_w, m_pool_scale, m_hgrn_lb_logits, m_hgrn_gnorm, m_w_out_even, m_w_in_odd, m_conv_w, m_conv_b, m_conv_ln_g, m_conv_ln_b, m_lru_conv_w, m_lru_conv_b, m_lru_wa, m_lru_ba, m_lru_wx, m_lru_bx, m_lru_lambda, m_w_out_odd, m_final_norm, v_meta_tokens, v_ffn1_norm, v_ffn1_wg, v_ffn1_wu, v_ffn1_wd, v_mix_norm, v_ffn2_norm, v_ffn2_wg, v_ffn2_wu, v_ffn2_wd, v_w_in_even, v_pool_w, v_pool_scale, v_hgrn_lb_logits, v_hgrn_gnorm, v_w_out_even, v_w_in_odd, v_conv_w, v_conv_b, v_conv_ln_g, v_conv_ln_b, v_lru_conv_w, v_lru_conv_b, v_lru_wa, v_lru_ba, v_lru_wx, v_lru_bx, v_lru_lambda, v_w_out_odd, v_final_norm):
    given = dict(x=x, meta_tokens=meta_tokens, ffn1_norm=ffn1_norm, ffn1_wg=ffn1_wg, ffn1_wu=ffn1_wu, ffn1_wd=ffn1_wd, mix_norm=mix_norm, ffn2_norm=ffn2_norm, ffn2_wg=ffn2_wg, ffn2_wu=ffn2_wu, ffn2_wd=ffn2_wd, w_in_even=w_in_even, pool_w=pool_w, pool_scale=pool_scale, hgrn_lb_logits=hgrn_lb_logits, hgrn_gnorm=hgrn_gnorm, w_out_even=w_out_even, w_in_odd=w_in_odd, conv_w=conv_w, conv_b=conv_b, conv_ln_g=conv_ln_g, conv_ln_b=conv_ln_b, lru_conv_w=lru_conv_w, lru_conv_b=lru_conv_b, lru_wa=lru_wa, lru_ba=lru_ba, lru_wx=lru_wx, lru_bx=lru_bx, lru_lambda=lru_lambda, w_out_odd=w_out_odd, final_norm=final_norm, loss_target=loss_target, m_meta_tokens=m_meta_tokens, m_ffn1_norm=m_ffn1_norm, m_ffn1_wg=m_ffn1_wg, m_ffn1_wu=m_ffn1_wu, m_ffn1_wd=m_ffn1_wd, m_mix_norm=m_mix_norm, m_ffn2_norm=m_ffn2_norm, m_ffn2_wg=m_ffn2_wg, m_ffn2_wu=m_ffn2_wu, m_ffn2_wd=m_ffn2_wd, m_w_in_even=m_w_in_even, m_pool_w=m_pool_w, m_pool_scale=m_pool_scale, m_hgrn_lb_logits=m_hgrn_lb_logits, m_hgrn_gnorm=m_hgrn_gnorm, m_w_out_even=m_w_out_even, m_w_in_odd=m_w_in_odd, m_conv_w=m_conv_w, m_conv_b=m_conv_b, m_conv_ln_g=m_conv_ln_g, m_conv_ln_b=m_conv_ln_b, m_lru_conv_w=m_lru_conv_w, m_lru_conv_b=m_lru_conv_b, m_lru_wa=m_lru_wa, m_lru_ba=m_lru_ba, m_lru_wx=m_lru_wx, m_lru_bx=m_lru_bx, m_lru_lambda=m_lru_lambda, m_w_out_odd=m_w_out_odd, m_final_norm=m_final_norm, v_meta_tokens=v_meta_tokens, v_ffn1_norm=v_ffn1_norm, v_ffn1_wg=v_ffn1_wg, v_ffn1_wu=v_ffn1_wu, v_ffn1_wd=v_ffn1_wd, v_mix_norm=v_mix_norm, v_ffn2_norm=v_ffn2_norm, v_ffn2_wg=v_ffn2_wg, v_ffn2_wu=v_ffn2_wu, v_ffn2_wd=v_ffn2_wd, v_w_in_even=v_w_in_even, v_pool_w=v_pool_w, v_pool_scale=v_pool_scale, v_hgrn_lb_logits=v_hgrn_lb_logits, v_hgrn_gnorm=v_hgrn_gnorm, v_w_out_even=v_w_out_even, v_w_in_odd=v_w_in_odd, v_conv_w=v_conv_w, v_conv_b=v_conv_b, v_conv_ln_g=v_conv_ln_g, v_conv_ln_b=v_conv_ln_b, v_lru_conv_w=v_lru_conv_w, v_lru_conv_b=v_lru_conv_b, v_lru_wa=v_lru_wa, v_lru_ba=v_lru_ba, v_lru_wx=v_lru_wx, v_lru_bx=v_lru_bx, v_lru_lambda=v_lru_lambda, v_w_out_odd=v_w_out_odd, v_final_norm=v_final_norm)
    weights = {n: given[n] for n in TWIN_WEIGHTS}
    shared = {n: given[n] for n in SHARED_INPUTS}
    per_example = {n: given[n] for n in ['x']}
    grad_fn = _jax.value_and_grad(_loss, argnums=(0, 1))

    def one_microbatch(ex, loss_target):
        ex = dict(ex)
        diff = ex.pop(TWIN_DIFF_INPUT)
        return grad_fn(weights, diff, {**shared, **ex}, loss_target)

    if N_MICROBATCH == 1:
        loss, (grad_w, grad_x) = one_microbatch(per_example, given["loss_target"])
    else:
        def body(carry, xs):
            loss_sum, grad_sum = carry
            l_k, (gw_k, gx_k) = one_microbatch(xs[0], xs[1])
            with _jax.named_scope("update"):
                return (loss_sum + l_k, _jax.tree.map(_jnp.add, grad_sum, gw_k)), gx_k

        init = (_jnp.zeros((), _jnp.float32), _jax.tree.map(_jnp.zeros_like, weights))
        (loss, grad_w), grad_x = _jax.lax.scan(body, init, (per_example, given["loss_target"]))
    with _jax.named_scope("update"):
        delta_w, new_m, new_v = {}, {}, {}
        for n in TWIN_WEIGHTS:
            delta_w[n], new_m[n], new_v[n] = _adamw(weights[n], grad_w[n], given["m_" + n], given["v_" + n])
    return (loss, grad_x, *[grad_w[n] for n in TWIN_WEIGHTS], *[delta_w[n] for n in TWIN_WEIGHTS],
            *[new_m[n] for n in TWIN_WEIGHTS], *[new_v[n] for n in TWIN_WEIGHTS])
```

```python
import functools
import math

import jax
import jax.numpy as jnp
from jax import lax
from jax.experimental import pallas as pl
from jax.experimental.pallas import tpu as pltpu

F32 = jnp.float32
MXU = jnp.bfloat16
EPS = 1e-6
CH = 128
HG = 64
N_META = 16
CONV_W = 31
LRU_W = 4
LRU_C = 8.0
VMEM_LIMIT = 48 * 2 ** 20
ADAM_LR, ADAM_B1, ADAM_B2, ADAM_EPS, ADAM_WD, ADAM_STEP = 0.001, 0.9, 0.999, 1e-08, 0.01, 10
MESH_AXES = ("x", "y", "c")
NDEV = 8

W_NAMES = ['meta_tokens', 'ffn1_norm', 'ffn1_wg', 'ffn1_wu', 'ffn1_wd', 'mix_norm', 'ffn2_norm', 'ffn2_wg', 'ffn2_wu',
           'ffn2_wd', 'w_in_even', 'pool_w', 'pool_scale', 'hgrn_lb_logits', 'hgrn_gnorm', 'w_out_even', 'w_in_odd',
           'conv_w', 'conv_b', 'conv_ln_g', 'conv_ln_b', 'lru_conv_w', 'lru_conv_b', 'lru_wa', 'lru_ba', 'lru_wx',
           'lru_bx', 'lru_lambda', 'w_out_odd', 'final_norm']
SHARD_AXIS = {'meta_tokens': 1, 'ffn1_wg': 2, 'ffn1_wu': 2, 'ffn1_wd': 1, 'ffn2_wg': 2, 'ffn2_wu': 2, 'ffn2_wd': 1,
              'w_in_even': 2, 'w_out_even': 1, 'w_in_odd': 2, 'conv_w': 2, 'conv_b': 1, 'conv_ln_g': 1,
              'conv_ln_b': 1, 'lru_conv_w': 2, 'lru_conv_b': 1, 'lru_ba': 1, 'lru_bx': 1, 'lru_lambda': 1,
              'w_out_odd': 1}
BIG = ['ffn1_wg', 'ffn1_wu', 'ffn1_wd', 'ffn2_wg', 'ffn2_wu', 'ffn2_wd', 'w_in_even', 'w_out_even', 'w_in_odd',
       'w_out_odd']
SMALL_SHARDED = [n for n in W_NAMES if n in SHARD_AXIS and n not in BIG]
REPLICATED = [n for n in W_NAMES if n not in SHARD_AXIS]


def _cparams(sem=None, vmem=VMEM_LIMIT):
    return pltpu.CompilerParams(dimension_semantics=sem, vmem_limit_bytes=vmem)


def _tile(n):
    for c in (640, 512, 256, 128):
        if n % c == 0:
            return c
    return n


def _rowtile(n):
    for c in (256, 352, 128, 64, 32, 16, 8):
        if n % c == 0:
            return c
    return n


def _exchange(arrays, bcast, name):
    n = len(arrays)

    def body(*refs):
        ins, outs = refs[:n], refs[n:2 * n]
        ssem, rsem, lsem = refs[2 * n], refs[2 * n + 1], refs[2 * n + 2]
        x, y, c = lax.axis_index("x"), lax.axis_index("y"), lax.axis_index("c")
        me = 4 * x + 2 * y + c
        sends, recvs, locs = [], [], []
        for a in range(n):
            loc = pltpu.make_async_copy(ins[a] if bcast[a] else ins[a].at[me], outs[a].at[me], lsem.at[a])
            loc.start()
            locs.append(loc)
            for m in range(1, NDEV):
                px = 1 - x if (m >> 2) & 1 else x
                py = 1 - y if (m >> 1) & 1 else y
                pc = 1 - c if m & 1 else c
                peer = 4 * px + 2 * py + pc
                src = ins[a] if bcast[a] else ins[a].at[peer]
                k = a * NDEV + m
                snd = pltpu.make_async_remote_copy(src_ref=src, dst_ref=outs[a].at[me], send_sem=ssem.at[k],
                                                   recv_sem=rsem.at[k], device_id=(px, py, pc),
                                                   device_id_type=pl.DeviceIdType.MESH)
                snd.start()
                sends.append(snd)
                recvs.append(pltpu.make_async_remote_copy(src_ref=src, dst_ref=outs[a].at[peer], send_sem=ssem.at[k],
                                                          recv_sem=rsem.at[k], device_id=(px, py, pc),
                                                          device_id_type=pl.DeviceIdType.MESH))
        for r in recvs:
            r.wait_recv()
        for s in sends:
            s.wait_send()
        for loc in locs:
            loc.wait()

    out_shape = []
    for a, arr in enumerate(arrays):
        shp = arr.shape if bcast[a] else arr.shape[1:]
        out_shape.append(jax.ShapeDtypeStruct((NDEV,) + tuple(shp), arr.dtype))
    any_spec = pl.BlockSpec(memory_space=pl.ANY)
    return pl.pallas_call(
        body, name=name, out_shape=tuple(out_shape), in_specs=[any_spec] * n, out_specs=tuple([any_spec] * n),
        scratch_shapes=[pltpu.SemaphoreType.DMA((n * NDEV,)), pltpu.SemaphoreType.DMA((n * NDEV,)),
                        pltpu.SemaphoreType.DMA((n,))],
    )(*arrays)


def _adamw(recv, w, m, v, name):
    R, C = w.shape
    br = _rowtile(R)

    def body(r_ref, w_ref, m_ref, v_ref, g_o, d_o, m_o, v_o):
        g = r_ref[0]
        for k in range(1, NDEV):
            g = g + r_ref[k]
        mn = ADAM_B1 * m_ref[...] + (1.0 - ADAM_B1) * g
        vn = ADAM_B2 * v_ref[...] + (1.0 - ADAM_B2) * (g * g)
        m_hat = mn / (1.0 - ADAM_B1 ** ADAM_STEP)
        v_hat = vn / (1.0 - ADAM_B2 ** ADAM_STEP)
        g_o[...] = g
        d_o[...] = -ADAM_LR * (m_hat / (jnp.sqrt(v_hat) + ADAM_EPS) + ADAM_WD * w_ref[...])
        m_o[...] = mn
        v_o[...] = vn

    blk = pl.BlockSpec((br, C), lambda i: (i, 0))
    sds = jax.ShapeDtypeStruct((R, C), F32)
    return pl.pallas_call(
        body, name=name, grid=(R // br,), out_shape=(sds, sds, sds, sds),
        in_specs=[pl.BlockSpec((NDEV, br, C), lambda i: (0, i, 0)), blk, blk, blk], out_specs=(blk, blk, blk, blk),
        compiler_params=_cparams(("parallel",)),
    )(recv, w, m, v)


_DIMS = {"NN": ((1,), (0,)), "NT": ((1,), (1,)), "TN": ((0,), (0,))}


def _dot(a, b, mode="NN"):
    return lax.dot_general(a.astype(MXU), b.astype(MXU), (_DIMS[mode], ((), ())), preferred_element_type=F32)


def _dotf(a, b, mode="NN"):
    return lax.dot_general(a, b, (_DIMS[mode], ((), ())), precision=lax.Precision.HIGHEST,
                           preferred_element_type=F32)


def _mm(pairs, mode, name, res=None, res_scale=1.0, out_dtype=F32):
    a0, b0 = pairs[0]
    M = a0.shape[1] if mode == "TN" else a0.shape[0]
    N = b0.shape[0] if mode == "NT" else b0.shape[1]
    tm, tn = _tile(M), _tile(N)
    npairs = len(pairs)

    def body(*refs):
        acc = None
        for p in range(npairs):
            d = _dot(refs[2 * p][...], refs[2 * p + 1][...], mode)
            acc = d if acc is None else acc + d
        if res_scale != 1.0:
            acc = res_scale * acc
        if res is not None:
            acc = refs[2 * npairs][...] + acc
        refs[-1][...] = acc.astype(out_dtype)

    in_specs, args = [], []
    for a, b in pairs:
        if mode == "TN":
            in_specs.append(pl.BlockSpec((a.shape[0], tm), lambda i, j: (0, i)))
        else:
            in_specs.append(pl.BlockSpec((tm, a.shape[1]), lambda i, j: (i, 0)))
        if mode == "NT":
            in_specs.append(pl.BlockSpec((tn, b.shape[1]), lambda i, j: (j, 0)))
        else:
            in_specs.append(pl.BlockSpec((b.shape[0], tn), lambda i, j: (0, j)))
        args += [a, b]
    if res is not None:
        in_specs.append(pl.BlockSpec((tm, tn), lambda i, j: (i, j)))
        args.append(res)
    return pl.pallas_call(
        body, name=name, grid=(M // tm, N // tn), out_shape=jax.ShapeDtypeStruct((M, N), out_dtype),
        in_specs=in_specs, out_specs=pl.BlockSpec((tm, tn), lambda i, j: (i, j)),
        compiler_params=_cparams(("parallel", "parallel")),
    )(*args)


def _rms_fwd(h, gamma, name):
    T, D = h.shape
    tm = _tile(T)

    def body(h_ref, g_ref, o_ref):
        x = h_ref[...]
        r = lax.rsqrt(jnp.mean(x * x, axis=-1, keepdims=True) + EPS)
        o_ref[...] = (x * r * g_ref[...]).astype(MXU)

    return pl.pallas_call(
        body, name=name, grid=(T // tm,), out_shape=jax.ShapeDtypeStruct((T, D), MXU),
        in_specs=[pl.BlockSpec((tm, D), lambda i: (i, 0)), pl.BlockSpec((1, D), lambda i: (0, 0))],
        out_specs=pl.BlockSpec((tm, D), lambda i: (i, 0)), compiler_params=_cparams(("parallel",)),
    )(h, gamma)


def _rms_bwd_math(x, gamma, dy):
    r = lax.rsqrt(jnp.mean(x * x, axis=-1, keepdims=True) + EPS)
    z = dy * gamma
    dx = r * z - x * (r * r * r) * jnp.mean(z * x, axis=-1, keepdims=True)
    dgamma = jnp.sum(dy * x * r, axis=0, keepdims=True)
    return dx, dgamma


def _rms_bwd(h, gamma, dxn, dres, name):
    T, D = h.shape
    tm = _tile(T)

    def body(h_ref, g_ref, dxn_ref, dres_ref, dh_ref, dg_ref):
        dx, dgamma = _rms_bwd_math(h_ref[...], g_ref[...], dxn_ref[...])
        dh_ref[...] = dres_ref[...] + dx

        @pl.when(pl.program_id(0) == 0)
        def _():
            dg_ref[...] = jnp.zeros_like(dg_ref)

        dg_ref[...] += dgamma

    row = pl.BlockSpec((tm, D), lambda i: (i, 0))
    vec = pl.BlockSpec((1, D), lambda i: (0, 0))
    return pl.pallas_call(
        body, name=name, grid=(T // tm,),
        out_shape=(jax.ShapeDtypeStruct((T, D), F32), jax.ShapeDtypeStruct((1, D), F32)),
        in_specs=[row, vec, row, row], out_specs=(row, vec), compiler_params=_cparams(("arbitrary",)),
    )(h, gamma, dxn, dres)


def _loss_head(h, gamma, tgt, lo, hi, name):
    T, D = h.shape
    tm = _tile(T)

    def body(h_ref, g_ref, t_ref, loss_ref, dh_ref, dg_ref):
        i = pl.program_id(0)
        x = h_ref[...]
        r = lax.rsqrt(jnp.mean(x * x, axis=-1, keepdims=True) + EPS)
        y = x * r * g_ref[...]
        rows = i * tm + lax.broadcasted_iota(jnp.int32, (tm, 1), 0)
        valid = jnp.logical_and(rows >= lo, rows < hi)
        diff = jnp.where(valid, y - t_ref[...], 0.0)
        part = 0.5 * jnp.sum(jnp.sum(diff * diff, axis=-1, keepdims=True) / D, axis=0, keepdims=True)
        dx, dgamma = _rms_bwd_math(x, g_ref[...], diff / D)
        dh_ref[...] = dx

        @pl.when(i == 0)
        def _():
            dg_ref[...] = jnp.zeros_like(dg_ref)
            loss_ref[...] = jnp.zeros_like(loss_ref)

        dg_ref[...] += dgamma
        loss_ref[...] += jnp.broadcast_to(part, loss_ref.shape)

    row = pl.BlockSpec((tm, D), lambda i: (i, 0))
    vec = pl.BlockSpec((1, D), lambda i: (0, 0))
    lsp = pl.BlockSpec((8, 128), lambda i: (0, 0))
    return pl.pallas_call(
        body, name=name, grid=(T // tm,),
        out_shape=(jax.ShapeDtypeStruct((8, 128), F32), jax.ShapeDtypeStruct((T, D), F32),
                   jax.ShapeDtypeStruct((1, D), F32)),
        in_specs=[row, vec, row], out_specs=(lsp, row, vec), compiler_params=_cparams(("arbitrary",)),
    )(h, gamma, tgt)


def _ffn_up(xn, wg, wu, name):
    T, D = xn.shape
    Fd = wg.shape[1]
    tm, tn = _tile(T), _tile(Fd)

    def body(x_ref, wg_ref, wu_ref, g_ref, u_ref, a_ref):
        x = x_ref[...]
        g = _dot(x, wg_ref[...])
        u = _dot(x, wu_ref[...])
        g_ref[...] = g.astype(MXU)
        u_ref[...] = u.astype(MXU)
        a_ref[...] = (g * jax.nn.sigmoid(g) * u).astype(MXU)

    wsp = pl.BlockSpec((D, tn), lambda i, j: (0, j))
    osp = pl.BlockSpec((tm, tn), lambda i, j: (i, j))
    sds = jax.ShapeDtypeStruct((T, Fd), MXU)
    return pl.pallas_call(
        body, name=name, grid=(T // tm, Fd // tn), out_shape=(sds, sds, sds),
        in_specs=[pl.BlockSpec((tm, D), lambda i, j: (i, 0)), wsp, wsp], out_specs=(osp, osp, osp),
        compiler_params=_cparams(("parallel", "parallel")),
    )(xn, wg, wu)


def _ffn_dact(dy, wd, g, u, scale, name):
    T, D = dy.shape
    Fd = wd.shape[0]
    tm, tn = _tile(T), _tile(Fd)

    def body(dy_ref, wd_ref, g_ref, u_ref, dg_ref, du_ref):
        da = scale * _dot(dy_ref[...], wd_ref[...], "NT")
        gg = g_ref[...].astype(F32)
        uu = u_ref[...].astype(F32)
        sg = jax.nn.sigmoid(gg)
        dg_ref[...] = (da * uu * (sg * (1.0 + gg * (1.0 - sg)))).astype(MXU)
        du_ref[...] = (da * gg * sg).astype(MXU)

    osp = pl.BlockSpec((tm, tn), lambda i, j: (i, j))
    sds = jax.ShapeDtypeStruct((T, Fd), MXU)
    return pl.pallas_call(
        body, name=name, grid=(T // tm, Fd // tn), out_shape=(sds, sds),
        in_specs=[pl.BlockSpec((tm, D), lambda i, j: (i, 0)), pl.BlockSpec((tn, D), lambda i, j: (j, 0)), osp, osp],
        out_specs=(osp, osp), compiler_params=_cparams(("parallel", "parallel")),
    )(dy, wd, g, u)


def _down(v, s):
    return v if s == 0 else pltpu.roll(v, s, 0)


def _up(v, s):
    return v if s == 0 else pltpu.roll(v, v.shape[0] - s, 0)


def _rows(n):
    return lax.broadcasted_iota(jnp.int32, (n, 1), 0)


def _zero_pad_rows(ref, lo_end, hi_start, T):
    ref[pl.ds(0, lo_end), :] = jnp.zeros((lo_end, ref.shape[1]), ref.dtype)
    if T > hi_start:
        ref[pl.ds(hi_start, T - hi_start), :] = jnp.zeros((T - hi_start, ref.shape[1]), ref.dtype)


def _colblock(T, off):
    return pl.BlockSpec((T, 128), lambda j: (0, off + j))


def _vecblock(rows=1):
    return pl.BlockSpec((rows, 128), lambda j: (0, j))


def _pool_lane_consts(n):
    lane = lax.broadcasted_iota(jnp.int32, (n, 256), 1)
    win = jnp.where(lane < 64, 2.0, jnp.where(lane < 128, 4.0, jnp.where(lane < 192, 8.0, 16.0)))
    return lane, win


def _pool_select(lane, s2, s4, s8, s16):
    return jnp.where(lane < 64, s2, jnp.where(lane < 128, s4, jnp.where(lane < 192, s8, s16)))


def _pool_mixed(xh, start):
    s2 = xh + _down(xh, 1)
    s4 = s2 + _down(s2, 2)
    s8 = s4 + _down(s4, 4)
    s16 = s8 + _down(s8, 8)
    n = xh.shape[0] - 16
    lane, _ = _pool_lane_consts(n + 16)
    _, win = _pool_lane_consts(n)
    t1 = (start - CH + 1 + _rows(n)).astype(F32)
    cnt = jnp.minimum(jnp.maximum(t1, 1.0), win)
    return _pool_select(lane, s2, s4, s8, s16)[16:] / cnt - xh[16:]


def _pool_fwd(p, wbd, scale, nreal, real_end, name):
    T = p.shape[0]

    def body(p_ref, w_ref, s_ref, y_ref):
        _zero_pad_rows(y_ref, CH, CH * (1 + nreal), T)

        def chunk(c, carry):
            start = pl.multiple_of(c * CH, CH)
            mixed = _pool_mixed(p_ref[pl.ds(start - 16, CH + 16), :], start)
            y = _dot(mixed, w_ref[...]) * s_ref[...]
            y_ref[pl.ds(start, CH), :] = jnp.where(start + _rows(CH) < real_end, y, 0.0)
            return carry

        lax.fori_loop(1, 1 + nreal, chunk, 0)

    return pl.pallas_call(
        body, name=name, grid=(1,), out_shape=jax.ShapeDtypeStruct((T, 256), F32),
        in_specs=[pl.BlockSpec((T, 256), lambda j: (0, 0)), pl.BlockSpec((256, 256), lambda j: (0, 0)),
                  pl.BlockSpec((1, 256), lambda j: (0, 0))],
        out_specs=pl.BlockSpec((T, 256), lambda j: (0, 0)), compiler_params=_cparams(("arbitrary",)),
    )(p, wbd, scale)


def _pool_bwd(p, wbd, scale, dy, nreal, real_end, name):
    T = p.shape[0]

    def body(p_ref, w_ref, s_ref, dy_ref, dp_ref, dw_ref, ds_ref):
        _zero_pad_rows(dp_ref, CH, CH * (1 + nreal), T)
        dw_ref[...] = jnp.zeros_like(dw_ref)
        ds_ref[...] = jnp.zeros_like(ds_ref)

        def chunk(c, carry):
            start = pl.multiple_of(c * CH, CH)
            mixed = _pool_mixed(p_ref[pl.ds(start - 16, CH + 16), :], start)
            ypre = _dot(mixed, w_ref[...])
            n = CH + 16
            dye = jnp.where(start + _rows(n) < real_end, dy_ref[pl.ds(start, n), :], 0.0)
            dys = dye * s_ref[...]
            ds_ref[...] += jnp.sum(dye[:CH] * ypre, axis=0, keepdims=True)
            dw_ref[...] += _dot(mixed, dys[:CH], "TN")
            dmix = _dot(dys, w_ref[...], "NT")
            lane, win = _pool_lane_consts(n)
            t1 = (start - CH + 1 + _rows(n)).astype(F32)
            z = dmix / jnp.minimum(jnp.maximum(t1, 1.0), win)
            r2 = z + _up(z, 1)
            r4 = r2 + _up(r2, 2)
            r8 = r4 + _up(r4, 4)
            r16 = r8 + _up(r8, 8)
            dp_ref[pl.ds(start, CH), :] = (_pool_select(lane, r2, r4, r8, r16) - dmix)[:CH]
            return carry

        lax.fori_loop(1, 1 + nreal, chunk, 0)

    full = lambda r, c: pl.BlockSpec((r, c), lambda j: (0, 0))
    return pl.pallas_call(
        body, name=name, grid=(1,),
        out_shape=(jax.ShapeDtypeStruct((T, 256), F32), jax.ShapeDtypeStruct((256, 256), F32),
                   jax.ShapeDtypeStruct((1, 256), F32)),
        in_specs=[full(T, 256), full(256, 256), full(1, 256), full(T, 256)],
        out_specs=(full(T, 256), full(256, 256), full(1, 256)), compiler_params=_cparams(("arbitrary",)),
    )(p, wbd, scale, dy)


def _hgrn_chunk(St, qr, fr, ir, gr, l0, l1, gn):
    rows = lax.broadcasted_iota(jnp.int32, (HG, HG), 0)
    cols = lax.broadcasted_iota(jnp.int32, (HG, HG), 1)
    causal = rows >= cols
    ltri = causal.astype(F32)
    lb = jax.nn.sigmoid(l0 - l1)
    sg = jax.nn.sigmoid(fr)
    logf = jnp.log(lb + (1.0 - lb) * sg)
    kk = (1.0 - lb) * (1.0 - sg)
    q = qr * jax.nn.sigmoid(qr)
    b = jnp.dot(ltri, logf, precision=lax.Precision.HIGHEST, preferred_element_type=F32)
    bl = jnp.sum(logf, axis=0, keepdims=True)
    bm = jnp.sum(jnp.where(_rows(HG) <= HG // 2, logf, 0.0), axis=0, keepdims=True)
    o = _dotf(q * jnp.exp(b), St, "NT")
    A = _dotf(q * jnp.exp(b - bm), kk * jnp.exp(bm - b), "NT")
    o = o + _dotf(jnp.where(causal, A, 0.0), ir)
    St_new = St * jnp.exp(bl) + _dotf(ir, kk * jnp.exp(bl - b), "TN")
    on = o * lax.rsqrt(jnp.mean(o * o, axis=-1, keepdims=True) + EPS) * gn
    return St_new, on * (gr * jax.nn.sigmoid(gr))


def _hgrn_specs(T):
    return [_colblock(T, 2), _colblock(T, 8), _colblock(T, 14), _colblock(T, 20), _vecblock(), _vecblock(),
            pl.BlockSpec((1, 128), lambda j: (0, 0))]


def _hgrn_fwd(p, l0, l1, gn, nreal, real_end, name):
    T = p.shape[0]
    nch = nreal * (CH // HG)

    def body(q_ref, f_ref, i_ref, g_ref, l0_ref, l1_ref, gn_ref, y_ref, s_ref):
        _zero_pad_rows(y_ref, CH, CH * (1 + nreal), T)

        def chunk(c, St):
            start = pl.multiple_of(CH + c * HG, HG)
            sl = pl.ds(start, HG)
            s_ref[0, c] = St
            St_new, y = _hgrn_chunk(St, q_ref[sl, :], f_ref[sl, :], i_ref[sl, :], g_ref[sl, :], l0_ref[...],
                                    l1_ref[...], gn_ref[...])
            y_ref[sl, :] = jnp.where(start + _rows(HG) < real_end, y, 0.0)
            return St_new

        lax.fori_loop(0, nch, chunk, jnp.zeros((128, 128), F32))

    return pl.pallas_call(
        body, name=name, grid=(6,),
        out_shape=(jax.ShapeDtypeStruct((T, 768), F32), jax.ShapeDtypeStruct((6, nch, 128, 128), F32)),
        in_specs=_hgrn_specs(T),
        out_specs=(_colblock(T, 0), pl.BlockSpec((1, nch, 128, 128), lambda j: (j, 0, 0, 0))),
        compiler_params=_cparams(("parallel",)),
    )(p, p, p, p, l0, l1, gn)


def _hgrn_bwd(p, l0, l1, gn, states, dy, nreal, real_end, name):
    T = p.shape[0]
    nch = nreal * (CH // HG)

    def body(q_ref, f_ref, i_ref, g_ref, l0_ref, l1_ref, gn_ref, s_ref, dy_ref,
             dq_ref, df_ref, di_ref, dg_ref, dl0_ref, dl1_ref, dgn_ref):
        for r in (dq_ref, df_ref, di_ref, dg_ref):
            _zero_pad_rows(r, CH, CH * (1 + nreal), T)

        def chunk(k, carry):
            dSt, a0, a1, agn = carry
            c = nch - 1 - k
            start = pl.multiple_of(CH + c * HG, HG)
            sl = pl.ds(start, HG)
            _, vjp = jax.vjp(_hgrn_chunk, s_ref[0, c], q_ref[sl, :], f_ref[sl, :], i_ref[sl, :], g_ref[sl, :],
                             l0_ref[...], l1_ref[...], gn_ref[...])
            dyc = jnp.where(start + _rows(HG) < real_end, dy_ref[sl, :], 0.0)
            dS, dq, df, di, dg, d0, d1, dgn = vjp((dSt, dyc))
            dq_ref[sl, :] = dq
            df_ref[sl, :] = df
            di_ref[sl, :] = di
            dg_ref[sl, :] = dg
            return dS, a0 + d0, a1 + d1, agn + dgn

        z = jnp.zeros((1, 128), F32)
        _, a0, a1, agn = lax.fori_loop(0, nch, chunk, (jnp.zeros((128, 128), F32), z, z, z))
        dl0_ref[...] = a0
        dl1_ref[...] = a1

        @pl.when(pl.program_id(0) == 0)
        def _():
            dgn_ref[...] = jnp.zeros_like(dgn_ref)

        dgn_ref[...] += agn

    big = jax.ShapeDtypeStruct((T, 768), F32)
    vec = jax.ShapeDtypeStruct((1, 768), F32)
    return pl.pallas_call(
        body, name=name, grid=(6,),
        out_shape=(big, big, big, big, vec, vec, jax.ShapeDtypeStruct((1, 128), F32)),
        in_specs=_hgrn_specs(T) + [pl.BlockSpec((1, nch, 128, 128), lambda j: (j, 0, 0, 0)), _colblock(T, 2)],
        out_specs=(_colblock(T, 0), _colblock(T, 0), _colblock(T, 0), _colblock(T, 0), _vecblock(), _vecblock(),
                   pl.BlockSpec((1, 128), lambda j: (0, 0))),
        compiler_params=_cparams(("arbitrary",), 60 * 2 ** 20),
    )(p, p, p, p, l0, l1, gn, states, dy)


def _glu(a, b):
    return a * jax.nn.sigmoid(b)


def _conv_post(cv, ln_g, ln_b):
    mu = jnp.mean(cv, axis=-1, keepdims=True)
    d = cv - mu
    var = jnp.mean(d * d, axis=-1, keepdims=True)
    un = d * lax.rsqrt(var + EPS) * ln_g + ln_b
    return un * jax.nn.sigmoid(un)


def _causal_conv(uh, w_ref, width, halo):
    acc = None
    for j in range(width):
        term = _down(uh, width - 1 - j) * w_ref[pl.ds(j, 1), :]
        acc = term if acc is None else acc + term
    return acc[halo:]


def _conf_fwd(p, cw, cb, lg, lb, nreal, real_end, name):
    T = p.shape[0]

    def body(a_ref, b_ref, w_ref, cb_ref, lg_ref, lb_ref, y_ref):
        _zero_pad_rows(y_ref, CH, CH * (1 + nreal), T)

        def chunk(c, carry):
            start = pl.multiple_of(c * CH, CH)
            ext = pl.ds(start - 32, CH + 32)
            cv = _causal_conv(_glu(a_ref[ext, :], b_ref[ext, :]), w_ref, CONV_W, 32) + cb_ref[...]
            y = _conv_post(cv, lg_ref[...], lb_ref[...])
            y_ref[pl.ds(start, CH), :] = jnp.where(start + _rows(CH) < real_end, y, 0.0)
            return carry

        lax.fori_loop(1, 1 + nreal, chunk, 0)

    return pl.pallas_call(
        body, name=name, grid=(4,), out_shape=jax.ShapeDtypeStruct((T, 512), F32),
        in_specs=[_colblock(T, 0), _colblock(T, 4), _vecblock(32), _vecblock(), _vecblock(), _vecblock()],
        out_specs=_colblock(T, 0), compiler_params=_cparams(("parallel",)),
    )(p, p, cw, cb, lg, lb)


def _conf_bwd(p, cw, cb, lg, lb, dy, nreal, real_end, name):
    T = p.shape[0]

    def body(a_ref, b_ref, w_ref, cb_ref, lg_ref, lb_ref, dy_ref, da_ref, db_ref, dw_ref, dcb_ref, dlg_ref, dlb_ref):
        _zero_pad_rows(da_ref, CH, CH * (1 + nreal), T)
        _zero_pad_rows(db_ref, CH, CH * (1 + nreal), T)
        for r in (dw_ref, dcb_ref, dlg_ref, dlb_ref):
            r[...] = jnp.zeros_like(r)

        def chunk(c, carry):
            start = pl.multiple_of(c * CH, CH)
            ext = pl.ds(start - 32, CH + 64)
            ue = _glu(a_ref[ext, :], b_ref[ext, :])
            cv = _causal_conv(ue, w_ref, CONV_W, 32) + cb_ref[...]
            dye = jnp.where(start + _rows(CH + 32) < real_end, dy_ref[pl.ds(start, CH + 32), :], 0.0)
            _, vjp_cur = jax.vjp(_conv_post, cv[:CH], lg_ref[...], lb_ref[...])
            dc_cur, dlg, dlb = vjp_cur(dye[:CH])
            _, vjp_halo = jax.vjp(_conv_post, cv[CH:], lg_ref[...], lb_ref[...])
            dce = jnp.concatenate([dc_cur, vjp_halo(dye[CH:])[0]], axis=0)
            dlg_ref[...] += dlg
            dlb_ref[...] += dlb
            dcb_ref[...] += jnp.sum(dc_cur, axis=0, keepdims=True)
            du = None
            for j in range(CONV_W):
                w_j = w_ref[pl.ds(j, 1), :]
                term = _up(dce, CONV_W - 1 - j)[:CH] * w_j
                du = term if du is None else du + term
                dw_ref[pl.ds(j, 1), :] += jnp.sum(dc_cur * _up(ue, 2 + j)[:CH], axis=0, keepdims=True)
            cur = pl.ds(start, CH)
            _, vjp_glu = jax.vjp(_glu, a_ref[cur, :], b_ref[cur, :])
            da, db = vjp_glu(du)
            da_ref[cur, :] = da
            db_ref[cur, :] = db
            return carry

        lax.fori_loop(1, 1 + nreal, chunk, 0)

    big = jax.ShapeDtypeStruct((T, 512), F32)
    vec = jax.ShapeDtypeStruct((1, 512), F32)
    return pl.pallas_call(
        body, name=name, grid=(4,), out_shape=(big, big, jax.ShapeDtypeStruct((32, 512), F32), vec, vec, vec),
        in_specs=[_colblock(T, 0), _colblock(T, 4), _vecblock(32), _vecblock(), _vecblock(), _vecblock(),
                  _colblock(T, 0)],
        out_specs=(_colblock(T, 0), _colblock(T, 0), _vecblock(32), _vecblock(), _vecblock(), _vecblock()),
        compiler_params=_cparams(("parallel",)),
    )(p, p, cw, cb, lg, lb, dy)


def _softplus_neg(lam):
    e = jnp.exp(-lam)
    small = e * (1.0 - e * (0.5 - e * (1.0 / 3.0 - e * 0.25)))
    return jnp.where(e < 0.02, small, jnp.log(1.0 + e))


def _one_minus_exp(x):
    series = -x * (1.0 + x * (0.5 + x * (1.0 / 6.0 + x * (1.0 / 24.0 + x * (1.0 / 120.0)))))
    return jnp.where(x > -0.05, series, 1.0 - jnp.exp(x))


def _lru_pre(u, wa, wx, ba, bx, lam, first):
    r = jax.nn.sigmoid(_dot(u, wa) + ba)
    i = jax.nn.sigmoid(_dot(u, wx) + bx)
    log_a = -LRU_C * r * _softplus_neg(lam)
    a = jnp.exp(log_a)
    mult = jnp.sqrt(_one_minus_exp(2.0 * log_a))
    return a, jnp.where(first, 1.0, mult) * (i * u)


def _gelu_gate(gate, h):
    inner = math.sqrt(2.0 / math.pi) * (gate + 0.044715 * (gate * gate * gate))
    return 0.5 * gate * (1.0 + jnp.tanh(inner)) * h


def _lru_specs(T):
    mat = pl.BlockSpec((1, 128, 128), lambda j: (j, 0, 0))
    return [_colblock(T, 8), _colblock(T, 12), _vecblock(8), _vecblock(), mat, mat, _vecblock(), _vecblock(),
            _vecblock()]


def _lru_fwd(p, cw, cb, wa, wx, ba, bx, lam, nreal, real_end, name):
    T = p.shape[0]

    def body(x_ref, g_ref, w_ref, cb_ref, wa_ref, wx_ref, ba_ref, bx_ref, lam_ref, y_ref, h_ref):
        _zero_pad_rows(y_ref, CH, CH * (1 + nreal), T)
        _zero_pad_rows(h_ref, CH, CH * (1 + nreal), T)
        rows = _rows(CH)

        def chunk(c, hprev):
            start = pl.multiple_of(c * CH, CH)
            u = _causal_conv(x_ref[pl.ds(start - 8, CH + 8), :], w_ref, LRU_W, 8) + cb_ref[...]
            A, B = _lru_pre(u, wa_ref[0], wx_ref[0], ba_ref[...], bx_ref[...], lam_ref[...], start + rows == CH)
            s = 1
            while s < CH:
                B = A * jnp.where(rows >= s, _down(B, s), 0.0) + B
                A = A * jnp.where(rows >= s, _down(A, s), 1.0)
                s *= 2
            h = B + A * hprev
            cur = pl.ds(start, CH)
            h_ref[cur, :] = h
            y_ref[cur, :] = jnp.where(start + rows < real_end, _gelu_gate(g_ref[cur, :], h), 0.0)
            return jnp.sum(jnp.where(rows == CH - 1, h, 0.0), axis=0, keepdims=True)

        lax.fori_loop(1, 1 + nreal, chunk, jnp.zeros((1, 128), F32))

    big = jax.ShapeDtypeStruct((T, 512), F32)
    return pl.pallas_call(
        body, name=name, grid=(4,), out_shape=(big, big), in_specs=_lru_specs(T),
        out_specs=(_colblock(T, 0), _colblock(T, 0)), compiler_params=_cparams(("parallel",)),
    )(p, p, cw, cb, wa, wx, ba, bx, lam)


def _lru_bwd(p, cw, cb, wa, wx, ba, bx, lam, hs, dy, nreal, real_end, name):
    T = p.shape[0]

    def body(x_ref, g_ref, w_ref, cb_ref, wa_ref, wx_ref, ba_ref, bx_ref, lam_ref, h_ref, dy_ref,
             dx_ref, dgate_ref, dw_ref, dcb_ref, dwa_ref, dwx_ref, dba_ref, dbx_ref, dlam_ref):
        _zero_pad_rows(dx_ref, CH, CH * (1 + nreal), T)
        _zero_pad_rows(dgate_ref, CH, CH * (1 + nreal), T)
        for r in (dw_ref, dcb_ref, dwa_ref, dwx_ref, dba_ref, dbx_ref, dlam_ref):
            r[...] = jnp.zeros_like(r)
        rows = _rows(CH)

        def chunk(k, carry):
            cdh, du_head = carry
            c = nreal - k
            start = pl.multiple_of(c * CH, CH)
            cur = pl.ds(start, CH)
            xe = x_ref[pl.ds(start - 8, CH + 8), :]
            u = _causal_conv(xe, w_ref, LRU_W, 8) + cb_ref[...]
            first = start + rows == CH
            (a, _), vjp_pre = jax.vjp(lambda uu, m1, m2, b1, b2, ll: _lru_pre(uu, m1, m2, b1, b2, ll, first),
                                      u, wa_ref[0], wx_ref[0], ba_ref[...], bx_ref[...], lam_ref[...])
            h = h_ref[cur, :]
            hm1 = _down(h_ref[pl.ds(start - 8, CH + 8), :], 1)[8:]
            _, vjp_post = jax.vjp(_gelu_gate, g_ref[cur, :], h)
            dgate, D = vjp_post(jnp.where(start + rows < real_end, dy_ref[cur, :], 0.0))
            dgate_ref[cur, :] = dgate
            D = D + jnp.where(rows == CH - 1, cdh, 0.0)
            C = jnp.where(rows < CH - 1, _up(a, 1), 0.0)
            s = 1
            while s < CH:
                D = D + C * jnp.where(rows + s < CH, _up(D, s), 0.0)
                C = C * jnp.where(rows + s < CH, _up(C, s), 1.0)
                s *= 2
            du, dwa, dwx, dba, dbx, dlam = vjp_pre((D * hm1, D))
            dwa_ref[0] += dwa
            dwx_ref[0] += dwx
            dba_ref[...] += dba
            dbx_ref[...] += dbx
            dlam_ref[...] += dlam
            dcb_ref[...] += jnp.sum(du, axis=0, keepdims=True)
            due = jnp.concatenate([du, du_head], axis=0)
            dx = None
            for j in range(LRU_W):
                term = _up(due, LRU_W - 1 - j)[:CH] * w_ref[pl.ds(j, 1), :]
                dx = term if dx is None else dx + term
                dw_ref[pl.ds(j, 1), :] += jnp.sum(du * _up(xe, 8 - (LRU_W - 1) + j)[:CH], axis=0, keepdims=True)
            dx_ref[cur, :] = dx
            return jnp.sum(jnp.where(rows == 0, a * D, 0.0), axis=0, keepdims=True), du[:8]

        lax.fori_loop(0, nreal, chunk, (jnp.zeros((1, 128), F32), jnp.zeros((8, 128), F32)))

    big = jax.ShapeDtypeStruct((T, 512), F32)
    vec = jax.ShapeDtypeStruct((1, 512), F32)
    mat = jax.ShapeDtypeStruct((4, 128, 128), F32)
    matspec = pl.BlockSpec((1, 128, 128), lambda j: (j, 0, 0))
    return pl.pallas_call(
        body, name=name, grid=(4,),
        out_shape=(big, big, jax.ShapeDtypeStruct((8, 512), F32), vec, mat, mat, vec, vec, vec),
        in_specs=_lru_specs(T) + [_colblock(T, 0), _colblock(T, 4)],
        out_specs=(_colblock(T, 0), _colblock(T, 0), _vecblock(8), _vecblock(), matspec, matspec, _vecblock(),
                   _vecblock(), _vecblock()),
        compiler_params=_cparams(("parallel",)),
    )(p, p, cw, cb, wa, wx, ba, bx, lam, hs, dy)


def _ffn_forward(h, gamma, wg, wu, wd, tag):
    xn = _rms_fwd(h, gamma, f"rms_fwd_{tag}")
    g, u, a = _ffn_up(xn, wg, wu, f"ffn_up_{tag}")
    out = _mm([(a, wd)], "NN", f"ffn_down_{tag}", res=h, res_scale=0.5)
    return out, (h, xn, g, u, a)


def _ffn_backward(saved, gamma, wg, wu, wd, dout, tag):
    h, xn, g, u, a = saved
    dwd = _mm([(a, dout)], "TN", f"ffn_dwd_{tag}", res_scale=0.5)
    dg, du = _ffn_dact(dout, wd, g, u, 0.5, f"ffn_dact_{tag}")
    dwg = _mm([(xn, dg)], "TN", f"ffn_dwg_{tag}")
    dwu = _mm([(xn, du)], "TN", f"ffn_dwu_{tag}")
    dxn = _mm([(dg, wg), (du, wu)], "NT", f"ffn_dxn_{tag}")
    dh, dgamma = _rms_bwd(h, gamma, dxn, dout, f"rms_bwd_{tag}")
    return dh, dgamma, dwg, dwu, dwd


def _blockdiag(w, per):
    n, k, _ = w.shape
    out = jnp.zeros((n // per, per * k, per * k), w.dtype)
    for i in range(per):
        out = out.at[:, i * k:(i + 1) * k, i * k:(i + 1) * k].set(w[i::per])
    return out


def _blockdiag_grad(g, per, k):
    parts = [g[:, i * k:(i + 1) * k, i * k:(i + 1) * k] for i in range(per)]
    return jnp.stack(parts, axis=1).reshape(-1, k, k)


def _local_step(x, tgt, W):
    seq, D = x.shape
    lr = N_META + seq
    nreal = -(-lr // CH)
    T = CH * (nreal + 2)
    if T > 640 and T % 640:
        T += 640 - T % 640
    lo, real_end = CH + N_META, CH + lr
    zf = lambda n: jnp.zeros((n, D), F32)
    h0 = jnp.concatenate([zf(CH), W['meta_tokens'], x, zf(T - real_end)], axis=0)
    tgt_p = jnp.concatenate([zf(lo), tgt, zf(T - real_end)], axis=0)
    row = lambda v: v.reshape(1, -1)
    G = {}

    h = h0
    saved = []
    for l in range(2):
        h, s1 = _ffn_forward(h, row(W['ffn1_norm'][l]), W['ffn1_wg'][l], W['ffn1_wu'][l], W['ffn1_wd'][l], f"a{l}")
        hm = h
        xn = _rms_fwd(hm, row(W['mix_norm'][l]), f"rms_fwd_mix{l}")
        if l == 0:
            p = _mm([(xn, W['w_in_even'][0])], "NN", "in_even")
            wbd = _blockdiag(W['pool_w'][0], 4)[0]
            l0, l1 = row(W['hgrn_lb_logits'][0]), row(W['hgrn_lb_logits'][1])
            ya = _pool_fwd(p, wbd, W['pool_scale'], nreal, real_end, "pool_fwd")
            yb, states = _hgrn_fwd(p, l0, l1, W['hgrn_gnorm'], nreal, real_end, "hgrn_fwd")
            wo = W['w_out_even'][0]
            h = _mm([(ya, wo[:256]), (yb, wo[256:])], "NN", "out_even", res=hm)
            sm = (hm, xn, p, wbd, l0, l1, ya, yb, states)
        else:
            p = _mm([(xn, W['w_in_odd'][0])], "NN", "in_odd")
            cw = jnp.pad(W['conv_w'][0], ((0, 1), (0, 0)))
            lw = jnp.pad(W['lru_conv_w'][0], ((0, 4), (0, 0)))
            wa, wx = _blockdiag(W['lru_wa'][0], 2), _blockdiag(W['lru_wx'][0], 2)
            yc = _conf_fwd(p, cw, W['conv_b'], W['conv_ln_g'], W['conv_ln_b'], nreal, real_end, "conf_fwd")
            yd, hs = _lru_fwd(p, lw, W['lru_conv_b'], wa, wx, W['lru_ba'], W['lru_bx'], W['lru_lambda'], nreal,
                              real_end, "lru_fwd")
            wo = W['w_out_odd'][0]
            h = _mm([(yc, wo[:512]), (yd, wo[512:])], "NN", "out_odd", res=hm)
            sm = (hm, xn, p, cw, lw, wa, wx, yc, yd, hs)
        h, s2 = _ffn_forward(h, row(W['ffn2_norm'][l]), W['ffn2_wg'][l], W['ffn2_wu'][l], W['ffn2_wd'][l], f"b{l}")
        saved.append((s1, sm, s2))

    loss8, dh, dfin = _loss_head(h, row(W['final_norm']), tgt_p, lo, real_end, "loss_head")
    G['final_norm'] = dfin[0]

    per_layer = {k: [None, None] for k in ('ffn1_norm', 'ffn1_wg', 'ffn1_wu', 'ffn1_wd', 'mix_norm', 'ffn2_norm',
                                           'ffn2_wg', 'ffn2_wu', 'ffn2_wd')}
    for l in (1, 0):
        s1, sm, s2 = saved[l]
        dh, dn, dwg, dwu, dwd = _ffn_backward(s2, row(W['ffn2_norm'][l]), W['ffn2_wg'][l], W['ffn2_wu'][l],
                                              W['ffn2_wd'][l], dh, f"b{l}")
        per_layer['ffn2_norm'][l], per_layer['ffn2_wg'][l] = dn[0], dwg
        per_layer['ffn2_wu'][l], per_layer['ffn2_wd'][l] = dwu, dwd
        if l == 0:
            hm, xn, p, wbd, l0, l1, ya, yb, states = sm
            wo, wi = W['w_out_even'][0], W['w_in_even'][0]
            G['w_out_even'] = jnp.concatenate([_mm([(ya, dh)], "TN", "dwo_even_a"),
                                               _mm([(yb, dh)], "TN", "dwo_even_b")], axis=0)[None]
            dy = _mm([(dh, wo)], "NT", "dy_even")
            dpp, dwbd, dsc = _pool_bwd(p, wbd, W['pool_scale'], dy, nreal, real_end, "pool_bwd")
            dq, df, di, dg, dl0, dl1, dgn = _hgrn_bwd(p, l0, l1, W['hgrn_gnorm'], states, dy, nreal, real_end,
                                                      "hgrn_bwd")
            G['pool_w'] = _blockdiag_grad(dwbd[None], 4, 64)[None]
            G['pool_scale'] = dsc
            G['hgrn_lb_logits'] = jnp.concatenate([dl0, dl1], axis=0)
            G['hgrn_gnorm'] = dgn
            parts = [dpp, dq, df, di, dg]
            offs = [0, 256, 1024, 1792, 2560, 3328]
            G['w_in_even'] = jnp.concatenate(
                [_mm([(xn, dpart)], "TN", f"dwi_even_{k}") for k, dpart in enumerate(parts)], axis=1)[None]
            dxn = _mm([(dpart, wi[:, offs[k]:offs[k + 1]]) for k, dpart in enumerate(parts)], "NT", "dxn_even")
        else:
            hm, xn, p, cw, lw, wa, wx, yc, yd, hs = sm
            wo, wi = W['w_out_odd'][0], W['w_in_odd'][0]
            G['w_out_odd'] = jnp.concatenate([_mm([(yc, dh)], "TN", "dwo_odd_c"),
                                              _mm([(yd, dh)], "TN", "dwo_odd_d")], axis=0)[None]
            dy = _mm([(dh, wo)], "NT", "dy_odd")
            da, db, dcw, dcb, dlg, dlb = _conf_bwd(p, cw, W['conv_b'], W['conv_ln_g'], W['conv_ln_b'], dy, nreal,
                                                   real_end, "conf_bwd")
            dx, dgate, dlw, dlcb, dwa, dwx, dba, dbx, dlam = _lru_bwd(
                p, lw, W['lru_conv_b'], wa, wx, W['lru_ba'], W['lru_bx'], W['lru_lambda'], hs, dy, nreal, real_end,
                "lru_bwd")
            G['conv_w'], G['conv_b'], G['conv_ln_g'], G['conv_ln_b'] = dcw[None, :CONV_W], dcb, dlg, dlb
            G['lru_conv_w'], G['lru_conv_b'] = dlw[None, :LRU_W], dlcb
            G['lru_wa'] = _blockdiag_grad(dwa, 2, 64)[None]
            G['lru_wx'] = _blockdiag_grad(dwx, 2, 64)[None]
            G['lru_ba'], G['lru_bx'], G['lru_lambda'] = dba, dbx, dlam
            parts = [da, db, dx, dgate]
            G['w_in_odd'] = jnp.concatenate(
                [_mm([(xn, dpart)], "TN", f"dwi_odd_{k}") for k, dpart in enumerate(parts)], axis=1)[None]
            dxn = _mm([(dpart, wi[:, 512 * k:512 * (k + 1)]) for k, dpart in enumerate(parts)], "NT", "dxn_odd")
        dh, dn = _rms_bwd(hm, row(W['mix_norm'][l]), dxn, dh, f"rms_bwd_mix{l}")
        per_layer['mix_norm'][l] = dn[0]
        dh, dn, dwg, dwu, dwd = _ffn_backward(s1, row(W['ffn1_norm'][l]), W['ffn1_wg'][l], W['ffn1_wu'][l],
                                              W['ffn1_wd'][l], dh, f"a{l}")
        per_layer['ffn1_norm'][l], per_layer['ffn1_wg'][l] = dn[0], dwg
        per_layer['ffn1_wu'][l], per_layer['ffn1_wd'][l] = dwu, dwd
    for k, v in per_layer.items():
        G[k] = jnp.stack(v, axis=0)
    G['meta_tokens'] = dh[CH:lo]
    return loss8[0, 0], dh[lo:real_end], G


def _pack(arrs):
    flat = jnp.concatenate([a.reshape(-1).astype(F32) for a in arrs])
    n = flat.shape[0]
    padded = -(-n // 1024) * 1024
    return jnp.pad(flat, (0, padded - n)).reshape(-1, 128)


def _unpack(packed, shapes):
    flat = packed.reshape(-1)
    out, off = [], 0
    for s in shapes:
        n = math.prod(s)
        out.append(flat[off:off + n].reshape(s))
        off += n
    return out


def _to_full(gathered, axis):
    s = gathered.shape[1:]
    return jnp.moveaxis(gathered, 0, axis).reshape(s[:axis] + (NDEV * s[axis],) + s[axis + 1:])


def _to_slots(full, axis):
    s = full.shape
    return jnp.moveaxis(full.reshape(s[:axis] + (NDEV, s[axis] // NDEV) + s[axis + 1:]), axis, 0)


def kernel(x, meta_tokens, ffn1_norm, ffn1_wg, ffn1_wu, ffn1_wd, mix_norm, ffn2_norm, ffn2_wg, ffn2_wu, ffn2_wd, w_in_even, pool_w, pool_scale, hgrn_lb_logits, hgrn_gnorm, w_out_even, w_in_odd, conv_w, conv_b, conv_ln_g, conv_ln_b, lru_conv_w, lru_conv_b, lru_wa, lru_ba, lru_wx, lru_bx, lru_lambda, w_out_odd, final_norm, loss_target, m_meta_tokens, m_ffn1_norm, m_ffn1_wg, m_ffn1_wu, m_ffn1_wd, m_mix_norm, m_ffn2_norm, m_ffn2_wg, m_ffn2_wu, m_ffn2_wd, m_w_in_even, m_pool_w, m_pool_scale, m_hgrn_lb_logits, m_hgrn_gnorm, m_w_out_even, m_w_in_odd, m_conv_w, m_conv_b, m_conv_ln_g, m_conv_ln_b, m_lru_conv_w, m_lru_conv_b, m_lru_wa, m_lru_ba, m_lru_wx, m_lru_bx, m_lru_lambda, m_w_out_odd, m_final_norm, v_meta_tokens, v_ffn1_norm, v_ffn1_wg, v_ffn1_wu, v_ffn1_wd, v_mix_norm, v_ffn2_norm, v_ffn2_wg, v_ffn2_wu, v_ffn2_wd, v_w_in_even, v_pool_w, v_pool_scale, v_hgrn_lb_logits, v_hgrn_gnorm, v_w_out_even, v_w_in_odd, v_conv_w, v_conv_b, v_conv_ln_g, v_conv_ln_b, v_lru_conv_w, v_lru_conv_b, v_lru_wa, v_lru_ba, v_lru_wx, v_lru_bx, v_lru_lambda, v_w_out_odd, v_final_norm):
    args = (meta_tokens, ffn1_norm, ffn1_wg, ffn1_wu, ffn1_wd, mix_norm, ffn2_norm, ffn2_wg, ffn2_wu, ffn2_wd, w_in_even, pool_w, pool_scale, hgrn_lb_logits, hgrn_gnorm, w_out_even, w_in_odd, conv_w, conv_b, conv_ln_g, conv_ln_b, lru_conv_w, lru_conv_b, lru_wa, lru_ba, lru_wx, lru_bx, lru_lambda, w_out_odd, final_norm)
    margs = (m_meta_tokens, m_ffn1_norm, m_ffn1_wg, m_ffn1_wu, m_ffn1_wd, m_mix_norm, m_ffn2_norm, m_ffn2_wg, m_ffn2_wu, m_ffn2_wd, m_w_in_even, m_pool_w, m_pool_scale, m_hgrn_lb_logits, m_hgrn_gnorm, m_w_out_even, m_w_in_odd, m_conv_w, m_conv_b, m_conv_ln_g, m_conv_ln_b, m_lru_conv_w, m_lru_conv_b, m_lru_wa, m_lru_ba, m_lru_wx, m_lru_bx, m_lru_lambda, m_w_out_odd, m_final_norm)
    vargs = (v_meta_tokens, v_ffn1_norm, v_ffn1_wg, v_ffn1_wu, v_ffn1_wd, v_mix_norm, v_ffn2_norm, v_ffn2_wg, v_ffn2_wu, v_ffn2_wd, v_w_in_even, v_pool_w, v_pool_scale, v_hgrn_lb_logits, v_hgrn_gnorm, v_w_out_even, v_w_in_odd, v_conv_w, v_conv_b, v_conv_ln_g, v_conv_ln_b, v_lru_conv_w, v_lru_conv_b, v_lru_wa, v_lru_ba, v_lru_wx, v_lru_bx, v_lru_lambda, v_w_out_odd, v_final_norm)
    Wl = dict(zip(W_NAMES, args))
    Ml = dict(zip(W_NAMES, margs))
    Vl = dict(zip(W_NAMES, vargs))

    small_shapes = [Wl[n].shape for n in SMALL_SHARDED]
    send = [Wl[n].astype(MXU) for n in BIG] + [_pack([Wl[n] for n in SMALL_SHARDED])]
    got = _exchange(send, [True] * len(send), "gather_weights")
    W = {n: Wl[n] for n in REPLICATED}
    for n, g in zip(BIG, got):
        W[n] = _to_full(g, SHARD_AXIS[n])
    per_dev = [_unpack(got[-1][d], small_shapes) for d in range(NDEV)]
    for k, n in enumerate(SMALL_SHARDED):
        W[n] = _to_full(jnp.stack([per_dev[d][k] for d in range(NDEV)]), SHARD_AXIS[n])

    loss_part, grad_x, G = _local_step(x[0], loss_target[0], W)
    loss = lax.psum(loss_part, MESH_AXES)

    send = [_to_slots(G[n].astype(F32), SHARD_AXIS[n]) for n in BIG]
    small_slots = [_to_slots(G[n].astype(F32), SHARD_AXIS[n]) for n in SMALL_SHARDED]
    send.append(jnp.stack([_pack([s[d] for s in small_slots]) for d in range(NDEV)]))
    send.append(_pack([G[n] for n in REPLICATED]))
    recv = _exchange(send, [False] * (len(send) - 1) + [True], "scatter_grads")

    outs = {}
    for n, r in zip(BIG, recv):
        shp = Wl[n].shape
        C = shp[-1]
        res = _adamw(r.reshape(NDEV, -1, C), Wl[n].reshape(-1, C), Ml[n].reshape(-1, C), Vl[n].reshape(-1, C),
                     f"adamw_{n}")
        outs[n] = [o.reshape(shp) for o in res]
    for names, r, tag in ((SMALL_SHARDED, recv[-2], "small"), (REPLICATED, recv[-1], "repl")):
        shapes = [Wl[n].shape for n in names]
        res = _adamw(r, _pack([Wl[n] for n in names]), _pack([Ml[n] for n in names]), _pack([Vl[n] for n in names]),
                     f"adamw_{tag}")
        unp = [_unpack(o, shapes) for o in res]
        for k, n in enumerate(names):
            outs[n] = [unp[j][k] for j in range(4)]

    result = [loss, grad_x[None]]
    for j in range(4):
        result += [outs[n][j] for n in W_NAMES]
    return tuple(result)
```

```python
import functools
import math

import jax
import jax.numpy as jnp
from jax import lax
from jax.experimental import pallas as pl
from jax.experimental.pallas import tpu as pltpu

F32 = jnp.float32
MXU = jnp.bfloat16
EPS = 1e-6
CH = 128
HG = 64
N_META = 16
CONV_W = 31
LRU_W = 4
LRU_C = 8.0
VMEM_LIMIT = 48 * 2 ** 20
ADAM_LR, ADAM_B1, ADAM_B2, ADAM_EPS, ADAM_WD, ADAM_STEP = 0.001, 0.9, 0.999, 1e-08, 0.01, 10
MESH_AXES = ("x", "y", "c")
NDEV = 8

W_NAMES = ['meta_tokens', 'ffn1_norm', 'ffn1_wg', 'ffn1_wu', 'ffn1_wd', 'mix_norm', 'ffn2_norm', 'ffn2_wg', 'ffn2_wu',
           'ffn2_wd', 'w_in_even', 'pool_w', 'pool_scale', 'hgrn_lb_logits', 'hgrn_gnorm', 'w_out_even', 'w_in_odd',
           'conv_w', 'conv_b', 'conv_ln_g', 'conv_ln_b', 'lru_conv_w', 'lru_conv_b', 'lru_wa', 'lru_ba', 'lru_wx',
           'lru_bx', 'lru_lambda', 'w_out_odd', 'final_norm']
SHARD_AXIS = {'meta_tokens': 1, 'ffn1_wg': 2, 'ffn1_wu': 2, 'ffn1_wd': 1, 'ffn2_wg': 2, 'ffn2_wu': 2, 'ffn2_wd': 1,
              'w_in_even': 2, 'w_out_even': 1, 'w_in_odd': 2, 'conv_w': 2, 'conv_b': 1, 'conv_ln_g': 1,
              'conv_ln_b': 1, 'lru_conv_w': 2, 'lru_conv_b': 1, 'lru_ba': 1, 'lru_bx': 1, 'lru_lambda': 1,
              'w_out_odd': 1}
BIG = ['ffn1_wg', 'ffn1_wu', 'ffn1_wd', 'ffn2_wg', 'ffn2_wu', 'ffn2_wd', 'w_in_even', 'w_out_even', 'w_in_odd',
       'w_out_odd']
SMALL_SHARDED = [n for n in W_NAMES if n in SHARD_AXIS and n not in BIG]
REPLICATED = [n for n in W_NAMES if n not in SHARD_AXIS]


def _cparams(sem=None, vmem=VMEM_LIMIT):
    return pltpu.CompilerParams(dimension_semantics=sem, vmem_limit_bytes=vmem)


def _tile(n):
    for c in (640, 512, 256, 128):
        if n % c == 0:
            return c
    return n


def _rowtile(n):
    for c in (256, 352, 128, 64, 32, 16, 8):
        if n % c == 0:
            return c
    return n


def _exchange(arrays, bcast, name):
    n = len(arrays)

    def body(*refs):
        ins, outs = refs[:n], refs[n:2 * n]
        ssem, rsem, lsem = refs[2 * n], refs[2 * n + 1], refs[2 * n + 2]
        x, y, c = lax.axis_index("x"), lax.axis_index("y"), lax.axis_index("c")
        me = 4 * x + 2 * y + c
        sends, recvs, locs = [], [], []
        for a in range(n):
            loc = pltpu.make_async_copy(ins[a] if bcast[a] else ins[a].at[me], outs[a].at[me], lsem.at[a])
            loc.start()
            locs.append(loc)
            for m in range(1, NDEV):
                px = 1 - x if (m >> 2) & 1 else x
                py = 1 - y if (m >> 1) & 1 else y
                pc = 1 - c if m & 1 else c
                peer = 4 * px + 2 * py + pc
                src = ins[a] if bcast[a] else ins[a].at[peer]
                k = a * NDEV + m
                snd = pltpu.make_async_remote_copy(src_ref=src, dst_ref=outs[a].at[me], send_sem=ssem.at[k],
                                                   recv_sem=rsem.at[k], device_id=(px, py, pc),
                                                   device_id_type=pl.DeviceIdType.MESH)
                snd.start()
                sends.append(snd)
                recvs.append(pltpu.make_async_remote_copy(src_ref=src, dst_ref=outs[a].at[peer], send_sem=ssem.at[k],
                                                          recv_sem=rsem.at[k], device_id=(px, py, pc),
                                                          device_id_type=pl.DeviceIdType.MESH))
        for r in recvs:
            r.wait_recv()
        for s in sends:
            s.wait_send()
        for loc in locs:
            loc.wait()

    out_shape = []
    for a, arr in enumerate(arrays):
        shp = arr.shape if bcast[a] else arr.shape[1:]
        out_shape.append(jax.ShapeDtypeStruct((NDEV,) + tuple(shp), arr.dtype))
    any_spec = pl.BlockSpec(memory_space=pl.ANY)
    return pl.pallas_call(
        body, name=name, out_shape=tuple(out_shape), in_specs=[any_spec] * n, out_specs=tuple([any_spec] * n),
        scratch_shapes=[pltpu.SemaphoreType.DMA((n * NDEV,)), pltpu.SemaphoreType.DMA((n * NDEV,)),
                        pltpu.SemaphoreType.DMA((n,))],
    )(*arrays)


def _adamw(recv, w, m, v, name):
    R, C = w.shape
    br = _rowtile(R)

    def body(r_ref, w_ref, m_ref, v_ref, g_o, d_o, m_o, v_o):
        g = r_ref[0].astype(F32)
        for k in range(1, NDEV):
            g = g + r_ref[k].astype(F32)
        mn = ADAM_B1 * m_ref[...] + (1.0 - ADAM_B1) * g
        vn = ADAM_B2 * v_ref[...] + (1.0 - ADAM_B2) * (g * g)
        m_hat = mn / (1.0 - ADAM_B1 ** ADAM_STEP)
        v_hat = vn / (1.0 - ADAM_B2 ** ADAM_STEP)
        g_o[...] = g
        d_o[...] = -ADAM_LR * (m_hat / (jnp.sqrt(v_hat) + ADAM_EPS) + ADAM_WD * w_ref[...])
        m_o[...] = mn
        v_o[...] = vn

    blk = pl.BlockSpec((br, C), lambda i: (i, 0))
    sds = jax.ShapeDtypeStruct((R, C), F32)
    return pl.pallas_call(
        body, name=name, grid=(R // br,), out_shape=(sds, sds, sds, sds),
        in_specs=[pl.BlockSpec((NDEV, br, C), lambda i: (0, i, 0)), blk, blk, blk], out_specs=(blk, blk, blk, blk),
        compiler_params=_cparams(("parallel",)),
    )(recv, w, m, v)


_DIMS = {"NN": ((1,), (0,)), "NT": ((1,), (1,)), "TN": ((0,), (0,))}


def _dot(a, b, mode="NN"):
    return lax.dot_general(a.astype(MXU), b.astype(MXU), (_DIMS[mode], ((), ())), preferred_element_type=F32)


def _dotf(a, b, mode="NN"):
    return lax.dot_general(a, b, (_DIMS[mode], ((), ())), precision=lax.Precision.HIGHEST,
                           preferred_element_type=F32)


def _mm(pairs, mode, name, res=None, res_scale=1.0, out_dtype=F32):
    a0, b0 = pairs[0]
    M = a0.shape[1] if mode == "TN" else a0.shape[0]
    N = b0.shape[0] if mode == "NT" else b0.shape[1]
    tm, tn = _tile(M), _tile(N)
    npairs = len(pairs)

    def body(*refs):
        acc = None
        for p in range(npairs):
            d = _dot(refs[2 * p][...], refs[2 * p + 1][...], mode)
            acc = d if acc is None else acc + d
        if res_scale != 1.0:
            acc = res_scale * acc
        if res is not None:
            acc = refs[2 * npairs][...] + acc
        refs[-1][...] = acc.astype(out_dtype)

    in_specs, args = [], []
    for a, b in pairs:
        if mode == "TN":
            in_specs.append(pl.BlockSpec((a.shape[0], tm), lambda i, j: (0, i)))
        else:
            in_specs.append(pl.BlockSpec((tm, a.shape[1]), lambda i, j: (i, 0)))
        if mode == "NT":
            in_specs.append(pl.BlockSpec((tn, b.shape[1]), lambda i, j: (j, 0)))
        else:
            in_specs.append(pl.BlockSpec((b.shape[0], tn), lambda i, j: (0, j)))
        args += [a, b]
    if res is not None:
        in_specs.append(pl.BlockSpec((tm, tn), lambda i, j: (i, j)))
        args.append(res)
    return pl.pallas_call(
        body, name=name, grid=(M // tm, N // tn), out_shape=jax.ShapeDtypeStruct((M, N), out_dtype),
        in_specs=in_specs, out_specs=pl.BlockSpec((tm, tn), lambda i, j: (i, j)),
        compiler_params=_cparams(("parallel", "parallel")),
    )(*args)


def _rms_fwd(h, gamma, name):
    T, D = h.shape
    tm = _tile(T)

    def body(h_ref, g_ref, o_ref):
        x = h_ref[...]
        r = lax.rsqrt(jnp.mean(x * x, axis=-1, keepdims=True) + EPS)
        o_ref[...] = (x * r * g_ref[...]).astype(MXU)

    return pl.pallas_call(
        body, name=name, grid=(T // tm,), out_shape=jax.ShapeDtypeStruct((T, D), MXU),
        in_specs=[pl.BlockSpec((tm, D), lambda i: (i, 0)), pl.BlockSpec((1, D), lambda i: (0, 0))],
        out_specs=pl.BlockSpec((tm, D), lambda i: (i, 0)), compiler_params=_cparams(("parallel",)),
    )(h, gamma)


def _rms_bwd_math(x, gamma, dy):
    r = lax.rsqrt(jnp.mean(x * x, axis=-1, keepdims=True) + EPS)
    z = dy * gamma
    dx = r * z - x * (r * r * r) * jnp.mean(z * x, axis=-1, keepdims=True)
    dgamma = jnp.sum(dy * x * r, axis=0, keepdims=True)
    return dx, dgamma


def _rms_bwd(h, gamma, dxn, dres, name):
    T, D = h.shape
    tm = _tile(T)

    def body(h_ref, g_ref, dxn_ref, dres_ref, dh_ref, dg_ref):
        dx, dgamma = _rms_bwd_math(h_ref[...], g_ref[...], dxn_ref[...])
        dh_ref[...] = dres_ref[...] + dx

        @pl.when(pl.program_id(0) == 0)
        def _():
            dg_ref[...] = jnp.zeros_like(dg_ref)

        dg_ref[...] += dgamma

    row = pl.BlockSpec((tm, D), lambda i: (i, 0))
    vec = pl.BlockSpec((1, D), lambda i: (0, 0))
    return pl.pallas_call(
        body, name=name, grid=(T // tm,),
        out_shape=(jax.ShapeDtypeStruct((T, D), F32), jax.ShapeDtypeStruct((1, D), F32)),
        in_specs=[row, vec, row, row], out_specs=(row, vec), compiler_params=_cparams(("arbitrary",)),
    )(h, gamma, dxn, dres)


def _loss_head(h, gamma, tgt, lo, hi, name):
    T, D = h.shape
    tm = _tile(T)

    def body(h_ref, g_ref, t_ref, loss_ref, dh_ref, dg_ref):
        i = pl.program_id(0)
        x = h_ref[...]
        r = lax.rsqrt(jnp.mean(x * x, axis=-1, keepdims=True) + EPS)
        y = x * r * g_ref[...]
        rows = i * tm + lax.broadcasted_iota(jnp.int32, (tm, 1), 0)
        valid = jnp.logical_and(rows >= lo, rows < hi)
        diff = jnp.where(valid, y - t_ref[...], 0.0)
        part = 0.5 * jnp.sum(jnp.sum(diff * diff, axis=-1, keepdims=True) / D, axis=0, keepdims=True)
        dx, dgamma = _rms_bwd_math(x, g_ref[...], diff / D)
        dh_ref[...] = dx

        @pl.when(i == 0)
        def _():
            dg_ref[...] = jnp.zeros_like(dg_ref)
            loss_ref[...] = jnp.zeros_like(loss_ref)

        dg_ref[...] += dgamma
        loss_ref[...] += jnp.broadcast_to(part, loss_ref.shape)

    row = pl.BlockSpec((tm, D), lambda i: (i, 0))
    vec = pl.BlockSpec((1, D), lambda i: (0, 0))
    lsp = pl.BlockSpec((8, 128), lambda i: (0, 0))
    return pl.pallas_call(
        body, name=name, grid=(T // tm,),
        out_shape=(jax.ShapeDtypeStruct((8, 128), F32), jax.ShapeDtypeStruct((T, D), F32),
                   jax.ShapeDtypeStruct((1, D), F32)),
        in_specs=[row, vec, row], out_specs=(lsp, row, vec), compiler_params=_cparams(("arbitrary",)),
    )(h, gamma, tgt)


def _ffn_up(xn, wg, wu, name):
    T, D = xn.shape
    Fd = wg.shape[1]
    tm, tn = _tile(T), _tile(Fd)

    def body(x_ref, wg_ref, wu_ref, g_ref, u_ref, a_ref):
        x = x_ref[...]
        g = _dot(x, wg_ref[...])
        u = _dot(x, wu_ref[...])
        g_ref[...] = g.astype(MXU)
        u_ref[...] = u.astype(MXU)
        a_ref[...] = (g * jax.nn.sigmoid(g) * u).astype(MXU)

    wsp = pl.BlockSpec((D, tn), lambda i, j: (0, j))
    osp = pl.BlockSpec((tm, tn), lambda i, j: (i, j))
    sds = jax.ShapeDtypeStruct((T, Fd), MXU)
    return pl.pallas_call(
        body, name=name, grid=(T // tm, Fd // tn), out_shape=(sds, sds, sds),
        in_specs=[pl.BlockSpec((tm, D), lambda i, j: (i, 0)), wsp, wsp], out_specs=(osp, osp, osp),
        compiler_params=_cparams(("parallel", "parallel")),
    )(xn, wg, wu)


def _ffn_dact(dy, wd, g, u, scale, name):
    T, D = dy.shape
    Fd = wd.shape[0]
    tm, tn = _tile(T), _tile(Fd)

    def body(dy_ref, wd_ref, g_ref, u_ref, dg_ref, du_ref):
        da = scale * _dot(dy_ref[...], wd_ref[...], "NT")
        gg = g_ref[...].astype(F32)
        uu = u_ref[...].astype(F32)
        sg = jax.nn.sigmoid(gg)
        dg_ref[...] = (da * uu * (sg * (1.0 + gg * (1.0 - sg)))).astype(MXU)
        du_ref[...] = (da * gg * sg).astype(MXU)

    osp = pl.BlockSpec((tm, tn), lambda i, j: (i, j))
    sds = jax.ShapeDtypeStruct((T, Fd), MXU)
    return pl.pallas_call(
        body, name=name, grid=(T // tm, Fd // tn), out_shape=(sds, sds),
        in_specs=[pl.BlockSpec((tm, D), lambda i, j: (i, 0)), pl.BlockSpec((tn, D), lambda i, j: (j, 0)), osp, osp],
        out_specs=(osp, osp), compiler_params=_cparams(("parallel", "parallel")),
    )(dy, wd, g, u)


def _down(v, s):
    return v if s == 0 else pltpu.roll(v, s, 0)


def _up(v, s):
    return v if s == 0 else pltpu.roll(v, v.shape[0] - s, 0)


def _rows(n):
    return lax.broadcasted_iota(jnp.int32, (n, 1), 0)


def _zero_pad_rows(ref, lo_end, hi_start, T):
    ref[pl.ds(0, lo_end), :] = jnp.zeros((lo_end, ref.shape[1]), ref.dtype)
    if T > hi_start:
        ref[pl.ds(hi_start, T - hi_start), :] = jnp.zeros((T - hi_start, ref.shape[1]), ref.dtype)


def _colblock(T, off):
    return pl.BlockSpec((T, 128), lambda j: (0, off + j))


def _vecblock(rows=1):
    return pl.BlockSpec((rows, 128), lambda j: (0, j))


def _pool_lane_consts(n):
    lane = lax.broadcasted_iota(jnp.int32, (n, 256), 1)
    win = jnp.where(lane < 64, 2.0, jnp.where(lane < 128, 4.0, jnp.where(lane < 192, 8.0, 16.0)))
    return lane, win


def _pool_select(lane, s2, s4, s8, s16):
    return jnp.where(lane < 64, s2, jnp.where(lane < 128, s4, jnp.where(lane < 192, s8, s16)))


def _pool_mixed(xh, start):
    s2 = xh + _down(xh, 1)
    s4 = s2 + _down(s2, 2)
    s8 = s4 + _down(s4, 4)
    s16 = s8 + _down(s8, 8)
    n = xh.shape[0] - 16
    lane, _ = _pool_lane_consts(n + 16)
    _, win = _pool_lane_consts(n)
    t1 = (start - CH + 1 + _rows(n)).astype(F32)
    cnt = jnp.minimum(jnp.maximum(t1, 1.0), win)
    return _pool_select(lane, s2, s4, s8, s16)[16:] / cnt - xh[16:]


def _pool_fwd(p, wbd, scale, nreal, real_end, name):
    T = p.shape[0]

    def body(p_ref, w_ref, s_ref, y_ref):
        _zero_pad_rows(y_ref, CH, CH * (1 + nreal), T)

        def chunk(c, carry):
            start = pl.multiple_of(c * CH, CH)
            mixed = _pool_mixed(p_ref[pl.ds(start - 16, CH + 16), :], start)
            y = _dot(mixed, w_ref[...]) * s_ref[...]
            y_ref[pl.ds(start, CH), :] = jnp.where(start + _rows(CH) < real_end, y, 0.0)
            return carry

        lax.fori_loop(1, 1 + nreal, chunk, 0)

    return pl.pallas_call(
        body, name=name, grid=(1,), out_shape=jax.ShapeDtypeStruct((T, 256), F32),
        in_specs=[pl.BlockSpec((T, 256), lambda j: (0, 0)), pl.BlockSpec((256, 256), lambda j: (0, 0)),
                  pl.BlockSpec((1, 256), lambda j: (0, 0))],
        out_specs=pl.BlockSpec((T, 256), lambda j: (0, 0)), compiler_params=_cparams(("arbitrary",)),
    )(p, wbd, scale)


def _pool_bwd(p, wbd, scale, dy, nreal, real_end, name):
    T = p.shape[0]

    def body(p_ref, w_ref, s_ref, dy_ref, dp_ref, dw_ref, ds_ref):
        _zero_pad_rows(dp_ref, CH, CH * (1 + nreal), T)
        dw_ref[...] = jnp.zeros_like(dw_ref)
        ds_ref[...] = jnp.zeros_like(ds_ref)

        def chunk(c, carry):
            start = pl.multiple_of(c * CH, CH)
            mixed = _pool_mixed(p_ref[pl.ds(start - 16, CH + 16), :], start)
            ypre = _dot(mixed, w_ref[...])
            n = CH + 16
            dye = jnp.where(start + _rows(n) < real_end, dy_ref[pl.ds(start, n), :], 0.0)
            dys = dye * s_ref[...]
            ds_ref[...] += jnp.sum(dye[:CH] * ypre, axis=0, keepdims=True)
            dw_ref[...] += _dot(mixed, dys[:CH], "TN")
            dmix = _dot(dys, w_ref[...], "NT")
            lane, win = _pool_lane_consts(n)
            t1 = (start - CH + 1 + _rows(n)).astype(F32)
            z = dmix / jnp.minimum(jnp.maximum(t1, 1.0), win)
            r2 = z + _up(z, 1)
            r4 = r2 + _up(r2, 2)
            r8 = r4 + _up(r4, 4)
            r16 = r8 + _up(r8, 8)
            dp_ref[pl.ds(start, CH), :] = (_pool_select(lane, r2, r4, r8, r16) - dmix)[:CH]
            return carry

        lax.fori_loop(1, 1 + nreal, chunk, 0)

    full = lambda r, c: pl.BlockSpec((r, c), lambda j: (0, 0))
    return pl.pallas_call(
        body, name=name, grid=(1,),
        out_shape=(jax.ShapeDtypeStruct((T, 256), F32), jax.ShapeDtypeStruct((256, 256), F32),
                   jax.ShapeDtypeStruct((1, 256), F32)),
        in_specs=[full(T, 256), full(256, 256), full(1, 256), full(T, 256)],
        out_specs=(full(T, 256), full(256, 256), full(1, 256)), compiler_params=_cparams(("arbitrary",)),
    )(p, wbd, scale, dy)


def _hgrn_chunk(St, qr, fr, ir, gr, l0, l1, gn):
    rows = lax.broadcasted_iota(jnp.int32, (HG, HG), 0)
    cols = lax.broadcasted_iota(jnp.int32, (HG, HG), 1)
    causal = rows >= cols
    ltri = causal.astype(F32)
    lb = jax.nn.sigmoid(l0 - l1)
    sg = jax.nn.sigmoid(fr)
    logf = jnp.log(lb + (1.0 - lb) * sg)
    kk = (1.0 - lb) * (1.0 - sg)
    q = qr * jax.nn.sigmoid(qr)
    b = jnp.dot(ltri, logf, precision=lax.Precision.HIGHEST, preferred_element_type=F32)
    bl = jnp.sum(logf, axis=0, keepdims=True)
    bm = jnp.sum(jnp.where(_rows(HG) <= HG // 2, logf, 0.0), axis=0, keepdims=True)
    o = _dotf(q * jnp.exp(b), St, "NT")
    A = _dotf(q * jnp.exp(b - bm), kk * jnp.exp(bm - b), "NT")
    o = o + _dotf(jnp.where(causal, A, 0.0), ir)
    St_new = St * jnp.exp(bl) + _dotf(ir, kk * jnp.exp(bl - b), "TN")
    on = o * lax.rsqrt(jnp.mean(o * o, axis=-1, keepdims=True) + EPS) * gn
    return St_new, on * (gr * jax.nn.sigmoid(gr))


def _hgrn_specs(T):
    return [_colblock(T, 2), _colblock(T, 8), _colblock(T, 14), _colblock(T, 20), _vecblock(), _vecblock(),
            pl.BlockSpec((1, 128), lambda j: (0, 0))]


def _hgrn_fwd(p, l0, l1, gn, nreal, real_end, name):
    T = p.shape[0]
    nch = nreal * (CH // HG)

    def body(q_ref, f_ref, i_ref, g_ref, l0_ref, l1_ref, gn_ref, y_ref, s_ref):
        _zero_pad_rows(y_ref, CH, CH * (1 + nreal), T)

        def chunk(c, St):
            start = pl.multiple_of(CH + c * HG, HG)
            sl = pl.ds(start, HG)
            s_ref[0, c] = St
            St_new, y = _hgrn_chunk(St, q_ref[sl, :], f_ref[sl, :], i_ref[sl, :], g_ref[sl, :], l0_ref[...],
                                    l1_ref[...], gn_ref[...])
            y_ref[sl, :] = jnp.where(start + _rows(HG) < real_end, y, 0.0)
            return St_new

        lax.fori_loop(0, nch, chunk, jnp.zeros((128, 128), F32), unroll=2)

    return pl.pallas_call(
        body, name=name, grid=(6,),
        out_shape=(jax.ShapeDtypeStruct((T, 768), F32), jax.ShapeDtypeStruct((6, nch, 128, 128), F32)),
        in_specs=_hgrn_specs(T),
        out_specs=(_colblock(T, 0), pl.BlockSpec((1, nch, 128, 128), lambda j: (j, 0, 0, 0))),
        compiler_params=_cparams(("parallel",)),
    )(p, p, p, p, l0, l1, gn)


def _hgrn_bwd(p, l0, l1, gn, states, dy, nreal, real_end, name):
    T = p.shape[0]
    nch = nreal * (CH // HG)

    def body(q_ref, f_ref, i_ref, g_ref, l0_ref, l1_ref, gn_ref, s_ref, dy_ref,
             dq_ref, df_ref, di_ref, dg_ref, dl0_ref, dl1_ref, dgn_ref):
        for r in (dq_ref, df_ref, di_ref, dg_ref):
            _zero_pad_rows(r, CH, CH * (1 + nreal), T)

        def chunk(k, carry):
            dSt, a0, a1, agn = carry
            c = nch - 1 - k
            start = pl.multiple_of(CH + c * HG, HG)
            sl = pl.ds(start, HG)
            _, vjp = jax.vjp(_hgrn_chunk, s_ref[0, c], q_ref[sl, :], f_ref[sl, :], i_ref[sl, :], g_ref[sl, :],
                             l0_ref[...], l1_ref[...], gn_ref[...])
            dyc = jnp.where(start + _rows(HG) < real_end, dy_ref[sl, :], 0.0)
            dS, dq, df, di, dg, d0, d1, dgn = vjp((dSt, dyc))
            dq_ref[sl, :] = dq
            df_ref[sl, :] = df
            di_ref[sl, :] = di
            dg_ref[sl, :] = dg
            return dS, a0 + d0, a1 + d1, agn + dgn

        z = jnp.zeros((1, 128), F32)
        _, a0, a1, agn = lax.fori_loop(0, nch, chunk, (jnp.zeros((128, 128), F32), z, z, z), unroll=2)
        dl0_ref[...] = a0
        dl1_ref[...] = a1

        @pl.when(pl.program_id(0) == 0)
        def _():
            dgn_ref[...] = jnp.zeros_like(dgn_ref)

        dgn_ref[...] += agn

    big = jax.ShapeDtypeStruct((T, 768), F32)
    vec = jax.ShapeDtypeStruct((1, 768), F32)
    return pl.pallas_call(
        body, name=name, grid=(6,),
        out_shape=(big, big, big, big, vec, vec, jax.ShapeDtypeStruct((1, 128), F32)),
        in_specs=_hgrn_specs(T) + [pl.BlockSpec((1, nch, 128, 128), lambda j: (j, 0, 0, 0)), _colblock(T, 2)],
        out_specs=(_colblock(T, 0), _colblock(T, 0), _colblock(T, 0), _colblock(T, 0), _vecblock(), _vecblock(),
                   pl.BlockSpec((1, 128), lambda j: (0, 0))),
        compiler_params=_cparams(("arbitrary",), 60 * 2 ** 20),
    )(p, p, p, p, l0, l1, gn, states, dy)


def _glu(a, b):
    return a * jax.nn.sigmoid(b)


def _conv_post(cv, ln_g, ln_b):
    mu = jnp.mean(cv, axis=-1, keepdims=True)
    d = cv - mu
    var = jnp.mean(d * d, axis=-1, keepdims=True)
    un = d * lax.rsqrt(var + EPS) * ln_g + ln_b
    return un * jax.nn.sigmoid(un)


def _causal_conv(uh, w_ref, width, halo):
    acc = None
    for j in range(width):
        term = _down(uh, width - 1 - j) * w_ref[pl.ds(j, 1), :]
        acc = term if acc is None else acc + term
    return acc[halo:]


def _conf_fwd(p, cw, cb, lg, lb, nreal, real_end, name):
    T = p.shape[0]

    def body(a_ref, b_ref, w_ref, cb_ref, lg_ref, lb_ref, y_ref):
        _zero_pad_rows(y_ref, CH, CH * (1 + nreal), T)

        def chunk(c, carry):
            start = pl.multiple_of(c * CH, CH)
            ext = pl.ds(start - 32, CH + 32)
            cv = _causal_conv(_glu(a_ref[ext, :], b_ref[ext, :]), w_ref, CONV_W, 32) + cb_ref[...]
            y = _conv_post(cv, lg_ref[...], lb_ref[...])
            y_ref[pl.ds(start, CH), :] = jnp.where(start + _rows(CH) < real_end, y, 0.0)
            return carry

        lax.fori_loop(1, 1 + nreal, chunk, 0)

    return pl.pallas_call(
        body, name=name, grid=(4,), out_shape=jax.ShapeDtypeStruct((T, 512), F32),
        in_specs=[_colblock(T, 0), _colblock(T, 4), _vecblock(32), _vecblock(), _vecblock(), _vecblock()],
        out_specs=_colblock(T, 0), compiler_params=_cparams(("parallel",)),
    )(p, p, cw, cb, lg, lb)


def _conf_bwd(p, cw, cb, lg, lb, dy, nreal, real_end, name):
    T = p.shape[0]

    def body(a_ref, b_ref, w_ref, cb_ref, lg_ref, lb_ref, dy_ref, da_ref, db_ref, dw_ref, dcb_ref, dlg_ref, dlb_ref):
        _zero_pad_rows(da_ref, CH, CH * (1 + nreal), T)
        _zero_pad_rows(db_ref, CH, CH * (1 + nreal), T)
        for r in (dw_ref, dcb_ref, dlg_ref, dlb_ref):
            r[...] = jnp.zeros_like(r)

        def chunk(c, carry):
            start = pl.multiple_of(c * CH, CH)
            ext = pl.ds(start - 32, CH + 64)
            ue = _glu(a_ref[ext, :], b_ref[ext, :])
            cv = _causal_conv(ue, w_ref, CONV_W, 32) + cb_ref[...]
            dye = jnp.where(start + _rows(CH + 32) < real_end, dy_ref[pl.ds(start, CH + 32), :], 0.0)
            _, vjp_cur = jax.vjp(_conv_post, cv[:CH], lg_ref[...], lb_ref[...])
            dc_cur, dlg, dlb = vjp_cur(dye[:CH])
            _, vjp_halo = jax.vjp(_conv_post, cv[CH:], lg_ref[...], lb_ref[...])
            dce = jnp.concatenate([dc_cur, vjp_halo(dye[CH:])[0]], axis=0)
            dlg_ref[...] += dlg
            dlb_ref[...] += dlb
            dcb_ref[...] += jnp.sum(dc_cur, axis=0, keepdims=True)
            du = None
            for j in range(CONV_W):
                w_j = w_ref[pl.ds(j, 1), :]
                term = _up(dce, CONV_W - 1 - j)[:CH] * w_j
                du = term if du is None else du + term
                dw_ref[pl.ds(j, 1), :] += jnp.sum(dc_cur * _up(ue, 2 + j)[:CH], axis=0, keepdims=True)
            cur = pl.ds(start, CH)
            _, vjp_glu = jax.vjp(_glu, a_ref[cur, :], b_ref[cur, :])
            da, db = vjp_glu(du)
            da_ref[cur, :] = da
            db_ref[cur, :] = db
            return carry

        lax.fori_loop(1, 1 + nreal, chunk, 0)

    big = jax.ShapeDtypeStruct((T, 512), F32)
    vec = jax.ShapeDtypeStruct((1, 512), F32)
    return pl.pallas_call(
        body, name=name, grid=(4,), out_shape=(big, big, jax.ShapeDtypeStruct((32, 512), F32), vec, vec, vec),
        in_specs=[_colblock(T, 0), _colblock(T, 4), _vecblock(32), _vecblock(), _vecblock(), _vecblock(),
                  _colblock(T, 0)],
        out_specs=(_colblock(T, 0), _colblock(T, 0), _vecblock(32), _vecblock(), _vecblock(), _vecblock()),
        compiler_params=_cparams(("parallel",)),
    )(p, p, cw, cb, lg, lb, dy)


def _softplus_neg(lam):
    e = jnp.exp(-lam)
    small = e * (1.0 - e * (0.5 - e * (1.0 / 3.0 - e * 0.25)))
    return jnp.where(e < 0.02, small, jnp.log(1.0 + e))


def _one_minus_exp(x):
    series = -x * (1.0 + x * (0.5 + x * (1.0 / 6.0 + x * (1.0 / 24.0 + x * (1.0 / 120.0)))))
    return jnp.where(x > -0.05, series, 1.0 - jnp.exp(x))


def _lru_pre(u, wa, wx, ba, bx, lam, first):
    r = jax.nn.sigmoid(_dot(u, wa) + ba)
    i = jax.nn.sigmoid(_dot(u, wx) + bx)
    log_a = -LRU_C * r * _softplus_neg(lam)
    a = jnp.exp(log_a)
    mult = jnp.sqrt(_one_minus_exp(2.0 * log_a))
    return a, jnp.where(first, 1.0, mult) * (i * u)


def _gelu_gate(gate, h):
    inner = math.sqrt(2.0 / math.pi) * (gate + 0.044715 * (gate * gate * gate))
    return 0.5 * gate * (1.0 + jnp.tanh(inner)) * h


def _lru_specs(T):
    mat = pl.BlockSpec((1, 128, 128), lambda j: (j, 0, 0))
    return [_colblock(T, 8), _colblock(T, 12), _vecblock(8), _vecblock(), mat, mat, _vecblock(), _vecblock(),
            _vecblock()]


def _lru_fwd(p, cw, cb, wa, wx, ba, bx, lam, nreal, real_end, name):
    T = p.shape[0]

    def body(x_ref, g_ref, w_ref, cb_ref, wa_ref, wx_ref, ba_ref, bx_ref, lam_ref, y_ref, h_ref):
        _zero_pad_rows(y_ref, CH, CH * (1 + nreal), T)
        _zero_pad_rows(h_ref, CH, CH * (1 + nreal), T)
        rows = _rows(CH)

        def chunk(c, hprev):
            start = pl.multiple_of(c * CH, CH)
            u = _causal_conv(x_ref[pl.ds(start - 8, CH + 8), :], w_ref, LRU_W, 8) + cb_ref[...]
            A, B = _lru_pre(u, wa_ref[0], wx_ref[0], ba_ref[...], bx_ref[...], lam_ref[...], start + rows == CH)
            s = 1
            while s < CH:
                B = A * jnp.where(rows >= s, _down(B, s), 0.0) + B
                A = A * jnp.where(rows >= s, _down(A, s), 1.0)
                s *= 2
            h = B + A * hprev
            cur = pl.ds(start, CH)
            h_ref[cur, :] = h
            y_ref[cur, :] = jnp.where(start + rows < real_end, _gelu_gate(g_ref[cur, :], h), 0.0)
            return jnp.sum(jnp.where(rows == CH - 1, h, 0.0), axis=0, keepdims=True)

        lax.fori_loop(1, 1 + nreal, chunk, jnp.zeros((1, 128), F32))

    big = jax.ShapeDtypeStruct((T, 512), F32)
    return pl.pallas_call(
        body, name=name, grid=(4,), out_shape=(big, big), in_specs=_lru_specs(T),
        out_specs=(_colblock(T, 0), _colblock(T, 0)), compiler_params=_cparams(("parallel",)),
    )(p, p, cw, cb, wa, wx, ba, bx, lam)


def _lru_bwd(p, cw, cb, wa, wx, ba, bx, lam, hs, dy, nreal, real_end, name):
    T = p.shape[0]

    def body(x_ref, g_ref, w_ref, cb_ref, wa_ref, wx_ref, ba_ref, bx_ref, lam_ref, h_ref, dy_ref,
             dx_ref, dgate_ref, dw_ref, dcb_ref, dwa_ref, dwx_ref, dba_ref, dbx_ref, dlam_ref):
        _zero_pad_rows(dx_ref, CH, CH * (1 + nreal), T)
        _zero_pad_rows(dgate_ref, CH, CH * (1 + nreal), T)
        for r in (dw_ref, dcb_ref, dwa_ref, dwx_ref, dba_ref, dbx_ref, dlam_ref):
            r[...] = jnp.zeros_like(r)
        rows = _rows(CH)

        def chunk(k, carry):
            cdh, du_head = carry
            c = nreal - k
            start = pl.multiple_of(c * CH, CH)
            cur = pl.ds(start, CH)
            xe = x_ref[pl.ds(start - 8, CH + 8), :]
            u = _causal_conv(xe, w_ref, LRU_W, 8) + cb_ref[...]
            first = start + rows == CH
            (a, _), vjp_pre = jax.vjp(lambda uu, m1, m2, b1, b2, ll: _lru_pre(uu, m1, m2, b1, b2, ll, first),
                                      u, wa_ref[0], wx_ref[0], ba_ref[...], bx_ref[...], lam_ref[...])
            h = h_ref[cur, :]
            hm1 = _down(h_ref[pl.ds(start - 8, CH + 8), :], 1)[8:]
            _, vjp_post = jax.vjp(_gelu_gate, g_ref[cur, :], h)
            dgate, D = vjp_post(jnp.where(start + rows < real_end, dy_ref[cur, :], 0.0))
            dgate_ref[cur, :] = dgate
            D = D + jnp.where(rows == CH - 1, cdh, 0.0)
            C = jnp.where(rows < CH - 1, _up(a, 1), 0.0)
            s = 1
            while s < CH:
                D = D + C * jnp.where(rows + s < CH, _up(D, s), 0.0)
                C = C * jnp.where(rows + s < CH, _up(C, s), 1.0)
                s *= 2
            du, dwa, dwx, dba, dbx, dlam = vjp_pre((D * hm1, D))
            dwa_ref[0] += dwa
            dwx_ref[0] += dwx
            dba_ref[...] += dba
            dbx_ref[...] += dbx
            dlam_ref[...] += dlam
            dcb_ref[...] += jnp.sum(du, axis=0, keepdims=True)
            due = jnp.concatenate([du, du_head], axis=0)
            dx = None
            for j in range(LRU_W):
                term = _up(due, LRU_W - 1 - j)[:CH] * w_ref[pl.ds(j, 1), :]
                dx = term if dx is None else dx + term
                dw_ref[pl.ds(j, 1), :] += jnp.sum(du * _up(xe, 8 - (LRU_W - 1) + j)[:CH], axis=0, keepdims=True)
            dx_ref[cur, :] = dx
            return jnp.sum(jnp.where(rows == 0, a * D, 0.0), axis=0, keepdims=True), du[:8]

        lax.fori_loop(0, nreal, chunk, (jnp.zeros((1, 128), F32), jnp.zeros((8, 128), F32)))

    big = jax.ShapeDtypeStruct((T, 512), F32)
    vec = jax.ShapeDtypeStruct((1, 512), F32)
    mat = jax.ShapeDtypeStruct((4, 128, 128), F32)
    matspec = pl.BlockSpec((1, 128, 128), lambda j: (j, 0, 0))
    return pl.pallas_call(
        body, name=name, grid=(4,),
        out_shape=(big, big, jax.ShapeDtypeStruct((8, 512), F32), vec, mat, mat, vec, vec, vec),
        in_specs=_lru_specs(T) + [_colblock(T, 0), _colblock(T, 4)],
        out_specs=(_colblock(T, 0), _colblock(T, 0), _vecblock(8), _vecblock(), matspec, matspec, _vecblock(),
                   _vecblock(), _vecblock()),
        compiler_params=_cparams(("parallel",)),
    )(p, p, cw, cb, wa, wx, ba, bx, lam, hs, dy)


def _ffn_forward(h, gamma, wg, wu, wd, tag):
    xn = _rms_fwd(h, gamma, f"rms_fwd_{tag}")
    g, u, a = _ffn_up(xn, wg, wu, f"ffn_up_{tag}")
    out = _mm([(a, wd)], "NN", f"ffn_down_{tag}", res=h, res_scale=0.5)
    return out, (h, xn, g, u, a)


def _ffn_backward(saved, gamma, wg, wu, wd, dout, tag):
    h, xn, g, u, a = saved
    dwd = _mm([(a, dout)], "TN", f"ffn_dwd_{tag}", res_scale=0.5)
    dg, du = _ffn_dact(dout, wd, g, u, 0.5, f"ffn_dact_{tag}")
    dwg = _mm([(xn, dg)], "TN", f"ffn_dwg_{tag}")
    dwu = _mm([(xn, du)], "TN", f"ffn_dwu_{tag}")
    dxn = _mm([(dg, wg), (du, wu)], "NT", f"ffn_dxn_{tag}")
    dh, dgamma = _rms_bwd(h, gamma, dxn, dout, f"rms_bwd_{tag}")
    return dh, dgamma, dwg, dwu, dwd


def _blockdiag(w, per):
    n, k, _ = w.shape
    out = jnp.zeros((n // per, per * k, per * k), w.dtype)
    for i in range(per):
        out = out.at[:, i * k:(i + 1) * k, i * k:(i + 1) * k].set(w[i::per])
    return out


def _blockdiag_grad(g, per, k):
    parts = [g[:, i * k:(i + 1) * k, i * k:(i + 1) * k] for i in range(per)]
    return jnp.stack(parts, axis=1).reshape(-1, k, k)


def _local_step(x, tgt, W):
    seq, D = x.shape
    lr = N_META + seq
    nreal = -(-lr // CH)
    T = CH * (nreal + 2)
    if T > 640 and T % 640:
        T += 640 - T % 640
    lo, real_end = CH + N_META, CH + lr
    zf = lambda n: jnp.zeros((n, D), F32)
    h0 = jnp.concatenate([zf(CH), W['meta_tokens'], x, zf(T - real_end)], axis=0)
    tgt_p = jnp.concatenate([zf(lo), tgt, zf(T - real_end)], axis=0)
    row = lambda v: v.reshape(1, -1)
    G = {}

    h = h0
    saved = []
    for l in range(2):
        h, s1 = _ffn_forward(h, row(W['ffn1_norm'][l]), W['ffn1_wg'][l], W['ffn1_wu'][l], W['ffn1_wd'][l], f"a{l}")
        hm = h
        xn = _rms_fwd(hm, row(W['mix_norm'][l]), f"rms_fwd_mix{l}")
        if l == 0:
            p = _mm([(xn, W['w_in_even'][0])], "NN", "in_even")
            wbd = _blockdiag(W['pool_w'][0], 4)[0]
            l0, l1 = row(W['hgrn_lb_logits'][0]), row(W['hgrn_lb_logits'][1])
            ya = _pool_fwd(p, wbd, W['pool_scale'], nreal, real_end, "pool_fwd")
            yb, states = _hgrn_fwd(p, l0, l1, W['hgrn_gnorm'], nreal, real_end, "hgrn_fwd")
            wo = W['w_out_even'][0]
            h = _mm([(ya, wo[:256]), (yb, wo[256:])], "NN", "out_even", res=hm)
            sm = (hm, xn, p, wbd, l0, l1, ya, yb, states)
        else:
            p = _mm([(xn, W['w_in_odd'][0])], "NN", "in_odd")
            cw = jnp.pad(W['conv_w'][0], ((0, 1), (0, 0)))
            lw = jnp.pad(W['lru_conv_w'][0], ((0, 4), (0, 0)))
            wa, wx = _blockdiag(W['lru_wa'][0], 2), _blockdiag(W['lru_wx'][0], 2)
            yc = _conf_fwd(p, cw, W['conv_b'], W['conv_ln_g'], W['conv_ln_b'], nreal, real_end, "conf_fwd")
            yd, hs = _lru_fwd(p, lw, W['lru_conv_b'], wa, wx, W['lru_ba'], W['lru_bx'], W['lru_lambda'], nreal,
                              real_end, "lru_fwd")
            wo = W['w_out_odd'][0]
            h = _mm([(yc, wo[:512]), (yd, wo[512:])], "NN", "out_odd", res=hm)
            sm = (hm, xn, p, cw, lw, wa, wx, yc, yd, hs)
        h, s2 = _ffn_forward(h, row(W['ffn2_norm'][l]), W['ffn2_wg'][l], W['ffn2_wu'][l], W['ffn2_wd'][l], f"b{l}")
        saved.append((s1, sm, s2))

    loss8, dh, dfin = _loss_head(h, row(W['final_norm']), tgt_p, lo, real_end, "loss_head")
    G['final_norm'] = dfin[0]

    per_layer = {k: [None, None] for k in ('ffn1_norm', 'ffn1_wg', 'ffn1_wu', 'ffn1_wd', 'mix_norm', 'ffn2_norm',
                                           'ffn2_wg', 'ffn2_wu', 'ffn2_wd')}
    for l in (1, 0):
        s1, sm, s2 = saved[l]
        dh, dn, dwg, dwu, dwd = _ffn_backward(s2, row(W['ffn2_norm'][l]), W['ffn2_wg'][l], W['ffn2_wu'][l],
                                              W['ffn2_wd'][l], dh, f"b{l}")
        per_layer['ffn2_norm'][l], per_layer['ffn2_wg'][l] = dn[0], dwg
        per_layer['ffn2_wu'][l], per_layer['ffn2_wd'][l] = dwu, dwd
        if l == 0:
            hm, xn, p, wbd, l0, l1, ya, yb, states = sm
            wo, wi = W['w_out_even'][0], W['w_in_even'][0]
            G['w_out_even'] = jnp.concatenate([_mm([(ya, dh)], "TN", "dwo_even_a"),
                                               _mm([(yb, dh)], "TN", "dwo_even_b")], axis=0)[None]
            dy = _mm([(dh, wo)], "NT", "dy_even")
            dpp, dwbd, dsc = _pool_bwd(p, wbd, W['pool_scale'], dy, nreal, real_end, "pool_bwd")
            dq, df, di, dg, dl0, dl1, dgn = _hgrn_bwd(p, l0, l1, W['hgrn_gnorm'], states, dy, nreal, real_end,
                                                      "hgrn_bwd")
            G['pool_w'] = _blockdiag_grad(dwbd[None], 4, 64)[None]
            G['pool_scale'] = dsc
            G['hgrn_lb_logits'] = jnp.concatenate([dl0, dl1], axis=0)
            G['hgrn_gnorm'] = dgn
            parts = [dpp, dq, df, di, dg]
            offs = [0, 256, 1024, 1792, 2560, 3328]
            G['w_in_even'] = jnp.concatenate(
                [_mm([(xn, dpart)], "TN", f"dwi_even_{k}") for k, dpart in enumerate(parts)], axis=1)[None]
            dxn = _mm([(dpart, wi[:, offs[k]:offs[k + 1]]) for k, dpart in enumerate(parts)], "NT", "dxn_even")
        else:
            hm, xn, p, cw, lw, wa, wx, yc, yd, hs = sm
            wo, wi = W['w_out_odd'][0], W['w_in_odd'][0]
            G['w_out_odd'] = jnp.concatenate([_mm([(yc, dh)], "TN", "dwo_odd_c"),
                                              _mm([(yd, dh)], "TN", "dwo_odd_d")], axis=0)[None]
            dy = _mm([(dh, wo)], "NT", "dy_odd")
            da, db, dcw, dcb, dlg, dlb = _conf_bwd(p, cw, W['conv_b'], W['conv_ln_g'], W['conv_ln_b'], dy, nreal,
                                                   real_end, "conf_bwd")
            dx, dgate, dlw, dlcb, dwa, dwx, dba, dbx, dlam = _lru_bwd(
                p, lw, W['lru_conv_b'], wa, wx, W['lru_ba'], W['lru_bx'], W['lru_lambda'], hs, dy, nreal, real_end,
                "lru_bwd")
            G['conv_w'], G['conv_b'], G['conv_ln_g'], G['conv_ln_b'] = dcw[None, :CONV_W], dcb, dlg, dlb
            G['lru_conv_w'], G['lru_conv_b'] = dlw[None, :LRU_W], dlcb
            G['lru_wa'] = _blockdiag_grad(dwa, 2, 64)[None]
            G['lru_wx'] = _blockdiag_grad(dwx, 2, 64)[None]
            G['lru_ba'], G['lru_bx'], G['lru_lambda'] = dba, dbx, dlam
            parts = [da, db, dx, dgate]
            G['w_in_odd'] = jnp.concatenate(
                [_mm([(xn, dpart)], "TN", f"dwi_odd_{k}") for k, dpart in enumerate(parts)], axis=1)[None]
            dxn = _mm([(dpart, wi[:, 512 * k:512 * (k + 1)]) for k, dpart in enumerate(parts)], "NT", "dxn_odd")
        dh, dn = _rms_bwd(hm, row(W['mix_norm'][l]), dxn, dh, f"rms_bwd_mix{l}")
        per_layer['mix_norm'][l] = dn[0]
        dh, dn, dwg, dwu, dwd = _ffn_backward(s1, row(W['ffn1_norm'][l]), W['ffn1_wg'][l], W['ffn1_wu'][l],
                                              W['ffn1_wd'][l], dh, f"a{l}")
        per_layer['ffn1_norm'][l], per_layer['ffn1_wg'][l] = dn[0], dwg
        per_layer['ffn1_wu'][l], per_layer['ffn1_wd'][l] = dwu, dwd
    for k, v in per_layer.items():
        G[k] = jnp.stack(v, axis=0)
    G['meta_tokens'] = dh[CH:lo]
    return loss8[0, 0], dh[lo:real_end], G


def _pack(arrs):
    flat = jnp.concatenate([a.reshape(-1).astype(F32) for a in arrs])
    n = flat.shape[0]
    padded = -(-n // 1024) * 1024
    return jnp.pad(flat, (0, padded - n)).reshape(-1, 128)


def _unpack(packed, shapes):
    flat = packed.reshape(-1)
    out, off = [], 0
    for s in shapes:
        n = math.prod(s)
        out.append(flat[off:off + n].reshape(s))
        off += n
    return out


def _to_full(gathered, axis):
    s = gathered.shape[1:]
    return jnp.moveaxis(gathered, 0, axis).reshape(s[:axis] + (NDEV * s[axis],) + s[axis + 1:])


def _to_slots(full, axis):
    s = full.shape
    return jnp.moveaxis(full.reshape(s[:axis] + (NDEV, s[axis] // NDEV) + s[axis + 1:]), axis, 0)


def kernel(x, meta_tokens, ffn1_norm, ffn1_wg, ffn1_wu, ffn1_wd, mix_norm, ffn2_norm, ffn2_wg, ffn2_wu, ffn2_wd, w_in_even, pool_w, pool_scale, hgrn_lb_logits, hgrn_gnorm, w_out_even, w_in_odd, conv_w, conv_b, conv_ln_g, conv_ln_b, lru_conv_w, lru_conv_b, lru_wa, lru_ba, lru_wx, lru_bx, lru_lambda, w_out_odd, final_norm, loss_target, m_meta_tokens, m_ffn1_norm, m_ffn1_wg, m_ffn1_wu, m_ffn1_wd, m_mix_norm, m_ffn2_norm, m_ffn2_wg, m_ffn2_wu, m_ffn2_wd, m_w_in_even, m_pool_w, m_pool_scale, m_hgrn_lb_logits, m_hgrn_gnorm, m_w_out_even, m_w_in_odd, m_conv_w, m_conv_b, m_conv_ln_g, m_conv_ln_b, m_lru_conv_w, m_lru_conv_b, m_lru_wa, m_lru_ba, m_lru_wx, m_lru_bx, m_lru_lambda, m_w_out_odd, m_final_norm, v_meta_tokens, v_ffn1_norm, v_ffn1_wg, v_ffn1_wu, v_ffn1_wd, v_mix_norm, v_ffn2_norm, v_ffn2_wg, v_ffn2_wu, v_ffn2_wd, v_w_in_even, v_pool_w, v_pool_scale, v_hgrn_lb_logits, v_hgrn_gnorm, v_w_out_even, v_w_in_odd, v_conv_w, v_conv_b, v_conv_ln_g, v_conv_ln_b, v_lru_conv_w, v_lru_conv_b, v_lru_wa, v_lru_ba, v_lru_wx, v_lru_bx, v_lru_lambda, v_w_out_odd, v_final_norm):
    args = (meta_tokens, ffn1_norm, ffn1_wg, ffn1_wu, ffn1_wd, mix_norm, ffn2_norm, ffn2_wg, ffn2_wu, ffn2_wd, w_in_even, pool_w, pool_scale, hgrn_lb_logits, hgrn_gnorm, w_out_even, w_in_odd, conv_w, conv_b, conv_ln_g, conv_ln_b, lru_conv_w, lru_conv_b, lru_wa, lru_ba, lru_wx, lru_bx, lru_lambda, w_out_odd, final_norm)
    margs = (m_meta_tokens, m_ffn1_norm, m_ffn1_wg, m_ffn1_wu, m_ffn1_wd, m_mix_norm, m_ffn2_norm, m_ffn2_wg, m_ffn2_wu, m_ffn2_wd, m_w_in_even, m_pool_w, m_pool_scale, m_hgrn_lb_logits, m_hgrn_gnorm, m_w_out_even, m_w_in_odd, m_conv_w, m_conv_b, m_conv_ln_g, m_conv_ln_b, m_lru_conv_w, m_lru_conv_b, m_lru_wa, m_lru_ba, m_lru_wx, m_lru_bx, m_lru_lambda, m_w_out_odd, m_final_norm)
    vargs = (v_meta_tokens, v_ffn1_norm, v_ffn1_wg, v_ffn1_wu, v_ffn1_wd, v_mix_norm, v_ffn2_norm, v_ffn2_wg, v_ffn2_wu, v_ffn2_wd, v_w_in_even, v_pool_w, v_pool_scale, v_hgrn_lb_logits, v_hgrn_gnorm, v_w_out_even, v_w_in_odd, v_conv_w, v_conv_b, v_conv_ln_g, v_conv_ln_b, v_lru_conv_w, v_lru_conv_b, v_lru_wa, v_lru_ba, v_lru_wx, v_lru_bx, v_lru_lambda, v_w_out_odd, v_final_norm)
    Wl = dict(zip(W_NAMES, args))
    Ml = dict(zip(W_NAMES, margs))
    Vl = dict(zip(W_NAMES, vargs))

    small_shapes = [Wl[n].shape for n in SMALL_SHARDED]
    send = [Wl[n].astype(MXU) for n in BIG] + [_pack([Wl[n] for n in SMALL_SHARDED])]
    got = _exchange(send, [True] * len(send), "gather_weights")
    W = {n: Wl[n] for n in REPLICATED}
    for n, g in zip(BIG, got):
        W[n] = _to_full(g, SHARD_AXIS[n])
    per_dev = [_unpack(got[-1][d], small_shapes) for d in range(NDEV)]
    for k, n in enumerate(SMALL_SHARDED):
        W[n] = _to_full(jnp.stack([per_dev[d][k] for d in range(NDEV)]), SHARD_AXIS[n])

    loss_part, grad_x, G = _local_step(x[0], loss_target[0], W)
    loss = lax.psum(loss_part, MESH_AXES)

    send = [_to_slots(G[n].astype(MXU), SHARD_AXIS[n]) for n in BIG]
    small_slots = [_to_slots(G[n].astype(F32), SHARD_AXIS[n]) for n in SMALL_SHARDED]
    send.append(jnp.stack([_pack([s[d] for s in small_slots]) for d in range(NDEV)]))
    send.append(_pack([G[n] for n in REPLICATED]))
    recv = _exchange(send, [False] * (len(send) - 1) + [True], "scatter_grads")

    outs = {}
    for n, r in zip(BIG, recv):
        shp = Wl[n].shape
        C = shp[-1]
        res = _adamw(r.reshape(NDEV, -1, C), Wl[n].reshape(-1, C), Ml[n].reshape(-1, C), Vl[n].reshape(-1, C),
                     f"adamw_{n}")
        outs[n] = [o.reshape(shp) for o in res]
    for names, r, tag in ((SMALL_SHARDED, recv[-2], "small"), (REPLICATED, recv[-1], "repl")):
        shapes = [Wl[n].shape for n in names]
        res = _adamw(r, _pack([Wl[n] for n in names]), _pack([Ml[n] for n in names]), _pack([Vl[n] for n in names]),
                     f"adamw_{tag}")
        unp = [_unpack(o, shapes) for o in res]
        for k, n in enumerate(names):
            outs[n] = [unp[j][k] for j in range(4)]

    result = [loss, grad_x[None]]
    for j in range(4):
        result += [outs[n][j] for n in W_NAMES]
    return tuple(result)
```

```python
import functools
import math

import jax
import jax.numpy as jnp
from jax import lax
from jax.experimental import pallas as pl
from jax.experimental.pallas import tpu as pltpu

F32 = jnp.float32
MXU = jnp.bfloat16
EPS = 1e-6
CH = 128
HG = 64
N_META = 16
CONV_W = 31
LRU_W = 4
LRU_C = 8.0
VMEM_LIMIT = 48 * 2 ** 20
ADAM_LR, ADAM_B1, ADAM_B2, ADAM_EPS, ADAM_WD, ADAM_STEP = 0.001, 0.9, 0.999, 1e-08, 0.01, 10
MESH_AXES = ("x", "y", "c")
NDEV = 8

W_NAMES = ['meta_tokens', 'ffn1_norm', 'ffn1_wg', 'ffn1_wu', 'ffn1_wd', 'mix_norm', 'ffn2_norm', 'ffn2_wg', 'ffn2_wu',
           'ffn2_wd', 'w_in_even', 'pool_w', 'pool_scale', 'hgrn_lb_logits', 'hgrn_gnorm', 'w_out_even', 'w_in_odd',
           'conv_w', 'conv_b', 'conv_ln_g', 'conv_ln_b', 'lru_conv_w', 'lru_conv_b', 'lru_wa', 'lru_ba', 'lru_wx',
           'lru_bx', 'lru_lambda', 'w_out_odd', 'final_norm']
SHARD_AXIS = {'meta_tokens': 1, 'ffn1_wg': 2, 'ffn1_wu': 2, 'ffn1_wd': 1, 'ffn2_wg': 2, 'ffn2_wu': 2, 'ffn2_wd': 1,
              'w_in_even': 2, 'w_out_even': 1, 'w_in_odd': 2, 'conv_w': 2, 'conv_b': 1, 'conv_ln_g': 1,
              'conv_ln_b': 1, 'lru_conv_w': 2, 'lru_conv_b': 1, 'lru_ba': 1, 'lru_bx': 1, 'lru_lambda': 1,
              'w_out_odd': 1}
BIG = ['ffn1_wg', 'ffn1_wu', 'ffn1_wd', 'ffn2_wg', 'ffn2_wu', 'ffn2_wd', 'w_in_even', 'w_out_even', 'w_in_odd',
       'w_out_odd']
SMALL_SHARDED = [n for n in W_NAMES if n in SHARD_AXIS and n not in BIG]
REPLICATED = [n for n in W_NAMES if n not in SHARD_AXIS]


def _cparams(sem=None, vmem=VMEM_LIMIT):
    return pltpu.CompilerParams(dimension_semantics=sem, vmem_limit_bytes=vmem)


def _tile(n):
    for c in (640, 512, 256, 128):
        if n % c == 0:
            return c
    return n


def _rowtile(n):
    for c in (256, 352, 128, 64, 32, 16, 8):
        if n % c == 0:
            return c
    return n


def _copies(srcs, lands, bcast, ssem, rsem, lsem):
    x, y, c = lax.axis_index("x"), lax.axis_index("y"), lax.axis_index("c")
    me = 4 * x + 2 * y + c
    locs, sends, recvs = [], [], []
    for a in range(len(srcs)):
        locs.append(pltpu.make_async_copy(srcs[a] if bcast[a] else srcs[a].at[me], lands[a].at[me], lsem.at[a]))
        for m in range(1, NDEV):
            px = 1 - x if (m >> 2) & 1 else x
            py = 1 - y if (m >> 1) & 1 else y
            pc = 1 - c if m & 1 else c
            peer = 4 * px + 2 * py + pc
            src = srcs[a] if bcast[a] else srcs[a].at[peer]
            k = a * NDEV + m
            for dst, out in ((lands[a].at[me], sends), (lands[a].at[peer], recvs)):
                out.append(pltpu.make_async_remote_copy(src_ref=src, dst_ref=dst, send_sem=ssem.at[k],
                                                        recv_sem=rsem.at[k], device_id=(px, py, pc),
                                                        device_id_type=pl.DeviceIdType.MESH))
    return locs, sends, recvs


def _land_shape(arr, bc):
    return (NDEV,) + tuple(arr.shape if bc else arr.shape[1:])


def _exchange(arrays, bcast, name):
    n = len(arrays)

    def body(*refs):
        locs, sends, recvs = _copies(refs[:n], refs[n:2 * n], bcast, refs[2 * n], refs[2 * n + 1], refs[2 * n + 2])
        for d in locs + sends:
            d.start()
        for r in recvs:
            r.wait_recv()
        for s in sends:
            s.wait_send()
        for loc in locs:
            loc.wait()

    out_shape = tuple(jax.ShapeDtypeStruct(_land_shape(arr, bc), arr.dtype) for arr, bc in zip(arrays, bcast))
    any_spec = pl.BlockSpec(memory_space=pl.ANY)
    return pl.pallas_call(
        body, name=name, out_shape=out_shape, in_specs=[any_spec] * n, out_specs=tuple([any_spec] * n),
        scratch_shapes=[pltpu.SemaphoreType.DMA((n * NDEV,)), pltpu.SemaphoreType.DMA((n * NDEV,)),
                        pltpu.SemaphoreType.DMA((n,))],
    )(*arrays)


_HBM = pl.BlockSpec(memory_space=pltpu.HBM)
_SEM = pl.BlockSpec(memory_space=pltpu.SEMAPHORE)
_EFFECT = pltpu.SideEffectType.DATAFLOW_SIDE_EFFECTING


def _exchange_start(groups, bcast, name):
    sizes = [len(g) for g in groups]
    srcs = [a for g in groups for a in g]
    n, ng = len(srcs), len(groups)
    lands = [lax.empty(_land_shape(a, bcast), a.dtype) for a in srcs]

    def body(*refs):
        off = 0
        for gi, sz in enumerate(sizes):
            sem = refs[2 * n + 3 * gi:2 * n + 3 * gi + 3]
            locs, sends, _ = _copies(refs[off:off + sz], refs[n + off:n + off + sz], [bcast] * sz, *sem)
            for d in locs + sends:
                d.start()
            off += sz
        refs[-1][...] = jnp.zeros((8, 128), F32)

    sems = []
    for sz in sizes:
        sems += [pltpu.SemaphoreType.DMA((sz * NDEV,)), pltpu.SemaphoreType.DMA((sz * NDEV,)),
                 pltpu.SemaphoreType.DMA((sz,))]
    thru = [pltpu.HBM(a.shape, a.dtype) for a in srcs + lands]
    outs = pl.pallas_call(
        body, name=name, out_shape=tuple(sems + thru + [jax.ShapeDtypeStruct((8, 128), F32)]),
        in_specs=[_HBM] * (2 * n),
        out_specs=tuple([_SEM] * (3 * ng) + [_HBM] * (2 * n) + [pl.BlockSpec(memory_space=pltpu.VMEM)]),
        input_output_aliases={i: 3 * ng + i for i in range(2 * n)},
        compiler_params=pltpu.CompilerParams(has_side_effects=_EFFECT),
    )(*[pltpu.with_memory_space_constraint(a, pltpu.HBM) for a in srcs + lands])
    handles, off = [], 0
    for gi, sz in enumerate(sizes):
        handles.append((outs[3 * gi:3 * gi + 3], outs[3 * ng + off:3 * ng + off + sz],
                        outs[3 * ng + n + off:3 * ng + n + off + sz]))
        off += sz
    return handles, outs[-1]


def _exchange_wait(handle, bcast, after, name):
    sems, srcs, lands = handle
    n = len(srcs)

    def body(*refs):
        locs, sends, recvs = _copies(refs[:n], refs[n:2 * n], [bcast] * n, *refs[2 * n:2 * n + 3])
        for r in recvs:
            r.wait_recv()
        for s in sends:
            s.wait_send()
        for loc in locs:
            loc.wait()

    outs = pl.pallas_call(
        body, name=name, out_shape=tuple(pltpu.HBM(a.shape, a.dtype) for a in list(srcs) + list(lands)),
        in_specs=[_HBM] * (2 * n) + [_SEM] * 3 + [pl.BlockSpec(memory_space=pl.ANY)],
        out_specs=tuple([_HBM] * (2 * n)), input_output_aliases={i: i for i in range(2 * n)},
        compiler_params=pltpu.CompilerParams(has_side_effects=_EFFECT),
    )(*srcs, *lands, *sems, after)
    return outs[n:]


def _adamw(recvs, w, m, v, name):
    R, C = w.shape
    nr = len(recvs)
    br = _rowtile(R // nr)
    nb0 = R // nr // br

    def body(*refs):
        w_ref, m_ref, v_ref, g_o, d_o, m_o, v_o = refs[nr:]
        g = None
        for j in range(nr):
            s = refs[j][0].astype(F32)
            for k in range(1, NDEV):
                s = s + refs[j][k].astype(F32)
            g = s if g is None else jnp.where(pl.program_id(0) >= j * nb0, s, g)
        mn = ADAM_B1 * m_ref[...] + (1.0 - ADAM_B1) * g
        vn = ADAM_B2 * v_ref[...] + (1.0 - ADAM_B2) * (g * g)
        m_hat = mn / (1.0 - ADAM_B1 ** ADAM_STEP)
        v_hat = vn / (1.0 - ADAM_B2 ** ADAM_STEP)
        g_o[...] = g
        d_o[...] = -ADAM_LR * (m_hat / (jnp.sqrt(v_hat) + ADAM_EPS) + ADAM_WD * w_ref[...])
        m_o[...] = mn
        v_o[...] = vn

    def rspec(j):
        return pl.BlockSpec((NDEV, br, C), lambda i: (0, jnp.clip(i - j * nb0, 0, nb0 - 1), 0))

    blk = pl.BlockSpec((br, C), lambda i: (i, 0))
    sds = jax.ShapeDtypeStruct((R, C), F32)
    return pl.pallas_call(
        body, name=name, grid=(R // br,), out_shape=(sds, sds, sds, sds),
        in_specs=[rspec(j) for j in range(nr)] + [blk, blk, blk], out_specs=(blk, blk, blk, blk),
        compiler_params=_cparams(("arbitrary",)),
    )(*recvs, w, m, v)


_DIMS = {"NN": ((1,), (0,)), "NT": ((1,), (1,)), "TN": ((0,), (0,))}


def _dot(a, b, mode="NN"):
    return lax.dot_general(a.astype(MXU), b.astype(MXU), (_DIMS[mode], ((), ())), preferred_element_type=F32)


def _dotf(a, b, mode="NN"):
    return lax.dot_general(a, b, (_DIMS[mode], ((), ())), precision=lax.Precision.HIGHEST,
                           preferred_element_type=F32)


def _mm(pairs, mode, name, res=None, res_scale=1.0, out_dtype=F32):
    a0, b0 = pairs[0]
    M = a0.shape[1] if mode == "TN" else a0.shape[0]
    N = b0.shape[0] if mode == "NT" else b0.shape[1]
    tm, tn = _tile(M), _tile(N)
    npairs = len(pairs)

    def body(*refs):
        acc = None
        for p in range(npairs):
            d = _dot(refs[2 * p][...], refs[2 * p + 1][...], mode)
            acc = d if acc is None else acc + d
        if res_scale != 1.0:
            acc = res_scale * acc
        if res is not None:
            acc = refs[2 * npairs][...] + acc
        refs[-1][...] = acc.astype(out_dtype)

    in_specs, args = [], []
    for a, b in pairs:
        if mode == "TN":
            in_specs.append(pl.BlockSpec((a.shape[0], tm), lambda i, j: (0, i)))
        else:
            in_specs.append(pl.BlockSpec((tm, a.shape[1]), lambda i, j: (i, 0)))
        if mode == "NT":
            in_specs.append(pl.BlockSpec((tn, b.shape[1]), lambda i, j: (j, 0)))
        else:
            in_specs.append(pl.BlockSpec((b.shape[0], tn), lambda i, j: (0, j)))
        args += [a, b]
    if res is not None:
        in_specs.append(pl.BlockSpec((tm, tn), lambda i, j: (i, j)))
        args.append(res)
    return pl.pallas_call(
        body, name=name, grid=(M // tm, N // tn), out_shape=jax.ShapeDtypeStruct((M, N), out_dtype),
        in_specs=in_specs, out_specs=pl.BlockSpec((tm, tn), lambda i, j: (i, j)),
        compiler_params=_cparams(("parallel", "parallel")),
    )(*args)


def _rms_fwd(h, gamma, name):
    T, D = h.shape
    tm = _tile(T)

    def body(h_ref, g_ref, o_ref):
        x = h_ref[...]
        r = lax.rsqrt(jnp.mean(x * x, axis=-1, keepdims=True) + EPS)
        o_ref[...] = (x * r * g_ref[...]).astype(MXU)

    return pl.pallas_call(
        body, name=name, grid=(T // tm,), out_shape=jax.ShapeDtypeStruct((T, D), MXU),
        in_specs=[pl.BlockSpec((tm, D), lambda i: (i, 0)), pl.BlockSpec((1, D), lambda i: (0, 0))],
        out_specs=pl.BlockSpec((tm, D), lambda i: (i, 0)), compiler_params=_cparams(("parallel",)),
    )(h, gamma)


def _rms_bwd_math(x, gamma, dy):
    r = lax.rsqrt(jnp.mean(x * x, axis=-1, keepdims=True) + EPS)
    z = dy * gamma
    dx = r * z - x * (r * r * r) * jnp.mean(z * x, axis=-1, keepdims=True)
    dgamma = jnp.sum(dy * x * r, axis=0, keepdims=True)
    return dx, dgamma


def _rms_bwd(h, gamma, dxn, dres, name):
    T, D = h.shape
    tm = _tile(T)

    def body(h_ref, g_ref, dxn_ref, dres_ref, dh_ref, dg_ref):
        dx, dgamma = _rms_bwd_math(h_ref[...], g_ref[...], dxn_ref[...])
        dh_ref[...] = dres_ref[...] + dx

        @pl.when(pl.program_id(0) == 0)
        def _():
            dg_ref[...] = jnp.zeros_like(dg_ref)

        dg_ref[...] += dgamma

    row = pl.BlockSpec((tm, D), lambda i: (i, 0))
    vec = pl.BlockSpec((1, D), lambda i: (0, 0))
    return pl.pallas_call(
        body, name=name, grid=(T // tm,),
        out_shape=(jax.ShapeDtypeStruct((T, D), F32), jax.ShapeDtypeStruct((1, D), F32)),
        in_specs=[row, vec, row, row], out_specs=(row, vec), compiler_params=_cparams(("arbitrary",)),
    )(h, gamma, dxn, dres)


def _loss_head(h, gamma, tgt, lo, hi, name):
    T, D = h.shape
    tm = _tile(T)

    def body(h_ref, g_ref, t_ref, loss_ref, dh_ref, dg_ref):
        i = pl.program_id(0)
        x = h_ref[...]
        r = lax.rsqrt(jnp.mean(x * x, axis=-1, keepdims=True) + EPS)
        y = x * r * g_ref[...]
        rows = i * tm + lax.broadcasted_iota(jnp.int32, (tm, 1), 0)
        valid = jnp.logical_and(rows >= lo, rows < hi)
        diff = jnp.where(valid, y - t_ref[...], 0.0)
        part = 0.5 * jnp.sum(jnp.sum(diff * diff, axis=-1, keepdims=True) / D, axis=0, keepdims=True)
        dx, dgamma = _rms_bwd_math(x, g_ref[...], diff / D)
        dh_ref[...] = dx

        @pl.when(i == 0)
        def _():
            dg_ref[...] = jnp.zeros_like(dg_ref)
            loss_ref[...] = jnp.zeros_like(loss_ref)

        dg_ref[...] += dgamma
        loss_ref[...] += jnp.broadcast_to(part, loss_ref.shape)

    row = pl.BlockSpec((tm, D), lambda i: (i, 0))
    vec = pl.BlockSpec((1, D), lambda i: (0, 0))
    lsp = pl.BlockSpec((8, 128), lambda i: (0, 0))
    return pl.pallas_call(
        body, name=name, grid=(T // tm,),
        out_shape=(jax.ShapeDtypeStruct((8, 128), F32), jax.ShapeDtypeStruct((T, D), F32),
                   jax.ShapeDtypeStruct((1, D), F32)),
        in_specs=[row, vec, row], out_specs=(lsp, row, vec), compiler_params=_cparams(("arbitrary",)),
    )(h, gamma, tgt)


def _ffn_up(xn, wg, wu, name):
    T, D = xn.shape
    Fd = wg.shape[1]
    tm, tn = _tile(T), _tile(Fd)

    def body(x_ref, wg_ref, wu_ref, g_ref, u_ref, a_ref):
        x = x_ref[...]
        g = _dot(x, wg_ref[...])
        u = _dot(x, wu_ref[...])
        g_ref[...] = g.astype(MXU)
        u_ref[...] = u.astype(MXU)
        a_ref[...] = (g * jax.nn.sigmoid(g) * u).astype(MXU)

    wsp = pl.BlockSpec((D, tn), lambda i, j: (0, j))
    osp = pl.BlockSpec((tm, tn), lambda i, j: (i, j))
    sds = jax.ShapeDtypeStruct((T, Fd), MXU)
    return pl.pallas_call(
        body, name=name, grid=(T // tm, Fd // tn), out_shape=(sds, sds, sds),
        in_specs=[pl.BlockSpec((tm, D), lambda i, j: (i, 0)), wsp, wsp], out_specs=(osp, osp, osp),
        compiler_params=_cparams(("parallel", "parallel")),
    )(xn, wg, wu)


def _ffn_dact(dy, wd, g, u, scale, name):
    T, D = dy.shape
    Fd = wd.shape[0]
    tm, tn = _tile(T), _tile(Fd)

    def body(dy_ref, wd_ref, g_ref, u_ref, dg_ref, du_ref):
        da = scale * _dot(dy_ref[...], wd_ref[...], "NT")
        gg = g_ref[...].astype(F32)
        uu = u_ref[...].astype(F32)
        sg = jax.nn.sigmoid(gg)
        dg_ref[...] = (da * uu * (sg * (1.0 + gg * (1.0 - sg)))).astype(MXU)
        du_ref[...] = (da * gg * sg).astype(MXU)

    osp = pl.BlockSpec((tm, tn), lambda i, j: (i, j))
    sds = jax.ShapeDtypeStruct((T, Fd), MXU)
    return pl.pallas_call(
        body, name=name, grid=(T // tm, Fd // tn), out_shape=(sds, sds),
        in_specs=[pl.BlockSpec((tm, D), lambda i, j: (i, 0)), pl.BlockSpec((tn, D), lambda i, j: (j, 0)), osp, osp],
        out_specs=(osp, osp), compiler_params=_cparams(("parallel", "parallel")),
    )(dy, wd, g, u)


def _down(v, s):
    return v if s == 0 else pltpu.roll(v, s, 0)


def _up(v, s):
    return v if s == 0 else pltpu.roll(v, v.shape[0] - s, 0)


def _rows(n):
    return lax.broadcasted_iota(jnp.int32, (n, 1), 0)


def _zero_pad_rows(ref, lo_end, hi_start, T):
    ref[pl.ds(0, lo_end), :] = jnp.zeros((lo_end, ref.shape[1]), ref.dtype)
    if T > hi_start:
        ref[pl.ds(hi_start, T - hi_start), :] = jnp.zeros((T - hi_start, ref.shape[1]), ref.dtype)


def _colblock(T, off):
    return pl.BlockSpec((T, 128), lambda j: (0, off + j))


def _vecblock(rows=1):
    return pl.BlockSpec((rows, 128), lambda j: (0, j))


def _pool_lane_consts(n):
    lane = lax.broadcasted_iota(jnp.int32, (n, 256), 1)
    win = jnp.where(lane < 64, 2.0, jnp.where(lane < 128, 4.0, jnp.where(lane < 192, 8.0, 16.0)))
    return lane, win


def _pool_select(lane, s2, s4, s8, s16):
    return jnp.where(lane < 64, s2, jnp.where(lane < 128, s4, jnp.where(lane < 192, s8, s16)))


def _pool_mixed(xh, start):
    s2 = xh + _down(xh, 1)
    s4 = s2 + _down(s2, 2)
    s8 = s4 + _down(s4, 4)
    s16 = s8 + _down(s8, 8)
    n = xh.shape[0] - 16
    lane, _ = _pool_lane_consts(n + 16)
    _, win = _pool_lane_consts(n)
    t1 = (start - CH + 1 + _rows(n)).astype(F32)
    cnt = jnp.minimum(jnp.maximum(t1, 1.0), win)
    return _pool_select(lane, s2, s4, s8, s16)[16:] / cnt - xh[16:]


def _pool_fwd(p, wbd, scale, nreal, real_end, name):
    T = p.shape[0]

    def body(p_ref, w_ref, s_ref, y_ref):
        _zero_pad_rows(y_ref, CH, CH * (1 + nreal), T)

        def chunk(c, carry):
            start = pl.multiple_of(c * CH, CH)
            mixed = _pool_mixed(p_ref[pl.ds(start - 16, CH + 16), :], start)
            y = _dot(mixed, w_ref[...]) * s_ref[...]
            y_ref[pl.ds(start, CH), :] = jnp.where(start + _rows(CH) < real_end, y, 0.0)
            return carry

        lax.fori_loop(1, 1 + nreal, chunk, 0)

    return pl.pallas_call(
        body, name=name, grid=(1,), out_shape=jax.ShapeDtypeStruct((T, 256), F32),
        in_specs=[pl.BlockSpec((T, 256), lambda j: (0, 0)), pl.BlockSpec((256, 256), lambda j: (0, 0)),
                  pl.BlockSpec((1, 256), lambda j: (0, 0))],
        out_specs=pl.BlockSpec((T, 256), lambda j: (0, 0)), compiler_params=_cparams(("arbitrary",)),
    )(p, wbd, scale)


def _pool_bwd(p, wbd, scale, dy, nreal, real_end, name):
    T = p.shape[0]

    def body(p_ref, w_ref, s_ref, dy_ref, dp_ref, dw_ref, ds_ref):
        _zero_pad_rows(dp_ref, CH, CH * (1 + nreal), T)
        dw_ref[...] = jnp.zeros_like(dw_ref)
        ds_ref[...] = jnp.zeros_like(ds_ref)

        def chunk(c, carry):
            start = pl.multiple_of(c * CH, CH)
            mixed = _pool_mixed(p_ref[pl.ds(start - 16, CH + 16), :], start)
            ypre = _dot(mixed, w_ref[...])
            n = CH + 16
            dye = jnp.where(start + _rows(n) < real_end, dy_ref[pl.ds(start, n), :], 0.0)
            dys = dye * s_ref[...]
            ds_ref[...] += jnp.sum(dye[:CH] * ypre, axis=0, keepdims=True)
            dw_ref[...] += _dot(mixed, dys[:CH], "TN")
            dmix = _dot(dys, w_ref[...], "NT")
            lane, win = _pool_lane_consts(n)
            t1 = (start - CH + 1 + _rows(n)).astype(F32)
            z = dmix / jnp.minimum(jnp.maximum(t1, 1.0), win)
            r2 = z + _up(z, 1)
            r4 = r2 + _up(r2, 2)
            r8 = r4 + _up(r4, 4)
            r16 = r8 + _up(r8, 8)
            dp_ref[pl.ds(start, CH), :] = (_pool_select(lane, r2, r4, r8, r16) - dmix)[:CH]
            return carry

        lax.fori_loop(1, 1 + nreal, chunk, 0)

    full = lambda r, c: pl.BlockSpec((r, c), lambda j: (0, 0))
    return pl.pallas_call(
        body, name=name, grid=(1,),
        out_shape=(jax.ShapeDtypeStruct((T, 256), F32), jax.ShapeDtypeStruct((256, 256), F32),
                   jax.ShapeDtypeStruct((1, 256), F32)),
        in_specs=[full(T, 256), full(256, 256), full(1, 256), full(T, 256)],
        out_specs=(full(T, 256), full(256, 256), full(1, 256)), compiler_params=_cparams(("arbitrary",)),
    )(p, wbd, scale, dy)


def _hgrn_chunk(St, qr, fr, ir, gr, l0, l1, gn):
    rows = lax.broadcasted_iota(jnp.int32, (HG, HG), 0)
    cols = lax.broadcasted_iota(jnp.int32, (HG, HG), 1)
    causal = rows >= cols
    ltri = causal.astype(F32)
    lb = jax.nn.sigmoid(l0 - l1)
    sg = jax.nn.sigmoid(fr)
    logf = jnp.log(lb + (1.0 - lb) * sg)
    kk = (1.0 - lb) * (1.0 - sg)
    q = qr * jax.nn.sigmoid(qr)
    b = jnp.dot(ltri, logf, precision=lax.Precision.HIGHEST, preferred_element_type=F32)
    bl = jnp.sum(logf, axis=0, keepdims=True)
    bm = jnp.sum(jnp.where(_rows(HG) <= HG // 2, logf, 0.0), axis=0, keepdims=True)
    o = _dotf(q * jnp.exp(b), St, "NT")
    A = _dotf(q * jnp.exp(b - bm), kk * jnp.exp(bm - b), "NT")
    o = o + _dotf(jnp.where(causal, A, 0.0), ir)
    St_new = St * jnp.exp(bl) + _dotf(ir, kk * jnp.exp(bl - b), "TN")
    on = o * lax.rsqrt(jnp.mean(o * o, axis=-1, keepdims=True) + EPS) * gn
    return St_new, on * (gr * jax.nn.sigmoid(gr))


def _hgrn_specs(T):
    return [_colblock(T, 2), _colblock(T, 8), _colblock(T, 14), _colblock(T, 20), _vecblock(), _vecblock(),
            pl.BlockSpec((1, 128), lambda j: (0, 0))]


def _hgrn_fwd(p, l0, l1, gn, nreal, real_end, name):
    T = p.shape[0]
    nch = nreal * (CH // HG)

    def body(q_ref, f_ref, i_ref, g_ref, l0_ref, l1_ref, gn_ref, y_ref, s_ref):
        _zero_pad_rows(y_ref, CH, CH * (1 + nreal), T)

        def chunk(c, St):
            start = pl.multiple_of(CH + c * HG, HG)
            sl = pl.ds(start, HG)
            s_ref[0, c] = St
            St_new, y = _hgrn_chunk(St, q_ref[sl, :], f_ref[sl, :], i_ref[sl, :], g_ref[sl, :], l0_ref[...],
                                    l1_ref[...], gn_ref[...])
            y_ref[sl, :] = jnp.where(start + _rows(HG) < real_end, y, 0.0)
            return St_new

        lax.fori_loop(0, nch, chunk, jnp.zeros((128, 128), F32), unroll=2)

    return pl.pallas_call(
        body, name=name, grid=(6,),
        out_shape=(jax.ShapeDtypeStruct((T, 768), F32), jax.ShapeDtypeStruct((6, nch, 128, 128), F32)),
        in_specs=_hgrn_specs(T),
        out_specs=(_colblock(T, 0), pl.BlockSpec((1, nch, 128, 128), lambda j: (j, 0, 0, 0))),
        compiler_params=_cparams(("parallel",)),
    )(p, p, p, p, l0, l1, gn)


def _hgrn_bwd(p, l0, l1, gn, states, dy, nreal, real_end, name):
    T = p.shape[0]
    nch = nreal * (CH // HG)

    def body(q_ref, f_ref, i_ref, g_ref, l0_ref, l1_ref, gn_ref, s_ref, dy_ref,
             dq_ref, df_ref, di_ref, dg_ref, dl0_ref, dl1_ref, dgn_ref):
        for r in (dq_ref, df_ref, di_ref, dg_ref):
            _zero_pad_rows(r, CH, CH * (1 + nreal), T)

        def chunk(k, carry):
            dSt, a0, a1, agn = carry
            c = nch - 1 - k
            start = pl.multiple_of(CH + c * HG, HG)
            sl = pl.ds(start, HG)
            _, vjp = jax.vjp(_hgrn_chunk, s_ref[0, c], q_ref[sl, :], f_ref[sl, :], i_ref[sl, :], g_ref[sl, :],
                             l0_ref[...], l1_ref[...], gn_ref[...])
            dyc = jnp.where(start + _rows(HG) < real_end, dy_ref[sl, :], 0.0)
            dS, dq, df, di, dg, d0, d1, dgn = vjp((dSt, dyc))
            dq_ref[sl, :] = dq
            df_ref[sl, :] = df
            di_ref[sl, :] = di
            dg_ref[sl, :] = dg
            return dS, a0 + d0, a1 + d1, agn + dgn

        z = jnp.zeros((1, 128), F32)
        _, a0, a1, agn = lax.fori_loop(0, nch, chunk, (jnp.zeros((128, 128), F32), z, z, z), unroll=2)
        dl0_ref[...] = a0
        dl1_ref[...] = a1

        @pl.when(pl.program_id(0) == 0)
        def _():
            dgn_ref[...] = jnp.zeros_like(dgn_ref)

        dgn_ref[...] += agn

    big = jax.ShapeDtypeStruct((T, 768), F32)
    vec = jax.ShapeDtypeStruct((1, 768), F32)
    return pl.pallas_call(
        body, name=name, grid=(6,),
        out_shape=(big, big, big, big, vec, vec, jax.ShapeDtypeStruct((1, 128), F32)),
        in_specs=_hgrn_specs(T) + [pl.BlockSpec((1, nch, 128, 128), lambda j: (j, 0, 0, 0)), _colblock(T, 2)],
        out_specs=(_colblock(T, 0), _colblock(T, 0), _colblock(T, 0), _colblock(T, 0), _vecblock(), _vecblock(),
                   pl.BlockSpec((1, 128), lambda j: (0, 0))),
        compiler_params=_cparams(("arbitrary",), 60 * 2 ** 20),
    )(p, p, p, p, l0, l1, gn, states, dy)


def _glu(a, b):
    return a * jax.nn.sigmoid(b)


def _conv_post(cv, ln_g, ln_b):
    mu = jnp.mean(cv, axis=-1, keepdims=True)
    d = cv - mu
    var = jnp.mean(d * d, axis=-1, keepdims=True)
    un = d * lax.rsqrt(var + EPS) * ln_g + ln_b
    return un * jax.nn.sigmoid(un)


def _causal_conv(uh, w_ref, width, halo):
    acc = None
    for j in range(width):
        term = _down(uh, width - 1 - j) * w_ref[pl.ds(j, 1), :]
        acc = term if acc is None else acc + term
    return acc[halo:]


def _conf_fwd(p, cw, cb, lg, lb, nreal, real_end, name):
    T = p.shape[0]

    def body(a_ref, b_ref, w_ref, cb_ref, lg_ref, lb_ref, y_ref):
        _zero_pad_rows(y_ref, CH, CH * (1 + nreal), T)

        def chunk(c, carry):
            start = pl.multiple_of(c * CH, CH)
            ext = pl.ds(start - 32, CH + 32)
            cv = _causal_conv(_glu(a_ref[ext, :], b_ref[ext, :]), w_ref, CONV_W, 32) + cb_ref[...]
            y = _conv_post(cv, lg_ref[...], lb_ref[...])
            y_ref[pl.ds(start, CH), :] = jnp.where(start + _rows(CH) < real_end, y, 0.0)
            return carry

        lax.fori_loop(1, 1 + nreal, chunk, 0)

    return pl.pallas_call(
        body, name=name, grid=(4,), out_shape=jax.ShapeDtypeStruct((T, 512), F32),
        in_specs=[_colblock(T, 0), _colblock(T, 4), _vecblock(32), _vecblock(), _vecblock(), _vecblock()],
        out_specs=_colblock(T, 0), compiler_params=_cparams(("parallel",)),
    )(p, p, cw, cb, lg, lb)


def _conf_bwd(p, cw, cb, lg, lb, dy, nreal, real_end, name):
    T = p.shape[0]

    def body(a_ref, b_ref, w_ref, cb_ref, lg_ref, lb_ref, dy_ref, da_ref, db_ref, dw_ref, dcb_ref, dlg_ref, dlb_ref):
        _zero_pad_rows(da_ref, CH, CH * (1 + nreal), T)
        _zero_pad_rows(db_ref, CH, CH * (1 + nreal), T)
        for r in (dw_ref, dcb_ref, dlg_ref, dlb_ref):
            r[...] = jnp.zeros_like(r)

        def chunk(c, carry):
            start = pl.multiple_of(c * CH, CH)
            ext = pl.ds(start - 32, CH + 64)
            ue = _glu(a_ref[ext, :], b_ref[ext, :])
            cv = _causal_conv(ue, w_ref, CONV_W, 32) + cb_ref[...]
            dye = jnp.where(start + _rows(CH + 32) < real_end, dy_ref[pl.ds(start, CH + 32), :], 0.0)
            _, vjp_cur = jax.vjp(_conv_post, cv[:CH], lg_ref[...], lb_ref[...])
            dc_cur, dlg, dlb = vjp_cur(dye[:CH])
            _, vjp_halo = jax.vjp(_conv_post, cv[CH:], lg_ref[...], lb_ref[...])
            dce = jnp.concatenate([dc_cur, vjp_halo(dye[CH:])[0]], axis=0)
            dlg_ref[...] += dlg
            dlb_ref[...] += dlb
            dcb_ref[...] += jnp.sum(dc_cur, axis=0, keepdims=True)
            du = None
            for j in range(CONV_W):
                w_j = w_ref[pl.ds(j, 1), :]
                term = _up(dce, CONV_W - 1 - j)[:CH] * w_j
                du = term if du is None else du + term
                dw_ref[pl.ds(j, 1), :] += jnp.sum(dc_cur * _up(ue, 2 + j)[:CH], axis=0, keepdims=True)
            cur = pl.ds(start, CH)
            _, vjp_glu = jax.vjp(_glu, a_ref[cur, :], b_ref[cur, :])
            da, db = vjp_glu(du)
            da_ref[cur, :] = da
            db_ref[cur, :] = db
            return carry

        lax.fori_loop(1, 1 + nreal, chunk, 0)

    big = jax.ShapeDtypeStruct((T, 512), F32)
    vec = jax.ShapeDtypeStruct((1, 512), F32)
    return pl.pallas_call(
        body, name=name, grid=(4,), out_shape=(big, big, jax.ShapeDtypeStruct((32, 512), F32), vec, vec, vec),
        in_specs=[_colblock(T, 0), _colblock(T, 4), _vecblock(32), _vecblock(), _vecblock(), _vecblock(),
                  _colblock(T, 0)],
        out_specs=(_colblock(T, 0), _colblock(T, 0), _vecblock(32), _vecblock(), _vecblock(), _vecblock()),
        compiler_params=_cparams(("parallel",)),
    )(p, p, cw, cb, lg, lb, dy)


def _softplus_neg(lam):
    e = jnp.exp(-lam)
    small = e * (1.0 - e * (0.5 - e * (1.0 / 3.0 - e * 0.25)))
    return jnp.where(e < 0.02, small, jnp.log(1.0 + e))


def _one_minus_exp(x):
    series = -x * (1.0 + x * (0.5 + x * (1.0 / 6.0 + x * (1.0 / 24.0 + x * (1.0 / 120.0)))))
    return jnp.where(x > -0.05, series, 1.0 - jnp.exp(x))


def _lru_pre(u, wa, wx, ba, bx, lam, first):
    r = jax.nn.sigmoid(_dot(u, wa) + ba)
    i = jax.nn.sigmoid(_dot(u, wx) + bx)
    log_a = -LRU_C * r * _softplus_neg(lam)
    a = jnp.exp(log_a)
    mult = jnp.sqrt(_one_minus_exp(2.0 * log_a))
    return a, jnp.where(first, 1.0, mult) * (i * u)


def _gelu_gate(gate, h):
    inner = math.sqrt(2.0 / math.pi) * (gate + 0.044715 * (gate * gate * gate))
    return 0.5 * gate * (1.0 + jnp.tanh(inner)) * h


def _lru_specs(T):
    mat = pl.BlockSpec((1, 128, 128), lambda j: (j, 0, 0))
    return [_colblock(T, 8), _colblock(T, 12), _vecblock(8), _vecblock(), mat, mat, _vecblock(), _vecblock(),
            _vecblock()]


def _lru_fwd(p, cw, cb, wa, wx, ba, bx, lam, nreal, real_end, name):
    T = p.shape[0]

    def body(x_ref, g_ref, w_ref, cb_ref, wa_ref, wx_ref, ba_ref, bx_ref, lam_ref, y_ref, h_ref):
        _zero_pad_rows(y_ref, CH, CH * (1 + nreal), T)
        _zero_pad_rows(h_ref, CH, CH * (1 + nreal), T)
        rows = _rows(CH)

        def chunk(c, hprev):
            start = pl.multiple_of(c * CH, CH)
            u = _causal_conv(x_ref[pl.ds(start - 8, CH + 8), :], w_ref, LRU_W, 8) + cb_ref[...]
            A, B = _lru_pre(u, wa_ref[0], wx_ref[0], ba_ref[...], bx_ref[...], lam_ref[...], start + rows == CH)
            s = 1
            while s < CH:
                B = A * jnp.where(rows >= s, _down(B, s), 0.0) + B
                A = A * jnp.where(rows >= s, _down(A, s), 1.0)
                s *= 2
            h = B + A * hprev
            cur = pl.ds(start, CH)
            h_ref[cur, :] = h
            y_ref[cur, :] = jnp.where(start + rows < real_end, _gelu_gate(g_ref[cur, :], h), 0.0)
            return jnp.sum(jnp.where(rows == CH - 1, h, 0.0), axis=0, keepdims=True)

        lax.fori_loop(1, 1 + nreal, chunk, jnp.zeros((1, 128), F32))

    big = jax.ShapeDtypeStruct((T, 512), F32)
    return pl.pallas_call(
        body, name=name, grid=(4,), out_shape=(big, big), in_specs=_lru_specs(T),
        out_specs=(_colblock(T, 0), _colblock(T, 0)), compiler_params=_cparams(("parallel",)),
    )(p, p, cw, cb, wa, wx, ba, bx, lam)


def _lru_bwd(p, cw, cb, wa, wx, ba, bx, lam, hs, dy, nreal, real_end, name):
    T = p.shape[0]

    def body(x_ref, g_ref, w_ref, cb_ref, wa_ref, wx_ref, ba_ref, bx_ref, lam_ref, h_ref, dy_ref,
             dx_ref, dgate_ref, dw_ref, dcb_ref, dwa_ref, dwx_ref, dba_ref, dbx_ref, dlam_ref):
        _zero_pad_rows(dx_ref, CH, CH * (1 + nreal), T)
        _zero_pad_rows(dgate_ref, CH, CH * (1 + nreal), T)
        for r in (dw_ref, dcb_ref, dwa_ref, dwx_ref, dba_ref, dbx_ref, dlam_ref):
            r[...] = jnp.zeros_like(r)
        rows = _rows(CH)

        def chunk(k, carry):
            cdh, du_head = carry
            c = nreal - k
            start = pl.multiple_of(c * CH, CH)
            cur = pl.ds(start, CH)
            xe = x_ref[pl.ds(start - 8, CH + 8), :]
            u = _causal_conv(xe, w_ref, LRU_W, 8) + cb_ref[...]
            first = start + rows == CH
            (a, _), vjp_pre = jax.vjp(lambda uu, m1, m2, b1, b2, ll: _lru_pre(uu, m1, m2, b1, b2, ll, first),
                                      u, wa_ref[0], wx_ref[0], ba_ref[...], bx_ref[...], lam_ref[...])
            h = h_ref[cur, :]
            hm1 = _down(h_ref[pl.ds(start - 8, CH + 8), :], 1)[8:]
            _, vjp_post = jax.vjp(_gelu_gate, g_ref[cur, :], h)
            dgate, D = vjp_post(jnp.where(start + rows < real_end, dy_ref[cur, :], 0.0))
            dgate_ref[cur, :] = dgate
            D = D + jnp.where(rows == CH - 1, cdh, 0.0)
            C = jnp.where(rows < CH - 1, _up(a, 1), 0.0)
            s = 1
            while s < CH:
                D = D + C * jnp.where(rows + s < CH, _up(D, s), 0.0)
                C = C * jnp.where(rows + s < CH, _up(C, s), 1.0)
                s *= 2
            du, dwa, dwx, dba, dbx, dlam = vjp_pre((D * hm1, D))
            dwa_ref[0] += dwa
            dwx_ref[0] += dwx
            dba_ref[...] += dba
            dbx_ref[...] += dbx
            dlam_ref[...] += dlam
            dcb_ref[...] += jnp.sum(du, axis=0, keepdims=True)
            due = jnp.concatenate([du, du_head], axis=0)
            dx = None
            for j in range(LRU_W):
                term = _up(due, LRU_W - 1 - j)[:CH] * w_ref[pl.ds(j, 1), :]
                dx = term if dx is None else dx + term
                dw_ref[pl.ds(j, 1), :] += jnp.sum(du * _up(xe, 8 - (LRU_W - 1) + j)[:CH], axis=0, keepdims=True)
            dx_ref[cur, :] = dx
            return jnp.sum(jnp.where(rows == 0, a * D, 0.0), axis=0, keepdims=True), du[:8]

        lax.fori_loop(0, nreal, chunk, (jnp.zeros((1, 128), F32), jnp.zeros((8, 128), F32)))

    big = jax.ShapeDtypeStruct((T, 512), F32)
    vec = jax.ShapeDtypeStruct((1, 512), F32)
    mat = jax.ShapeDtypeStruct((4, 128, 128), F32)
    matspec = pl.BlockSpec((1, 128, 128), lambda j: (j, 0, 0))
    return pl.pallas_call(
        body, name=name, grid=(4,),
        out_shape=(big, big, jax.ShapeDtypeStruct((8, 512), F32), vec, mat, mat, vec, vec, vec),
        in_specs=_lru_specs(T) + [_colblock(T, 0), _colblock(T, 4)],
        out_specs=(_colblock(T, 0), _colblock(T, 0), _vecblock(8), _vecblock(), matspec, matspec, _vecblock(),
                   _vecblock(), _vecblock()),
        compiler_params=_cparams(("parallel",)),
    )(p, p, cw, cb, wa, wx, ba, bx, lam, hs, dy)


def _ffn_forward(h, gamma, wg, wu, wd, tag):
    xn = _rms_fwd(h, gamma, f"rms_fwd_{tag}")
    g, u, a = _ffn_up(xn, wg, wu, f"ffn_up_{tag}")
    out = _mm([(a, wd)], "NN", f"ffn_down_{tag}", res=h, res_scale=0.5)
    return out, (h, xn, g, u, a)


def _after(w, tok):
    return w if tok is None else w + tok.astype(w.dtype)


def _ffn_backward(saved, gamma, wg, wu, wd, dout, tok, tag):
    h, xn, g, u, a = saved
    wd = _after(wd, tok)
    dwd = _mm([(a, dout)], "TN", f"ffn_dwd_{tag}", res_scale=0.5)
    dg, du = _ffn_dact(dout, wd, g, u, 0.5, f"ffn_dact_{tag}")
    dwg = _mm([(xn, dg)], "TN", f"ffn_dwg_{tag}")
    dwu = _mm([(xn, du)], "TN", f"ffn_dwu_{tag}")
    dxn = _mm([(dg, wg), (du, wu)], "NT", f"ffn_dxn_{tag}")
    dh, dgamma = _rms_bwd(h, gamma, dxn, dout, f"rms_bwd_{tag}")
    return dh, dgamma, dwg, dwu, dwd


def _blockdiag(w, per):
    n, k, _ = w.shape
    out = jnp.zeros((n // per, per * k, per * k), w.dtype)
    for i in range(per):
        out = out.at[:, i * k:(i + 1) * k, i * k:(i + 1) * k].set(w[i::per])
    return out


def _blockdiag_grad(g, per, k):
    parts = [g[:, i * k:(i + 1) * k, i * k:(i + 1) * k] for i in range(per)]
    return jnp.stack(parts, axis=1).reshape(-1, k, k)


def _local_step(x, tgt, W, fetch, emit):
    fetch(0, x)
    seq, D = x.shape
    lr = N_META + seq
    nreal = -(-lr // CH)
    T = CH * (nreal + 2)
    if T > 640 and T % 640:
        T += 640 - T % 640
    lo, real_end = CH + N_META, CH + lr
    zf = lambda n: jnp.zeros((n, D), F32)
    h0 = jnp.concatenate([zf(CH), W['meta_tokens'], x, zf(T - real_end)], axis=0)
    tgt_p = jnp.concatenate([zf(lo), tgt, zf(T - real_end)], axis=0)
    row = lambda v: v.reshape(1, -1)
    G = {}

    h = h0
    saved = []
    for l in range(2):
        h, s1 = _ffn_forward(h, row(W['ffn1_norm'][l]), W['ffn1_wg', l], W['ffn1_wu', l], W['ffn1_wd', l], f"a{l}")
        hm = h
        fetch(3 * l + 1, hm)
        xn = _rms_fwd(hm, row(W['mix_norm'][l]), f"rms_fwd_mix{l}")
        if l == 0:
            p = _mm([(xn, W['w_in_even'])], "NN", "in_even")
            wbd = _blockdiag(W['pool_w'][0], 4)[0]
            l0, l1 = row(W['hgrn_lb_logits'][0]), row(W['hgrn_lb_logits'][1])
            ya = _pool_fwd(p, wbd, W['pool_scale'], nreal, real_end, "pool_fwd")
            yb, states = _hgrn_fwd(p, l0, l1, W['hgrn_gnorm'], nreal, real_end, "hgrn_fwd")
            wo = W['w_out_even']
            h = _mm([(ya, wo[:256]), (yb, wo[256:])], "NN", "out_even", res=hm)
            sm = (hm, xn, p, wbd, l0, l1, ya, yb, states)
        else:
            p = _mm([(xn, W['w_in_odd'])], "NN", "in_odd")
            cw = jnp.pad(W['conv_w'][0], ((0, 1), (0, 0)))
            lw = jnp.pad(W['lru_conv_w'][0], ((0, 4), (0, 0)))
            wa, wx = _blockdiag(W['lru_wa'][0], 2), _blockdiag(W['lru_wx'][0], 2)
            yc = _conf_fwd(p, cw, W['conv_b'], W['conv_ln_g'], W['conv_ln_b'], nreal, real_end, "conf_fwd")
            yd, hs = _lru_fwd(p, lw, W['lru_conv_b'], wa, wx, W['lru_ba'], W['lru_bx'], W['lru_lambda'], nreal,
                              real_end, "lru_fwd")
            wo = W['w_out_odd']
            h = _mm([(yc, wo[:512]), (yd, wo[512:])], "NN", "out_odd", res=hm)
            sm = (hm, xn, p, cw, lw, wa, wx, yc, yd, hs)
        fetch(3 * l + 2, h)
        h, s2 = _ffn_forward(h, row(W['ffn2_norm'][l]), W['ffn2_wg', l], W['ffn2_wu', l], W['ffn2_wd', l], f"b{l}")
        if l == 0:
            fetch(3, h)
        saved.append((s1, sm, s2))

    loss8, dh, dfin = _loss_head(h, row(W['final_norm']), tgt_p, lo, real_end, "loss_head")
    G['final_norm'] = dfin[0]

    per_layer = {k: [None, None] for k in ('ffn1_norm', 'mix_norm', 'ffn2_norm')}
    tok = None
    for l in (1, 0):
        s1, sm, s2 = saved[l]
        dh, dn, dwg, dwu, dwd = _ffn_backward(s2, row(W['ffn2_norm'][l]), W['ffn2_wg', l], W['ffn2_wu', l],
                                              W['ffn2_wd', l], dh, tok, f"b{l}")
        per_layer['ffn2_norm'][l] = dn[0]
        tok = emit(f"ffn2_{l}", [('ffn2_wg', l, dwg), ('ffn2_wu', l, dwu), ('ffn2_wd', l, dwd)])
        if l == 0:
            hm, xn, p, wbd, l0, l1, ya, yb, states = sm
            wo, wi = _after(W['w_out_even'], tok), W['w_in_even']
            dwo = jnp.concatenate([_mm([(ya, dh)], "TN", "dwo_even_a"), _mm([(yb, dh)], "TN", "dwo_even_b")], axis=0)
            dy = _mm([(dh, wo)], "NT", "dy_even")
            dpp, dwbd, dsc = _pool_bwd(p, wbd, W['pool_scale'], dy, nreal, real_end, "pool_bwd")
            dq, df, di, dg, dl0, dl1, dgn = _hgrn_bwd(p, l0, l1, W['hgrn_gnorm'], states, dy, nreal, real_end,
                                                      "hgrn_bwd")
            G['pool_w'] = _blockdiag_grad(dwbd[None], 4, 64)[None]
            G['pool_scale'] = dsc
            G['hgrn_lb_logits'] = jnp.concatenate([dl0, dl1], axis=0)
            G['hgrn_gnorm'] = dgn
            parts = [dpp, dq, df, di, dg]
            offs = [0, 256, 1024, 1792, 2560, 3328]
            dwi = jnp.concatenate(
                [_mm([(xn, dpart)], "TN", f"dwi_even_{k}") for k, dpart in enumerate(parts)], axis=1)
            dxn = _mm([(dpart, wi[:, offs[k]:offs[k + 1]]) for k, dpart in enumerate(parts)], "NT", "dxn_even")
            tok = emit("even", [('w_in_even', None, dwi), ('w_out_even', None, dwo)])
        else:
            hm, xn, p, cw, lw, wa, wx, yc, yd, hs = sm
            wo, wi = _after(W['w_out_odd'], tok), W['w_in_odd']
            dwo = jnp.concatenate([_mm([(yc, dh)], "TN", "dwo_odd_c"), _mm([(yd, dh)], "TN", "dwo_odd_d")], axis=0)
            dy = _mm([(dh, wo)], "NT", "dy_odd")
            da, db, dcw, dcb, dlg, dlb = _conf_bwd(p, cw, W['conv_b'], W['conv_ln_g'], W['conv_ln_b'], dy, nreal,
                                                   real_end, "conf_bwd")
            dx, dgate, dlw, dlcb, dwa, dwx, dba, dbx, dlam = _lru_bwd(
                p, lw, W['lru_conv_b'], wa, wx, W['lru_ba'], W['lru_bx'], W['lru_lambda'], hs, dy, nreal, real_end,
                "lru_bwd")
            G['conv_w'], G['conv_b'], G['conv_ln_g'], G['conv_ln_b'] = dcw[None, :CONV_W], dcb, dlg, dlb
            G['lru_conv_w'], G['lru_conv_b'] = dlw[None, :LRU_W], dlcb
            G['lru_wa'] = _blockdiag_grad(dwa, 2, 64)[None]
            G['lru_wx'] = _blockdiag_grad(dwx, 2, 64)[None]
            G['lru_ba'], G['lru_bx'], G['lru_lambda'] = dba, dbx, dlam
            parts = [da, db, dx, dgate]
            dwi = jnp.concatenate(
                [_mm([(xn, dpart)], "TN", f"dwi_odd_{k}") for k, dpart in enumerate(parts)], axis=1)
            dxn = _mm([(dpart, wi[:, 512 * k:512 * (k + 1)]) for k, dpart in enumerate(parts)], "NT", "dxn_odd")
            tok = emit("odd", [('w_in_odd', None, dwi), ('w_out_odd', None, dwo)])
        dh, dn = _rms_bwd(hm, _after(row(W['mix_norm'][l]), tok), dxn, dh, f"rms_bwd_mix{l}")
        per_layer['mix_norm'][l] = dn[0]
        dh, dn, dwg, dwu, dwd = _ffn_backward(s1, row(W['ffn1_norm'][l]), W['ffn1_wg', l], W['ffn1_wu', l],
                                              W['ffn1_wd', l], dh, None, f"a{l}")
        per_layer['ffn1_norm'][l] = dn[0]
        last = [('ffn1_wg', l, dwg), ('ffn1_wu', l, dwu), ('ffn1_wd', l, dwd)]
        if l == 1:
            tok = emit("ffn1_1", last)
    for k, v in per_layer.items():
        G[k] = jnp.stack(v, axis=0)
    G['meta_tokens'] = dh[CH:lo]
    return loss8[0, 0], dh[lo:real_end], G, last


def _pack(arrs):
    flat = jnp.concatenate([a.reshape(-1).astype(F32) for a in arrs])
    n = flat.shape[0]
    padded = -(-n // 1024) * 1024
    return jnp.pad(flat, (0, padded - n)).reshape(-1, 128)


def _unpack(packed, shapes):
    flat = packed.reshape(-1)
    out, off = [], 0
    for s in shapes:
        n = math.prod(s)
        out.append(flat[off:off + n].reshape(s))
        off += n
    return out


def _to_full(gathered, axis):
    s = gathered.shape[1:]
    return jnp.moveaxis(gathered, 0, axis).reshape(s[:axis] + (NDEV * s[axis],) + s[axis + 1:])


def _to_slots(full, axis):
    s = full.shape
    return jnp.moveaxis(full.reshape(s[:axis] + (NDEV, s[axis] // NDEV) + s[axis + 1:]), axis, 0)


def kernel(x, meta_tokens, ffn1_norm, ffn1_wg, ffn1_wu, ffn1_wd, mix_norm, ffn2_norm, ffn2_wg, ffn2_wu, ffn2_wd, w_in_even, pool_w, pool_scale, hgrn_lb_logits, hgrn_gnorm, w_out_even, w_in_odd, conv_w, conv_b, conv_ln_g, conv_ln_b, lru_conv_w, lru_conv_b, lru_wa, lru_ba, lru_wx, lru_bx, lru_lambda, w_out_odd, final_norm, loss_target, m_meta_tokens, m_ffn1_norm, m_ffn1_wg, m_ffn1_wu, m_ffn1_wd, m_mix_norm, m_ffn2_norm, m_ffn2_wg, m_ffn2_wu, m_ffn2_wd, m_w_in_even, m_pool_w, m_pool_scale, m_hgrn_lb_logits, m_hgrn_gnorm, m_w_out_even, m_w_in_odd, m_conv_w, m_conv_b, m_conv_ln_g, m_conv_ln_b, m_lru_conv_w, m_lru_conv_b, m_lru_wa, m_lru_ba, m_lru_wx, m_lru_bx, m_lru_lambda, m_w_out_odd, m_final_norm, v_meta_tokens, v_ffn1_norm, v_ffn1_wg, v_ffn1_wu, v_ffn1_wd, v_mix_norm, v_ffn2_norm, v_ffn2_wg, v_ffn2_wu, v_ffn2_wd, v_w_in_even, v_pool_w, v_pool_scale, v_hgrn_lb_logits, v_hgrn_gnorm, v_w_out_even, v_w_in_odd, v_conv_w, v_conv_b, v_conv_ln_g, v_conv_ln_b, v_lru_conv_w, v_lru_conv_b, v_lru_wa, v_lru_ba, v_lru_wx, v_lru_bx, v_lru_lambda, v_w_out_odd, v_final_norm):
    args = (meta_tokens, ffn1_norm, ffn1_wg, ffn1_wu, ffn1_wd, mix_norm, ffn2_norm, ffn2_wg, ffn2_wu, ffn2_wd, w_in_even, pool_w, pool_scale, hgrn_lb_logits, hgrn_gnorm, w_out_even, w_in_odd, conv_w, conv_b, conv_ln_g, conv_ln_b, lru_conv_w, lru_conv_b, lru_wa, lru_ba, lru_wx, lru_bx, lru_lambda, w_out_odd, final_norm)
    margs = (m_meta_tokens, m_ffn1_norm, m_ffn1_wg, m_ffn1_wu, m_ffn1_wd, m_mix_norm, m_ffn2_norm, m_ffn2_wg, m_ffn2_wu, m_ffn2_wd, m_w_in_even, m_pool_w, m_pool_scale, m_hgrn_lb_logits, m_hgrn_gnorm, m_w_out_even, m_w_in_odd, m_conv_w, m_conv_b, m_conv_ln_g, m_conv_ln_b, m_lru_conv_w, m_lru_conv_b, m_lru_wa, m_lru_ba, m_lru_wx, m_lru_bx, m_lru_lambda, m_w_out_odd, m_final_norm)
    vargs = (v_meta_tokens, v_ffn1_norm, v_ffn1_wg, v_ffn1_wu, v_ffn1_wd, v_mix_norm, v_ffn2_norm, v_ffn2_wg, v_ffn2_wu, v_ffn2_wd, v_w_in_even, v_pool_w, v_pool_scale, v_hgrn_lb_logits, v_hgrn_gnorm, v_w_out_even, v_w_in_odd, v_conv_w, v_conv_b, v_conv_ln_g, v_conv_ln_b, v_lru_conv_w, v_lru_conv_b, v_lru_wa, v_lru_ba, v_lru_wx, v_lru_bx, v_lru_lambda, v_w_out_odd, v_final_norm)
    Wl = dict(zip(W_NAMES, args))
    Ml = dict(zip(W_NAMES, margs))
    Vl = dict(zip(W_NAMES, vargs))

    small_shapes = [Wl[n].shape for n in SMALL_SHARDED]
    ffn = lambda p, l: [(p + s, l) for s in ('_wg', '_wu', '_wd')]
    mix = lambda p: [('w_in_' + p, None), ('w_out_' + p, None)]
    ggroups = [ffn('ffn1', 0), mix('even'), ffn('ffn2', 0), ffn('ffn1', 1), mix('odd'), ffn('ffn2', 1)]
    shard = lambda n, l: Wl[n][0 if l is None else l].astype(MXU)
    srcs = [[shard(n, l) for n, l in g] for g in ggroups]
    srcs[0] = [_pack([Wl[n] for n in SMALL_SHARDED])] + srcs[0]
    handles, _ = _exchange_start(srcs, True, "gather_start")
    W = {n: Wl[n] for n in REPLICATED}

    def fetch(k, after):
        lands = _exchange_wait(handles[k], True, after, f"gather_wait_{k}")
        if k == 0:
            per_dev = [_unpack(lands[0][d], small_shapes) for d in range(NDEV)]
            for j, n in enumerate(SMALL_SHARDED):
                W[n] = _to_full(jnp.stack([per_dev[d][j] for d in range(NDEV)]), SHARD_AXIS[n])
            lands = lands[1:]
        for (n, l), g in zip(ggroups[k], lands):
            W[n if l is None else (n, l)] = _to_full(g, SHARD_AXIS[n] - 1)

    pending = []

    def emit(tag, grads):
        slots = [_to_slots(g.astype(MXU), SHARD_AXIS[n] - 1) for n, _, g in grads]
        hs, token = _exchange_start([slots], False, f"scatter_start_{tag}")
        pending.append((tag, hs[0], [(n, l) for n, l, _ in grads]))
        return token[0, 0]

    loss_part, grad_x, G, last = _local_step(x[0], loss_target[0], W, fetch, emit)
    loss = lax.psum(loss_part, MESH_AXES)

    recv = {}
    for tag, handle, keys in pending:
        for key, r in zip(keys, _exchange_wait(handle, False, grad_x, f"scatter_wait_{tag}")):
            recv[key] = r
    send = [_to_slots(g.astype(MXU), SHARD_AXIS[n] - 1) for n, _, g in last]
    small_slots = [_to_slots(G[n].astype(F32), SHARD_AXIS[n]) for n in SMALL_SHARDED]
    send.append(jnp.stack([_pack([s[d] for s in small_slots]) for d in range(NDEV)]))
    send.append(_pack([G[n] for n in REPLICATED]))
    got = _exchange(send, [False] * (len(send) - 1) + [True], "scatter_last")
    for (n, l, _), r in zip(last, got):
        recv[n, l] = r

    outs = {}
    for n in BIG:
        shp = Wl[n].shape
        C = shp[-1]
        rs = [recv[n, None]] if shp[0] == 1 else [recv[n, l] for l in range(shp[0])]
        res = _adamw(rs, Wl[n].reshape(-1, C), Ml[n].reshape(-1, C), Vl[n].reshape(-1, C), f"adamw_{n}")
        outs[n] = [o.reshape(shp) for o in res]
    for names, r, tag in ((SMALL_SHARDED, got[-2], "small"), (REPLICATED, got[-1], "repl")):
        shapes = [Wl[n].shape for n in names]
        res = _adamw([r], _pack([Wl[n] for n in names]), _pack([Ml[n] for n in names]),
                     _pack([Vl[n] for n in names]), f"adamw_{tag}")
        unp = [_unpack(o, shapes) for o in res]
        for k, n in enumerate(names):
            outs[n] = [unp[j][k] for j in range(4)]

    result = [loss, grad_x[None]]
    for j in range(4):
        result += [outs[n][j] for n in W_NAMES]
    return tuple(result)
```

```python
import functools
import math

import jax
import jax.numpy as jnp
from jax import lax
from jax.experimental import pallas as pl
from jax.experimental.pallas import tpu as pltpu

F32 = jnp.float32
MXU = jnp.bfloat16
EPS = 1e-6
CH = 128
HG = 64
N_META = 16
CONV_W = 31
LRU_W = 4
LRU_C = 8.0
VMEM_LIMIT = 48 * 2 ** 20
ADAM_LR, ADAM_B1, ADAM_B2, ADAM_EPS, ADAM_WD, ADAM_STEP = 0.001, 0.9, 0.999, 1e-08, 0.01, 10
MESH_AXES = ("x", "y", "c")
NDEV = 8

W_NAMES = ['meta_tokens', 'ffn1_norm', 'ffn1_wg', 'ffn1_wu', 'ffn1_wd', 'mix_norm', 'ffn2_norm', 'ffn2_wg', 'ffn2_wu',
           'ffn2_wd', 'w_in_even', 'pool_w', 'pool_scale', 'hgrn_lb_logits', 'hgrn_gnorm', 'w_out_even', 'w_in_odd',
           'conv_w', 'conv_b', 'conv_ln_g', 'conv_ln_b', 'lru_conv_w', 'lru_conv_b', 'lru_wa', 'lru_ba', 'lru_wx',
           'lru_bx', 'lru_lambda', 'w_out_odd', 'final_norm']
SHARD_AXIS = {'meta_tokens': 1, 'ffn1_wg': 2, 'ffn1_wu': 2, 'ffn1_wd': 1, 'ffn2_wg': 2, 'ffn2_wu': 2, 'ffn2_wd': 1,
              'w_in_even': 2, 'w_out_even': 1, 'w_in_odd': 2, 'conv_w': 2, 'conv_b': 1, 'conv_ln_g': 1,
              'conv_ln_b': 1, 'lru_conv_w': 2, 'lru_conv_b': 1, 'lru_ba': 1, 'lru_bx': 1, 'lru_lambda': 1,
              'w_out_odd': 1}
BIG = ['ffn1_wg', 'ffn1_wu', 'ffn1_wd', 'ffn2_wg', 'ffn2_wu', 'ffn2_wd', 'w_in_even', 'w_out_even', 'w_in_odd',
       'w_out_odd']
SMALL_SHARDED = [n for n in W_NAMES if n in SHARD_AXIS and n not in BIG]
REPLICATED = [n for n in W_NAMES if n not in SHARD_AXIS]


def _cparams(sem=None, vmem=VMEM_LIMIT):
    return pltpu.CompilerParams(dimension_semantics=sem, vmem_limit_bytes=vmem)


def _tile(n):
    for c in (640, 512, 256, 128):
        if n % c == 0:
            return c
    return n


def _rowtile(n):
    for c in (256, 352, 128, 64, 32, 16, 8):
        if n % c == 0:
            return c
    return n


def _copies(srcs, lands, bcast, ssem, rsem, lsem):
    x, y, c = lax.axis_index("x"), lax.axis_index("y"), lax.axis_index("c")
    me = 4 * x + 2 * y + c
    locs, sends, recvs = [], [], []
    for a in range(len(srcs)):
        locs.append(pltpu.make_async_copy(srcs[a] if bcast[a] else srcs[a].at[me], lands[a].at[me], lsem.at[a]))
        for m in range(1, NDEV):
            px = 1 - x if (m >> 2) & 1 else x
            py = 1 - y if (m >> 1) & 1 else y
            pc = 1 - c if m & 1 else c
            peer = 4 * px + 2 * py + pc
            src = srcs[a] if bcast[a] else srcs[a].at[peer]
            k = a * NDEV + m
            for dst, out in ((lands[a].at[me], sends), (lands[a].at[peer], recvs)):
                out.append(pltpu.make_async_remote_copy(src_ref=src, dst_ref=dst, send_sem=ssem.at[k],
                                                        recv_sem=rsem.at[k], device_id=(px, py, pc),
                                                        device_id_type=pl.DeviceIdType.MESH))
    return locs, sends, recvs


def _land_shape(arr, bc):
    return (NDEV,) + tuple(arr.shape if bc else arr.shape[1:])


def _exchange(arrays, bcast, name):
    n = len(arrays)

    def body(*refs):
        locs, sends, recvs = _copies(refs[:n], refs[n:2 * n], bcast, refs[2 * n], refs[2 * n + 1], refs[2 * n + 2])
        for d in locs + sends:
            d.start()
        for r in recvs:
            r.wait_recv()
        for s in sends:
            s.wait_send()
        for loc in locs:
            loc.wait()

    out_shape = tuple(jax.ShapeDtypeStruct(_land_shape(arr, bc), arr.dtype) for arr, bc in zip(arrays, bcast))
    any_spec = pl.BlockSpec(memory_space=pl.ANY)
    return pl.pallas_call(
        body, name=name, out_shape=out_shape, in_specs=[any_spec] * n, out_specs=tuple([any_spec] * n),
        scratch_shapes=[pltpu.SemaphoreType.DMA((n * NDEV,)), pltpu.SemaphoreType.DMA((n * NDEV,)),
                        pltpu.SemaphoreType.DMA((n,))],
    )(*arrays)


_HBM = pl.BlockSpec(memory_space=pltpu.HBM)
_SEM = pl.BlockSpec(memory_space=pltpu.SEMAPHORE)
_EFFECT = pltpu.SideEffectType.DATAFLOW_SIDE_EFFECTING


def _exchange_start(groups, bcast, name):
    sizes = [len(g) for g in groups]
    srcs = [a for g in groups for a in g]
    n, ng = len(srcs), len(groups)
    lands = [lax.empty(_land_shape(a, bcast), a.dtype) for a in srcs]

    def body(*refs):
        off = 0
        for gi, sz in enumerate(sizes):
            sem = refs[2 * n + 3 * gi:2 * n + 3 * gi + 3]
            locs, sends, _ = _copies(refs[off:off + sz], refs[n + off:n + off + sz], [bcast] * sz, *sem)
            for d in locs + sends:
                d.start()
            off += sz
        refs[-1][...] = jnp.zeros((8, 128), F32)

    sems = []
    for sz in sizes:
        sems += [pltpu.SemaphoreType.DMA((sz * NDEV,)), pltpu.SemaphoreType.DMA((sz * NDEV,)),
                 pltpu.SemaphoreType.DMA((sz,))]
    thru = [pltpu.HBM(a.shape, a.dtype) for a in srcs + lands]
    outs = pl.pallas_call(
        body, name=name, out_shape=tuple(sems + thru + [jax.ShapeDtypeStruct((8, 128), F32)]),
        in_specs=[_HBM] * (2 * n),
        out_specs=tuple([_SEM] * (3 * ng) + [_HBM] * (2 * n) + [pl.BlockSpec(memory_space=pltpu.VMEM)]),
        input_output_aliases={i: 3 * ng + i for i in range(2 * n)},
        compiler_params=pltpu.CompilerParams(has_side_effects=_EFFECT),
    )(*[pltpu.with_memory_space_constraint(a, pltpu.HBM) for a in srcs + lands])
    handles, off = [], 0
    for gi, sz in enumerate(sizes):
        handles.append((outs[3 * gi:3 * gi + 3], outs[3 * ng + off:3 * ng + off + sz],
                        outs[3 * ng + n + off:3 * ng + n + off + sz]))
        off += sz
    return handles, outs[-1]


def _exchange_wait(handle, bcast, after, name):
    sems, srcs, lands = handle
    n = len(srcs)

    def body(*refs):
        locs, sends, recvs = _copies(refs[:n], refs[n:2 * n], [bcast] * n, *refs[2 * n:2 * n + 3])
        for r in recvs:
            r.wait_recv()
        for s in sends:
            s.wait_send()
        for loc in locs:
            loc.wait()

    outs = pl.pallas_call(
        body, name=name, out_shape=tuple(pltpu.HBM(a.shape, a.dtype) for a in list(srcs) + list(lands)),
        in_specs=[_HBM] * (2 * n) + [_SEM] * 3 + [pl.BlockSpec(memory_space=pl.ANY)],
        out_specs=tuple([_HBM] * (2 * n)), input_output_aliases={i: i for i in range(2 * n)},
        compiler_params=pltpu.CompilerParams(has_side_effects=_EFFECT),
    )(*srcs, *lands, *sems, after)
    return outs[n:]


def _adamw(recvs, w, m, v, name):
    R, C = w.shape
    nr = len(recvs)
    br = _rowtile(R // nr)
    nb0 = R // nr // br

    def body(*refs):
        w_ref, m_ref, v_ref, g_o, d_o, m_o, v_o = refs[nr:]
        g = None
        for j in range(nr):
            s = refs[j][0].astype(F32)
            for k in range(1, NDEV):
                s = s + refs[j][k].astype(F32)
            g = s if g is None else jnp.where(pl.program_id(0) >= j * nb0, s, g)
        mn = ADAM_B1 * m_ref[...] + (1.0 - ADAM_B1) * g
        vn = ADAM_B2 * v_ref[...] + (1.0 - ADAM_B2) * (g * g)
        m_hat = mn / (1.0 - ADAM_B1 ** ADAM_STEP)
        v_hat = vn / (1.0 - ADAM_B2 ** ADAM_STEP)
        g_o[...] = g
        d_o[...] = -ADAM_LR * (m_hat / (jnp.sqrt(v_hat) + ADAM_EPS) + ADAM_WD * w_ref[...])
        m_o[...] = mn
        v_o[...] = vn

    def rspec(j):
        return pl.BlockSpec((NDEV, br, C), lambda i: (0, jnp.clip(i - j * nb0, 0, nb0 - 1), 0))

    blk = pl.BlockSpec((br, C), lambda i: (i, 0))
    sds = jax.ShapeDtypeStruct((R, C), F32)
    return pl.pallas_call(
        body, name=name, grid=(R // br,), out_shape=(sds, sds, sds, sds),
        in_specs=[rspec(j) for j in range(nr)] + [blk, blk, blk], out_specs=(blk, blk, blk, blk),
        compiler_params=_cparams(("arbitrary",)),
    )(*recvs, w, m, v)


_DIMS = {"NN": ((1,), (0,)), "NT": ((1,), (1,)), "TN": ((0,), (0,))}


def _dot(a, b, mode="NN"):
    return lax.dot_general(a.astype(MXU), b.astype(MXU), (_DIMS[mode], ((), ())), preferred_element_type=F32)


def _dotf(a, b, mode="NN"):
    return lax.dot_general(a, b, (_DIMS[mode], ((), ())), precision=lax.Precision.HIGHEST,
                           preferred_element_type=F32)


def _mm(pairs, mode, name, res=None, res_scale=1.0, out_dtype=F32):
    a0, b0 = pairs[0]
    M = a0.shape[1] if mode == "TN" else a0.shape[0]
    N = b0.shape[0] if mode == "NT" else b0.shape[1]
    tm, tn = _tile(M), _tile(N)
    npairs = len(pairs)

    def body(*refs):
        acc = None
        for p in range(npairs):
            d = _dot(refs[2 * p][...], refs[2 * p + 1][...], mode)
            acc = d if acc is None else acc + d
        if res_scale != 1.0:
            acc = res_scale * acc
        if res is not None:
            acc = refs[2 * npairs][...] + acc
        refs[-1][...] = acc.astype(out_dtype)

    in_specs, args = [], []
    for a, b in pairs:
        if mode == "TN":
            in_specs.append(pl.BlockSpec((a.shape[0], tm), lambda i, j: (0, i)))
        else:
            in_specs.append(pl.BlockSpec((tm, a.shape[1]), lambda i, j: (i, 0)))
        if mode == "NT":
            in_specs.append(pl.BlockSpec((tn, b.shape[1]), lambda i, j: (j, 0)))
        else:
            in_specs.append(pl.BlockSpec((b.shape[0], tn), lambda i, j: (0, j)))
        args += [a, b]
    if res is not None:
        in_specs.append(pl.BlockSpec((tm, tn), lambda i, j: (i, j)))
        args.append(res)
    return pl.pallas_call(
        body, name=name, grid=(M // tm, N // tn), out_shape=jax.ShapeDtypeStruct((M, N), out_dtype),
        in_specs=in_specs, out_specs=pl.BlockSpec((tm, tn), lambda i, j: (i, j)),
        compiler_params=_cparams(("parallel", "parallel")),
    )(*args)


def _rms_fwd(h, gamma, name):
    T, D = h.shape
    tm = _tile(T)

    def body(h_ref, g_ref, o_ref):
        x = h_ref[...]
        r = lax.rsqrt(jnp.mean(x * x, axis=-1, keepdims=True) + EPS)
        o_ref[...] = (x * r * g_ref[...]).astype(MXU)

    return pl.pallas_call(
        body, name=name, grid=(T // tm,), out_shape=jax.ShapeDtypeStruct((T, D), MXU),
        in_specs=[pl.BlockSpec((tm, D), lambda i: (i, 0)), pl.BlockSpec((1, D), lambda i: (0, 0))],
        out_specs=pl.BlockSpec((tm, D), lambda i: (i, 0)), compiler_params=_cparams(("parallel",)),
    )(h, gamma)


def _rms_bwd_math(x, gamma, dy):
    r = lax.rsqrt(jnp.mean(x * x, axis=-1, keepdims=True) + EPS)
    z = dy * gamma
    dx = r * z - x * (r * r * r) * jnp.mean(z * x, axis=-1, keepdims=True)
    dgamma = jnp.sum(dy * x * r, axis=0, keepdims=True)
    return dx, dgamma


def _rms_bwd(h, gamma, dxn, dres, name):
    T, D = h.shape
    tm = _tile(T)

    def body(h_ref, g_ref, dxn_ref, dres_ref, dh_ref, dg_ref):
        dx, dgamma = _rms_bwd_math(h_ref[...], g_ref[...], dxn_ref[...])
        dh_ref[...] = dres_ref[...] + dx

        @pl.when(pl.program_id(0) == 0)
        def _():
            dg_ref[...] = jnp.zeros_like(dg_ref)

        dg_ref[...] += dgamma

    row = pl.BlockSpec((tm, D), lambda i: (i, 0))
    vec = pl.BlockSpec((1, D), lambda i: (0, 0))
    return pl.pallas_call(
        body, name=name, grid=(T // tm,),
        out_shape=(jax.ShapeDtypeStruct((T, D), F32), jax.ShapeDtypeStruct((1, D), F32)),
        in_specs=[row, vec, row, row], out_specs=(row, vec), compiler_params=_cparams(("arbitrary",)),
    )(h, gamma, dxn, dres)


def _loss_head(h, gamma, tgt, lo, hi, name):
    T, D = h.shape
    tm = _tile(T)

    def body(h_ref, g_ref, t_ref, loss_ref, dh_ref, dg_ref):
        i = pl.program_id(0)
        x = h_ref[...]
        r = lax.rsqrt(jnp.mean(x * x, axis=-1, keepdims=True) + EPS)
        y = x * r * g_ref[...]
        rows = i * tm + lax.broadcasted_iota(jnp.int32, (tm, 1), 0)
        valid = jnp.logical_and(rows >= lo, rows < hi)
        diff = jnp.where(valid, y - t_ref[...], 0.0)
        part = 0.5 * jnp.sum(jnp.sum(diff * diff, axis=-1, keepdims=True) / D, axis=0, keepdims=True)
        dx, dgamma = _rms_bwd_math(x, g_ref[...], diff / D)
        dh_ref[...] = dx

        @pl.when(i == 0)
        def _():
            dg_ref[...] = jnp.zeros_like(dg_ref)
            loss_ref[...] = jnp.zeros_like(loss_ref)

        dg_ref[...] += dgamma
        loss_ref[...] += jnp.broadcast_to(part, loss_ref.shape)

    row = pl.BlockSpec((tm, D), lambda i: (i, 0))
    vec = pl.BlockSpec((1, D), lambda i: (0, 0))
    lsp = pl.BlockSpec((8, 128), lambda i: (0, 0))
    return pl.pallas_call(
        body, name=name, grid=(T // tm,),
        out_shape=(jax.ShapeDtypeStruct((8, 128), F32), jax.ShapeDtypeStruct((T, D), F32),
                   jax.ShapeDtypeStruct((1, D), F32)),
        in_specs=[row, vec, row], out_specs=(lsp, row, vec), compiler_params=_cparams(("arbitrary",)),
    )(h, gamma, tgt)


def _ffn_up(xn, wg, wu, name):
    T, D = xn.shape
    Fd = wg.shape[1]
    tm, tn = _tile(T), _tile(Fd)

    def body(x_ref, wg_ref, wu_ref, g_ref, u_ref, a_ref):
        x = x_ref[...]
        g = _dot(x, wg_ref[...])
        u = _dot(x, wu_ref[...])
        g_ref[...] = g.astype(MXU)
        u_ref[...] = u.astype(MXU)
        a_ref[...] = (g * jax.nn.sigmoid(g) * u).astype(MXU)

    wsp = pl.BlockSpec((D, tn), lambda i, j: (0, j))
    osp = pl.BlockSpec((tm, tn), lambda i, j: (i, j))
    sds = jax.ShapeDtypeStruct((T, Fd), MXU)
    return pl.pallas_call(
        body, name=name, grid=(T // tm, Fd // tn), out_shape=(sds, sds, sds),
        in_specs=[pl.BlockSpec((tm, D), lambda i, j: (i, 0)), wsp, wsp], out_specs=(osp, osp, osp),
        compiler_params=_cparams(("parallel", "parallel")),
    )(xn, wg, wu)


def _ffn_dact(dy, wd, g, u, scale, name):
    T, D = dy.shape
    Fd = wd.shape[0]
    tm, tn = _tile(T), _tile(Fd)

    def body(dy_ref, wd_ref, g_ref, u_ref, dg_ref, du_ref):
        da = scale * _dot(dy_ref[...], wd_ref[...], "NT")
        gg = g_ref[...].astype(F32)
        uu = u_ref[...].astype(F32)
        sg = jax.nn.sigmoid(gg)
        dg_ref[...] = (da * uu * (sg * (1.0 + gg * (1.0 - sg)))).astype(MXU)
        du_ref[...] = (da * gg * sg).astype(MXU)

    osp = pl.BlockSpec((tm, tn), lambda i, j: (i, j))
    sds = jax.ShapeDtypeStruct((T, Fd), MXU)
    return pl.pallas_call(
        body, name=name, grid=(T // tm, Fd // tn), out_shape=(sds, sds),
        in_specs=[pl.BlockSpec((tm, D), lambda i, j: (i, 0)), pl.BlockSpec((tn, D), lambda i, j: (j, 0)), osp, osp],
        out_specs=(osp, osp), compiler_params=_cparams(("parallel", "parallel")),
    )(dy, wd, g, u)


def _down(v, s):
    return v if s == 0 else pltpu.roll(v, s, 0)


def _up(v, s):
    return v if s == 0 else pltpu.roll(v, v.shape[0] - s, 0)


def _rows(n):
    return lax.broadcasted_iota(jnp.int32, (n, 1), 0)


def _zero_pad_rows(ref, lo_end, hi_start, T):
    ref[pl.ds(0, lo_end), :] = jnp.zeros((lo_end, ref.shape[1]), ref.dtype)
    if T > hi_start:
        ref[pl.ds(hi_start, T - hi_start), :] = jnp.zeros((T - hi_start, ref.shape[1]), ref.dtype)


def _colblock(T, off):
    return pl.BlockSpec((T, 128), lambda j: (0, off + j))


def _vecblock(rows=1):
    return pl.BlockSpec((rows, 128), lambda j: (0, j))


def _pool_lane_consts(n):
    lane = lax.broadcasted_iota(jnp.int32, (n, 256), 1)
    win = jnp.where(lane < 64, 2.0, jnp.where(lane < 128, 4.0, jnp.where(lane < 192, 8.0, 16.0)))
    return lane, win


def _pool_select(lane, s2, s4, s8, s16):
    return jnp.where(lane < 64, s2, jnp.where(lane < 128, s4, jnp.where(lane < 192, s8, s16)))


def _pool_mixed(xh, start):
    s2 = xh + _down(xh, 1)
    s4 = s2 + _down(s2, 2)
    s8 = s4 + _down(s4, 4)
    s16 = s8 + _down(s8, 8)
    n = xh.shape[0] - 16
    lane, _ = _pool_lane_consts(n + 16)
    _, win = _pool_lane_consts(n)
    t1 = (start - CH + 1 + _rows(n)).astype(F32)
    cnt = jnp.minimum(jnp.maximum(t1, 1.0), win)
    return _pool_select(lane, s2, s4, s8, s16)[16:] / cnt - xh[16:]


def _pool_fwd(p, wbd, scale, nreal, real_end, name):
    T = p.shape[0]

    def body(p_ref, w_ref, s_ref, y_ref):
        _zero_pad_rows(y_ref, CH, CH * (1 + nreal), T)

        def chunk(c, carry):
            start = pl.multiple_of(c * CH, CH)
            mixed = _pool_mixed(p_ref[pl.ds(start - 16, CH + 16), :], start)
            y = _dot(mixed, w_ref[...]) * s_ref[...]
            y_ref[pl.ds(start, CH), :] = jnp.where(start + _rows(CH) < real_end, y, 0.0)
            return carry

        lax.fori_loop(1, 1 + nreal, chunk, 0)

    return pl.pallas_call(
        body, name=name, grid=(1,), out_shape=jax.ShapeDtypeStruct((T, 256), F32),
        in_specs=[pl.BlockSpec((T, 256), lambda j: (0, 0)), pl.BlockSpec((256, 256), lambda j: (0, 0)),
                  pl.BlockSpec((1, 256), lambda j: (0, 0))],
        out_specs=pl.BlockSpec((T, 256), lambda j: (0, 0)), compiler_params=_cparams(("arbitrary",)),
    )(p, wbd, scale)


def _pool_bwd(p, wbd, scale, dy, nreal, real_end, name):
    T = p.shape[0]

    def body(p_ref, w_ref, s_ref, dy_ref, dp_ref, dw_ref, ds_ref):
        _zero_pad_rows(dp_ref, CH, CH * (1 + nreal), T)
        dw_ref[...] = jnp.zeros_like(dw_ref)
        ds_ref[...] = jnp.zeros_like(ds_ref)

        def chunk(c, carry):
            start = pl.multiple_of(c * CH, CH)
            mixed = _pool_mixed(p_ref[pl.ds(start - 16, CH + 16), :], start)
            ypre = _dot(mixed, w_ref[...])
            n = CH + 16
            dye = jnp.where(start + _rows(n) < real_end, dy_ref[pl.ds(start, n), :], 0.0)
            dys = dye * s_ref[...]
            ds_ref[...] += jnp.sum(dye[:CH] * ypre, axis=0, keepdims=True)
            dw_ref[...] += _dot(mixed, dys[:CH], "TN")
            dmix = _dot(dys, w_ref[...], "NT")
            lane, win = _pool_lane_consts(n)
            t1 = (start - CH + 1 + _rows(n)).astype(F32)
            z = dmix / jnp.minimum(jnp.maximum(t1, 1.0), win)
            r2 = z + _up(z, 1)
            r4 = r2 + _up(r2, 2)
            r8 = r4 + _up(r4, 4)
            r16 = r8 + _up(r8, 8)
            dp_ref[pl.ds(start, CH), :] = (_pool_select(lane, r2, r4, r8, r16) - dmix)[:CH]
            return carry

        lax.fori_loop(1, 1 + nreal, chunk, 0)

    full = lambda r, c: pl.BlockSpec((r, c), lambda j: (0, 0))
    return pl.pallas_call(
        body, name=name, grid=(1,),
        out_shape=(jax.ShapeDtypeStruct((T, 256), F32), jax.ShapeDtypeStruct((256, 256), F32),
                   jax.ShapeDtypeStruct((1, 256), F32)),
        in_specs=[full(T, 256), full(256, 256), full(1, 256), full(T, 256)],
        out_specs=(full(T, 256), full(256, 256), full(1, 256)), compiler_params=_cparams(("arbitrary",)),
    )(p, wbd, scale, dy)


def _hgrn_chunk(St, qr, fr, ir, gr, l0, l1, gn):
    rows = lax.broadcasted_iota(jnp.int32, (HG, HG), 0)
    cols = lax.broadcasted_iota(jnp.int32, (HG, HG), 1)
    causal = rows >= cols
    ltri = causal.astype(F32)
    lb = jax.nn.sigmoid(l0 - l1)
    sg = jax.nn.sigmoid(fr)
    logf = jnp.log(lb + (1.0 - lb) * sg)
    kk = (1.0 - lb) * (1.0 - sg)
    q = qr * jax.nn.sigmoid(qr)
    b = jnp.dot(ltri, logf, precision=lax.Precision.HIGHEST, preferred_element_type=F32)
    bl = jnp.sum(logf, axis=0, keepdims=True)
    bm = jnp.sum(jnp.where(_rows(HG) <= HG // 2, logf, 0.0), axis=0, keepdims=True)
    o = _dotf(q * jnp.exp(b), St, "NT")
    A = _dotf(q * jnp.exp(b - bm), kk * jnp.exp(bm - b), "NT")
    o = o + _dotf(jnp.where(causal, A, 0.0), ir)
    St_new = St * jnp.exp(bl) + _dotf(ir, kk * jnp.exp(bl - b), "TN")
    on = o * lax.rsqrt(jnp.mean(o * o, axis=-1, keepdims=True) + EPS) * gn
    return St_new, on * (gr * jax.nn.sigmoid(gr))


def _pairs_loop(n, step, init):
    return lax.fori_loop(0, n // 2, lambda i, carry: step(2 * i + 1, step(2 * i, carry)), init)


def _hgrn_specs(T):
    return [_colblock(T, 2), _colblock(T, 8), _colblock(T, 14), _colblock(T, 20), _vecblock(), _vecblock(),
            pl.BlockSpec((1, 128), lambda j: (0, 0))]


def _hgrn_fwd(p, l0, l1, gn, nreal, real_end, name):
    T = p.shape[0]
    nch = nreal * (CH // HG)

    def body(q_ref, f_ref, i_ref, g_ref, l0_ref, l1_ref, gn_ref, y_ref, s_ref):
        _zero_pad_rows(y_ref, CH, CH * (1 + nreal), T)

        def chunk(c, St):
            start = pl.multiple_of(CH + c * HG, HG)
            sl = pl.ds(start, HG)
            s_ref[0, c] = St
            St_new, y = _hgrn_chunk(St, q_ref[sl, :], f_ref[sl, :], i_ref[sl, :], g_ref[sl, :], l0_ref[...],
                                    l1_ref[...], gn_ref[...])
            y_ref[sl, :] = jnp.where(start + _rows(HG) < real_end, y, 0.0)
            return St_new

        _pairs_loop(nch, chunk, jnp.zeros((128, 128), F32))

    return pl.pallas_call(
        body, name=name, grid=(6,),
        out_shape=(jax.ShapeDtypeStruct((T, 768), F32), jax.ShapeDtypeStruct((6, nch, 128, 128), F32)),
        in_specs=_hgrn_specs(T),
        out_specs=(_colblock(T, 0), pl.BlockSpec((1, nch, 128, 128), lambda j: (j, 0, 0, 0))),
        compiler_params=_cparams(("parallel",)),
    )(p, p, p, p, l0, l1, gn)


def _hgrn_bwd(p, l0, l1, gn, states, dy, nreal, real_end, name):
    T = p.shape[0]
    nch = nreal * (CH // HG)

    def body(q_ref, f_ref, i_ref, g_ref, l0_ref, l1_ref, gn_ref, s_ref, dy_ref,
             dq_ref, df_ref, di_ref, dg_ref, dl0_ref, dl1_ref, dgn_ref):
        for r in (dq_ref, df_ref, di_ref, dg_ref):
            _zero_pad_rows(r, CH, CH * (1 + nreal), T)

        def chunk(k, carry):
            dSt, a0, a1, agn = carry
            c = nch - 1 - k
            start = pl.multiple_of(CH + c * HG, HG)
            sl = pl.ds(start, HG)
            _, vjp = jax.vjp(_hgrn_chunk, s_ref[0, c], q_ref[sl, :], f_ref[sl, :], i_ref[sl, :], g_ref[sl, :],
                             l0_ref[...], l1_ref[...], gn_ref[...])
            dyc = jnp.where(start + _rows(HG) < real_end, dy_ref[sl, :], 0.0)
            dS, dq, df, di, dg, d0, d1, dgn = vjp((dSt, dyc))
            dq_ref[sl, :] = dq
            df_ref[sl, :] = df
            di_ref[sl, :] = di
            dg_ref[sl, :] = dg
            return dS, a0 + d0, a1 + d1, agn + dgn

        z = jnp.zeros((1, 128), F32)
        _, a0, a1, agn = _pairs_loop(nch, chunk, (jnp.zeros((128, 128), F32), z, z, z))
        dl0_ref[...] = a0
        dl1_ref[...] = a1

        @pl.when(pl.program_id(0) == 0)
        def _():
            dgn_ref[...] = jnp.zeros_like(dgn_ref)

        dgn_ref[...] += agn

    big = jax.ShapeDtypeStruct((T, 768), F32)
    vec = jax.ShapeDtypeStruct((1, 768), F32)
    return pl.pallas_call(
        body, name=name, grid=(6,),
        out_shape=(big, big, big, big, vec, vec, jax.ShapeDtypeStruct((1, 128), F32)),
        in_specs=_hgrn_specs(T) + [pl.BlockSpec((1, nch, 128, 128), lambda j: (j, 0, 0, 0)), _colblock(T, 2)],
        out_specs=(_colblock(T, 0), _colblock(T, 0), _colblock(T, 0), _colblock(T, 0), _vecblock(), _vecblock(),
                   pl.BlockSpec((1, 128), lambda j: (0, 0))),
        compiler_params=_cparams(("arbitrary",), 60 * 2 ** 20),
    )(p, p, p, p, l0, l1, gn, states, dy)


def _glu(a, b):
    return a * jax.nn.sigmoid(b)


def _conv_post(cv, ln_g, ln_b):
    mu = jnp.mean(cv, axis=-1, keepdims=True)
    d = cv - mu
    var = jnp.mean(d * d, axis=-1, keepdims=True)
    un = d * lax.rsqrt(var + EPS) * ln_g + ln_b
    return un * jax.nn.sigmoid(un)


def _causal_conv(uh, w_ref, width, halo):
    acc = None
    for j in range(width):
        term = _down(uh, width - 1 - j) * w_ref[pl.ds(j, 1), :]
        acc = term if acc is None else acc + term
    return acc[halo:]


def _conf_fwd(p, cw, cb, lg, lb, nreal, real_end, name):
    T = p.shape[0]

    def body(a_ref, b_ref, w_ref, cb_ref, lg_ref, lb_ref, y_ref):
        _zero_pad_rows(y_ref, CH, CH * (1 + nreal), T)

        def chunk(c, carry):
            start = pl.multiple_of(c * CH, CH)
            ext = pl.ds(start - 32, CH + 32)
            cv = _causal_conv(_glu(a_ref[ext, :], b_ref[ext, :]), w_ref, CONV_W, 32) + cb_ref[...]
            y = _conv_post(cv, lg_ref[...], lb_ref[...])
            y_ref[pl.ds(start, CH), :] = jnp.where(start + _rows(CH) < real_end, y, 0.0)
            return carry

        lax.fori_loop(1, 1 + nreal, chunk, 0)

    return pl.pallas_call(
        body, name=name, grid=(4,), out_shape=jax.ShapeDtypeStruct((T, 512), F32),
        in_specs=[_colblock(T, 0), _colblock(T, 4), _vecblock(32), _vecblock(), _vecblock(), _vecblock()],
        out_specs=_colblock(T, 0), compiler_params=_cparams(("parallel",)),
    )(p, p, cw, cb, lg, lb)


def _conf_bwd(p, cw, cb, lg, lb, dy, nreal, real_end, name):
    T = p.shape[0]

    def body(a_ref, b_ref, w_ref, cb_ref, lg_ref, lb_ref, dy_ref, da_ref, db_ref, dw_ref, dcb_ref, dlg_ref, dlb_ref):
        _zero_pad_rows(da_ref, CH, CH * (1 + nreal), T)
        _zero_pad_rows(db_ref, CH, CH * (1 + nreal), T)
        for r in (dw_ref, dcb_ref, dlg_ref, dlb_ref):
            r[...] = jnp.zeros_like(r)

        def chunk(c, carry):
            start = pl.multiple_of(c * CH, CH)
            ext = pl.ds(start - 32, CH + 64)
            ue = _glu(a_ref[ext, :], b_ref[ext, :])
            cv = _causal_conv(ue, w_ref, CONV_W, 32) + cb_ref[...]
            dye = jnp.where(start + _rows(CH + 32) < real_end, dy_ref[pl.ds(start, CH + 32), :], 0.0)
            _, vjp_cur = jax.vjp(_conv_post, cv[:CH], lg_ref[...], lb_ref[...])
            dc_cur, dlg, dlb = vjp_cur(dye[:CH])
            _, vjp_halo = jax.vjp(_conv_post, cv[CH:], lg_ref[...], lb_ref[...])
            dce = jnp.concatenate([dc_cur, vjp_halo(dye[CH:])[0]], axis=0)
            dlg_ref[...] += dlg
            dlb_ref[...] += dlb
            dcb_ref[...] += jnp.sum(dc_cur, axis=0, keepdims=True)
            du = None
            for j in range(CONV_W):
                w_j = w_ref[pl.ds(j, 1), :]
                term = _up(dce, CONV_W - 1 - j)[:CH] * w_j
                du = term if du is None else du + term
                dw_ref[pl.ds(j, 1), :] += jnp.sum(dc_cur * _up(ue, 2 + j)[:CH], axis=0, keepdims=True)
            cur = pl.ds(start, CH)
            _, vjp_glu = jax.vjp(_glu, a_ref[cur, :], b_ref[cur, :])
            da, db = vjp_glu(du)
            da_ref[cur, :] = da
            db_ref[cur, :] = db
            return carry

        lax.fori_loop(1, 1 + nreal, chunk, 0)

    big = jax.ShapeDtypeStruct((T, 512), F32)
    vec = jax.ShapeDtypeStruct((1, 512), F32)
    return pl.pallas_call(
        body, name=name, grid=(4,), out_shape=(big, big, jax.ShapeDtypeStruct((32, 512), F32), vec, vec, vec),
        in_specs=[_colblock(T, 0), _colblock(T, 4), _vecblock(32), _vecblock(), _vecblock(), _vecblock(),
                  _colblock(T, 0)],
        out_specs=(_colblock(T, 0), _colblock(T, 0), _vecblock(32), _vecblock(), _vecblock(), _vecblock()),
        compiler_params=_cparams(("parallel",)),
    )(p, p, cw, cb, lg, lb, dy)


def _softplus_neg(lam):
    e = jnp.exp(-lam)
    small = e * (1.0 - e * (0.5 - e * (1.0 / 3.0 - e * 0.25)))
    return jnp.where(e < 0.02, small, jnp.log(1.0 + e))


def _one_minus_exp(x):
    series = -x * (1.0 + x * (0.5 + x * (1.0 / 6.0 + x * (1.0 / 24.0 + x * (1.0 / 120.0)))))
    return jnp.where(x > -0.05, series, 1.0 - jnp.exp(x))


def _lru_pre(u, wa, wx, ba, bx, lam, first):
    r = jax.nn.sigmoid(_dot(u, wa) + ba)
    i = jax.nn.sigmoid(_dot(u, wx) + bx)
    log_a = -LRU_C * r * _softplus_neg(lam)
    a = jnp.exp(log_a)
    mult = jnp.sqrt(_one_minus_exp(2.0 * log_a))
    return a, jnp.where(first, 1.0, mult) * (i * u)


def _gelu_gate(gate, h):
    inner = math.sqrt(2.0 / math.pi) * (gate + 0.044715 * (gate * gate * gate))
    return 0.5 * gate * (1.0 + jnp.tanh(inner)) * h


def _lru_specs(T):
    mat = pl.BlockSpec((1, 128, 128), lambda j: (j, 0, 0))
    return [_colblock(T, 8), _colblock(T, 12), _vecblock(8), _vecblock(), mat, mat, _vecblock(), _vecblock(),
            _vecblock()]


def _lru_fwd(p, cw, cb, wa, wx, ba, bx, lam, nreal, real_end, name):
    T = p.shape[0]

    def body(x_ref, g_ref, w_ref, cb_ref, wa_ref, wx_ref, ba_ref, bx_ref, lam_ref, y_ref, h_ref):
        _zero_pad_rows(y_ref, CH, CH * (1 + nreal), T)
        _zero_pad_rows(h_ref, CH, CH * (1 + nreal), T)
        rows = _rows(CH)

        def chunk(c, hprev):
            start = pl.multiple_of(c * CH, CH)
            u = _causal_conv(x_ref[pl.ds(start - 8, CH + 8), :], w_ref, LRU_W, 8) + cb_ref[...]
            A, B = _lru_pre(u, wa_ref[0], wx_ref[0], ba_ref[...], bx_ref[...], lam_ref[...], start + rows == CH)
            s = 1
            while s < CH:
                B = A * jnp.where(rows >= s, _down(B, s), 0.0) + B
                A = A * jnp.where(rows >= s, _down(A, s), 1.0)
                s *= 2
            h = B + A * hprev
            cur = pl.ds(start, CH)
            h_ref[cur, :] = h
            y_ref[cur, :] = jnp.where(start + rows < real_end, _gelu_gate(g_ref[cur, :], h), 0.0)
            return jnp.sum(jnp.where(rows == CH - 1, h, 0.0), axis=0, keepdims=True)

        lax.fori_loop(1, 1 + nreal, chunk, jnp.zeros((1, 128), F32))

    big = jax.ShapeDtypeStruct((T, 512), F32)
    return pl.pallas_call(
        body, name=name, grid=(4,), out_shape=(big, big), in_specs=_lru_specs(T),
        out_specs=(_colblock(T, 0), _colblock(T, 0)), compiler_params=_cparams(("parallel",)),
    )(p, p, cw, cb, wa, wx, ba, bx, lam)


def _lru_bwd(p, cw, cb, wa, wx, ba, bx, lam, hs, dy, nreal, real_end, name):
    T = p.shape[0]

    def body(x_ref, g_ref, w_ref, cb_ref, wa_ref, wx_ref, ba_ref, bx_ref, lam_ref, h_ref, dy_ref,
             dx_ref, dgate_ref, dw_ref, dcb_ref, dwa_ref, dwx_ref, dba_ref, dbx_ref, dlam_ref):
        _zero_pad_rows(dx_ref, CH, CH * (1 + nreal), T)
        _zero_pad_rows(dgate_ref, CH, CH * (1 + nreal), T)
        for r in (dw_ref, dcb_ref, dwa_ref, dwx_ref, dba_ref, dbx_ref, dlam_ref):
            r[...] = jnp.zeros_like(r)
        rows = _rows(CH)

        def chunk(k, carry):
            cdh, du_head = carry
            c = nreal - k
            start = pl.multiple_of(c * CH, CH)
            cur = pl.ds(start, CH)
            xe = x_ref[pl.ds(start - 8, CH + 8), :]
            u = _causal_conv(xe, w_ref, LRU_W, 8) + cb_ref[...]
            first = start + rows == CH
            (a, _), vjp_pre = jax.vjp(lambda uu, m1, m2, b1, b2, ll: _lru_pre(uu, m1, m2, b1, b2, ll, first),
                                      u, wa_ref[0], wx_ref[0], ba_ref[...], bx_ref[...], lam_ref[...])
            h = h_ref[cur, :]
            hm1 = _down(h_ref[pl.ds(start - 8, CH + 8), :], 1)[8:]
            _, vjp_post = jax.vjp(_gelu_gate, g_ref[cur, :], h)
            dgate, D = vjp_post(jnp.where(start + rows < real_end, dy_ref[cur, :], 0.0))
            dgate_ref[cur, :] = dgate
            D = D + jnp.where(rows == CH - 1, cdh, 0.0)
            C = jnp.where(rows < CH - 1, _up(a, 1), 0.0)
            s = 1
            while s < CH:
                D = D + C * jnp.where(rows + s < CH, _up(D, s), 0.0)
                C = C * jnp.where(rows + s < CH, _up(C, s), 1.0)
                s *= 2
            du, dwa, dwx, dba, dbx, dlam = vjp_pre((D * hm1, D))
            dwa_ref[0] += dwa
            dwx_ref[0] += dwx
            dba_ref[...] += dba
            dbx_ref[...] += dbx
            dlam_ref[...] += dlam
            dcb_ref[...] += jnp.sum(du, axis=0, keepdims=True)
            due = jnp.concatenate([du, du_head], axis=0)
            dx = None
            for j in range(LRU_W):
                term = _up(due, LRU_W - 1 - j)[:CH] * w_ref[pl.ds(j, 1), :]
                dx = term if dx is None else dx + term
                dw_ref[pl.ds(j, 1), :] += jnp.sum(du * _up(xe, 8 - (LRU_W - 1) + j)[:CH], axis=0, keepdims=True)
            dx_ref[cur, :] = dx
            return jnp.sum(jnp.where(rows == 0, a * D, 0.0), axis=0, keepdims=True), du[:8]

        lax.fori_loop(0, nreal, chunk, (jnp.zeros((1, 128), F32), jnp.zeros((8, 128), F32)))

    big = jax.ShapeDtypeStruct((T, 512), F32)
    vec = jax.ShapeDtypeStruct((1, 512), F32)
    mat = jax.ShapeDtypeStruct((4, 128, 128), F32)
    matspec = pl.BlockSpec((1, 128, 128), lambda j: (j, 0, 0))
    return pl.pallas_call(
        body, name=name, grid=(4,),
        out_shape=(big, big, jax.ShapeDtypeStruct((8, 512), F32), vec, mat, mat, vec, vec, vec),
        in_specs=_lru_specs(T) + [_colblock(T, 0), _colblock(T, 4)],
        out_specs=(_colblock(T, 0), _colblock(T, 0), _vecblock(8), _vecblock(), matspec, matspec, _vecblock(),
                   _vecblock(), _vecblock()),
        compiler_params=_cparams(("parallel",)),
    )(p, p, cw, cb, wa, wx, ba, bx, lam, hs, dy)


def _ffn_forward(h, gamma, wg, wu, wd, tag):
    xn = _rms_fwd(h, gamma, f"rms_fwd_{tag}")
    g, u, a = _ffn_up(xn, wg, wu, f"ffn_up_{tag}")
    if callable(wd):
        wd = wd(a)
    out = _mm([(a, wd)], "NN", f"ffn_down_{tag}", res=h, res_scale=0.5)
    return out, (h, xn, g, u, a)


def _after(w, tok):
    return w if tok is None else w + tok.astype(w.dtype)


def _ffn_backward(saved, gamma, wg, wu, wd, dout, tok, tag, emit_one=None):
    h, xn, g, u, a = saved
    wd = _after(wd, tok)
    dwd = _mm([(a, dout)], "TN", f"ffn_dwd_{tag}", res_scale=0.5)
    if emit_one is not None:
        wd = _after(wd, emit_one('wd', dwd))
    dg, du = _ffn_dact(dout, wd, g, u, 0.5, f"ffn_dact_{tag}")
    dwg = _mm([(xn, dg)], "TN", f"ffn_dwg_{tag}")
    if emit_one is not None:
        wg = _after(wg, emit_one('wg', dwg))
    dwu = _mm([(xn, du)], "TN", f"ffn_dwu_{tag}")
    if emit_one is not None:
        wu = _after(wu, emit_one('wu', dwu))
    dxn = _mm([(dg, wg), (du, wu)], "NT", f"ffn_dxn_{tag}")
    dh, dgamma = _rms_bwd(h, gamma, dxn, dout, f"rms_bwd_{tag}")
    return dh, dgamma, dwg, dwu, dwd


def _blockdiag(w, per):
    n, k, _ = w.shape
    out = jnp.zeros((n // per, per * k, per * k), w.dtype)
    for i in range(per):
        out = out.at[:, i * k:(i + 1) * k, i * k:(i + 1) * k].set(w[i::per])
    return out


def _blockdiag_grad(g, per, k):
    parts = [g[:, i * k:(i + 1) * k, i * k:(i + 1) * k] for i in range(per)]
    return jnp.stack(parts, axis=1).reshape(-1, k, k)


def _local_step(x, tgt, W, fetch, emit):
    fetch(0, x)
    seq, D = x.shape
    lr = N_META + seq
    nreal = -(-lr // CH)
    T = CH * (nreal + 2)
    if T > 640 and T % 640:
        T += 640 - T % 640
    lo, real_end = CH + N_META, CH + lr
    zf = lambda n: jnp.zeros((n, D), F32)
    h0 = jnp.concatenate([zf(CH), W['meta_tokens'], x, zf(T - real_end)], axis=0)
    tgt_p = jnp.concatenate([zf(lo), tgt, zf(T - real_end)], axis=0)
    row = lambda v: v.reshape(1, -1)
    G = {}

    h = h0
    saved = []
    for l in range(2):
        wd1 = W['ffn1_wd', l] if l else (lambda after: (fetch(1, after), W['ffn1_wd', 0])[1])
        h, s1 = _ffn_forward(h, row(W['ffn1_norm'][l]), W['ffn1_wg', l], W['ffn1_wu', l], wd1, f"a{l}")
        hm = h
        fetch(3 * l + 2, hm)
        xn = _rms_fwd(hm, row(W['mix_norm'][l]), f"rms_fwd_mix{l}")
        if l == 0:
            p = _mm([(xn, W['w_in_even'])], "NN", "in_even")
            wbd = _blockdiag(W['pool_w'][0], 4)[0]
            l0, l1 = row(W['hgrn_lb_logits'][0]), row(W['hgrn_lb_logits'][1])
            ya = _pool_fwd(p, wbd, W['pool_scale'], nreal, real_end, "pool_fwd")
            yb, states = _hgrn_fwd(p, l0, l1, W['hgrn_gnorm'], nreal, real_end, "hgrn_fwd")
            wo = W['w_out_even']
            h = _mm([(ya, wo[:256]), (yb, wo[256:])], "NN", "out_even", res=hm)
            sm = (hm, xn, p, wbd, l0, l1, ya, yb, states)
        else:
            p = _mm([(xn, W['w_in_odd'])], "NN", "in_odd")
            cw = jnp.pad(W['conv_w'][0], ((0, 1), (0, 0)))
            lw = jnp.pad(W['lru_conv_w'][0], ((0, 4), (0, 0)))
            wa, wx = _blockdiag(W['lru_wa'][0], 2), _blockdiag(W['lru_wx'][0], 2)
            yc = _conf_fwd(p, cw, W['conv_b'], W['conv_ln_g'], W['conv_ln_b'], nreal, real_end, "conf_fwd")
            yd, hs = _lru_fwd(p, lw, W['lru_conv_b'], wa, wx, W['lru_ba'], W['lru_bx'], W['lru_lambda'], nreal,
                              real_end, "lru_fwd")
            wo = W['w_out_odd']
            h = _mm([(yc, wo[:512]), (yd, wo[512:])], "NN", "out_odd", res=hm)
            sm = (hm, xn, p, cw, lw, wa, wx, yc, yd, hs)
        fetch(3 * l + 3, h)
        h, s2 = _ffn_forward(h, row(W['ffn2_norm'][l]), W['ffn2_wg', l], W['ffn2_wu', l], W['ffn2_wd', l], f"b{l}")
        if l == 0:
            fetch(4, h)
        saved.append((s1, sm, s2))

    loss8, dh, dfin = _loss_head(h, row(W['final_norm']), tgt_p, lo, real_end, "loss_head")
    G['final_norm'] = dfin[0]

    per_layer = {k: [None, None] for k in ('ffn1_norm', 'mix_norm', 'ffn2_norm')}
    tok = None
    for l in (1, 0):
        s1, sm, s2 = saved[l]
        dh, dn, dwg, dwu, dwd = _ffn_backward(s2, row(W['ffn2_norm'][l]), W['ffn2_wg', l], W['ffn2_wu', l],
                                              W['ffn2_wd', l], dh, tok, f"b{l}")
        per_layer['ffn2_norm'][l] = dn[0]
        tok = emit(f"ffn2_{l}", [('ffn2_wg', l, dwg), ('ffn2_wu', l, dwu), ('ffn2_wd', l, dwd)])
        if l == 0:
            hm, xn, p, wbd, l0, l1, ya, yb, states = sm
            wo, wi = _after(W['w_out_even'], tok), W['w_in_even']
            dwo = jnp.concatenate([_mm([(ya, dh)], "TN", "dwo_even_a"), _mm([(yb, dh)], "TN", "dwo_even_b")], axis=0)
            dy = _mm([(dh, wo)], "NT", "dy_even")
            dpp, dwbd, dsc = _pool_bwd(p, wbd, W['pool_scale'], dy, nreal, real_end, "pool_bwd")
            dq, df, di, dg, dl0, dl1, dgn = _hgrn_bwd(p, l0, l1, W['hgrn_gnorm'], states, dy, nreal, real_end,
                                                      "hgrn_bwd")
            G['pool_w'] = _blockdiag_grad(dwbd[None], 4, 64)[None]
            G['pool_scale'] = dsc
            G['hgrn_lb_logits'] = jnp.concatenate([dl0, dl1], axis=0)
            G['hgrn_gnorm'] = dgn
            parts = [dpp, dq, df, di, dg]
            offs = [0, 256, 1024, 1792, 2560, 3328]
            dwi = jnp.concatenate(
                [_mm([(xn, dpart)], "TN", f"dwi_even_{k}") for k, dpart in enumerate(parts)], axis=1)
            dxn = _mm([(dpart, wi[:, offs[k]:offs[k + 1]]) for k, dpart in enumerate(parts)], "NT", "dxn_even")
            tok = emit("even", [('w_in_even', None, dwi), ('w_out_even', None, dwo)])
        else:
            hm, xn, p, cw, lw, wa, wx, yc, yd, hs = sm
            wo, wi = _after(W['w_out_odd'], tok), W['w_in_odd']
            dwo = jnp.concatenate([_mm([(yc, dh)], "TN", "dwo_odd_c"), _mm([(yd, dh)], "TN", "dwo_odd_d")], axis=0)
            dy = _mm([(dh, wo)], "NT", "dy_odd")
            da, db, dcw, dcb, dlg, dlb = _conf_bwd(p, cw, W['conv_b'], W['conv_ln_g'], W['conv_ln_b'], dy, nreal,
                                                   real_end, "conf_bwd")
            dx, dgate, dlw, dlcb, dwa, dwx, dba, dbx, dlam = _lru_bwd(
                p, lw, W['lru_conv_b'], wa, wx, W['lru_ba'], W['lru_bx'], W['lru_lambda'], hs, dy, nreal, real_end,
                "lru_bwd")
            G['conv_w'], G['conv_b'], G['conv_ln_g'], G['conv_ln_b'] = dcw[None, :CONV_W], dcb, dlg, dlb
            G['lru_conv_w'], G['lru_conv_b'] = dlw[None, :LRU_W], dlcb
            G['lru_wa'] = _blockdiag_grad(dwa, 2, 64)[None]
            G['lru_wx'] = _blockdiag_grad(dwx, 2, 64)[None]
            G['lru_ba'], G['lru_bx'], G['lru_lambda'] = dba, dbx, dlam
            parts = [da, db, dx, dgate]
            dwi = jnp.concatenate(
                [_mm([(xn, dpart)], "TN", f"dwi_odd_{k}") for k, dpart in enumerate(parts)], axis=1)
            dxn = _mm([(dpart, wi[:, 512 * k:512 * (k + 1)]) for k, dpart in enumerate(parts)], "NT", "dxn_odd")
            tok = emit("odd", [('w_in_odd', None, dwi), ('w_out_odd', None, dwo)])
        dh, dn = _rms_bwd(hm, _after(row(W['mix_norm'][l]), tok), dxn, dh, f"rms_bwd_mix{l}")
        per_layer['mix_norm'][l] = dn[0]
        one = None if l == 1 else (lambda sfx, g: emit(f"ffn1_0_{sfx}", [('ffn1_' + sfx, 0, g)]))
        dh, dn, dwg, dwu, dwd = _ffn_backward(s1, row(W['ffn1_norm'][l]), W['ffn1_wg', l], W['ffn1_wu', l],
                                              W['ffn1_wd', l], dh, None, f"a{l}", one)
        per_layer['ffn1_norm'][l] = dn[0]
        if l == 1:
            tok = emit("ffn1_1", [('ffn1_wg', l, dwg), ('ffn1_wu', l, dwu), ('ffn1_wd', l, dwd)])
    for k, v in per_layer.items():
        G[k] = jnp.stack(v, axis=0)
    G['meta_tokens'] = dh[CH:lo]
    return loss8[0, 0], dh[lo:real_end], G


def _pack(arrs):
    flat = jnp.concatenate([a.reshape(-1).astype(F32) for a in arrs])
    n = flat.shape[0]
    padded = -(-n // 1024) * 1024
    return jnp.pad(flat, (0, padded - n)).reshape(-1, 128)


def _unpack(packed, shapes):
    flat = packed.reshape(-1)
    out, off = [], 0
    for s in shapes:
        n = math.prod(s)
        out.append(flat[off:off + n].reshape(s))
        off += n
    return out


def _to_full(gathered, axis):
    s = gathered.shape[1:]
    return jnp.moveaxis(gathered, 0, axis).reshape(s[:axis] + (NDEV * s[axis],) + s[axis + 1:])


def _to_slots(full, axis):
    s = full.shape
    return jnp.moveaxis(full.reshape(s[:axis] + (NDEV, s[axis] // NDEV) + s[axis + 1:]), axis, 0)


def kernel(x, meta_tokens, ffn1_norm, ffn1_wg, ffn1_wu, ffn1_wd, mix_norm, ffn2_norm, ffn2_wg, ffn2_wu, ffn2_wd, w_in_even, pool_w, pool_scale, hgrn_lb_logits, hgrn_gnorm, w_out_even, w_in_odd, conv_w, conv_b, conv_ln_g, conv_ln_b, lru_conv_w, lru_conv_b, lru_wa, lru_ba, lru_wx, lru_bx, lru_lambda, w_out_odd, final_norm, loss_target, m_meta_tokens, m_ffn1_norm, m_ffn1_wg, m_ffn1_wu, m_ffn1_wd, m_mix_norm, m_ffn2_norm, m_ffn2_wg, m_ffn2_wu, m_ffn2_wd, m_w_in_even, m_pool_w, m_pool_scale, m_hgrn_lb_logits, m_hgrn_gnorm, m_w_out_even, m_w_in_odd, m_conv_w, m_conv_b, m_conv_ln_g, m_conv_ln_b, m_lru_conv_w, m_lru_conv_b, m_lru_wa, m_lru_ba, m_lru_wx, m_lru_bx, m_lru_lambda, m_w_out_odd, m_final_norm, v_meta_tokens, v_ffn1_norm, v_ffn1_wg, v_ffn1_wu, v_ffn1_wd, v_mix_norm, v_ffn2_norm, v_ffn2_wg, v_ffn2_wu, v_ffn2_wd, v_w_in_even, v_pool_w, v_pool_scale, v_hgrn_lb_logits, v_hgrn_gnorm, v_w_out_even, v_w_in_odd, v_conv_w, v_conv_b, v_conv_ln_g, v_conv_ln_b, v_lru_conv_w, v_lru_conv_b, v_lru_wa, v_lru_ba, v_lru_wx, v_lru_bx, v_lru_lambda, v_w_out_odd, v_final_norm):
    args = (meta_tokens, ffn1_norm, ffn1_wg, ffn1_wu, ffn1_wd, mix_norm, ffn2_norm, ffn2_wg, ffn2_wu, ffn2_wd, w_in_even, pool_w, pool_scale, hgrn_lb_logits, hgrn_gnorm, w_out_even, w_in_odd, conv_w, conv_b, conv_ln_g, conv_ln_b, lru_conv_w, lru_conv_b, lru_wa, lru_ba, lru_wx, lru_bx, lru_lambda, w_out_odd, final_norm)
    margs = (m_meta_tokens, m_ffn1_norm, m_ffn1_wg, m_ffn1_wu, m_ffn1_wd, m_mix_norm, m_ffn2_norm, m_ffn2_wg, m_ffn2_wu, m_ffn2_wd, m_w_in_even, m_pool_w, m_pool_scale, m_hgrn_lb_logits, m_hgrn_gnorm, m_w_out_even, m_w_in_odd, m_conv_w, m_conv_b, m_conv_ln_g, m_conv_ln_b, m_lru_conv_w, m_lru_conv_b, m_lru_wa, m_lru_ba, m_lru_wx, m_lru_bx, m_lru_lambda, m_w_out_odd, m_final_norm)
    vargs = (v_meta_tokens, v_ffn1_norm, v_ffn1_wg, v_ffn1_wu, v_ffn1_wd, v_mix_norm, v_ffn2_norm, v_ffn2_wg, v_ffn2_wu, v_ffn2_wd, v_w_in_even, v_pool_w, v_pool_scale, v_hgrn_lb_logits, v_hgrn_gnorm, v_w_out_even, v_w_in_odd, v_conv_w, v_conv_b, v_conv_ln_g, v_conv_ln_b, v_lru_conv_w, v_lru_conv_b, v_lru_wa, v_lru_ba, v_lru_wx, v_lru_bx, v_lru_lambda, v_w_out_odd, v_final_norm)
    Wl = dict(zip(W_NAMES, args))
    Ml = dict(zip(W_NAMES, margs))
    Vl = dict(zip(W_NAMES, vargs))

    small_shapes = [Wl[n].shape for n in SMALL_SHARDED]
    ffn = lambda p, l: [(p + s, l) for s in ('_wg', '_wu', '_wd')]
    mix = lambda p: [('w_in_' + p, None), ('w_out_' + p, None)]
    ggroups = [ffn('ffn1', 0)[:2], ffn('ffn1', 0)[2:], mix('even'), ffn('ffn2', 0), ffn('ffn1', 1), mix('odd'),
               ffn('ffn2', 1)]
    shard = lambda n, l: Wl[n][0 if l is None else l].astype(MXU)
    srcs = [[shard(n, l) for n, l in g] for g in ggroups]
    srcs[0] = [_pack([Wl[n] for n in SMALL_SHARDED])] + srcs[0]
    handles, _ = _exchange_start(srcs, True, "gather_start")
    W = {n: Wl[n] for n in REPLICATED}

    def fetch(k, after):
        lands = _exchange_wait(handles[k], True, after, f"gather_wait_{k}")
        if k == 0:
            per_dev = [_unpack(lands[0][d], small_shapes) for d in range(NDEV)]
            for j, n in enumerate(SMALL_SHARDED):
                W[n] = _to_full(jnp.stack([per_dev[d][j] for d in range(NDEV)]), SHARD_AXIS[n])
            lands = lands[1:]
        for (n, l), g in zip(ggroups[k], lands):
            W[n if l is None else (n, l)] = _to_full(g, SHARD_AXIS[n] - 1)

    pending = []

    def emit(tag, grads):
        slots = [_to_slots(g.astype(MXU), SHARD_AXIS[n] - 1) for n, _, g in grads]
        hs, token = _exchange_start([slots], False, f"scatter_start_{tag}")
        pending.append((tag, hs[0], [(n, l) for n, l, _ in grads]))
        return token[0, 0]

    loss_part, grad_x, G = _local_step(x[0], loss_target[0], W, fetch, emit)
    loss = lax.psum(loss_part, MESH_AXES)

    small_slots = [_to_slots(G[n].astype(F32), SHARD_AXIS[n]) for n in SMALL_SHARDED]
    send = [jnp.stack([_pack([s[d] for s in small_slots]) for d in range(NDEV)]), _pack([G[n] for n in REPLICATED])]
    got = _exchange(send, [False, True], "scatter_small")
    recv = {}
    for tag, handle, keys in pending:
        for key, r in zip(keys, _exchange_wait(handle, False, got[0], f"scatter_wait_{tag}")):
            recv[key] = r

    outs = {}
    for n in BIG:
        shp = Wl[n].shape
        C = shp[-1]
        rs = [recv[n, None]] if shp[0] == 1 else [recv[n, l] for l in range(shp[0])]
        res = _adamw(rs, Wl[n].reshape(-1, C), Ml[n].reshape(-1, C), Vl[n].reshape(-1, C), f"adamw_{n}")
        outs[n] = [o.reshape(shp) for o in res]
    for names, r, tag in ((SMALL_SHARDED, got[-2], "small"), (REPLICATED, got[-1], "repl")):
        shapes = [Wl[n].shape for n in names]
        res = _adamw([r], _pack([Wl[n] for n in names]), _pack([Ml[n] for n in names]),
                     _pack([Vl[n] for n in names]), f"adamw_{tag}")
        unp = [_unpack(o, shapes) for o in res]
        for k, n in enumerate(names):
            outs[n] = [unp[j][k] for j in range(4)]

    result = [loss, grad_x[None]]
    for j in range(4):
        result += [outs[n][j] for n in W_NAMES]
    return tuple(result)
```

```python
import functools
import math

import jax
import jax.numpy as jnp
from jax import lax
from jax.experimental import pallas as pl
from jax.experimental.pallas import tpu as pltpu

F32 = jnp.float32
MXU = jnp.bfloat16
EPS = 1e-6
CH = 128
HG = 64
N_META = 16
CONV_W = 31
LRU_W = 4
LRU_C = 8.0
VMEM_LIMIT = 48 * 2 ** 20
MM_VMEM_BUDGET = 36 * 2 ** 20
ADAM_LR, ADAM_B1, ADAM_B2, ADAM_EPS, ADAM_WD, ADAM_STEP = 0.001, 0.9, 0.999, 1e-08, 0.01, 10
MESH_AXES = ("x", "y", "c")
NDEV = 8

W_NAMES = ['meta_tokens', 'ffn1_norm', 'ffn1_wg', 'ffn1_wu', 'ffn1_wd', 'mix_norm', 'ffn2_norm', 'ffn2_wg', 'ffn2_wu',
           'ffn2_wd', 'w_in_even', 'pool_w', 'pool_scale', 'hgrn_lb_logits', 'hgrn_gnorm', 'w_out_even', 'w_in_odd',
           'conv_w', 'conv_b', 'conv_ln_g', 'conv_ln_b', 'lru_conv_w', 'lru_conv_b', 'lru_wa', 'lru_ba', 'lru_wx',
           'lru_bx', 'lru_lambda', 'w_out_odd', 'final_norm']
SHARD_AXIS = {'meta_tokens': 1, 'ffn1_wg': 2, 'ffn1_wu': 2, 'ffn1_wd': 1, 'ffn2_wg': 2, 'ffn2_wu': 2, 'ffn2_wd': 1,
              'w_in_even': 2, 'w_out_even': 1, 'w_in_odd': 2, 'conv_w': 2, 'conv_b': 1, 'conv_ln_g': 1,
              'conv_ln_b': 1, 'lru_conv_w': 2, 'lru_conv_b': 1, 'lru_ba': 1, 'lru_bx': 1, 'lru_lambda': 1,
              'w_out_odd': 1}
BIG = ['ffn1_wg', 'ffn1_wu', 'ffn1_wd', 'ffn2_wg', 'ffn2_wu', 'ffn2_wd', 'w_in_even', 'w_out_even', 'w_in_odd',
       'w_out_odd']
SMALL_SHARDED = [n for n in W_NAMES if n in SHARD_AXIS and n not in BIG]
REPLICATED = [n for n in W_NAMES if n not in SHARD_AXIS]


def _cparams(sem=None, vmem=VMEM_LIMIT):
    return pltpu.CompilerParams(dimension_semantics=sem, vmem_limit_bytes=vmem)


def _tiles(n):
    return [c for c in range(128, n + 1, 128) if n % c == 0] or [n]


def _tile(n, cap=1024):
    return max([c for c in _tiles(n) if c <= cap], default=_tiles(n)[0])


def _rowtile(n):
    for c in (256, 352, 128, 64, 32, 16, 8):
        if n % c == 0:
            return c
    return n


def _copies(srcs, lands, bcast, ssem, rsem, lsem):
    x, y, c = lax.axis_index("x"), lax.axis_index("y"), lax.axis_index("c")
    me = 4 * x + 2 * y + c
    locs, sends, recvs = [], [], []
    for a in range(len(srcs)):
        locs.append(pltpu.make_async_copy(srcs[a] if bcast[a] else srcs[a].at[me], lands[a].at[me], lsem.at[a]))
        for m in range(1, NDEV):
            px = 1 - x if (m >> 2) & 1 else x
            py = 1 - y if (m >> 1) & 1 else y
            pc = 1 - c if m & 1 else c
            peer = 4 * px + 2 * py + pc
            src = srcs[a] if bcast[a] else srcs[a].at[peer]
            k = a * NDEV + m
            for dst, out in ((lands[a].at[me], sends), (lands[a].at[peer], recvs)):
                out.append(pltpu.make_async_remote_copy(src_ref=src, dst_ref=dst, send_sem=ssem.at[k],
                                                        recv_sem=rsem.at[k], device_id=(px, py, pc),
                                                        device_id_type=pl.DeviceIdType.MESH))
    return locs, sends, recvs


def _land_shape(arr, bc):
    return (NDEV,) + tuple(arr.shape if bc else arr.shape[1:])


def _exchange(arrays, bcast, name):
    n = len(arrays)

    def body(*refs):
        locs, sends, recvs = _copies(refs[:n], refs[n:2 * n], bcast, refs[2 * n], refs[2 * n + 1], refs[2 * n + 2])
        for d in locs + sends:
            d.start()
        for r in recvs:
            r.wait_recv()
        for s in sends:
            s.wait_send()
        for loc in locs:
            loc.wait()

    out_shape = tuple(jax.ShapeDtypeStruct(_land_shape(arr, bc), arr.dtype) for arr, bc in zip(arrays, bcast))
    any_spec = pl.BlockSpec(memory_space=pl.ANY)
    return pl.pallas_call(
        body, name=name, out_shape=out_shape, in_specs=[any_spec] * n, out_specs=tuple([any_spec] * n),
        scratch_shapes=[pltpu.SemaphoreType.DMA((n * NDEV,)), pltpu.SemaphoreType.DMA((n * NDEV,)),
                        pltpu.SemaphoreType.DMA((n,))],
    )(*arrays)


_HBM = pl.BlockSpec(memory_space=pltpu.HBM)
_SEM = pl.BlockSpec(memory_space=pltpu.SEMAPHORE)
_EFFECT = pltpu.SideEffectType.DATAFLOW_SIDE_EFFECTING


def _exchange_start(groups, bcast, name):
    sizes = [len(g) for g in groups]
    srcs = [a for g in groups for a in g]
    n, ng = len(srcs), len(groups)
    lands = [lax.empty(_land_shape(a, bcast), a.dtype) for a in srcs]

    def body(*refs):
        off = 0
        for gi, sz in enumerate(sizes):
            sem = refs[2 * n + 3 * gi:2 * n + 3 * gi + 3]
            locs, sends, _ = _copies(refs[off:off + sz], refs[n + off:n + off + sz], [bcast] * sz, *sem)
            for d in locs + sends:
                d.start()
            off += sz
        refs[-1][...] = jnp.zeros((8, 128), F32)

    sems = []
    for sz in sizes:
        sems += [pltpu.SemaphoreType.DMA((sz * NDEV,)), pltpu.SemaphoreType.DMA((sz * NDEV,)),
                 pltpu.SemaphoreType.DMA((sz,))]
    thru = [pltpu.HBM(a.shape, a.dtype) for a in srcs + lands]
    outs = pl.pallas_call(
        body, name=name, out_shape=tuple(sems + thru + [jax.ShapeDtypeStruct((8, 128), F32)]),
        in_specs=[_HBM] * (2 * n),
        out_specs=tuple([_SEM] * (3 * ng) + [_HBM] * (2 * n) + [pl.BlockSpec(memory_space=pltpu.VMEM)]),
        input_output_aliases={i: 3 * ng + i for i in range(2 * n)},
        compiler_params=pltpu.CompilerParams(has_side_effects=_EFFECT),
    )(*[pltpu.with_memory_space_constraint(a, pltpu.HBM) for a in srcs + lands])
    handles, off = [], 0
    for gi, sz in enumerate(sizes):
        handles.append((outs[3 * gi:3 * gi + 3], outs[3 * ng + off:3 * ng + off + sz],
                        outs[3 * ng + n + off:3 * ng + n + off + sz]))
        off += sz
    return handles, outs[-1]


def _exchange_wait(handle, bcast, after, name):
    sems, srcs, lands = handle
    n = len(srcs)

    def body(*refs):
        locs, sends, recvs = _copies(refs[:n], refs[n:2 * n], [bcast] * n, *refs[2 * n:2 * n + 3])
        for r in recvs:
            r.wait_recv()
        for s in sends:
            s.wait_send()
        for loc in locs:
            loc.wait()

    outs = pl.pallas_call(
        body, name=name, out_shape=tuple(pltpu.HBM(a.shape, a.dtype) for a in list(srcs) + list(lands)),
        in_specs=[_HBM] * (2 * n) + [_SEM] * 3 + [pl.BlockSpec(memory_space=pl.ANY)],
        out_specs=tuple([_HBM] * (2 * n)), input_output_aliases={i: i for i in range(2 * n)},
        compiler_params=pltpu.CompilerParams(has_side_effects=_EFFECT),
    )(*srcs, *lands, *sems, after)
    return outs[n:]


def _adamw(recvs, w, m, v, name):
    R, C = w.shape
    nr = len(recvs)
    br = _rowtile(R // nr)
    nb0 = R // nr // br

    def body(*refs):
        w_ref, m_ref, v_ref, g_o, d_o, m_o, v_o = refs[nr:]
        g = None
        for j in range(nr):
            s = refs[j][0].astype(F32)
            for k in range(1, NDEV):
                s = s + refs[j][k].astype(F32)
            g = s if g is None else jnp.where(pl.program_id(0) >= j * nb0, s, g)
        mn = ADAM_B1 * m_ref[...] + (1.0 - ADAM_B1) * g
        vn = ADAM_B2 * v_ref[...] + (1.0 - ADAM_B2) * (g * g)
        m_hat = mn / (1.0 - ADAM_B1 ** ADAM_STEP)
        v_hat = vn / (1.0 - ADAM_B2 ** ADAM_STEP)
        g_o[...] = g
        d_o[...] = -ADAM_LR * (m_hat / (jnp.sqrt(v_hat) + ADAM_EPS) + ADAM_WD * w_ref[...])
        m_o[...] = mn
        v_o[...] = vn

    def rspec(j):
        return pl.BlockSpec((NDEV, br, C), lambda i: (0, jnp.clip(i - j * nb0, 0, nb0 - 1), 0))

    blk = pl.BlockSpec((br, C), lambda i: (i, 0))
    sds = jax.ShapeDtypeStruct((R, C), F32)
    return pl.pallas_call(
        body, name=name, grid=(R // br,), out_shape=(sds, sds, sds, sds),
        in_specs=[rspec(j) for j in range(nr)] + [blk, blk, blk], out_specs=(blk, blk, blk, blk),
        compiler_params=_cparams(("arbitrary",)),
    )(*recvs, w, m, v)


_DIMS = {"NN": ((1,), (0,)), "NT": ((1,), (1,)), "TN": ((0,), (0,))}


def _dot(a, b, mode="NN"):
    return lax.dot_general(a.astype(MXU), b.astype(MXU), (_DIMS[mode], ((), ())), preferred_element_type=F32)


def _dotf(a, b, mode="NN"):
    return lax.dot_general(a, b, (_DIMS[mode], ((), ())), precision=lax.Precision.HIGH,
                           preferred_element_type=F32)


def _mm(pairs, mode, name, res=None, res_scale=1.0, out_dtype=F32):
    a0, b0 = pairs[0]
    M = a0.shape[1] if mode == "TN" else a0.shape[0]
    N = b0.shape[0] if mode == "NT" else b0.shape[1]
    npairs = len(pairs)

    def vmem_bytes(tm, tn):
        total = tm * tn * 4 * (2 + (2 if res is not None else 0) + 2)
        for a, b in pairs:
            ka = a.shape[0] if mode == "TN" else a.shape[1]
            kb = b.shape[1] if mode == "NT" else b.shape[0]
            for k, t, arr in ((ka, tm, a), (kb, tn, b)):
                total += k * t * (2 * arr.dtype.itemsize + (2 if arr.dtype == F32 else 0))
        return total

    tm, tn = max(((tm, tn) for tm in _tiles(M) for tn in _tiles(N) if vmem_bytes(tm, tn) <= MM_VMEM_BUDGET),
                 key=lambda t: (t[0] * t[1], t[1]), default=(_tiles(M)[0], _tiles(N)[0]))

    def body(*refs):
        acc = None
        for p in range(npairs):
            d = _dot(refs[2 * p][...], refs[2 * p + 1][...], mode)
            acc = d if acc is None else acc + d
        if res_scale != 1.0:
            acc = res_scale * acc
        if res is not None:
            acc = refs[2 * npairs][...] + acc
        refs[-1][...] = acc.astype(out_dtype)

    in_specs, args = [], []
    for a, b in pairs:
        if mode == "TN":
            in_specs.append(pl.BlockSpec((a.shape[0], tm), lambda i, j: (0, i)))
        else:
            in_specs.append(pl.BlockSpec((tm, a.shape[1]), lambda i, j: (i, 0)))
        if mode == "NT":
            in_specs.append(pl.BlockSpec((tn, b.shape[1]), lambda i, j: (j, 0)))
        else:
            in_specs.append(pl.BlockSpec((b.shape[0], tn), lambda i, j: (0, j)))
        args += [a, b]
    if res is not None:
        in_specs.append(pl.BlockSpec((tm, tn), lambda i, j: (i, j)))
        args.append(res)
    return pl.pallas_call(
        body, name=name, grid=(M // tm, N // tn), out_shape=jax.ShapeDtypeStruct((M, N), out_dtype),
        in_specs=in_specs, out_specs=pl.BlockSpec((tm, tn), lambda i, j: (i, j)),
        compiler_params=_cparams(("parallel", "parallel")),
    )(*args)


def _rms_fwd(h, gamma, name):
    T, D = h.shape
    tm = _tile(T)

    def body(h_ref, g_ref, o_ref):
        x = h_ref[...]
        r = lax.rsqrt(jnp.mean(x * x, axis=-1, keepdims=True) + EPS)
        o_ref[...] = (x * r * g_ref[...]).astype(MXU)

    return pl.pallas_call(
        body, name=name, grid=(T // tm,), out_shape=jax.ShapeDtypeStruct((T, D), MXU),
        in_specs=[pl.BlockSpec((tm, D), lambda i: (i, 0)), pl.BlockSpec((1, D), lambda i: (0, 0))],
        out_specs=pl.BlockSpec((tm, D), lambda i: (i, 0)), compiler_params=_cparams(("parallel",)),
    )(h, gamma)


def _rms_bwd_math(x, gamma, dy):
    r = lax.rsqrt(jnp.mean(x * x, axis=-1, keepdims=True) + EPS)
    z = dy * gamma
    dx = r * z - x * (r * r * r) * jnp.mean(z * x, axis=-1, keepdims=True)
    dgamma = jnp.sum(dy * x * r, axis=0, keepdims=True)
    return dx, dgamma


def _rms_bwd(h, gamma, dxn, dres, name):
    T, D = h.shape
    tm = _tile(T)

    def body(h_ref, g_ref, dxn_ref, dres_ref, dh_ref, dh16_ref, dg_ref):
        dx, dgamma = _rms_bwd_math(h_ref[...], g_ref[...], dxn_ref[...])
        dh = dres_ref[...] + dx
        dh_ref[...] = dh
        dh16_ref[...] = dh.astype(MXU)

        @pl.when(pl.program_id(0) == 0)
        def _():
            dg_ref[...] = jnp.zeros_like(dg_ref)

        dg_ref[...] += dgamma

    row = pl.BlockSpec((tm, D), lambda i: (i, 0))
    vec = pl.BlockSpec((1, D), lambda i: (0, 0))
    return pl.pallas_call(
        body, name=name, grid=(T // tm,),
        out_shape=(jax.ShapeDtypeStruct((T, D), F32), jax.ShapeDtypeStruct((T, D), MXU),
                   jax.ShapeDtypeStruct((1, D), F32)),
        in_specs=[row, vec, row, row], out_specs=(row, row, vec), compiler_params=_cparams(("arbitrary",)),
    )(h, gamma, dxn, dres)


def _loss_head(h, gamma, tgt, lo, hi, name):
    T, D = h.shape
    tm = _tile(T)

    def body(h_ref, g_ref, t_ref, loss_ref, dh_ref, dh16_ref, dg_ref):
        i = pl.program_id(0)
        x = h_ref[...]
        r = lax.rsqrt(jnp.mean(x * x, axis=-1, keepdims=True) + EPS)
        y = x * r * g_ref[...]
        rows = i * tm + lax.broadcasted_iota(jnp.int32, (tm, 1), 0)
        valid = jnp.logical_and(rows >= lo, rows < hi)
        diff = jnp.where(valid, y - t_ref[...], 0.0)
        part = 0.5 * jnp.sum(jnp.sum(diff * diff, axis=-1, keepdims=True) / D, axis=0, keepdims=True)
        dx, dgamma = _rms_bwd_math(x, g_ref[...], diff / D)
        dh_ref[...] = dx
        dh16_ref[...] = dx.astype(MXU)

        @pl.when(i == 0)
        def _():
            dg_ref[...] = jnp.zeros_like(dg_ref)
            loss_ref[...] = jnp.zeros_like(loss_ref)

        dg_ref[...] += dgamma
        loss_ref[...] += jnp.broadcast_to(part, loss_ref.shape)

    row = pl.BlockSpec((tm, D), lambda i: (i, 0))
    vec = pl.BlockSpec((1, D), lambda i: (0, 0))
    lsp = pl.BlockSpec((8, 128), lambda i: (0, 0))
    return pl.pallas_call(
        body, name=name, grid=(T // tm,),
        out_shape=(jax.ShapeDtypeStruct((8, 128), F32), jax.ShapeDtypeStruct((T, D), F32),
                   jax.ShapeDtypeStruct((T, D), MXU), jax.ShapeDtypeStruct((1, D), F32)),
        in_specs=[row, vec, row], out_specs=(lsp, row, row, vec), compiler_params=_cparams(("arbitrary",)),
    )(h, gamma, tgt)


def _ffn_tiles(T, Fd):
    return (448 if T % 448 == 0 else _tile(T)), max(c for c in _tiles(Fd) if c <= 1536)


def _ffn_up(xn, wg, wu, name):
    T, D = xn.shape
    Fd = wg.shape[1]
    tm, tn = _ffn_tiles(T, Fd)

    def body(x_ref, wg_ref, wu_ref, p_ref, q_ref, a_ref):
        x = x_ref[...]
        g = _dot(x, wg_ref[...])
        u = _dot(x, wu_ref[...])
        sg = jax.nn.sigmoid(g)
        q = g * sg
        p_ref[...] = (u * (sg + q * (1.0 - sg))).astype(MXU)
        q_ref[...] = q.astype(MXU)
        a_ref[...] = (q * u).astype(MXU)

    wsp = pl.BlockSpec((D, tn), lambda i, j: (0, j))
    osp = pl.BlockSpec((tm, tn), lambda i, j: (i, j))
    sds = jax.ShapeDtypeStruct((T, Fd), MXU)
    return pl.pallas_call(
        body, name=name, grid=(T // tm, Fd // tn), out_shape=(sds, sds, sds),
        in_specs=[pl.BlockSpec((tm, D), lambda i, j: (i, 0)), wsp, wsp], out_specs=(osp, osp, osp),
        compiler_params=_cparams(("parallel", "parallel")),
    )(xn, wg, wu)


def _ffn_dact(dy, wd, p, q, scale, name):
    T, D = dy.shape
    Fd = wd.shape[0]
    tm, tn = _ffn_tiles(T, Fd)

    def body(dy_ref, wd_ref, p_ref, q_ref, dg_ref, du_ref):
        da = scale * _dot(dy_ref[...], wd_ref[...], "NT")
        dg_ref[...] = (da * p_ref[...].astype(F32)).astype(MXU)
        du_ref[...] = (da * q_ref[...].astype(F32)).astype(MXU)

    osp = pl.BlockSpec((tm, tn), lambda i, j: (i, j))
    sds = jax.ShapeDtypeStruct((T, Fd), MXU)
    return pl.pallas_call(
        body, name=name, grid=(T // tm, Fd // tn), out_shape=(sds, sds),
        in_specs=[pl.BlockSpec((tm, D), lambda i, j: (i, 0)), pl.BlockSpec((tn, D), lambda i, j: (j, 0)), osp, osp],
        out_specs=(osp, osp), compiler_params=_cparams(("parallel", "parallel")),
    )(dy, wd, p, q)


def _down(v, s):
    return v if s == 0 else pltpu.roll(v, s, 0)


def _up(v, s):
    return v if s == 0 else pltpu.roll(v, v.shape[0] - s, 0)


def _rows(n):
    return lax.broadcasted_iota(jnp.int32, (n, 1), 0)


def _zero_pad_rows(ref, lo_end, hi_start, T):
    ref[pl.ds(0, lo_end), :] = jnp.zeros((lo_end, ref.shape[1]), ref.dtype)
    if T > hi_start:
        ref[pl.ds(hi_start, T - hi_start), :] = jnp.zeros((T - hi_start, ref.shape[1]), ref.dtype)


def _colblock(T, off):
    return pl.BlockSpec((T, 128), lambda j: (0, off + j))


def _vecblock(rows=1):
    return pl.BlockSpec((rows, 128), lambda j: (0, j))


def _pool_lane_consts(n):
    lane = lax.broadcasted_iota(jnp.int32, (n, 256), 1)
    win = jnp.where(lane < 64, 2.0, jnp.where(lane < 128, 4.0, jnp.where(lane < 192, 8.0, 16.0)))
    return lane, win


def _pool_select(lane, s2, s4, s8, s16):
    return jnp.where(lane < 64, s2, jnp.where(lane < 128, s4, jnp.where(lane < 192, s8, s16)))


def _pool_mixed(xh, start):
    s2 = xh + _down(xh, 1)
    s4 = s2 + _down(s2, 2)
    s8 = s4 + _down(s4, 4)
    s16 = s8 + _down(s8, 8)
    n = xh.shape[0] - 16
    lane, _ = _pool_lane_consts(n + 16)
    _, win = _pool_lane_consts(n)
    t1 = (start - CH + 1 + _rows(n)).astype(F32)
    cnt = jnp.minimum(jnp.maximum(t1, 1.0), win)
    return _pool_select(lane, s2, s4, s8, s16)[16:] / cnt - xh[16:]


def _pool_fwd(p, wbd, scale, nreal, real_end, name):
    T = p.shape[0]

    def body(p_ref, w_ref, s_ref, y_ref):
        _zero_pad_rows(y_ref, CH, CH * (1 + nreal), T)

        def chunk(c, carry):
            start = pl.multiple_of(c * CH, CH)
            mixed = _pool_mixed(p_ref[pl.ds(start - 16, CH + 16), :], start)
            y = _dot(mixed, w_ref[...]) * s_ref[...]
            y_ref[pl.ds(start, CH), :] = jnp.where(start + _rows(CH) < real_end, y, 0.0)
            return carry

        lax.fori_loop(1, 1 + nreal, chunk, 0)

    return pl.pallas_call(
        body, name=name, grid=(1,), out_shape=jax.ShapeDtypeStruct((T, 256), F32),
        in_specs=[pl.BlockSpec((T, 256), lambda j: (0, 0)), pl.BlockSpec((256, 256), lambda j: (0, 0)),
                  pl.BlockSpec((1, 256), lambda j: (0, 0))],
        out_specs=pl.BlockSpec((T, 256), lambda j: (0, 0)), compiler_params=_cparams(("arbitrary",)),
    )(p, wbd, scale)


def _pool_bwd(p, wbd, scale, dy, nreal, real_end, name):
    T = p.shape[0]

    def body(p_ref, w_ref, s_ref, dy_ref, dp_ref, dw_ref, ds_ref):
        _zero_pad_rows(dp_ref, CH, CH * (1 + nreal), T)
        dw_ref[...] = jnp.zeros_like(dw_ref)
        ds_ref[...] = jnp.zeros_like(ds_ref)

        def chunk(c, carry):
            start = pl.multiple_of(c * CH, CH)
            mixed = _pool_mixed(p_ref[pl.ds(start - 16, CH + 16), :], start)
            ypre = _dot(mixed, w_ref[...])
            n = CH + 16
            dye = jnp.where(start + _rows(n) < real_end, dy_ref[pl.ds(start, n), :], 0.0)
            dys = dye * s_ref[...]
            ds_ref[...] += jnp.sum(dye[:CH] * ypre, axis=0, keepdims=True)
            dw_ref[...] += _dot(mixed, dys[:CH], "TN")
            dmix = _dot(dys, w_ref[...], "NT")
            lane, win = _pool_lane_consts(n)
            t1 = (start - CH + 1 + _rows(n)).astype(F32)
            z = dmix / jnp.minimum(jnp.maximum(t1, 1.0), win)
            r2 = z + _up(z, 1)
            r4 = r2 + _up(r2, 2)
            r8 = r4 + _up(r4, 4)
            r16 = r8 + _up(r8, 8)
            dp_ref[pl.ds(start, CH), :] = (_pool_select(lane, r2, r4, r8, r16) - dmix)[:CH]
            return carry

        lax.fori_loop(1, 1 + nreal, chunk, 0)

    full = lambda r, c: pl.BlockSpec((r, c), lambda j: (0, 0))
    return pl.pallas_call(
        body, name=name, grid=(1,),
        out_shape=(jax.ShapeDtypeStruct((T, 256), F32), jax.ShapeDtypeStruct((256, 256), F32),
                   jax.ShapeDtypeStruct((1, 256), F32)),
        in_specs=[full(T, 256), full(256, 256), full(1, 256), full(T, 256)],
        out_specs=(full(T, 256), full(256, 256), full(1, 256)), compiler_params=_cparams(("arbitrary",)),
    )(p, wbd, scale, dy)


def _hgrn_chunk(St, qr, fr, ir, gr, l0, l1, gn):
    rows = lax.broadcasted_iota(jnp.int32, (HG, HG), 0)
    cols = lax.broadcasted_iota(jnp.int32, (HG, HG), 1)
    causal = rows >= cols
    ltri = causal.astype(F32)
    lb = jax.nn.sigmoid(l0 - l1)
    sg = jax.nn.sigmoid(fr)
    logf = jnp.log(lb + (1.0 - lb) * sg)
    kk = (1.0 - lb) * (1.0 - sg)
    q = qr * jax.nn.sigmoid(qr)
    b = jnp.dot(ltri, logf, precision=lax.Precision.HIGH, preferred_element_type=F32)
    bl = jnp.sum(logf, axis=0, keepdims=True)
    bm = jnp.sum(jnp.where(_rows(HG) <= HG // 2, logf, 0.0), axis=0, keepdims=True)
    o = _dotf(q * jnp.exp(b), St, "NT")
    A = _dotf(q * jnp.exp(b - bm), kk * jnp.exp(bm - b), "NT")
    o = o + _dotf(jnp.where(causal, A, 0.0), ir)
    St_new = St * jnp.exp(bl) + _dotf(ir, kk * jnp.exp(bl - b), "TN")
    on = o * lax.rsqrt(jnp.mean(o * o, axis=-1, keepdims=True) + EPS) * gn
    return St_new, on * (gr * jax.nn.sigmoid(gr))


def _pairs_loop(n, step, init):
    u = 3 if n % 3 == 0 else 2

    def body(i, carry):
        for j in range(u):
            carry = step(u * i + j, carry)
        return carry

    return lax.fori_loop(0, n // u, body, init)


def _hgrn_specs(T):
    return [_colblock(T, 2), _colblock(T, 8), _colblock(T, 14), _colblock(T, 20), _vecblock(), _vecblock(),
            pl.BlockSpec((1, 128), lambda j: (0, 0))]


def _hgrn_fwd(p, l0, l1, gn, nreal, real_end, name):
    T = p.shape[0]
    nch = nreal * (CH // HG)

    def body(q_ref, f_ref, i_ref, g_ref, l0_ref, l1_ref, gn_ref, y_ref, s_ref):
        _zero_pad_rows(y_ref, CH, CH * (1 + nreal), T)

        def chunk(c, St):
            start = pl.multiple_of(CH + c * HG, HG)
            sl = pl.ds(start, HG)
            s_ref[0, c] = St
            St_new, y = _hgrn_chunk(St, q_ref[sl, :], f_ref[sl, :], i_ref[sl, :], g_ref[sl, :], l0_ref[...],
                                    l1_ref[...], gn_ref[...])
            y_ref[sl, :] = jnp.where(start + _rows(HG) < real_end, y, 0.0)
            return St_new

        _pairs_loop(nch, chunk, jnp.zeros((128, 128), F32))

    return pl.pallas_call(
        body, name=name, grid=(6,),
        out_shape=(jax.ShapeDtypeStruct((T, 768), F32), jax.ShapeDtypeStruct((6, nch, 128, 128), F32)),
        in_specs=_hgrn_specs(T),
        out_specs=(_colblock(T, 0), pl.BlockSpec((1, nch, 128, 128), lambda j: (j, 0, 0, 0))),
        compiler_params=_cparams(("parallel",)),
    )(p, p, p, p, l0, l1, gn)


def _hgrn_bwd(p, l0, l1, gn, states, dy, nreal, real_end, name):
    T = p.shape[0]
    nch = nreal * (CH // HG)

    def body(q_ref, f_ref, i_ref, g_ref, l0_ref, l1_ref, gn_ref, s_ref, dy_ref,
             dq_ref, df_ref, di_ref, dg_ref, dl0_ref, dl1_ref, dgn_ref):
        for r in (dq_ref, df_ref, di_ref, dg_ref):
            _zero_pad_rows(r, CH, CH * (1 + nreal), T)

        def chunk(k, carry):
            dSt, a0, a1, agn = carry
            c = nch - 1 - k
            start = pl.multiple_of(CH + c * HG, HG)
            sl = pl.ds(start, HG)
            _, vjp = jax.vjp(_hgrn_chunk, s_ref[0, c], q_ref[sl, :], f_ref[sl, :], i_ref[sl, :], g_ref[sl, :],
                             l0_ref[...], l1_ref[...], gn_ref[...])
            dyc = jnp.where(start + _rows(HG) < real_end, dy_ref[sl, :], 0.0)
            dS, dq, df, di, dg, d0, d1, dgn = vjp((dSt, dyc))
            dq_ref[sl, :] = dq
            df_ref[sl, :] = df
            di_ref[sl, :] = di
            dg_ref[sl, :] = dg
            return dS, a0 + d0, a1 + d1, agn + dgn

        z = jnp.zeros((1, 128), F32)
        _, a0, a1, agn = _pairs_loop(nch, chunk, (jnp.zeros((128, 128), F32), z, z, z))
        dl0_ref[...] = a0
        dl1_ref[...] = a1

        @pl.when(pl.program_id(0) == 0)
        def _():
            dgn_ref[...] = jnp.zeros_like(dgn_ref)

        dgn_ref[...] += agn

    big = jax.ShapeDtypeStruct((T, 768), F32)
    vec = jax.ShapeDtypeStruct((1, 768), F32)
    return pl.pallas_call(
        body, name=name, grid=(6,),
        out_shape=(big, big, big, big, vec, vec, jax.ShapeDtypeStruct((1, 128), F32)),
        in_specs=_hgrn_specs(T) + [pl.BlockSpec((1, nch, 128, 128), lambda j: (j, 0, 0, 0)), _colblock(T, 2)],
        out_specs=(_colblock(T, 0), _colblock(T, 0), _colblock(T, 0), _colblock(T, 0), _vecblock(), _vecblock(),
                   pl.BlockSpec((1, 128), lambda j: (0, 0))),
        compiler_params=_cparams(("arbitrary",), 60 * 2 ** 20),
    )(p, p, p, p, l0, l1, gn, states, dy)


def _glu(a, b):
    return a * jax.nn.sigmoid(b)


def _conv_post(cv, ln_g, ln_b):
    mu = jnp.mean(cv, axis=-1, keepdims=True)
    d = cv - mu
    var = jnp.mean(d * d, axis=-1, keepdims=True)
    un = d * lax.rsqrt(var + EPS) * ln_g + ln_b
    return un * jax.nn.sigmoid(un)


def _causal_conv(uh, w_ref, width, halo):
    acc = None
    for j in range(width):
        term = _down(uh, width - 1 - j) * w_ref[pl.ds(j, 1), :]
        acc = term if acc is None else acc + term
    return acc[halo:]


def _conf_fwd(p, cw, cb, lg, lb, nreal, real_end, name):
    T = p.shape[0]

    def body(a_ref, b_ref, w_ref, cb_ref, lg_ref, lb_ref, y_ref):
        _zero_pad_rows(y_ref, CH, CH * (1 + nreal), T)

        def chunk(c, carry):
            start = pl.multiple_of(c * CH, CH)
            ext = pl.ds(start - 32, CH + 32)
            cv = _causal_conv(_glu(a_ref[ext, :], b_ref[ext, :]), w_ref, CONV_W, 32) + cb_ref[...]
            y = _conv_post(cv, lg_ref[...], lb_ref[...])
            y_ref[pl.ds(start, CH), :] = jnp.where(start + _rows(CH) < real_end, y, 0.0)
            return carry

        lax.fori_loop(1, 1 + nreal, chunk, 0)

    return pl.pallas_call(
        body, name=name, grid=(4,), out_shape=jax.ShapeDtypeStruct((T, 512), F32),
        in_specs=[_colblock(T, 0), _colblock(T, 4), _vecblock(32), _vecblock(), _vecblock(), _vecblock()],
        out_specs=_colblock(T, 0), compiler_params=_cparams(("parallel",)),
    )(p, p, cw, cb, lg, lb)


def _conf_bwd(p, cw, cb, lg, lb, dy, nreal, real_end, name):
    T = p.shape[0]

    def body(a_ref, b_ref, w_ref, cb_ref, lg_ref, lb_ref, dy_ref, da_ref, db_ref, dw_ref, dcb_ref, dlg_ref, dlb_ref):
        _zero_pad_rows(da_ref, CH, CH * (1 + nreal), T)
        _zero_pad_rows(db_ref, CH, CH * (1 + nreal), T)
        for r in (dw_ref, dcb_ref, dlg_ref, dlb_ref):
            r[...] = jnp.zeros_like(r)

        def chunk(c, carry):
            start = pl.multiple_of(c * CH, CH)
            ext = pl.ds(start - 32, CH + 64)
            ue = _glu(a_ref[ext, :], b_ref[ext, :])
            cv = _causal_conv(ue, w_ref, CONV_W, 32) + cb_ref[...]
            dye = jnp.where(start + _rows(CH + 32) < real_end, dy_ref[pl.ds(start, CH + 32), :], 0.0)
            _, vjp_cur = jax.vjp(_conv_post, cv[:CH], lg_ref[...], lb_ref[...])
            dc_cur, dlg, dlb = vjp_cur(dye[:CH])
            _, vjp_halo = jax.vjp(_conv_post, cv[CH:], lg_ref[...], lb_ref[...])
            dce = jnp.concatenate([dc_cur, vjp_halo(dye[CH:])[0]], axis=0)
            dlg_ref[...] += dlg
            dlb_ref[...] += dlb
            dcb_ref[...] += jnp.sum(dc_cur, axis=0, keepdims=True)
            du = None
            for j in range(CONV_W):
                w_j = w_ref[pl.ds(j, 1), :]
                term = _up(dce, CONV_W - 1 - j)[:CH] * w_j
                du = term if du is None else du + term
                dw_ref[pl.ds(j, 1), :] += jnp.sum(dc_cur * _up(ue, 2 + j)[:CH], axis=0, keepdims=True)
            cur = pl.ds(start, CH)
            _, vjp_glu = jax.vjp(_glu, a_ref[cur, :], b_ref[cur, :])
            da, db = vjp_glu(du)
            da_ref[cur, :] = da
            db_ref[cur, :] = db
            return carry

        lax.fori_loop(1, 1 + nreal, chunk, 0)

    big = jax.ShapeDtypeStruct((T, 512), F32)
    vec = jax.ShapeDtypeStruct((1, 512), F32)
    return pl.pallas_call(
        body, name=name, grid=(4,), out_shape=(big, big, jax.ShapeDtypeStruct((32, 512), F32), vec, vec, vec),
        in_specs=[_colblock(T, 0), _colblock(T, 4), _vecblock(32), _vecblock(), _vecblock(), _vecblock(),
                  _colblock(T, 0)],
        out_specs=(_colblock(T, 0), _colblock(T, 0), _vecblock(32), _vecblock(), _vecblock(), _vecblock()),
        compiler_params=_cparams(("parallel",)),
    )(p, p, cw, cb, lg, lb, dy)


def _softplus_neg(lam):
    e = jnp.exp(-lam)
    small = e * (1.0 - e * (0.5 - e * (1.0 / 3.0 - e * 0.25)))
    return jnp.where(e < 0.02, small, jnp.log(1.0 + e))


def _one_minus_exp(x):
    series = -x * (1.0 + x * (0.5 + x * (1.0 / 6.0 + x * (1.0 / 24.0 + x * (1.0 / 120.0)))))
    return jnp.where(x > -0.05, series, 1.0 - jnp.exp(x))


def _lru_pre(u, wa, wx, ba, bx, lam, first):
    r = jax.nn.sigmoid(_dot(u, wa) + ba)
    i = jax.nn.sigmoid(_dot(u, wx) + bx)
    log_a = -LRU_C * r * _softplus_neg(lam)
    a = jnp.exp(log_a)
    mult = jnp.sqrt(_one_minus_exp(2.0 * log_a))
    return a, jnp.where(first, 1.0, mult) * (i * u)


def _gelu_gate(gate, h):
    inner = math.sqrt(2.0 / math.pi) * (gate + 0.044715 * (gate * gate * gate))
    return 0.5 * gate * (1.0 + jnp.tanh(inner)) * h


def _lru_specs(T):
    mat = pl.BlockSpec((1, 128, 128), lambda j: (j, 0, 0))
    return [_colblock(T, 8), _colblock(T, 12), _vecblock(8), _vecblock(), mat, mat, _vecblock(), _vecblock(),
            _vecblock()]


def _lru_fwd(p, cw, cb, wa, wx, ba, bx, lam, nreal, real_end, name):
    T = p.shape[0]

    def body(x_ref, g_ref, w_ref, cb_ref, wa_ref, wx_ref, ba_ref, bx_ref, lam_ref, y_ref, h_ref):
        _zero_pad_rows(y_ref, CH, CH * (1 + nreal), T)
        _zero_pad_rows(h_ref, CH, CH * (1 + nreal), T)
        rows = _rows(CH)

        def chunk(c, hprev):
            start = pl.multiple_of(c * CH, CH)
            u = _causal_conv(x_ref[pl.ds(start - 8, CH + 8), :], w_ref, LRU_W, 8) + cb_ref[...]
            A, B = _lru_pre(u, wa_ref[0], wx_ref[0], ba_ref[...], bx_ref[...], lam_ref[...], start + rows == CH)
            s = 1
            while s < CH:
                B = A * jnp.where(rows >= s, _down(B, s), 0.0) + B
                A = A * jnp.where(rows >= s, _down(A, s), 1.0)
                s *= 2
            h = B + A * hprev
            cur = pl.ds(start, CH)
            h_ref[cur, :] = h
            y_ref[cur, :] = jnp.where(start + rows < real_end, _gelu_gate(g_ref[cur, :], h), 0.0)
            return jnp.sum(jnp.where(rows == CH - 1, h, 0.0), axis=0, keepdims=True)

        lax.fori_loop(1, 1 + nreal, chunk, jnp.zeros((1, 128), F32))

    big = jax.ShapeDtypeStruct((T, 512), F32)
    return pl.pallas_call(
        body, name=name, grid=(4,), out_shape=(big, big), in_specs=_lru_specs(T),
        out_specs=(_colblock(T, 0), _colblock(T, 0)), compiler_params=_cparams(("parallel",)),
    )(p, p, cw, cb, wa, wx, ba, bx, lam)


def _lru_bwd(p, cw, cb, wa, wx, ba, bx, lam, hs, dy, nreal, real_end, name):
    T = p.shape[0]

    def body(x_ref, g_ref, w_ref, cb_ref, wa_ref, wx_ref, ba_ref, bx_ref, lam_ref, h_ref, dy_ref,
             dx_ref, dgate_ref, dw_ref, dcb_ref, dwa_ref, dwx_ref, dba_ref, dbx_ref, dlam_ref):
        _zero_pad_rows(dx_ref, CH, CH * (1 + nreal), T)
        _zero_pad_rows(dgate_ref, CH, CH * (1 + nreal), T)
        for r in (dw_ref, dcb_ref, dwa_ref, dwx_ref, dba_ref, dbx_ref, dlam_ref):
            r[...] = jnp.zeros_like(r)
        rows = _rows(CH)

        def chunk(k, carry):
            cdh, du_head = carry
            c = nreal - k
            start = pl.multiple_of(c * CH, CH)
            cur = pl.ds(start, CH)
            xe = x_ref[pl.ds(start - 8, CH + 8), :]
            u = _causal_conv(xe, w_ref, LRU_W, 8) + cb_ref[...]
            first = start + rows == CH
            (a, _), vjp_pre = jax.vjp(lambda uu, m1, m2, b1, b2, ll: _lru_pre(uu, m1, m2, b1, b2, ll, first),
                                      u, wa_ref[0], wx_ref[0], ba_ref[...], bx_ref[...], lam_ref[...])
            h = h_ref[cur, :]
            hm1 = _down(h_ref[pl.ds(start - 8, CH + 8), :], 1)[8:]
            _, vjp_post = jax.vjp(_gelu_gate, g_ref[cur, :], h)
            dgate, D = vjp_post(jnp.where(start + rows < real_end, dy_ref[cur, :], 0.0))
            dgate_ref[cur, :] = dgate
            D = D + jnp.where(rows == CH - 1, cdh, 0.0)
            C = jnp.where(rows < CH - 1, _up(a, 1), 0.0)
            s = 1
            while s < CH:
                D = D + C * jnp.where(rows + s < CH, _up(D, s), 0.0)
                C = C * jnp.where(rows + s < CH, _up(C, s), 1.0)
                s *= 2
            du, dwa, dwx, dba, dbx, dlam = vjp_pre((D * hm1, D))
            dwa_ref[0] += dwa
            dwx_ref[0] += dwx
            dba_ref[...] += dba
            dbx_ref[...] += dbx
            dlam_ref[...] += dlam
            dcb_ref[...] += jnp.sum(du, axis=0, keepdims=True)
            due = jnp.concatenate([du, du_head], axis=0)
            dx = None
            for j in range(LRU_W):
                term = _up(due, LRU_W - 1 - j)[:CH] * w_ref[pl.ds(j, 1), :]
                dx = term if dx is None else dx + term
                dw_ref[pl.ds(j, 1), :] += jnp.sum(du * _up(xe, 8 - (LRU_W - 1) + j)[:CH], axis=0, keepdims=True)
            dx_ref[cur, :] = dx
            return jnp.sum(jnp.where(rows == 0, a * D, 0.0), axis=0, keepdims=True), du[:8]

        lax.fori_loop(0, nreal, chunk, (jnp.zeros((1, 128), F32), jnp.zeros((8, 128), F32)))

    big = jax.ShapeDtypeStruct((T, 512), F32)
    vec = jax.ShapeDtypeStruct((1, 512), F32)
    mat = jax.ShapeDtypeStruct((4, 128, 128), F32)
    matspec = pl.BlockSpec((1, 128, 128), lambda j: (j, 0, 0))
    return pl.pallas_call(
        body, name=name, grid=(4,),
        out_shape=(big, big, jax.ShapeDtypeStruct((8, 512), F32), vec, mat, mat, vec, vec, vec),
        in_specs=_lru_specs(T) + [_colblock(T, 0), _colblock(T, 4)],
        out_specs=(_colblock(T, 0), _colblock(T, 0), _vecblock(8), _vecblock(), matspec, matspec, _vecblock(),
                   _vecblock(), _vecblock()),
        compiler_params=_cparams(("parallel",)),
    )(p, p, cw, cb, wa, wx, ba, bx, lam, hs, dy)


def _ffn_forward(h, gamma, wg, wu, wd, tag):
    xn = _rms_fwd(h, gamma, f"rms_fwd_{tag}")
    g, u, a = _ffn_up(xn, wg, wu, f"ffn_up_{tag}")
    if callable(wd):
        wd = wd(a)
    out = _mm([(a, wd)], "NN", f"ffn_down_{tag}", res=h, res_scale=0.5)
    return out, (h, xn, g, u, a)


def _after(w, tok):
    return w if tok is None else w + tok.astype(w.dtype)


def _ffn_backward(saved, gamma, wg, wu, wd, dout, tok, tag, emit_one=None):
    h, xn, p, q, a = saved
    dout, dout16 = dout
    wd = _after(wd, tok)
    dwd = _mm([(a, dout16)], "TN", f"ffn_dwd_{tag}", res_scale=0.5, out_dtype=MXU)
    if emit_one is not None:
        wd = _after(wd, emit_one('wd', dwd))
    dg, du = _ffn_dact(dout16, wd, p, q, 0.5, f"ffn_dact_{tag}")
    dwg = _mm([(xn, dg)], "TN", f"ffn_dwg_{tag}", out_dtype=MXU)
    if emit_one is not None:
        wg = _after(wg, emit_one('wg', dwg))
    dwu = _mm([(xn, du)], "TN", f"ffn_dwu_{tag}", out_dtype=MXU)
    if emit_one is not None:
        wu = _after(wu, emit_one('wu', dwu))
    dxn = _mm([(dg, wg), (du, wu)], "NT", f"ffn_dxn_{tag}")
    dh, dh16, dgamma = _rms_bwd(h, gamma, dxn, dout, f"rms_bwd_{tag}")
    return (dh, dh16), dgamma, dwg, dwu, dwd


def _blockdiag(w, per):
    n, k, _ = w.shape
    out = jnp.zeros((n // per, per * k, per * k), w.dtype)
    for i in range(per):
        out = out.at[:, i * k:(i + 1) * k, i * k:(i + 1) * k].set(w[i::per])
    return out


def _blockdiag_grad(g, per, k):
    parts = [g[:, i * k:(i + 1) * k, i * k:(i + 1) * k] for i in range(per)]
    return jnp.stack(parts, axis=1).reshape(-1, k, k)


def _local_step(x, tgt, W, fetch, emit):
    fetch(0, x)
    seq, D = x.shape
    lr = N_META + seq
    nreal = -(-lr // CH)
    T = CH * (nreal + 2)
    if T > 640 and T % 640:
        T += 640 - T % 640
    lo, real_end = CH + N_META, CH + lr
    zf = lambda n: jnp.zeros((n, D), F32)
    h0 = jnp.concatenate([zf(CH), W['meta_tokens'], x, zf(T - real_end)], axis=0)
    tgt_p = jnp.concatenate([zf(lo), tgt, zf(T - real_end)], axis=0)
    row = lambda v: v.reshape(1, -1)
    G = {}

    h = h0
    saved = []
    for l in range(2):
        wd1 = W['ffn1_wd', l] if l else (lambda after: (fetch(1, after), W['ffn1_wd', 0])[1])
        h, s1 = _ffn_forward(h, row(W['ffn1_norm'][l]), W['ffn1_wg', l], W['ffn1_wu', l], wd1, f"a{l}")
        hm = h
        fetch(3 * l + 2, hm)
        xn = _rms_fwd(hm, row(W['mix_norm'][l]), f"rms_fwd_mix{l}")
        if l == 0:
            p = _mm([(xn, W['w_in_even'])], "NN", "in_even")
            wbd = _blockdiag(W['pool_w'][0], 4)[0]
            l0, l1 = row(W['hgrn_lb_logits'][0]), row(W['hgrn_lb_logits'][1])
            ya = _pool_fwd(p, wbd, W['pool_scale'], nreal, real_end, "pool_fwd")
            yb, states = _hgrn_fwd(p, l0, l1, W['hgrn_gnorm'], nreal, real_end, "hgrn_fwd")
            wo = W['w_out_even']
            h = _mm([(ya, wo[:256]), (yb, wo[256:])], "NN", "out_even", res=hm)
            sm = (hm, xn, p, wbd, l0, l1, ya, yb, states)
        else:
            p = _mm([(xn, W['w_in_odd'])], "NN", "in_odd")
            cw = jnp.pad(W['conv_w'][0], ((0, 1), (0, 0)))
            lw = jnp.pad(W['lru_conv_w'][0], ((0, 4), (0, 0)))
            wa, wx = _blockdiag(W['lru_wa'][0], 2), _blockdiag(W['lru_wx'][0], 2)
            yc = _conf_fwd(p, cw, W['conv_b'], W['conv_ln_g'], W['conv_ln_b'], nreal, real_end, "conf_fwd")
            yd, hs = _lru_fwd(p, lw, W['lru_conv_b'], wa, wx, W['lru_ba'], W['lru_bx'], W['lru_lambda'], nreal,
                              real_end, "lru_fwd")
            wo = W['w_out_odd']
            h = _mm([(yc, wo[:512]), (yd, wo[512:])], "NN", "out_odd", res=hm)
            sm = (hm, xn, p, cw, lw, wa, wx, yc, yd, hs)
        fetch(3 * l + 3, h)
        h, s2 = _ffn_forward(h, row(W['ffn2_norm'][l]), W['ffn2_wg', l], W['ffn2_wu', l], W['ffn2_wd', l], f"b{l}")
        if l == 0:
            fetch(4, h)
        saved.append((s1, sm, s2))

    loss8, dh, dh16, dfin = _loss_head(h, row(W['final_norm']), tgt_p, lo, real_end, "loss_head")
    dh = (dh, dh16)
    G['final_norm'] = dfin[0]

    per_layer = {k: [None, None] for k in ('ffn1_norm', 'mix_norm', 'ffn2_norm')}
    tok = None
    for l in (1, 0):
        s1, sm, s2 = saved[l]
        dh, dn, dwg, dwu, dwd = _ffn_backward(s2, row(W['ffn2_norm'][l]), W['ffn2_wg', l], W['ffn2_wu', l],
                                              W['ffn2_wd', l], dh, tok, f"b{l}")
        per_layer['ffn2_norm'][l] = dn[0]
        tok = emit(f"ffn2_{l}", [('ffn2_wg', l, dwg), ('ffn2_wu', l, dwu), ('ffn2_wd', l, dwd)])
        if l == 0:
            hm, xn, p, wbd, l0, l1, ya, yb, states = sm
            wo, wi = _after(W['w_out_even'], tok), W['w_in_even']
            dwo = jnp.concatenate([_mm([(ya, dh[1])], "TN", "dwo_even_a", out_dtype=MXU),
                                   _mm([(yb, dh[1])], "TN", "dwo_even_b", out_dtype=MXU)], axis=0)
            dy = _mm([(dh[1], wo)], "NT", "dy_even")
            dpp, dwbd, dsc = _pool_bwd(p, wbd, W['pool_scale'], dy, nreal, real_end, "pool_bwd")
            dq, df, di, dg, dl0, dl1, dgn = _hgrn_bwd(p, l0, l1, W['hgrn_gnorm'], states, dy, nreal, real_end,
                                                      "hgrn_bwd")
            G['pool_w'] = _blockdiag_grad(dwbd[None], 4, 64)[None]
            G['pool_scale'] = dsc
            G['hgrn_lb_logits'] = jnp.concatenate([dl0, dl1], axis=0)
            G['hgrn_gnorm'] = dgn
            parts = [dpp, dq, df, di, dg]
            offs = [0, 256, 1024, 1792, 2560, 3328]
            dwi = jnp.concatenate(
                [_mm([(xn, dpart)], "TN", f"dwi_even_{k}", out_dtype=MXU) for k, dpart in enumerate(parts)], axis=1)
            dxn = _mm([(dpart, wi[:, offs[k]:offs[k + 1]]) for k, dpart in enumerate(parts)], "NT", "dxn_even")
            tok = emit("even", [('w_in_even', None, dwi), ('w_out_even', None, dwo)])
        else:
            hm, xn, p, cw, lw, wa, wx, yc, yd, hs = sm
            wo, wi = _after(W['w_out_odd'], tok), W['w_in_odd']
            dwo = jnp.concatenate([_mm([(yc, dh[1])], "TN", "dwo_odd_c", out_dtype=MXU),
                                   _mm([(yd, dh[1])], "TN", "dwo_odd_d", out_dtype=MXU)], axis=0)
            dy = _mm([(dh[1], wo)], "NT", "dy_odd")
            da, db, dcw, dcb, dlg, dlb = _conf_bwd(p, cw, W['conv_b'], W['conv_ln_g'], W['conv_ln_b'], dy, nreal,
                                                   real_end, "conf_bwd")
            dx, dgate, dlw, dlcb, dwa, dwx, dba, dbx, dlam = _lru_bwd(
                p, lw, W['lru_conv_b'], wa, wx, W['lru_ba'], W['lru_bx'], W['lru_lambda'], hs, dy, nreal, real_end,
                "lru_bwd")
            G['conv_w'], G['conv_b'], G['conv_ln_g'], G['conv_ln_b'] = dcw[None, :CONV_W], dcb, dlg, dlb
            G['lru_conv_w'], G['lru_conv_b'] = dlw[None, :LRU_W], dlcb
            G['lru_wa'] = _blockdiag_grad(dwa, 2, 64)[None]
            G['lru_wx'] = _blockdiag_grad(dwx, 2, 64)[None]
            G['lru_ba'], G['lru_bx'], G['lru_lambda'] = dba, dbx, dlam
            parts = [da, db, dx, dgate]
            dwi = jnp.concatenate(
                [_mm([(xn, dpart)], "TN", f"dwi_odd_{k}", out_dtype=MXU) for k, dpart in enumerate(parts)], axis=1)
            dxn = _mm([(dpart, wi[:, 512 * k:512 * (k + 1)]) for k, dpart in enumerate(parts)], "NT", "dxn_odd")
            tok = emit("odd", [('w_in_odd', None, dwi), ('w_out_odd', None, dwo)])
        dh, dh16, dn = _rms_bwd(hm, _after(row(W['mix_norm'][l]), tok), dxn, dh[0], f"rms_bwd_mix{l}")
        dh = (dh, dh16)
        per_layer['mix_norm'][l] = dn[0]
        one = None if l == 1 else (lambda sfx, g: emit(f"ffn1_0_{sfx}", [('ffn1_' + sfx, 0, g)]))
        dh, dn, dwg, dwu, dwd = _ffn_backward(s1, row(W['ffn1_norm'][l]), W['ffn1_wg', l], W['ffn1_wu', l],
                                              W['ffn1_wd', l], dh, None, f"a{l}", one)
        per_layer['ffn1_norm'][l] = dn[0]
        if l == 1:
            tok = emit("ffn1_1", [('ffn1_wg', l, dwg), ('ffn1_wu', l, dwu), ('ffn1_wd', l, dwd)])
    for k, v in per_layer.items():
        G[k] = jnp.stack(v, axis=0)
    G['meta_tokens'] = dh[0][CH:lo]
    return loss8[0, 0], dh[0][lo:real_end], G


def _pack(arrs):
    flat = jnp.concatenate([a.reshape(-1).astype(F32) for a in arrs])
    n = flat.shape[0]
    padded = -(-n // 1024) * 1024
    return jnp.pad(flat, (0, padded - n)).reshape(-1, 128)


def _unpack(packed, shapes):
    flat = packed.reshape(-1)
    out, off = [], 0
    for s in shapes:
        n = math.prod(s)
        out.append(flat[off:off + n].reshape(s))
        off += n
    return out


def _to_full(gathered, axis):
    s = gathered.shape[1:]
    return jnp.moveaxis(gathered, 0, axis).reshape(s[:axis] + (NDEV * s[axis],) + s[axis + 1:])


def _to_slots(full, axis):
    s = full.shape
    return jnp.moveaxis(full.reshape(s[:axis] + (NDEV, s[axis] // NDEV) + s[axis + 1:]), axis, 0)


def kernel(x, meta_tokens, ffn1_norm, ffn1_wg, ffn1_wu, ffn1_wd, mix_norm, ffn2_norm, ffn2_wg, ffn2_wu, ffn2_wd, w_in_even, pool_w, pool_scale, hgrn_lb_logits, hgrn_gnorm, w_out_even, w_in_odd, conv_w, conv_b, conv_ln_g, conv_ln_b, lru_conv_w, lru_conv_b, lru_wa, lru_ba, lru_wx, lru_bx, lru_lambda, w_out_odd, final_norm, loss_target, m_meta_tokens, m_ffn1_norm, m_ffn1_wg, m_ffn1_wu, m_ffn1_wd, m_mix_norm, m_ffn2_norm, m_ffn2_wg, m_ffn2_wu, m_ffn2_wd, m_w_in_even, m_pool_w, m_pool_scale, m_hgrn_lb_logits, m_hgrn_gnorm, m_w_out_even, m_w_in_odd, m_conv_w, m_conv_b, m_conv_ln_g, m_conv_ln_b, m_lru_conv_w, m_lru_conv_b, m_lru_wa, m_lru_ba, m_lru_wx, m_lru_bx, m_lru_lambda, m_w_out_odd, m_final_norm, v_meta_tokens, v_ffn1_norm, v_ffn1_wg, v_ffn1_wu, v_ffn1_wd, v_mix_norm, v_ffn2_norm, v_ffn2_wg, v_ffn2_wu, v_ffn2_wd, v_w_in_even, v_pool_w, v_pool_scale, v_hgrn_lb_logits, v_hgrn_gnorm, v_w_out_even, v_w_in_odd, v_conv_w, v_conv_b, v_conv_ln_g, v_conv_ln_b, v_lru_conv_w, v_lru_conv_b, v_lru_wa, v_lru_ba, v_lru_wx, v_lru_bx, v_lru_lambda, v_w_out_odd, v_final_norm):
    args = (meta_tokens, ffn1_norm, ffn1_wg, ffn1_wu, ffn1_wd, mix_norm, ffn2_norm, ffn2_wg, ffn2_wu, ffn2_wd, w_in_even, pool_w, pool_scale, hgrn_lb_logits, hgrn_gnorm, w_out_even, w_in_odd, conv_w, conv_b, conv_ln_g, conv_ln_b, lru_conv_w, lru_conv_b, lru_wa, lru_ba, lru_wx, lru_bx, lru_lambda, w_out_odd, final_norm)
    margs = (m_meta_tokens, m_ffn1_norm, m_ffn1_wg, m_ffn1_wu, m_ffn1_wd, m_mix_norm, m_ffn2_norm, m_ffn2_wg, m_ffn2_wu, m_ffn2_wd, m_w_in_even, m_pool_w, m_pool_scale, m_hgrn_lb_logits, m_hgrn_gnorm, m_w_out_even, m_w_in_odd, m_conv_w, m_conv_b, m_conv_ln_g, m_conv_ln_b, m_lru_conv_w, m_lru_conv_b, m_lru_wa, m_lru_ba, m_lru_wx, m_lru_bx, m_lru_lambda, m_w_out_odd, m_final_norm)
    vargs = (v_meta_tokens, v_ffn1_norm, v_ffn1_wg, v_ffn1_wu, v_ffn1_wd, v_mix_norm, v_ffn2_norm, v_ffn2_wg, v_ffn2_wu, v_ffn2_wd, v_w_in_even, v_pool_w, v_pool_scale, v_hgrn_lb_logits, v_hgrn_gnorm, v_w_out_even, v_w_in_odd, v_conv_w, v_conv_b, v_conv_ln_g, v_conv_ln_b, v_lru_conv_w, v_lru_conv_b, v_lru_wa, v_lru_ba, v_lru_wx, v_lru_bx, v_lru_lambda, v_w_out_odd, v_final_norm)
    Wl = dict(zip(W_NAMES, args))
    Ml = dict(zip(W_NAMES, margs))
    Vl = dict(zip(W_NAMES, vargs))

    small_shapes = [Wl[n].shape for n in SMALL_SHARDED]
    ffn = lambda p, l: [(p + s, l) for s in ('_wg', '_wu', '_wd')]
    mix = lambda p: [('w_in_' + p, None), ('w_out_' + p, None)]
    ggroups = [ffn('ffn1', 0)[:2], ffn('ffn1', 0)[2:], mix('even'), ffn('ffn2', 0), ffn('ffn1', 1), mix('odd'),
               ffn('ffn2', 1)]
    shard = lambda n, l: Wl[n][0 if l is None else l].astype(MXU)
    srcs = [[shard(n, l) for n, l in g] for g in ggroups]
    srcs[0] = [_pack([Wl[n] for n in SMALL_SHARDED])] + srcs[0]
    handles, _ = _exchange_start(srcs, True, "gather_start")
    W = {n: Wl[n] for n in REPLICATED}

    def fetch(k, after):
        lands = _exchange_wait(handles[k], True, after, f"gather_wait_{k}")
        if k == 0:
            per_dev = [_unpack(lands[0][d], small_shapes) for d in range(NDEV)]
            for j, n in enumerate(SMALL_SHARDED):
                W[n] = _to_full(jnp.stack([per_dev[d][j] for d in range(NDEV)]), SHARD_AXIS[n])
            lands = lands[1:]
        for (n, l), g in zip(ggroups[k], lands):
            W[n if l is None else (n, l)] = _to_full(g, SHARD_AXIS[n] - 1)

    pending = []

    def emit(tag, grads):
        slots = [_to_slots(g.astype(MXU), SHARD_AXIS[n] - 1) for n, _, g in grads]
        hs, token = _exchange_start([slots], False, f"scatter_start_{tag}")
        pending.append((tag, hs[0], [(n, l) for n, l, _ in grads]))
        return token[0, 0]

    loss_part, grad_x, G = _local_step(x[0], loss_target[0], W, fetch, emit)

    small_slots = [_to_slots(G[n].astype(F32), SHARD_AXIS[n]) for n in SMALL_SHARDED]
    send = [jnp.stack([_pack([s[d] for s in small_slots]) for d in range(NDEV)]), _pack([G[n] for n in REPLICATED]),
            jnp.broadcast_to(loss_part, (8, 128))]
    got = _exchange(send, [False, True, True], "scatter_small")
    loss = jnp.sum(got[2][:, 0, 0])
    recv = {}
    for tag, handle, keys in pending:
        for key, r in zip(keys, _exchange_wait(handle, False, got[0], f"scatter_wait_{tag}")):
            recv[key] = r

    outs = {}
    for n in BIG:
        shp = Wl[n].shape
        C = shp[-1]
        rs = [recv[n, None]] if shp[0] == 1 else [recv[n, l] for l in range(shp[0])]
        res = _adamw(rs, Wl[n].reshape(-1, C), Ml[n].reshape(-1, C), Vl[n].reshape(-1, C), f"adamw_{n}")
        outs[n] = [o.reshape(shp) for o in res]
    for names, r, tag in ((SMALL_SHARDED, got[0], "small"), (REPLICATED, got[1], "repl")):
        shapes = [Wl[n].shape for n in names]
        res = _adamw([r], _pack([Wl[n] for n in names]), _pack([Ml[n] for n in names]),
                     _pack([Vl[n] for n in names]), f"adamw_{tag}")
        unp = [_unpack(o, shapes) for o in res]
        for k, n in enumerate(names):
            outs[n] = [unp[j][k] for j in range(4)]

    result = [loss, grad_x[None]]
    for j in range(4):
        result += [outs[n][j] for n in W_NAMES]
    return tuple(result)
```

```python
import functools
import math

import jax
import jax.numpy as jnp
from jax import lax
from jax.experimental import pallas as pl
from jax.experimental.pallas import tpu as pltpu

F32 = jnp.float32
MXU = jnp.bfloat16
EPS = 1e-6
CH = 128
HG = 128
N_META = 16
CONV_W = 31
LRU_W = 4
LRU_C = 8.0
VMEM_LIMIT = 48 * 2 ** 20
MM_VMEM_BUDGET = 36 * 2 ** 20
ADAM_LR, ADAM_B1, ADAM_B2, ADAM_EPS, ADAM_WD, ADAM_STEP = 0.001, 0.9, 0.999, 1e-08, 0.01, 10
MESH_AXES = ("x", "y", "c")
NDEV = 8

W_NAMES = ['meta_tokens', 'ffn1_norm', 'ffn1_wg', 'ffn1_wu', 'ffn1_wd', 'mix_norm', 'ffn2_norm', 'ffn2_wg', 'ffn2_wu',
           'ffn2_wd', 'w_in_even', 'pool_w', 'pool_scale', 'hgrn_lb_logits', 'hgrn_gnorm', 'w_out_even', 'w_in_odd',
           'conv_w', 'conv_b', 'conv_ln_g', 'conv_ln_b', 'lru_conv_w', 'lru_conv_b', 'lru_wa', 'lru_ba', 'lru_wx',
           'lru_bx', 'lru_lambda', 'w_out_odd', 'final_norm']
SHARD_AXIS = {'meta_tokens': 1, 'ffn1_wg': 2, 'ffn1_wu': 2, 'ffn1_wd': 1, 'ffn2_wg': 2, 'ffn2_wu': 2, 'ffn2_wd': 1,
              'w_in_even': 2, 'w_out_even': 1, 'w_in_odd': 2, 'conv_w': 2, 'conv_b': 1, 'conv_ln_g': 1,
              'conv_ln_b': 1, 'lru_conv_w': 2, 'lru_conv_b': 1, 'lru_ba': 1, 'lru_bx': 1, 'lru_lambda': 1,
              'w_out_odd': 1}
BIG = ['ffn1_wg', 'ffn1_wu', 'ffn1_wd', 'ffn2_wg', 'ffn2_wu', 'ffn2_wd', 'w_in_even', 'w_out_even', 'w_in_odd',
       'w_out_odd']
SMALL_SHARDED = [n for n in W_NAMES if n in SHARD_AXIS and n not in BIG]
REPLICATED = [n for n in W_NAMES if n not in SHARD_AXIS]


def _cparams(sem=None, vmem=VMEM_LIMIT):
    return pltpu.CompilerParams(dimension_semantics=sem, vmem_limit_bytes=vmem)


def _tiles(n):
    return [c for c in range(128, n + 1, 128) if n % c == 0] or [n]


def _tile(n, cap=1024):
    return max([c for c in _tiles(n) if c <= cap], default=_tiles(n)[0])


def _rowtile(n):
    for c in (256, 352, 128, 64, 32, 16, 8):
        if n % c == 0:
            return c
    return n


def _copies(srcs, lands, bcast, ssem, rsem, lsem):
    x, y, c = lax.axis_index("x"), lax.axis_index("y"), lax.axis_index("c")
    me = 4 * x + 2 * y + c
    locs, sends, recvs = [], [], []
    for a in range(len(srcs)):
        locs.append(pltpu.make_async_copy(srcs[a] if bcast[a] else srcs[a].at[me], lands[a].at[me], lsem.at[a]))
        for m in range(1, NDEV):
            px = 1 - x if (m >> 2) & 1 else x
            py = 1 - y if (m >> 1) & 1 else y
            pc = 1 - c if m & 1 else c
            peer = 4 * px + 2 * py + pc
            src = srcs[a] if bcast[a] else srcs[a].at[peer]
            k = a * NDEV + m
            for dst, out in ((lands[a].at[me], sends), (lands[a].at[peer], recvs)):
                out.append(pltpu.make_async_remote_copy(src_ref=src, dst_ref=dst, send_sem=ssem.at[k],
                                                        recv_sem=rsem.at[k], device_id=(px, py, pc),
                                                        device_id_type=pl.DeviceIdType.MESH))
    return locs, sends, recvs


def _land_shape(arr, bc):
    return (NDEV,) + tuple(arr.shape if bc else arr.shape[1:])


def _exchange(arrays, bcast, name):
    n = len(arrays)

    def body(*refs):
        locs, sends, recvs = _copies(refs[:n], refs[n:2 * n], bcast, refs[2 * n], refs[2 * n + 1], refs[2 * n + 2])
        for d in locs + sends:
            d.start()
        for r in recvs:
            r.wait_recv()
        for s in sends:
            s.wait_send()
        for loc in locs:
            loc.wait()

    out_shape = tuple(jax.ShapeDtypeStruct(_land_shape(arr, bc), arr.dtype) for arr, bc in zip(arrays, bcast))
    any_spec = pl.BlockSpec(memory_space=pl.ANY)
    return pl.pallas_call(
        body, name=name, out_shape=out_shape, in_specs=[any_spec] * n, out_specs=tuple([any_spec] * n),
        scratch_shapes=[pltpu.SemaphoreType.DMA((n * NDEV,)), pltpu.SemaphoreType.DMA((n * NDEV,)),
                        pltpu.SemaphoreType.DMA((n,))],
    )(*arrays)


_HBM = pl.BlockSpec(memory_space=pltpu.HBM)
_SEM = pl.BlockSpec(memory_space=pltpu.SEMAPHORE)
_EFFECT = pltpu.SideEffectType.DATAFLOW_SIDE_EFFECTING


def _exchange_start(groups, bcast, name):
    sizes = [len(g) for g in groups]
    srcs = [a for g in groups for a in g]
    n, ng = len(srcs), len(groups)
    lands = [lax.empty(_land_shape(a, bcast), a.dtype) for a in srcs]

    def body(*refs):
        off = 0
        for gi, sz in enumerate(sizes):
            sem = refs[2 * n + 3 * gi:2 * n + 3 * gi + 3]
            locs, sends, _ = _copies(refs[off:off + sz], refs[n + off:n + off + sz], [bcast] * sz, *sem)
            for d in locs + sends:
                d.start()
            off += sz
        refs[-1][...] = jnp.zeros((8, 128), F32)

    sems = []
    for sz in sizes:
        sems += [pltpu.SemaphoreType.DMA((sz * NDEV,)), pltpu.SemaphoreType.DMA((sz * NDEV,)),
                 pltpu.SemaphoreType.DMA((sz,))]
    thru = [pltpu.HBM(a.shape, a.dtype) for a in srcs + lands]
    outs = pl.pallas_call(
        body, name=name, out_shape=tuple(sems + thru + [jax.ShapeDtypeStruct((8, 128), F32)]),
        in_specs=[_HBM] * (2 * n),
        out_specs=tuple([_SEM] * (3 * ng) + [_HBM] * (2 * n) + [pl.BlockSpec(memory_space=pltpu.VMEM)]),
        input_output_aliases={i: 3 * ng + i for i in range(2 * n)},
        compiler_params=pltpu.CompilerParams(has_side_effects=_EFFECT),
    )(*[pltpu.with_memory_space_constraint(a, pltpu.HBM) for a in srcs + lands])
    handles, off = [], 0
    for gi, sz in enumerate(sizes):
        handles.append((outs[3 * gi:3 * gi + 3], outs[3 * ng + off:3 * ng + off + sz],
                        outs[3 * ng + n + off:3 * ng + n + off + sz]))
        off += sz
    return handles, outs[-1]


def _exchange_wait(handle, bcast, after, name):
    sems, srcs, lands = handle
    n = len(srcs)

    def body(*refs):
        locs, sends, recvs = _copies(refs[:n], refs[n:2 * n], [bcast] * n, *refs[2 * n:2 * n + 3])
        for r in recvs:
            r.wait_recv()
        for s in sends:
            s.wait_send()
        for loc in locs:
            loc.wait()

    outs = pl.pallas_call(
        body, name=name, out_shape=tuple(pltpu.HBM(a.shape, a.dtype) for a in list(srcs) + list(lands)),
        in_specs=[_HBM] * (2 * n) + [_SEM] * 3 + [pl.BlockSpec(memory_space=pl.ANY)],
        out_specs=tuple([_HBM] * (2 * n)), input_output_aliases={i: i for i in range(2 * n)},
        compiler_params=pltpu.CompilerParams(has_side_effects=_EFFECT),
    )(*srcs, *lands, *sems, after)
    return outs[n:]


def _adamw(recvs, w, m, v, name):
    R, C = w.shape
    nr = len(recvs)
    br = _rowtile(R // nr)
    nb0 = R // nr // br

    def body(*refs):
        w_ref, m_ref, v_ref, g_o, d_o, m_o, v_o = refs[nr:]
        g = None
        for j in range(nr):
            s = refs[j][0].astype(F32)
            for k in range(1, NDEV):
                s = s + refs[j][k].astype(F32)
            g = s if g is None else jnp.where(pl.program_id(0) >= j * nb0, s, g)
        mn = ADAM_B1 * m_ref[...] + (1.0 - ADAM_B1) * g
        vn = ADAM_B2 * v_ref[...] + (1.0 - ADAM_B2) * (g * g)
        m_hat = mn / (1.0 - ADAM_B1 ** ADAM_STEP)
        v_hat = vn / (1.0 - ADAM_B2 ** ADAM_STEP)
        g_o[...] = g
        d_o[...] = -ADAM_LR * (m_hat / (jnp.sqrt(v_hat) + ADAM_EPS) + ADAM_WD * w_ref[...])
        m_o[...] = mn
        v_o[...] = vn

    def rspec(j):
        return pl.BlockSpec((NDEV, br, C), lambda i: (0, jnp.clip(i - j * nb0, 0, nb0 - 1), 0))

    blk = pl.BlockSpec((br, C), lambda i: (i, 0))
    sds = jax.ShapeDtypeStruct((R, C), F32)
    return pl.pallas_call(
        body, name=name, grid=(R // br,), out_shape=(sds, sds, sds, sds),
        in_specs=[rspec(j) for j in range(nr)] + [blk, blk, blk], out_specs=(blk, blk, blk, blk),
        compiler_params=_cparams(("arbitrary",)),
    )(*recvs, w, m, v)


_DIMS = {"NN": ((1,), (0,)), "NT": ((1,), (1,)), "TN": ((0,), (0,))}


def _dot(a, b, mode="NN"):
    return lax.dot_general(a.astype(MXU), b.astype(MXU), (_DIMS[mode], ((), ())), preferred_element_type=F32)


def _dotf(a, b, mode="NN"):
    return lax.dot_general(a, b, (_DIMS[mode], ((), ())), precision=lax.Precision.HIGH,
                           preferred_element_type=F32)


def _mm(pairs, mode, name, res=None, res_scale=1.0, out_dtype=F32):
    a0, b0 = pairs[0]
    M = a0.shape[1] if mode == "TN" else a0.shape[0]
    N = b0.shape[0] if mode == "NT" else b0.shape[1]
    npairs = len(pairs)

    def vmem_bytes(tm, tn):
        total = tm * tn * 4 * (2 + (2 if res is not None else 0) + 2)
        for a, b in pairs:
            ka = a.shape[0] if mode == "TN" else a.shape[1]
            kb = b.shape[1] if mode == "NT" else b.shape[0]
            for k, t, arr in ((ka, tm, a), (kb, tn, b)):
                total += k * t * (2 * arr.dtype.itemsize + (2 if arr.dtype == F32 else 0))
        return total

    tm, tn = max(((tm, tn) for tm in _tiles(M) for tn in _tiles(N) if vmem_bytes(tm, tn) <= MM_VMEM_BUDGET),
                 key=lambda t: (t[0] * t[1], t[1]), default=(_tiles(M)[0], _tiles(N)[0]))

    def body(*refs):
        acc = None
        for p in range(npairs):
            d = _dot(refs[2 * p][...], refs[2 * p + 1][...], mode)
            acc = d if acc is None else acc + d
        if res_scale != 1.0:
            acc = res_scale * acc
        if res is not None:
            acc = refs[2 * npairs][...] + acc
        refs[-1][...] = acc.astype(out_dtype)

    in_specs, args = [], []
    for a, b in pairs:
        if mode == "TN":
            in_specs.append(pl.BlockSpec((a.shape[0], tm), lambda i, j: (0, i)))
        else:
            in_specs.append(pl.BlockSpec((tm, a.shape[1]), lambda i, j: (i, 0)))
        if mode == "NT":
            in_specs.append(pl.BlockSpec((tn, b.shape[1]), lambda i, j: (j, 0)))
        else:
            in_specs.append(pl.BlockSpec((b.shape[0], tn), lambda i, j: (0, j)))
        args += [a, b]
    if res is not None:
        in_specs.append(pl.BlockSpec((tm, tn), lambda i, j: (i, j)))
        args.append(res)
    return pl.pallas_call(
        body, name=name, grid=(M // tm, N // tn), out_shape=jax.ShapeDtypeStruct((M, N), out_dtype),
        in_specs=in_specs, out_specs=pl.BlockSpec((tm, tn), lambda i, j: (i, j)),
        compiler_params=_cparams(("parallel", "parallel")),
    )(*args)


def _rms_fwd(h, gamma, name):
    T, D = h.shape
    tm = _tile(T)

    def body(h_ref, g_ref, o_ref):
        x = h_ref[...]
        r = lax.rsqrt(jnp.mean(x * x, axis=-1, keepdims=True) + EPS)
        o_ref[...] = (x * r * g_ref[...]).astype(MXU)

    return pl.pallas_call(
        body, name=name, grid=(T // tm,), out_shape=jax.ShapeDtypeStruct((T, D), MXU),
        in_specs=[pl.BlockSpec((tm, D), lambda i: (i, 0)), pl.BlockSpec((1, D), lambda i: (0, 0))],
        out_specs=pl.BlockSpec((tm, D), lambda i: (i, 0)), compiler_params=_cparams(("parallel",)),
    )(h, gamma)


def _rms_bwd_math(x, gamma, dy):
    r = lax.rsqrt(jnp.mean(x * x, axis=-1, keepdims=True) + EPS)
    z = dy * gamma
    dx = r * z - x * (r * r * r) * jnp.mean(z * x, axis=-1, keepdims=True)
    dgamma = jnp.sum(dy * x * r, axis=0, keepdims=True)
    return dx, dgamma


def _rms_bwd(h, gamma, dxn, dres, name):
    T, D = h.shape
    tm = _tile(T)

    def body(h_ref, g_ref, dxn_ref, dres_ref, dh_ref, dh16_ref, dg_ref):
        dx, dgamma = _rms_bwd_math(h_ref[...], g_ref[...], dxn_ref[...])
        dh = dres_ref[...] + dx
        dh_ref[...] = dh
        dh16_ref[...] = dh.astype(MXU)

        @pl.when(pl.program_id(0) == 0)
        def _():
            dg_ref[...] = jnp.zeros_like(dg_ref)

        dg_ref[...] += dgamma

    row = pl.BlockSpec((tm, D), lambda i: (i, 0))
    vec = pl.BlockSpec((1, D), lambda i: (0, 0))
    return pl.pallas_call(
        body, name=name, grid=(T // tm,),
        out_shape=(jax.ShapeDtypeStruct((T, D), F32), jax.ShapeDtypeStruct((T, D), MXU),
                   jax.ShapeDtypeStruct((1, D), F32)),
        in_specs=[row, vec, row, row], out_specs=(row, row, vec), compiler_params=_cparams(("arbitrary",)),
    )(h, gamma, dxn, dres)


def _loss_head(h, gamma, tgt, lo, hi, name):
    T, D = h.shape
    tm = _tile(T)

    def body(h_ref, g_ref, t_ref, loss_ref, dh_ref, dh16_ref, dg_ref):
        i = pl.program_id(0)
        x = h_ref[...]
        r = lax.rsqrt(jnp.mean(x * x, axis=-1, keepdims=True) + EPS)
        y = x * r * g_ref[...]
        rows = i * tm + lax.broadcasted_iota(jnp.int32, (tm, 1), 0)
        valid = jnp.logical_and(rows >= lo, rows < hi)
        diff = jnp.where(valid, y - t_ref[...], 0.0)
        part = 0.5 * jnp.sum(jnp.sum(diff * diff, axis=-1, keepdims=True) / D, axis=0, keepdims=True)
        dx, dgamma = _rms_bwd_math(x, g_ref[...], diff / D)
        dh_ref[...] = dx
        dh16_ref[...] = dx.astype(MXU)

        @pl.when(i == 0)
        def _():
            dg_ref[...] = jnp.zeros_like(dg_ref)
            loss_ref[...] = jnp.zeros_like(loss_ref)

        dg_ref[...] += dgamma
        loss_ref[...] += jnp.broadcast_to(part, loss_ref.shape)

    row = pl.BlockSpec((tm, D), lambda i: (i, 0))
    vec = pl.BlockSpec((1, D), lambda i: (0, 0))
    lsp = pl.BlockSpec((8, 128), lambda i: (0, 0))
    return pl.pallas_call(
        body, name=name, grid=(T // tm,),
        out_shape=(jax.ShapeDtypeStruct((8, 128), F32), jax.ShapeDtypeStruct((T, D), F32),
                   jax.ShapeDtypeStruct((T, D), MXU), jax.ShapeDtypeStruct((1, D), F32)),
        in_specs=[row, vec, row], out_specs=(lsp, row, row, vec), compiler_params=_cparams(("arbitrary",)),
    )(h, gamma, tgt)


def _ffn_tiles(T, Fd):
    return (448 if T % 448 == 0 else _tile(T)), max(c for c in _tiles(Fd) if c <= 1536)


def _ffn_up(xn, wg, wu, name):
    T, D = xn.shape
    Fd = wg.shape[1]
    tm, tn = _ffn_tiles(T, Fd)

    def body(x_ref, wg_ref, wu_ref, p_ref, q_ref, a_ref):
        x = x_ref[...]
        g = _dot(x, wg_ref[...])
        u = _dot(x, wu_ref[...])
        sg = jax.nn.sigmoid(g)
        q = g * sg
        p_ref[...] = (u * (sg + q * (1.0 - sg))).astype(MXU)
        q_ref[...] = q.astype(MXU)
        a_ref[...] = (q * u).astype(MXU)

    wsp = pl.BlockSpec((D, tn), lambda i, j: (0, j))
    osp = pl.BlockSpec((tm, tn), lambda i, j: (i, j))
    sds = jax.ShapeDtypeStruct((T, Fd), MXU)
    return pl.pallas_call(
        body, name=name, grid=(T // tm, Fd // tn), out_shape=(sds, sds, sds),
        in_specs=[pl.BlockSpec((tm, D), lambda i, j: (i, 0)), wsp, wsp], out_specs=(osp, osp, osp),
        compiler_params=_cparams(("parallel", "parallel")),
    )(xn, wg, wu)


def _ffn_dact(dy, wd, p, q, scale, name):
    T, D = dy.shape
    Fd = wd.shape[0]
    tm, tn = _ffn_tiles(T, Fd)

    def body(dy_ref, wd_ref, p_ref, q_ref, dg_ref, du_ref):
        da = scale * _dot(dy_ref[...], wd_ref[...], "NT")
        dg_ref[...] = (da * p_ref[...].astype(F32)).astype(MXU)
        du_ref[...] = (da * q_ref[...].astype(F32)).astype(MXU)

    osp = pl.BlockSpec((tm, tn), lambda i, j: (i, j))
    sds = jax.ShapeDtypeStruct((T, Fd), MXU)
    return pl.pallas_call(
        body, name=name, grid=(T // tm, Fd // tn), out_shape=(sds, sds),
        in_specs=[pl.BlockSpec((tm, D), lambda i, j: (i, 0)), pl.BlockSpec((tn, D), lambda i, j: (j, 0)), osp, osp],
        out_specs=(osp, osp), compiler_params=_cparams(("parallel", "parallel")),
    )(dy, wd, p, q)


def _down(v, s):
    return v if s == 0 else pltpu.roll(v, s, 0)


def _up(v, s):
    return v if s == 0 else pltpu.roll(v, v.shape[0] - s, 0)


def _rows(n):
    return lax.broadcasted_iota(jnp.int32, (n, 1), 0)


def _zero_pad_rows(ref, lo_end, hi_start, T):
    ref[pl.ds(0, lo_end), :] = jnp.zeros((lo_end, ref.shape[1]), ref.dtype)
    if T > hi_start:
        ref[pl.ds(hi_start, T - hi_start), :] = jnp.zeros((T - hi_start, ref.shape[1]), ref.dtype)


def _colblock(T, off):
    return pl.BlockSpec((T, 128), lambda j: (0, off + j))


def _vecblock(rows=1):
    return pl.BlockSpec((rows, 128), lambda j: (0, j))


def _pool_lane_consts(n):
    lane = lax.broadcasted_iota(jnp.int32, (n, 256), 1)
    win = jnp.where(lane < 64, 2.0, jnp.where(lane < 128, 4.0, jnp.where(lane < 192, 8.0, 16.0)))
    return lane, win


def _pool_select(lane, s2, s4, s8, s16):
    return jnp.where(lane < 64, s2, jnp.where(lane < 128, s4, jnp.where(lane < 192, s8, s16)))


def _pool_mixed(xh, start):
    s2 = xh + _down(xh, 1)
    s4 = s2 + _down(s2, 2)
    s8 = s4 + _down(s4, 4)
    s16 = s8 + _down(s8, 8)
    n = xh.shape[0] - 16
    lane, _ = _pool_lane_consts(n + 16)
    _, win = _pool_lane_consts(n)
    t1 = (start - CH + 1 + _rows(n)).astype(F32)
    cnt = jnp.minimum(jnp.maximum(t1, 1.0), win)
    return _pool_select(lane, s2, s4, s8, s16)[16:] / cnt - xh[16:]


def _pool_fwd(p, wbd, scale, nreal, real_end, name):
    T = p.shape[0]

    def body(p_ref, w_ref, s_ref, y_ref):
        _zero_pad_rows(y_ref, CH, CH * (1 + nreal), T)

        def chunk(c, carry):
            start = pl.multiple_of(c * CH, CH)
            mixed = _pool_mixed(p_ref[pl.ds(start - 16, CH + 16), :], start)
            y = _dot(mixed, w_ref[...]) * s_ref[...]
            y_ref[pl.ds(start, CH), :] = jnp.where(start + _rows(CH) < real_end, y, 0.0)
            return carry

        _pairs_loop(nreal, lambda c, carry: chunk(c + 1, carry), 0)

    return pl.pallas_call(
        body, name=name, grid=(1,), out_shape=jax.ShapeDtypeStruct((T, 256), F32),
        in_specs=[pl.BlockSpec((T, 256), lambda j: (0, 0)), pl.BlockSpec((256, 256), lambda j: (0, 0)),
                  pl.BlockSpec((1, 256), lambda j: (0, 0))],
        out_specs=pl.BlockSpec((T, 256), lambda j: (0, 0)), compiler_params=_cparams(("arbitrary",)),
    )(p, wbd, scale)


def _pool_bwd(p, wbd, scale, dy, nreal, real_end, name):
    T = p.shape[0]

    def body(p_ref, w_ref, s_ref, dy_ref, dp_ref, dw_ref, ds_ref):
        _zero_pad_rows(dp_ref, CH, CH * (1 + nreal), T)
        dw_ref[...] = jnp.zeros_like(dw_ref)
        ds_ref[...] = jnp.zeros_like(ds_ref)

        def chunk(c, carry):
            start = pl.multiple_of(c * CH, CH)
            mixed = _pool_mixed(p_ref[pl.ds(start - 16, CH + 16), :], start)
            ypre = _dot(mixed, w_ref[...])
            n = CH + 16
            dye = jnp.where(start + _rows(n) < real_end, dy_ref[pl.ds(start, n), :], 0.0)
            dys = dye * s_ref[...]
            ds_ref[...] += jnp.sum(dye[:CH] * ypre, axis=0, keepdims=True)
            dw_ref[...] += _dot(mixed, dys[:CH], "TN")
            dmix = _dot(dys, w_ref[...], "NT")
            lane, win = _pool_lane_consts(n)
            t1 = (start - CH + 1 + _rows(n)).astype(F32)
            z = dmix / jnp.minimum(jnp.maximum(t1, 1.0), win)
            r2 = z + _up(z, 1)
            r4 = r2 + _up(r2, 2)
            r8 = r4 + _up(r4, 4)
            r16 = r8 + _up(r8, 8)
            dp_ref[pl.ds(start, CH), :] = (_pool_select(lane, r2, r4, r8, r16) - dmix)[:CH]
            return carry

        _pairs_loop(nreal, lambda c, carry: chunk(c + 1, carry), 0)

    full = lambda r, c: pl.BlockSpec((r, c), lambda j: (0, 0))
    return pl.pallas_call(
        body, name=name, grid=(1,),
        out_shape=(jax.ShapeDtypeStruct((T, 256), F32), jax.ShapeDtypeStruct((256, 256), F32),
                   jax.ShapeDtypeStruct((1, 256), F32)),
        in_specs=[full(T, 256), full(256, 256), full(1, 256), full(T, 256)],
        out_specs=(full(T, 256), full(256, 256), full(1, 256)), compiler_params=_cparams(("arbitrary",)),
    )(p, wbd, scale, dy)


def _hgrn_chunk(St, qr, fr, ir, gr, l0, l1, gn):
    rows = lax.broadcasted_iota(jnp.int32, (HG, HG), 0)
    cols = lax.broadcasted_iota(jnp.int32, (HG, HG), 1)
    causal = rows >= cols
    ltri = causal.astype(F32)
    lb = jax.nn.sigmoid(l0 - l1)
    sg = jax.nn.sigmoid(fr)
    logf = jnp.log(lb + (1.0 - lb) * sg)
    kk = (1.0 - lb) * (1.0 - sg)
    q = qr * jax.nn.sigmoid(qr)
    b = jnp.dot(ltri, logf, precision=lax.Precision.HIGH, preferred_element_type=F32)
    bl = jnp.sum(logf, axis=0, keepdims=True)
    bm = jnp.sum(jnp.where(_rows(HG) <= HG // 2, logf, 0.0), axis=0, keepdims=True)
    o = _dotf(q * jnp.exp(b), St, "NT")
    A = _dotf(q * jnp.exp(b - bm), kk * jnp.exp(bm - b), "NT")
    o = o + _dotf(jnp.where(causal, A, 0.0), ir)
    St_new = St * jnp.exp(bl) + _dotf(ir, kk * jnp.exp(bl - b), "TN")
    on = o * lax.rsqrt(jnp.mean(o * o, axis=-1, keepdims=True) + EPS) * gn
    return St_new, on * (gr * jax.nn.sigmoid(gr))


def _pairs_loop(n, step, init):
    u = 3 if n % 3 == 0 else 2

    def body(i, carry):
        for j in range(u):
            carry = step(u * i + j, carry)
        return carry

    return lax.fori_loop(0, n // u, body, init)


def _hgrn_specs(T):
    return [_colblock(T, 2), _colblock(T, 8), _colblock(T, 14), _colblock(T, 20), _vecblock(), _vecblock(),
            pl.BlockSpec((1, 128), lambda j: (0, 0))]


def _hgrn_fwd(p, l0, l1, gn, nreal, real_end, name):
    T = p.shape[0]
    nch = nreal * (CH // HG)

    def body(q_ref, f_ref, i_ref, g_ref, l0_ref, l1_ref, gn_ref, y_ref, s_ref):
        _zero_pad_rows(y_ref, CH, CH * (1 + nreal), T)

        def chunk(c, St):
            start = pl.multiple_of(CH + c * HG, HG)
            sl = pl.ds(start, HG)
            s_ref[0, c] = St
            St_new, y = _hgrn_chunk(St, q_ref[sl, :], f_ref[sl, :], i_ref[sl, :], g_ref[sl, :], l0_ref[...],
                                    l1_ref[...], gn_ref[...])
            y_ref[sl, :] = jnp.where(start + _rows(HG) < real_end, y, 0.0)
            return St_new

        _pairs_loop(nch, chunk, jnp.zeros((128, 128), F32))

    return pl.pallas_call(
        body, name=name, grid=(6,),
        out_shape=(jax.ShapeDtypeStruct((T, 768), F32), jax.ShapeDtypeStruct((6, nch, 128, 128), F32)),
        in_specs=_hgrn_specs(T),
        out_specs=(_colblock(T, 0), pl.BlockSpec((1, nch, 128, 128), lambda j: (j, 0, 0, 0))),
        compiler_params=_cparams(("parallel",)),
    )(p, p, p, p, l0, l1, gn)


def _hgrn_bwd(p, l0, l1, gn, states, dy, nreal, real_end, name):
    T = p.shape[0]
    nch = nreal * (CH // HG)

    def body(q_ref, f_ref, i_ref, g_ref, l0_ref, l1_ref, gn_ref, s_ref, dy_ref,
             dq_ref, df_ref, di_ref, dg_ref, dl0_ref, dl1_ref, dgn_ref):
        for r in (dq_ref, df_ref, di_ref, dg_ref):
            _zero_pad_rows(r, CH, CH * (1 + nreal), T)

        def chunk(k, carry):
            dSt, a0, a1, agn = carry
            c = nch - 1 - k
            start = pl.multiple_of(CH + c * HG, HG)
            sl = pl.ds(start, HG)
            _, vjp = jax.vjp(_hgrn_chunk, s_ref[0, c], q_ref[sl, :], f_ref[sl, :], i_ref[sl, :], g_ref[sl, :],
                             l0_ref[...], l1_ref[...], gn_ref[...])
            dyc = jnp.where(start + _rows(HG) < real_end, dy_ref[sl, :], 0.0)
            dS, dq, df, di, dg, d0, d1, dgn = vjp((dSt, dyc))
            dq_ref[sl, :] = dq
            df_ref[sl, :] = df
            di_ref[sl, :] = di
            dg_ref[sl, :] = dg
            return dS, a0 + d0, a1 + d1, agn + dgn

        z = jnp.zeros((1, 128), F32)
        _, a0, a1, agn = _pairs_loop(nch, chunk, (jnp.zeros((128, 128), F32), z, z, z))
        dl0_ref[...] = a0
        dl1_ref[...] = a1

        @pl.when(pl.program_id(0) == 0)
        def _():
            dgn_ref[...] = jnp.zeros_like(dgn_ref)

        dgn_ref[...] += agn

    big = jax.ShapeDtypeStruct((T, 768), F32)
    vec = jax.ShapeDtypeStruct((1, 768), F32)
    return pl.pallas_call(
        body, name=name, grid=(6,),
        out_shape=(big, big, big, big, vec, vec, jax.ShapeDtypeStruct((1, 128), F32)),
        in_specs=_hgrn_specs(T) + [pl.BlockSpec((1, nch, 128, 128), lambda j: (j, 0, 0, 0)), _colblock(T, 2)],
        out_specs=(_colblock(T, 0), _colblock(T, 0), _colblock(T, 0), _colblock(T, 0), _vecblock(), _vecblock(),
                   pl.BlockSpec((1, 128), lambda j: (0, 0))),
        compiler_params=_cparams(("arbitrary",), 60 * 2 ** 20),
    )(p, p, p, p, l0, l1, gn, states, dy)


def _glu(a, b):
    return a * jax.nn.sigmoid(b)


def _conv_post(cv, ln_g, ln_b):
    mu = jnp.mean(cv, axis=-1, keepdims=True)
    d = cv - mu
    var = jnp.mean(d * d, axis=-1, keepdims=True)
    un = d * lax.rsqrt(var + EPS) * ln_g + ln_b
    return un * jax.nn.sigmoid(un)


def _causal_conv(uh, w_ref, width, halo):
    acc = None
    for j in range(width):
        term = _down(uh, width - 1 - j) * w_ref[pl.ds(j, 1), :]
        acc = term if acc is None else acc + term
    return acc[halo:]


def _conf_fwd(p, cw, cb, lg, lb, nreal, real_end, name):
    T = p.shape[0]

    def body(a_ref, b_ref, w_ref, cb_ref, lg_ref, lb_ref, y_ref):
        _zero_pad_rows(y_ref, CH, CH * (1 + nreal), T)

        def chunk(c, carry):
            start = pl.multiple_of(c * CH, CH)
            ext = pl.ds(start - 32, CH + 32)
            cv = _causal_conv(_glu(a_ref[ext, :], b_ref[ext, :]), w_ref, CONV_W, 32) + cb_ref[...]
            y = _conv_post(cv, lg_ref[...], lb_ref[...])
            y_ref[pl.ds(start, CH), :] = jnp.where(start + _rows(CH) < real_end, y, 0.0)
            return carry

        _pairs_loop(nreal, lambda c, carry: chunk(c + 1, carry), 0)

    return pl.pallas_call(
        body, name=name, grid=(4,), out_shape=jax.ShapeDtypeStruct((T, 512), F32),
        in_specs=[_colblock(T, 0), _colblock(T, 4), _vecblock(32), _vecblock(), _vecblock(), _vecblock()],
        out_specs=_colblock(T, 0), compiler_params=_cparams(("parallel",)),
    )(p, p, cw, cb, lg, lb)


def _conf_bwd(p, cw, cb, lg, lb, dy, nreal, real_end, name):
    T = p.shape[0]

    def body(a_ref, b_ref, w_ref, cb_ref, lg_ref, lb_ref, dy_ref, da_ref, db_ref, dw_ref, dcb_ref, dlg_ref, dlb_ref):
        _zero_pad_rows(da_ref, CH, CH * (1 + nreal), T)
        _zero_pad_rows(db_ref, CH, CH * (1 + nreal), T)
        for r in (dw_ref, dcb_ref, dlg_ref, dlb_ref):
            r[...] = jnp.zeros_like(r)

        def chunk(c, carry):
            start = pl.multiple_of(c * CH, CH)
            ext = pl.ds(start - 32, CH + 64)
            ue = _glu(a_ref[ext, :], b_ref[ext, :])
            cv = _causal_conv(ue, w_ref, CONV_W, 32) + cb_ref[...]
            dye = jnp.where(start + _rows(CH + 32) < real_end, dy_ref[pl.ds(start, CH + 32), :], 0.0)
            _, vjp_cur = jax.vjp(_conv_post, cv[:CH], lg_ref[...], lb_ref[...])
            dc_cur, dlg, dlb = vjp_cur(dye[:CH])
            _, vjp_halo = jax.vjp(_conv_post, cv[CH:], lg_ref[...], lb_ref[...])
            dce = jnp.concatenate([dc_cur, vjp_halo(dye[CH:])[0]], axis=0)
            dlg_ref[...] += dlg
            dlb_ref[...] += dlb
            dcb_ref[...] += jnp.sum(dc_cur, axis=0, keepdims=True)
            du = None
            for j in range(CONV_W):
                w_j = w_ref[pl.ds(j, 1), :]
                term = _up(dce, CONV_W - 1 - j)[:CH] * w_j
                du = term if du is None else du + term
                dw_ref[pl.ds(j, 1), :] += jnp.sum(dc_cur * _up(ue, 2 + j)[:CH], axis=0, keepdims=True)
            cur = pl.ds(start, CH)
            _, vjp_glu = jax.vjp(_glu, a_ref[cur, :], b_ref[cur, :])
            da, db = vjp_glu(du)
            da_ref[cur, :] = da
            db_ref[cur, :] = db
            return carry

        _pairs_loop(nreal, lambda c, carry: chunk(c + 1, carry), 0)

    big = jax.ShapeDtypeStruct((T, 512), F32)
    vec = jax.ShapeDtypeStruct((1, 512), F32)
    return pl.pallas_call(
        body, name=name, grid=(4,), out_shape=(big, big, jax.ShapeDtypeStruct((32, 512), F32), vec, vec, vec),
        in_specs=[_colblock(T, 0), _colblock(T, 4), _vecblock(32), _vecblock(), _vecblock(), _vecblock(),
                  _colblock(T, 0)],
        out_specs=(_colblock(T, 0), _colblock(T, 0), _vecblock(32), _vecblock(), _vecblock(), _vecblock()),
        compiler_params=_cparams(("parallel",)),
    )(p, p, cw, cb, lg, lb, dy)


def _softplus_neg(lam):
    e = jnp.exp(-lam)
    small = e * (1.0 - e * (0.5 - e * (1.0 / 3.0 - e * 0.25)))
    return jnp.where(e < 0.02, small, jnp.log(1.0 + e))


def _one_minus_exp(x):
    series = -x * (1.0 + x * (0.5 + x * (1.0 / 6.0 + x * (1.0 / 24.0 + x * (1.0 / 120.0)))))
    return jnp.where(x > -0.05, series, 1.0 - jnp.exp(x))


def _lru_pre(u, wa, wx, ba, bx, lam, first):
    r = jax.nn.sigmoid(_dot(u, wa) + ba)
    i = jax.nn.sigmoid(_dot(u, wx) + bx)
    log_a = -LRU_C * r * _softplus_neg(lam)
    a = jnp.exp(log_a)
    mult = jnp.sqrt(_one_minus_exp(2.0 * log_a))
    return a, jnp.where(first, 1.0, mult) * (i * u)


def _gelu_gate(gate, h):
    inner = math.sqrt(2.0 / math.pi) * (gate + 0.044715 * (gate * gate * gate))
    return 0.5 * gate * (1.0 + jnp.tanh(inner)) * h


def _lru_specs(T):
    mat = pl.BlockSpec((1, 128, 128), lambda j: (j, 0, 0))
    return [_colblock(T, 8), _colblock(T, 12), _vecblock(8), _vecblock(), mat, mat, _vecblock(), _vecblock(),
            _vecblock()]


def _lru_fwd(p, cw, cb, wa, wx, ba, bx, lam, nreal, real_end, name):
    T = p.shape[0]

    def body(x_ref, g_ref, w_ref, cb_ref, wa_ref, wx_ref, ba_ref, bx_ref, lam_ref, y_ref, h_ref):
        _zero_pad_rows(y_ref, CH, CH * (1 + nreal), T)
        _zero_pad_rows(h_ref, CH, CH * (1 + nreal), T)
        rows = _rows(CH)

        def chunk(c, hprev):
            start = pl.multiple_of(c * CH, CH)
            u = _causal_conv(x_ref[pl.ds(start - 8, CH + 8), :], w_ref, LRU_W, 8) + cb_ref[...]
            A, B = _lru_pre(u, wa_ref[0], wx_ref[0], ba_ref[...], bx_ref[...], lam_ref[...], start + rows == CH)
            s = 1
            while s < CH:
                B = A * jnp.where(rows >= s, _down(B, s), 0.0) + B
                A = A * jnp.where(rows >= s, _down(A, s), 1.0)
                s *= 2
            h = B + A * hprev
            cur = pl.ds(start, CH)
            h_ref[cur, :] = h
            y_ref[cur, :] = jnp.where(start + rows < real_end, _gelu_gate(g_ref[cur, :], h), 0.0)
            return jnp.sum(jnp.where(rows == CH - 1, h, 0.0), axis=0, keepdims=True)

        _pairs_loop(nreal, lambda c, carry: chunk(c + 1, carry), jnp.zeros((1, 128), F32))

    big = jax.ShapeDtypeStruct((T, 512), F32)
    return pl.pallas_call(
        body, name=name, grid=(4,), out_shape=(big, big), in_specs=_lru_specs(T),
        out_specs=(_colblock(T, 0), _colblock(T, 0)), compiler_params=_cparams(("parallel",)),
    )(p, p, cw, cb, wa, wx, ba, bx, lam)


def _lru_bwd(p, cw, cb, wa, wx, ba, bx, lam, hs, dy, nreal, real_end, name):
    T = p.shape[0]

    def body(x_ref, g_ref, w_ref, cb_ref, wa_ref, wx_ref, ba_ref, bx_ref, lam_ref, h_ref, dy_ref,
             dx_ref, dgate_ref, dw_ref, dcb_ref, dwa_ref, dwx_ref, dba_ref, dbx_ref, dlam_ref):
        _zero_pad_rows(dx_ref, CH, CH * (1 + nreal), T)
        _zero_pad_rows(dgate_ref, CH, CH * (1 + nreal), T)
        for r in (dw_ref, dcb_ref, dwa_ref, dwx_ref, dba_ref, dbx_ref, dlam_ref):
            r[...] = jnp.zeros_like(r)
        rows = _rows(CH)

        def chunk(k, carry):
            cdh, du_head = carry
            c = nreal - k
            start = pl.multiple_of(c * CH, CH)
            cur = pl.ds(start, CH)
            xe = x_ref[pl.ds(start - 8, CH + 8), :]
            u = _causal_conv(xe, w_ref, LRU_W, 8) + cb_ref[...]
            first = start + rows == CH
            (a, _), vjp_pre = jax.vjp(lambda uu, m1, m2, b1, b2, ll: _lru_pre(uu, m1, m2, b1, b2, ll, first),
                                      u, wa_ref[0], wx_ref[0], ba_ref[...], bx_ref[...], lam_ref[...])
            h = h_ref[cur, :]
            hm1 = _down(h_ref[pl.ds(start - 8, CH + 8), :], 1)[8:]
            _, vjp_post = jax.vjp(_gelu_gate, g_ref[cur, :], h)
            dgate, D = vjp_post(jnp.where(start + rows < real_end, dy_ref[cur, :], 0.0))
            dgate_ref[cur, :] = dgate
            D = D + jnp.where(rows == CH - 1, cdh, 0.0)
            C = jnp.where(rows < CH - 1, _up(a, 1), 0.0)
            s = 1
            while s < CH:
                D = D + C * jnp.where(rows + s < CH, _up(D, s), 0.0)
                C = C * jnp.where(rows + s < CH, _up(C, s), 1.0)
                s *= 2
            du, dwa, dwx, dba, dbx, dlam = vjp_pre((D * hm1, D))
            dwa_ref[0] += dwa
            dwx_ref[0] += dwx
            dba_ref[...] += dba
            dbx_ref[...] += dbx
            dlam_ref[...] += dlam
            dcb_ref[...] += jnp.sum(du, axis=0, keepdims=True)
            due = jnp.concatenate([du, du_head], axis=0)
            dx = None
            for j in range(LRU_W):
                term = _up(due, LRU_W - 1 - j)[:CH] * w_ref[pl.ds(j, 1), :]
                dx = term if dx is None else dx + term
                dw_ref[pl.ds(j, 1), :] += jnp.sum(du * _up(xe, 8 - (LRU_W - 1) + j)[:CH], axis=0, keepdims=True)
            dx_ref[cur, :] = dx
            return jnp.sum(jnp.where(rows == 0, a * D, 0.0), axis=0, keepdims=True), du[:8]

        _pairs_loop(nreal, chunk, (jnp.zeros((1, 128), F32), jnp.zeros((8, 128), F32)))

    big = jax.ShapeDtypeStruct((T, 512), F32)
    vec = jax.ShapeDtypeStruct((1, 512), F32)
    mat = jax.ShapeDtypeStruct((4, 128, 128), F32)
    matspec = pl.BlockSpec((1, 128, 128), lambda j: (j, 0, 0))
    return pl.pallas_call(
        body, name=name, grid=(4,),
        out_shape=(big, big, jax.ShapeDtypeStruct((8, 512), F32), vec, mat, mat, vec, vec, vec),
        in_specs=_lru_specs(T) + [_colblock(T, 0), _colblock(T, 4)],
        out_specs=(_colblock(T, 0), _colblock(T, 0), _vecblock(8), _vecblock(), matspec, matspec, _vecblock(),
                   _vecblock(), _vecblock()),
        compiler_params=_cparams(("parallel",)),
    )(p, p, cw, cb, wa, wx, ba, bx, lam, hs, dy)


def _ffn_forward(h, gamma, wg, wu, wd, tag):
    xn = _rms_fwd(h, gamma, f"rms_fwd_{tag}")
    g, u, a = _ffn_up(xn, wg, wu, f"ffn_up_{tag}")
    if callable(wd):
        wd = wd(a)
    out = _mm([(a, wd)], "NN", f"ffn_down_{tag}", res=h, res_scale=0.5)
    return out, (h, xn, g, u, a)


def _after(w, tok):
    return w if tok is None else w + tok.astype(w.dtype)


def _ffn_backward(saved, gamma, wg, wu, wd, dout, tok, tag, emit_one=None):
    h, xn, p, q, a = saved
    dout, dout16 = dout
    wd = _after(wd, tok)
    dwd = _mm([(a, dout16)], "TN", f"ffn_dwd_{tag}", res_scale=0.5, out_dtype=MXU)
    if emit_one is not None:
        wd = _after(wd, emit_one('wd', dwd))
    dg, du = _ffn_dact(dout16, wd, p, q, 0.5, f"ffn_dact_{tag}")
    dwg = _mm([(xn, dg)], "TN", f"ffn_dwg_{tag}", out_dtype=MXU)
    if emit_one is not None:
        wg = _after(wg, emit_one('wg', dwg))
    dwu = _mm([(xn, du)], "TN", f"ffn_dwu_{tag}", out_dtype=MXU)
    if emit_one is not None:
        wu = _after(wu, emit_one('wu', dwu))
    dxn = _mm([(dg, wg), (du, wu)], "NT", f"ffn_dxn_{tag}")
    dh, dh16, dgamma = _rms_bwd(h, gamma, dxn, dout, f"rms_bwd_{tag}")
    return (dh, dh16), dgamma, dwg, dwu, dwd


def _blockdiag(w, per):
    n, k, _ = w.shape
    out = jnp.zeros((n // per, per * k, per * k), w.dtype)
    for i in range(per):
        out = out.at[:, i * k:(i + 1) * k, i * k:(i + 1) * k].set(w[i::per])
    return out


def _blockdiag_grad(g, per, k):
    parts = [g[:, i * k:(i + 1) * k, i * k:(i + 1) * k] for i in range(per)]
    return jnp.stack(parts, axis=1).reshape(-1, k, k)


def _local_step(x, tgt, W, fetch, emit):
    fetch(0, x)
    seq, D = x.shape
    lr = N_META + seq
    nreal = -(-lr // CH)
    T = CH * (nreal + 2)
    if T > 640 and T % 640:
        T += 640 - T % 640
    lo, real_end = CH + N_META, CH + lr
    zf = lambda n: jnp.zeros((n, D), F32)
    h0 = jnp.concatenate([zf(CH), W['meta_tokens'], x, zf(T - real_end)], axis=0)
    tgt_p = jnp.concatenate([zf(lo), tgt, zf(T - real_end)], axis=0)
    row = lambda v: v.reshape(1, -1)
    G = {}

    h = h0
    saved = []
    for l in range(2):
        wd1 = W['ffn1_wd', l] if l else (lambda after: (fetch(1, after), W['ffn1_wd', 0])[1])
        h, s1 = _ffn_forward(h, row(W['ffn1_norm'][l]), W['ffn1_wg', l], W['ffn1_wu', l], wd1, f"a{l}")
        hm = h
        fetch(3 * l + 2, hm)
        xn = _rms_fwd(hm, row(W['mix_norm'][l]), f"rms_fwd_mix{l}")
        if l == 0:
            p = _mm([(xn, W['w_in_even'])], "NN", "in_even")
            wbd = _blockdiag(W['pool_w'][0], 4)[0]
            l0, l1 = row(W['hgrn_lb_logits'][0]), row(W['hgrn_lb_logits'][1])
            ya = _pool_fwd(p, wbd, W['pool_scale'], nreal, real_end, "pool_fwd")
            yb, states = _hgrn_fwd(p, l0, l1, W['hgrn_gnorm'], nreal, real_end, "hgrn_fwd")
            wo = W['w_out_even']
            h = _mm([(ya, wo[:256]), (yb, wo[256:])], "NN", "out_even", res=hm)
            sm = (hm, xn, p, wbd, l0, l1, ya, yb, states)
        else:
            p = _mm([(xn, W['w_in_odd'])], "NN", "in_odd")
            cw = jnp.pad(W['conv_w'][0], ((0, 1), (0, 0)))
            lw = jnp.pad(W['lru_conv_w'][0], ((0, 4), (0, 0)))
            wa, wx = _blockdiag(W['lru_wa'][0], 2), _blockdiag(W['lru_wx'][0], 2)
            yc = _conf_fwd(p, cw, W['conv_b'], W['conv_ln_g'], W['conv_ln_b'], nreal, real_end, "conf_fwd")
            yd, hs = _lru_fwd(p, lw, W['lru_conv_b'], wa, wx, W['lru_ba'], W['lru_bx'], W['lru_lambda'], nreal,
                              real_end, "lru_fwd")
            wo = W['w_out_odd']
            h = _mm([(yc, wo[:512]), (yd, wo[512:])], "NN", "out_odd", res=hm)
            sm = (hm, xn, p, cw, lw, wa, wx, yc, yd, hs)
        fetch(3 * l + 3, h)
        h, s2 = _ffn_forward(h, row(W['ffn2_norm'][l]), W['ffn2_wg', l], W['ffn2_wu', l], W['ffn2_wd', l], f"b{l}")
        if l == 0:
            fetch(4, h)
        saved.append((s1, sm, s2))

    loss8, dh, dh16, dfin = _loss_head(h, row(W['final_norm']), tgt_p, lo, real_end, "loss_head")
    dh = (dh, dh16)
    G['final_norm'] = dfin[0]

    per_layer = {k: [None, None] for k in ('ffn1_norm', 'mix_norm', 'ffn2_norm')}
    tok = None
    for l in (1, 0):
        s1, sm, s2 = saved[l]
        dh, dn, dwg, dwu, dwd = _ffn_backward(s2, row(W['ffn2_norm'][l]), W['ffn2_wg', l], W['ffn2_wu', l],
                                              W['ffn2_wd', l], dh, tok, f"b{l}")
        per_layer['ffn2_norm'][l] = dn[0]
        tok = emit(f"ffn2_{l}", [('ffn2_wg', l, dwg), ('ffn2_wu', l, dwu), ('ffn2_wd', l, dwd)])
        if l == 0:
            hm, xn, p, wbd, l0, l1, ya, yb, states = sm
            wo, wi = _after(W['w_out_even'], tok), W['w_in_even']
            dwo = jnp.concatenate([_mm([(ya, dh[1])], "TN", "dwo_even_a", out_dtype=MXU),
                                   _mm([(yb, dh[1])], "TN", "dwo_even_b", out_dtype=MXU)], axis=0)
            dy = _mm([(dh[1], wo)], "NT", "dy_even")
            dpp, dwbd, dsc = _pool_bwd(p, wbd, W['pool_scale'], dy, nreal, real_end, "pool_bwd")
            dq, df, di, dg, dl0, dl1, dgn = _hgrn_bwd(p, l0, l1, W['hgrn_gnorm'], states, dy, nreal, real_end,
                                                      "hgrn_bwd")
            G['pool_w'] = _blockdiag_grad(dwbd[None], 4, 64)[None]
            G['pool_scale'] = dsc
            G['hgrn_lb_logits'] = jnp.concatenate([dl0, dl1], axis=0)
            G['hgrn_gnorm'] = dgn
            parts = [dpp, dq, df, di, dg]
            offs = [0, 256, 1024, 1792, 2560, 3328]
            dwi = jnp.concatenate(
                [_mm([(xn, dpart)], "TN", f"dwi_even_{k}", out_dtype=MXU) for k, dpart in enumerate(parts)], axis=1)
            dxn = _mm([(dpart, wi[:, offs[k]:offs[k + 1]]) for k, dpart in enumerate(parts)], "NT", "dxn_even")
            tok = emit("even", [('w_in_even', None, dwi), ('w_out_even', None, dwo)])
        else:
            hm, xn, p, cw, lw, wa, wx, yc, yd, hs = sm
            wo, wi = _after(W['w_out_odd'], tok), W['w_in_odd']
            dwo = jnp.concatenate([_mm([(yc, dh[1])], "TN", "dwo_odd_c", out_dtype=MXU),
                                   _mm([(yd, dh[1])], "TN", "dwo_odd_d", out_dtype=MXU)], axis=0)
            dy = _mm([(dh[1], wo)], "NT", "dy_odd")
            da, db, dcw, dcb, dlg, dlb = _conf_bwd(p, cw, W['conv_b'], W['conv_ln_g'], W['conv_ln_b'], dy, nreal,
                                                   real_end, "conf_bwd")
            dx, dgate, dlw, dlcb, dwa, dwx, dba, dbx, dlam = _lru_bwd(
                p, lw, W['lru_conv_b'], wa, wx, W['lru_ba'], W['lru_bx'], W['lru_lambda'], hs, dy, nreal, real_end,
                "lru_bwd")
            G['conv_w'], G['conv_b'], G['conv_ln_g'], G['conv_ln_b'] = dcw[None, :CONV_W], dcb, dlg, dlb
            G['lru_conv_w'], G['lru_conv_b'] = dlw[None, :LRU_W], dlcb
            G['lru_wa'] = _blockdiag_grad(dwa, 2, 64)[None]
            G['lru_wx'] = _blockdiag_grad(dwx, 2, 64)[None]
            G['lru_ba'], G['lru_bx'], G['lru_lambda'] = dba, dbx, dlam
            parts = [da, db, dx, dgate]
            dwi = jnp.concatenate(
                [_mm([(xn, dpart)], "TN", f"dwi_odd_{k}", out_dtype=MXU) for k, dpart in enumerate(parts)], axis=1)
            dxn = _mm([(dpart, wi[:, 512 * k:512 * (k + 1)]) for k, dpart in enumerate(parts)], "NT", "dxn_odd")
            tok = emit("odd", [('w_in_odd', None, dwi), ('w_out_odd', None, dwo)])
        dh, dh16, dn = _rms_bwd(hm, _after(row(W['mix_norm'][l]), tok), dxn, dh[0], f"rms_bwd_mix{l}")
        dh = (dh, dh16)
        per_layer['mix_norm'][l] = dn[0]
        one = None if l == 1 else (lambda sfx, g: emit(f"ffn1_0_{sfx}", [('ffn1_' + sfx, 0, g)]))
        dh, dn, dwg, dwu, dwd = _ffn_backward(s1, row(W['ffn1_norm'][l]), W['ffn1_wg', l], W['ffn1_wu', l],
                                              W['ffn1_wd', l], dh, None, f"a{l}", one)
        per_layer['ffn1_norm'][l] = dn[0]
        if l == 1:
            tok = emit("ffn1_1", [('ffn1_wg', l, dwg), ('ffn1_wu', l, dwu), ('ffn1_wd', l, dwd)])
    for k, v in per_layer.items():
        G[k] = jnp.stack(v, axis=0)
    G['meta_tokens'] = dh[0][CH:lo]
    return loss8[0, 0], dh[0][lo:real_end], G


def _pack(arrs):
    flat = jnp.concatenate([a.reshape(-1).astype(F32) for a in arrs])
    n = flat.shape[0]
    padded = -(-n // 1024) * 1024
    return jnp.pad(flat, (0, padded - n)).reshape(-1, 128)


def _unpack(packed, shapes):
    flat = packed.reshape(-1)
    out, off = [], 0
    for s in shapes:
        n = math.prod(s)
        out.append(flat[off:off + n].reshape(s))
        off += n
    return out


def _to_full(gathered, axis):
    s = gathered.shape[1:]
    return jnp.moveaxis(gathered, 0, axis).reshape(s[:axis] + (NDEV * s[axis],) + s[axis + 1:])


def _to_slots(full, axis):
    s = full.shape
    return jnp.moveaxis(full.reshape(s[:axis] + (NDEV, s[axis] // NDEV) + s[axis + 1:]), axis, 0)


def kernel(x, meta_tokens, ffn1_norm, ffn1_wg, ffn1_wu, ffn1_wd, mix_norm, ffn2_norm, ffn2_wg, ffn2_wu, ffn2_wd, w_in_even, pool_w, pool_scale, hgrn_lb_logits, hgrn_gnorm, w_out_even, w_in_odd, conv_w, conv_b, conv_ln_g, conv_ln_b, lru_conv_w, lru_conv_b, lru_wa, lru_ba, lru_wx, lru_bx, lru_lambda, w_out_odd, final_norm, loss_target, m_meta_tokens, m_ffn1_norm, m_ffn1_wg, m_ffn1_wu, m_ffn1_wd, m_mix_norm, m_ffn2_norm, m_ffn2_wg, m_ffn2_wu, m_ffn2_wd, m_w_in_even, m_pool_w, m_pool_scale, m_hgrn_lb_logits, m_hgrn_gnorm, m_w_out_even, m_w_in_odd, m_conv_w, m_conv_b, m_conv_ln_g, m_conv_ln_b, m_lru_conv_w, m_lru_conv_b, m_lru_wa, m_lru_ba, m_lru_wx, m_lru_bx, m_lru_lambda, m_w_out_odd, m_final_norm, v_meta_tokens, v_ffn1_norm, v_ffn1_wg, v_ffn1_wu, v_ffn1_wd, v_mix_norm, v_ffn2_norm, v_ffn2_wg, v_ffn2_wu, v_ffn2_wd, v_w_in_even, v_pool_w, v_pool_scale, v_hgrn_lb_logits, v_hgrn_gnorm, v_w_out_even, v_w_in_odd, v_conv_w, v_conv_b, v_conv_ln_g, v_conv_ln_b, v_lru_conv_w, v_lru_conv_b, v_lru_wa, v_lru_ba, v_lru_wx, v_lru_bx, v_lru_lambda, v_w_out_odd, v_final_norm):
    args = (meta_tokens, ffn1_norm, ffn1_wg, ffn1_wu, ffn1_wd, mix_norm, ffn2_norm, ffn2_wg, ffn2_wu, ffn2_wd, w_in_even, pool_w, pool_scale, hgrn_lb_logits, hgrn_gnorm, w_out_even, w_in_odd, conv_w, conv_b, conv_ln_g, conv_ln_b, lru_conv_w, lru_conv_b, lru_wa, lru_ba, lru_wx, lru_bx, lru_lambda, w_out_odd, final_norm)
    margs = (m_meta_tokens, m_ffn1_norm, m_ffn1_wg, m_ffn1_wu, m_ffn1_wd, m_mix_norm, m_ffn2_norm, m_ffn2_wg, m_ffn2_wu, m_ffn2_wd, m_w_in_even, m_pool_w, m_pool_scale, m_hgrn_lb_logits, m_hgrn_gnorm, m_w_out_even, m_w_in_odd, m_conv_w, m_conv_b, m_conv_ln_g, m_conv_ln_b, m_lru_conv_w, m_lru_conv_b, m_lru_wa, m_lru_ba, m_lru_wx, m_lru_bx, m_lru_lambda, m_w_out_odd, m_final_norm)
    vargs = (v_meta_tokens, v_ffn1_norm, v_ffn1_wg, v_ffn1_wu, v_ffn1_wd, v_mix_norm, v_ffn2_norm, v_ffn2_wg, v_ffn2_wu, v_ffn2_wd, v_w_in_even, v_pool_w, v_pool_scale, v_hgrn_lb_logits, v_hgrn_gnorm, v_w_out_even, v_w_in_odd, v_conv_w, v_conv_b, v_conv_ln_g, v_conv_ln_b, v_lru_conv_w, v_lru_conv_b, v_lru_wa, v_lru_ba, v_lru_wx, v_lru_bx, v_lru_lambda, v_w_out_odd, v_final_norm)
    Wl = dict(zip(W_NAMES, args))
    Ml = dict(zip(W_NAMES, margs))
    Vl = dict(zip(W_NAMES, vargs))

    small_shapes = [Wl[n].shape for n in SMALL_SHARDED]
    ffn = lambda p, l: [(p + s, l) for s in ('_wg', '_wu', '_wd')]
    mix = lambda p: [('w_in_' + p, None), ('w_out_' + p, None)]
    ggroups = [ffn('ffn1', 0)[:2], ffn('ffn1', 0)[2:], mix('even'), ffn('ffn2', 0), ffn('ffn1', 1), mix('odd'),
               ffn('ffn2', 1)]
    shard = lambda n, l: Wl[n][0 if l is None else l].astype(MXU)
    srcs = [[shard(n, l) for n, l in g] for g in ggroups]
    srcs[0] = [_pack([Wl[n] for n in SMALL_SHARDED])] + srcs[0]
    handles, _ = _exchange_start(srcs, True, "gather_start")
    W = {n: Wl[n] for n in REPLICATED}

    def fetch(k, after):
        lands = _exchange_wait(handles[k], True, after, f"gather_wait_{k}")
        if k == 0:
            per_dev = [_unpack(lands[0][d], small_shapes) for d in range(NDEV)]
            for j, n in enumerate(SMALL_SHARDED):
                W[n] = _to_full(jnp.stack([per_dev[d][j] for d in range(NDEV)]), SHARD_AXIS[n])
            lands = lands[1:]
        for (n, l), g in zip(ggroups[k], lands):
            W[n if l is None else (n, l)] = _to_full(g, SHARD_AXIS[n] - 1)

    pending = []

    def emit(tag, grads):
        slots = [_to_slots(g.astype(MXU), SHARD_AXIS[n] - 1) for n, _, g in grads]
        hs, token = _exchange_start([slots], False, f"scatter_start_{tag}")
        pending.append((tag, hs[0], [(n, l) for n, l, _ in grads]))
        return token[0, 0]

    loss_part, grad_x, G = _local_step(x[0], loss_target[0], W, fetch, emit)

    small_slots = [_to_slots(G[n].astype(F32), SHARD_AXIS[n]) for n in SMALL_SHARDED]
    send = [jnp.stack([_pack([s[d] for s in small_slots]) for d in range(NDEV)]), _pack([G[n] for n in REPLICATED]),
            jnp.broadcast_to(loss_part, (8, 128))]
    got = _exchange(send, [False, True, True], "scatter_small")
    loss = jnp.sum(got[2][:, 0, 0])
    recv = {}
    for tag, handle, keys in pending:
        for key, r in zip(keys, _exchange_wait(handle, False, got[0], f"scatter_wait_{tag}")):
            recv[key] = r

    outs = {}
    for n in BIG:
        shp = Wl[n].shape
        C = shp[-1]
        rs = [recv[n, None]] if shp[0] == 1 else [recv[n, l] for l in range(shp[0])]
        res = _adamw(rs, Wl[n].reshape(-1, C), Ml[n].reshape(-1, C), Vl[n].reshape(-1, C), f"adamw_{n}")
        outs[n] = [o.reshape(shp) for o in res]
    for names, r, tag in ((SMALL_SHARDED, got[0], "small"), (REPLICATED, got[1], "repl")):
        shapes = [Wl[n].shape for n in names]
        res = _adamw([r], _pack([Wl[n] for n in names]), _pack([Ml[n] for n in names]),
                     _pack([Vl[n] for n in names]), f"adamw_{tag}")
        unp = [_unpack(o, shapes) for o in res]
        for k, n in enumerate(names):
            outs[n] = [unp[j][k] for j in range(4)]

    result = [loss, grad_x[None]]
    for j in range(4):
        result += [outs[n][j] for n in W_NAMES]
    return tuple(result)
```

```python
import functools
import math

import jax
import jax.numpy as jnp
from jax import lax
from jax.experimental import pallas as pl
from jax.experimental.pallas import tpu as pltpu

F32 = jnp.float32
MXU = jnp.bfloat16
EPS = 1e-6
CH = 128
HG = 128
N_META = 16
CONV_W = 31
LRU_W = 4
LRU_C = 8.0
VMEM_LIMIT = 48 * 2 ** 20
MM_VMEM_BUDGET = 36 * 2 ** 20
ADAM_LR, ADAM_B1, ADAM_B2, ADAM_EPS, ADAM_WD, ADAM_STEP = 0.001, 0.9, 0.999, 1e-08, 0.01, 10
MESH_AXES = ("x", "y", "c")
NDEV = 8

W_NAMES = ['meta_tokens', 'ffn1_norm', 'ffn1_wg', 'ffn1_wu', 'ffn1_wd', 'mix_norm', 'ffn2_norm', 'ffn2_wg', 'ffn2_wu',
           'ffn2_wd', 'w_in_even', 'pool_w', 'pool_scale', 'hgrn_lb_logits', 'hgrn_gnorm', 'w_out_even', 'w_in_odd',
           'conv_w', 'conv_b', 'conv_ln_g', 'conv_ln_b', 'lru_conv_w', 'lru_conv_b', 'lru_wa', 'lru_ba', 'lru_wx',
           'lru_bx', 'lru_lambda', 'w_out_odd', 'final_norm']
SHARD_AXIS = {'meta_tokens': 1, 'ffn1_wg': 2, 'ffn1_wu': 2, 'ffn1_wd': 1, 'ffn2_wg': 2, 'ffn2_wu': 2, 'ffn2_wd': 1,
              'w_in_even': 2, 'w_out_even': 1, 'w_in_odd': 2, 'conv_w': 2, 'conv_b': 1, 'conv_ln_g': 1,
              'conv_ln_b': 1, 'lru_conv_w': 2, 'lru_conv_b': 1, 'lru_ba': 1, 'lru_bx': 1, 'lru_lambda': 1,
              'w_out_odd': 1}
BIG = ['ffn1_wg', 'ffn1_wu', 'ffn1_wd', 'ffn2_wg', 'ffn2_wu', 'ffn2_wd', 'w_in_even', 'w_out_even', 'w_in_odd',
       'w_out_odd']
SMALL_SHARDED = [n for n in W_NAMES if n in SHARD_AXIS and n not in BIG]
REPLICATED = [n for n in W_NAMES if n not in SHARD_AXIS]


def _cparams(sem=None, vmem=VMEM_LIMIT):
    return pltpu.CompilerParams(dimension_semantics=sem, vmem_limit_bytes=vmem)


def _tiles(n):
    return [c for c in range(128, n + 1, 128) if n % c == 0] or [n]


def _tile(n, cap=1024):
    return max([c for c in _tiles(n) if c <= cap], default=_tiles(n)[0])


def _rowtile(n):
    for c in (256, 352, 128, 64, 32, 16, 8):
        if n % c == 0:
            return c
    return n


def _copies(srcs, lands, bcast, ssem, rsem, lsem):
    x, y, c = lax.axis_index("x"), lax.axis_index("y"), lax.axis_index("c")
    me = 4 * x + 2 * y + c
    locs, sends, recvs = [], [], []
    for a in range(len(srcs)):
        locs.append(pltpu.make_async_copy(srcs[a] if bcast[a] else srcs[a].at[me], lands[a].at[me], lsem.at[a]))
        for m in range(1, NDEV):
            px = 1 - x if (m >> 2) & 1 else x
            py = 1 - y if (m >> 1) & 1 else y
            pc = 1 - c if m & 1 else c
            peer = 4 * px + 2 * py + pc
            src = srcs[a] if bcast[a] else srcs[a].at[peer]
            k = a * NDEV + m
            for dst, out in ((lands[a].at[me], sends), (lands[a].at[peer], recvs)):
                out.append(pltpu.make_async_remote_copy(src_ref=src, dst_ref=dst, send_sem=ssem.at[k],
                                                        recv_sem=rsem.at[k], device_id=(px, py, pc),
                                                        device_id_type=pl.DeviceIdType.MESH))
    return locs, sends, recvs


def _land_shape(arr, bc):
    return (NDEV,) + tuple(arr.shape if bc else arr.shape[1:])


def _exchange(arrays, bcast, name):
    n = len(arrays)

    def body(*refs):
        locs, sends, recvs = _copies(refs[:n], refs[n:2 * n], bcast, refs[2 * n], refs[2 * n + 1], refs[2 * n + 2])
        for d in locs + sends:
            d.start()
        for r in recvs:
            r.wait_recv()
        for s in sends:
            s.wait_send()
        for loc in locs:
            loc.wait()

    out_shape = tuple(jax.ShapeDtypeStruct(_land_shape(arr, bc), arr.dtype) for arr, bc in zip(arrays, bcast))
    any_spec = pl.BlockSpec(memory_space=pl.ANY)
    return pl.pallas_call(
        body, name=name, out_shape=out_shape, in_specs=[any_spec] * n, out_specs=tuple([any_spec] * n),
        scratch_shapes=[pltpu.SemaphoreType.DMA((n * NDEV,)), pltpu.SemaphoreType.DMA((n * NDEV,)),
                        pltpu.SemaphoreType.DMA((n,))],
    )(*arrays)


_HBM = pl.BlockSpec(memory_space=pltpu.HBM)
_SEM = pl.BlockSpec(memory_space=pltpu.SEMAPHORE)
_EFFECT = pltpu.SideEffectType.DATAFLOW_SIDE_EFFECTING


def _exchange_start(groups, bcast, name):
    sizes = [len(g) for g in groups]
    srcs = [a for g in groups for a in g]
    n, ng = len(srcs), len(groups)
    lands = [lax.empty(_land_shape(a, bcast), a.dtype) for a in srcs]

    def body(*refs):
        off = 0
        for gi, sz in enumerate(sizes):
            sem = refs[2 * n + 3 * gi:2 * n + 3 * gi + 3]
            locs, sends, _ = _copies(refs[off:off + sz], refs[n + off:n + off + sz], [bcast] * sz, *sem)
            for d in locs + sends:
                d.start()
            off += sz
        refs[-1][...] = jnp.zeros((8, 128), F32)

    sems = []
    for sz in sizes:
        sems += [pltpu.SemaphoreType.DMA((sz * NDEV,)), pltpu.SemaphoreType.DMA((sz * NDEV,)),
                 pltpu.SemaphoreType.DMA((sz,))]
    thru = [pltpu.HBM(a.shape, a.dtype) for a in srcs + lands]
    outs = pl.pallas_call(
        body, name=name, out_shape=tuple(sems + thru + [jax.ShapeDtypeStruct((8, 128), F32)]),
        in_specs=[_HBM] * (2 * n),
        out_specs=tuple([_SEM] * (3 * ng) + [_HBM] * (2 * n) + [pl.BlockSpec(memory_space=pltpu.VMEM)]),
        input_output_aliases={i: 3 * ng + i for i in range(2 * n)},
        compiler_params=pltpu.CompilerParams(has_side_effects=_EFFECT),
    )(*[pltpu.with_memory_space_constraint(a, pltpu.HBM) for a in srcs + lands])
    handles, off = [], 0
    for gi, sz in enumerate(sizes):
        handles.append((outs[3 * gi:3 * gi + 3], outs[3 * ng + off:3 * ng + off + sz],
                        outs[3 * ng + n + off:3 * ng + n + off + sz]))
        off += sz
    return handles, outs[-1]


def _exchange_wait(handle, bcast, after, name):
    sems, srcs, lands = handle
    n = len(srcs)

    def body(*refs):
        locs, sends, recvs = _copies(refs[:n], refs[n:2 * n], [bcast] * n, *refs[2 * n:2 * n + 3])
        for r in recvs:
            r.wait_recv()
        for s in sends:
            s.wait_send()
        for loc in locs:
            loc.wait()

    outs = pl.pallas_call(
        body, name=name, out_shape=tuple(pltpu.HBM(a.shape, a.dtype) for a in list(srcs) + list(lands)),
        in_specs=[_HBM] * (2 * n) + [_SEM] * 3 + [pl.BlockSpec(memory_space=pl.ANY)],
        out_specs=tuple([_HBM] * (2 * n)), input_output_aliases={i: i for i in range(2 * n)},
        compiler_params=pltpu.CompilerParams(has_side_effects=_EFFECT),
    )(*srcs, *lands, *sems, after)
    return outs[n:]


def _sum8(recv, name):
    _, R, C = recv.shape
    br = _rowtile(R)

    def body(r_ref, o_ref):
        s = r_ref[0].astype(F32)
        for k in range(1, NDEV):
            s = s + r_ref[k].astype(F32)
        o_ref[...] = s

    return pl.pallas_call(
        body, name=name, grid=(R // br,), out_shape=jax.ShapeDtypeStruct((R, C), F32),
        in_specs=[pl.BlockSpec((NDEV, br, C), lambda i: (0, i, 0))], out_specs=pl.BlockSpec((br, C), lambda i: (i, 0)),
        compiler_params=_cparams(("parallel",)),
    )(recv)


def _adamw(recvs, w, m, v, name, summed=False):
    R, C = w.shape
    nr = len(recvs)
    br = _rowtile(R // nr)
    nb0 = R // nr // br

    def body(*refs):
        w_ref, m_ref, v_ref, g_o, d_o, m_o, v_o = refs[nr:]
        g = None
        for j in range(nr):
            if summed:
                s = refs[j][...]
            else:
                s = refs[j][0].astype(F32)
                for k in range(1, NDEV):
                    s = s + refs[j][k].astype(F32)
            g = s if g is None else jnp.where(pl.program_id(0) >= j * nb0, s, g)
        mn = ADAM_B1 * m_ref[...] + (1.0 - ADAM_B1) * g
        vn = ADAM_B2 * v_ref[...] + (1.0 - ADAM_B2) * (g * g)
        m_hat = mn / (1.0 - ADAM_B1 ** ADAM_STEP)
        v_hat = vn / (1.0 - ADAM_B2 ** ADAM_STEP)
        g_o[...] = g
        d_o[...] = -ADAM_LR * (m_hat / (jnp.sqrt(v_hat) + ADAM_EPS) + ADAM_WD * w_ref[...])
        m_o[...] = mn
        v_o[...] = vn

    def rspec(j):
        if summed:
            return pl.BlockSpec((br, C), lambda i: (jnp.clip(i - j * nb0, 0, nb0 - 1), 0))
        return pl.BlockSpec((NDEV, br, C), lambda i: (0, jnp.clip(i - j * nb0, 0, nb0 - 1), 0))

    blk = pl.BlockSpec((br, C), lambda i: (i, 0))
    sds = jax.ShapeDtypeStruct((R, C), F32)
    return pl.pallas_call(
        body, name=name, grid=(R // br,), out_shape=(sds, sds, sds, sds),
        in_specs=[rspec(j) for j in range(nr)] + [blk, blk, blk], out_specs=(blk, blk, blk, blk),
        compiler_params=_cparams(("arbitrary",)),
    )(*recvs, w, m, v)


_DIMS = {"NN": ((1,), (0,)), "NT": ((1,), (1,)), "TN": ((0,), (0,))}


def _dot(a, b, mode="NN"):
    return lax.dot_general(a.astype(MXU), b.astype(MXU), (_DIMS[mode], ((), ())), preferred_element_type=F32)


def _dotf(a, b, mode="NN"):
    return lax.dot_general(a, b, (_DIMS[mode], ((), ())), precision=lax.Precision.HIGH,
                           preferred_element_type=F32)


def _mm(pairs, mode, name, res=None, res_scale=1.0, out_dtype=F32):
    a0, b0 = pairs[0]
    M = a0.shape[1] if mode == "TN" else a0.shape[0]
    N = b0.shape[0] if mode == "NT" else b0.shape[1]
    npairs = len(pairs)

    def vmem_bytes(tm, tn):
        total = tm * tn * 4 * (2 + (2 if res is not None else 0) + 2)
        for a, b in pairs:
            ka = a.shape[0] if mode == "TN" else a.shape[1]
            kb = b.shape[1] if mode == "NT" else b.shape[0]
            for k, t, arr in ((ka, tm, a), (kb, tn, b)):
                total += k * t * (2 * arr.dtype.itemsize + (2 if arr.dtype == F32 else 0))
        return total

    tm, tn = max(((tm, tn) for tm in _tiles(M) for tn in _tiles(N) if vmem_bytes(tm, tn) <= MM_VMEM_BUDGET),
                 key=lambda t: (t[0] * t[1], t[1]), default=(_tiles(M)[0], _tiles(N)[0]))

    def body(*refs):
        acc = None
        for p in range(npairs):
            d = _dot(refs[2 * p][...], refs[2 * p + 1][...], mode)
            acc = d if acc is None else acc + d
        if res_scale != 1.0:
            acc = res_scale * acc
        if res is not None:
            acc = refs[2 * npairs][...] + acc
        refs[-1][...] = acc.astype(out_dtype)

    in_specs, args = [], []
    for a, b in pairs:
        if mode == "TN":
            in_specs.append(pl.BlockSpec((a.shape[0], tm), lambda i, j: (0, i)))
        else:
            in_specs.append(pl.BlockSpec((tm, a.shape[1]), lambda i, j: (i, 0)))
        if mode == "NT":
            in_specs.append(pl.BlockSpec((tn, b.shape[1]), lambda i, j: (j, 0)))
        else:
            in_specs.append(pl.BlockSpec((b.shape[0], tn), lambda i, j: (0, j)))
        args += [a, b]
    if res is not None:
        in_specs.append(pl.BlockSpec((tm, tn), lambda i, j: (i, j)))
        args.append(res)
    return pl.pallas_call(
        body, name=name, grid=(M // tm, N // tn), out_shape=jax.ShapeDtypeStruct((M, N), out_dtype),
        in_specs=in_specs, out_specs=pl.BlockSpec((tm, tn), lambda i, j: (i, j)),
        compiler_params=_cparams(("parallel", "parallel")),
    )(*args)


def _rms_fwd(h, gamma, name):
    T, D = h.shape
    tm = _tile(T)

    def body(h_ref, g_ref, o_ref):
        x = h_ref[...]
        r = lax.rsqrt(jnp.mean(x * x, axis=-1, keepdims=True) + EPS)
        o_ref[...] = (x * r * g_ref[...]).astype(MXU)

    return pl.pallas_call(
        body, name=name, grid=(T // tm,), out_shape=jax.ShapeDtypeStruct((T, D), MXU),
        in_specs=[pl.BlockSpec((tm, D), lambda i: (i, 0)), pl.BlockSpec((1, D), lambda i: (0, 0))],
        out_specs=pl.BlockSpec((tm, D), lambda i: (i, 0)), compiler_params=_cparams(("parallel",)),
    )(h, gamma)


def _rms_bwd_math(x, gamma, dy):
    r = lax.rsqrt(jnp.mean(x * x, axis=-1, keepdims=True) + EPS)
    z = dy * gamma
    dx = r * z - x * (r * r * r) * jnp.mean(z * x, axis=-1, keepdims=True)
    dgamma = jnp.sum(dy * x * r, axis=0, keepdims=True)
    return dx, dgamma


def _rms_bwd(h, gamma, dxn, dres, name):
    T, D = h.shape
    tm = _tile(T)

    def body(h_ref, g_ref, dxn_ref, dres_ref, dh_ref, dh16_ref, dg_ref):
        dx, dgamma = _rms_bwd_math(h_ref[...], g_ref[...], dxn_ref[...])
        dh = dres_ref[...] + dx
        dh_ref[...] = dh
        dh16_ref[...] = dh.astype(MXU)

        @pl.when(pl.program_id(0) == 0)
        def _():
            dg_ref[...] = jnp.zeros_like(dg_ref)

        dg_ref[...] += dgamma

    row = pl.BlockSpec((tm, D), lambda i: (i, 0))
    vec = pl.BlockSpec((1, D), lambda i: (0, 0))
    return pl.pallas_call(
        body, name=name, grid=(T // tm,),
        out_shape=(jax.ShapeDtypeStruct((T, D), F32), jax.ShapeDtypeStruct((T, D), MXU),
                   jax.ShapeDtypeStruct((1, D), F32)),
        in_specs=[row, vec, row, row], out_specs=(row, row, vec), compiler_params=_cparams(("arbitrary",)),
    )(h, gamma, dxn, dres)


def _loss_head(h, gamma, tgt, lo, hi, name):
    T, D = h.shape
    tm = _tile(T)

    def body(h_ref, g_ref, t_ref, loss_ref, dh_ref, dh16_ref, dg_ref):
        i = pl.program_id(0)
        x = h_ref[...]
        r = lax.rsqrt(jnp.mean(x * x, axis=-1, keepdims=True) + EPS)
        y = x * r * g_ref[...]
        rows = i * tm + lax.broadcasted_iota(jnp.int32, (tm, 1), 0)
        valid = jnp.logical_and(rows >= lo, rows < hi)
        diff = jnp.where(valid, y - t_ref[...], 0.0)
        part = 0.5 * jnp.sum(jnp.sum(diff * diff, axis=-1, keepdims=True) / D, axis=0, keepdims=True)
        dx, dgamma = _rms_bwd_math(x, g_ref[...], diff / D)
        dh_ref[...] = dx
        dh16_ref[...] = dx.astype(MXU)

        @pl.when(i == 0)
        def _():
            dg_ref[...] = jnp.zeros_like(dg_ref)
            loss_ref[...] = jnp.zeros_like(loss_ref)

        dg_ref[...] += dgamma
        loss_ref[...] += jnp.broadcast_to(part, loss_ref.shape)

    row = pl.BlockSpec((tm, D), lambda i: (i, 0))
    vec = pl.BlockSpec((1, D), lambda i: (0, 0))
    lsp = pl.BlockSpec((8, 128), lambda i: (0, 0))
    return pl.pallas_call(
        body, name=name, grid=(T // tm,),
        out_shape=(jax.ShapeDtypeStruct((8, 128), F32), jax.ShapeDtypeStruct((T, D), F32),
                   jax.ShapeDtypeStruct((T, D), MXU), jax.ShapeDtypeStruct((1, D), F32)),
        in_specs=[row, vec, row], out_specs=(lsp, row, row, vec), compiler_params=_cparams(("arbitrary",)),
    )(h, gamma, tgt)


def _ffn_tiles(T, Fd):
    return (448 if T % 448 == 0 else _tile(T)), max(c for c in _tiles(Fd) if c <= 1536)


def _ffn_up(xn, wg, wu, name):
    T, D = xn.shape
    Fd = wg.shape[0]
    tm, tn = _ffn_tiles(T, Fd)

    def body(x_ref, wg_ref, wu_ref, p_ref, q_ref, a_ref):
        x = x_ref[...]
        g = _dot(x, wg_ref[...], "NT")
        u = _dot(x, wu_ref[...], "NT")
        sg = jax.nn.sigmoid(g)
        q = g * sg
        p_ref[...] = (u * (sg + q * (1.0 - sg))).astype(MXU)
        q_ref[...] = q.astype(MXU)
        a_ref[...] = (q * u).astype(MXU)

    wsp = pl.BlockSpec((tn, D), lambda i, j: (j, 0))
    osp = pl.BlockSpec((tm, tn), lambda i, j: (i, j))
    sds = jax.ShapeDtypeStruct((T, Fd), MXU)
    return pl.pallas_call(
        body, name=name, grid=(T // tm, Fd // tn), out_shape=(sds, sds, sds),
        in_specs=[pl.BlockSpec((tm, D), lambda i, j: (i, 0)), wsp, wsp], out_specs=(osp, osp, osp),
        compiler_params=_cparams(("parallel", "parallel")),
    )(xn, wg, wu)


def _ffn_dact(dy, wd, p, q, scale, name):
    T, D = dy.shape
    Fd = wd.shape[0]
    tm, tn = _ffn_tiles(T, Fd)

    def body(dy_ref, wd_ref, p_ref, q_ref, dg_ref, du_ref):
        da = scale * _dot(dy_ref[...], wd_ref[...], "NT")
        dg_ref[...] = (da * p_ref[...].astype(F32)).astype(MXU)
        du_ref[...] = (da * q_ref[...].astype(F32)).astype(MXU)

    osp = pl.BlockSpec((tm, tn), lambda i, j: (i, j))
    sds = jax.ShapeDtypeStruct((T, Fd), MXU)
    return pl.pallas_call(
        body, name=name, grid=(T // tm, Fd // tn), out_shape=(sds, sds),
        in_specs=[pl.BlockSpec((tm, D), lambda i, j: (i, 0)), pl.BlockSpec((tn, D), lambda i, j: (j, 0)), osp, osp],
        out_specs=(osp, osp), compiler_params=_cparams(("parallel", "parallel")),
    )(dy, wd, p, q)


def _down(v, s):
    return v if s == 0 else pltpu.roll(v, s, 0)


def _up(v, s):
    return v if s == 0 else pltpu.roll(v, v.shape[0] - s, 0)


def _rows(n):
    return lax.broadcasted_iota(jnp.int32, (n, 1), 0)


def _zero_pad_rows(ref, lo_end, hi_start, T):
    ref[pl.ds(0, lo_end), :] = jnp.zeros((lo_end, ref.shape[1]), ref.dtype)
    if T > hi_start:
        ref[pl.ds(hi_start, T - hi_start), :] = jnp.zeros((T - hi_start, ref.shape[1]), ref.dtype)


def _colblock(T, off):
    return pl.BlockSpec((T, 128), lambda j: (0, off + j))


def _vecblock(rows=1):
    return pl.BlockSpec((rows, 128), lambda j: (0, j))


def _pool_lane_consts(n):
    lane = lax.broadcasted_iota(jnp.int32, (n, 256), 1)
    win = jnp.where(lane < 64, 2.0, jnp.where(lane < 128, 4.0, jnp.where(lane < 192, 8.0, 16.0)))
    return lane, win


def _pool_select(lane, s2, s4, s8, s16):
    return jnp.where(lane < 64, s2, jnp.where(lane < 128, s4, jnp.where(lane < 192, s8, s16)))


def _pool_mixed(xh, start):
    s2 = xh + _down(xh, 1)
    s4 = s2 + _down(s2, 2)
    s8 = s4 + _down(s4, 4)
    s16 = s8 + _down(s8, 8)
    n = xh.shape[0] - 16
    lane, _ = _pool_lane_consts(n + 16)
    _, win = _pool_lane_consts(n)
    t1 = (start - CH + 1 + _rows(n)).astype(F32)
    cnt = jnp.minimum(jnp.maximum(t1, 1.0), win)
    return _pool_select(lane, s2, s4, s8, s16)[16:] / cnt - xh[16:]


def _pool_fwd(p, wbd, scale, nreal, real_end, name):
    T = p.shape[0]

    def body(p_ref, w_ref, s_ref, y_ref):
        _zero_pad_rows(y_ref, CH, CH * (1 + nreal), T)

        def chunk(c, carry):
            start = pl.multiple_of(c * CH, CH)
            mixed = _pool_mixed(p_ref[pl.ds(start - 16, CH + 16), :], start)
            y = _dot(mixed, w_ref[...]) * s_ref[...]
            y_ref[pl.ds(start, CH), :] = jnp.where(start + _rows(CH) < real_end, y, 0.0)
            return carry

        _pairs_loop(nreal, lambda c, carry: chunk(c + 1, carry), 0)

    return pl.pallas_call(
        body, name=name, grid=(1,), out_shape=jax.ShapeDtypeStruct((T, 256), F32),
        in_specs=[pl.BlockSpec((T, 256), lambda j: (0, 0)), pl.BlockSpec((256, 256), lambda j: (0, 0)),
                  pl.BlockSpec((1, 256), lambda j: (0, 0))],
        out_specs=pl.BlockSpec((T, 256), lambda j: (0, 0)), compiler_params=_cparams(("arbitrary",)),
    )(p, wbd, scale)


def _pool_bwd(p, wbd, scale, dy, nreal, real_end, name):
    T = p.shape[0]

    def body(p_ref, w_ref, s_ref, dy_ref, dp_ref, dw_ref, ds_ref):
        _zero_pad_rows(dp_ref, CH, CH * (1 + nreal), T)
        dw_ref[...] = jnp.zeros_like(dw_ref)
        ds_ref[...] = jnp.zeros_like(ds_ref)

        def chunk(c, carry):
            start = pl.multiple_of(c * CH, CH)
            mixed = _pool_mixed(p_ref[pl.ds(start - 16, CH + 16), :], start)
            ypre = _dot(mixed, w_ref[...])
            n = CH + 16
            dye = jnp.where(start + _rows(n) < real_end, dy_ref[pl.ds(start, n), :], 0.0)
            dys = dye * s_ref[...]
            ds_ref[...] += jnp.sum(dye[:CH] * ypre, axis=0, keepdims=True)
            dw_ref[...] += _dot(mixed, dys[:CH], "TN")
            dmix = _dot(dys, w_ref[...], "NT")
            lane, win = _pool_lane_consts(n)
            t1 = (start - CH + 1 + _rows(n)).astype(F32)
            z = dmix / jnp.minimum(jnp.maximum(t1, 1.0), win)
            r2 = z + _up(z, 1)
            r4 = r2 + _up(r2, 2)
            r8 = r4 + _up(r4, 4)
            r16 = r8 + _up(r8, 8)
            dp_ref[pl.ds(start, CH), :] = (_pool_select(lane, r2, r4, r8, r16) - dmix)[:CH]
            return carry

        _pairs_loop(nreal, lambda c, carry: chunk(c + 1, carry), 0)

    full = lambda r, c: pl.BlockSpec((r, c), lambda j: (0, 0))
    return pl.pallas_call(
        body, name=name, grid=(1,),
        out_shape=(jax.ShapeDtypeStruct((T, 256), F32), jax.ShapeDtypeStruct((256, 256), F32),
                   jax.ShapeDtypeStruct((1, 256), F32)),
        in_specs=[full(T, 256), full(256, 256), full(1, 256), full(T, 256)],
        out_specs=(full(T, 256), full(256, 256), full(1, 256)), compiler_params=_cparams(("arbitrary",)),
    )(p, wbd, scale, dy)


def _hgrn_chunk(St, qr, fr, ir, gr, l0, l1, gn):
    rows = lax.broadcasted_iota(jnp.int32, (HG, HG), 0)
    cols = lax.broadcasted_iota(jnp.int32, (HG, HG), 1)
    causal = rows >= cols
    ltri = causal.astype(F32)
    lb = jax.nn.sigmoid(l0 - l1)
    sg = jax.nn.sigmoid(fr)
    logf = jnp.log(lb + (1.0 - lb) * sg)
    kk = (1.0 - lb) * (1.0 - sg)
    q = qr * jax.nn.sigmoid(qr)
    b = jnp.dot(ltri, logf, precision=lax.Precision.HIGH, preferred_element_type=F32)
    bl = jnp.sum(logf, axis=0, keepdims=True)
    bm = jnp.sum(jnp.where(_rows(HG) <= HG // 2, logf, 0.0), axis=0, keepdims=True)
    o = _dotf(q * jnp.exp(b), St, "NT")
    A = _dotf(q * jnp.exp(b - bm), kk * jnp.exp(bm - b), "NT")
    o = o + _dotf(jnp.where(causal, A, 0.0), ir)
    St_new = St * jnp.exp(bl) + _dotf(ir, kk * jnp.exp(bl - b), "TN")
    on = o * lax.rsqrt(jnp.mean(o * o, axis=-1, keepdims=True) + EPS) * gn
    return St_new, on * (gr * jax.nn.sigmoid(gr))


def _pairs_loop(n, step, init):
    u = 3 if n % 3 == 0 else 2 if n % 2 == 0 else 1

    def body(i, carry):
        for j in range(u):
            carry = step(u * i + j, carry)
        return carry

    return lax.fori_loop(0, n // u, body, init)


def _hgrn_specs(T):
    return [_colblock(T, 2), _colblock(T, 8), _colblock(T, 14), _colblock(T, 20), _vecblock(), _vecblock(),
            pl.BlockSpec((1, 128), lambda j: (0, 0))]


def _hgrn_fwd(p, l0, l1, gn, nreal, real_end, name):
    T = p.shape[0]
    nch = nreal * (CH // HG)

    def body(q_ref, f_ref, i_ref, g_ref, l0_ref, l1_ref, gn_ref, y_ref, s_ref):
        _zero_pad_rows(y_ref, CH, CH * (1 + nreal), T)

        def chunk(c, St):
            start = pl.multiple_of(CH + c * HG, HG)
            sl = pl.ds(start, HG)
            s_ref[0, c] = St
            St_new, y = _hgrn_chunk(St, q_ref[sl, :], f_ref[sl, :], i_ref[sl, :], g_ref[sl, :], l0_ref[...],
                                    l1_ref[...], gn_ref[...])
            y_ref[sl, :] = jnp.where(start + _rows(HG) < real_end, y, 0.0)
            return St_new

        _pairs_loop(nch, chunk, jnp.zeros((128, 128), F32))

    return pl.pallas_call(
        body, name=name, grid=(6,),
        out_shape=(jax.ShapeDtypeStruct((T, 768), F32), jax.ShapeDtypeStruct((6, nch, 128, 128), F32)),
        in_specs=_hgrn_specs(T),
        out_specs=(_colblock(T, 0), pl.BlockSpec((1, nch, 128, 128), lambda j: (j, 0, 0, 0))),
        compiler_params=_cparams(("parallel",)),
    )(p, p, p, p, l0, l1, gn)


def _hgrn_bwd(p, l0, l1, gn, states, dy, nreal, real_end, name):
    T = p.shape[0]
    nch = nreal * (CH // HG)

    def body(q_ref, f_ref, i_ref, g_ref, l0_ref, l1_ref, gn_ref, s_ref, dy_ref,
             dq_ref, df_ref, di_ref, dg_ref, dl0_ref, dl1_ref, dgn_ref):
        for r in (dq_ref, df_ref, di_ref, dg_ref):
            _zero_pad_rows(r, CH, CH * (1 + nreal), T)

        def chunk(k, carry):
            dSt, a0, a1, agn = carry
            c = nch - 1 - k
            start = pl.multiple_of(CH + c * HG, HG)
            sl = pl.ds(start, HG)
            _, vjp = jax.vjp(_hgrn_chunk, s_ref[0, c], q_ref[sl, :], f_ref[sl, :], i_ref[sl, :], g_ref[sl, :],
                             l0_ref[...], l1_ref[...], gn_ref[...])
            dyc = jnp.where(start + _rows(HG) < real_end, dy_ref[sl, :], 0.0)
            dS, dq, df, di, dg, d0, d1, dgn = vjp((dSt, dyc))
            dq_ref[sl, :] = dq
            df_ref[sl, :] = df
            di_ref[sl, :] = di
            dg_ref[sl, :] = dg
            return dS, a0 + d0, a1 + d1, agn + dgn

        z = jnp.zeros((1, 128), F32)
        _, a0, a1, agn = _pairs_loop(nch, chunk, (jnp.zeros((128, 128), F32), z, z, z))
        dl0_ref[...] = a0
        dl1_ref[...] = a1

        @pl.when(pl.program_id(0) == 0)
        def _():
            dgn_ref[...] = jnp.zeros_like(dgn_ref)

        dgn_ref[...] += agn

    big = jax.ShapeDtypeStruct((T, 768), F32)
    vec = jax.ShapeDtypeStruct((1, 768), F32)
    return pl.pallas_call(
        body, name=name, grid=(6,),
        out_shape=(big, big, big, big, vec, vec, jax.ShapeDtypeStruct((1, 128), F32)),
        in_specs=_hgrn_specs(T) + [pl.BlockSpec((1, nch, 128, 128), lambda j: (j, 0, 0, 0)), _colblock(T, 2)],
        out_specs=(_colblock(T, 0), _colblock(T, 0), _colblock(T, 0), _colblock(T, 0), _vecblock(), _vecblock(),
                   pl.BlockSpec((1, 128), lambda j: (0, 0))),
        compiler_params=_cparams(("arbitrary",), 60 * 2 ** 20),
    )(p, p, p, p, l0, l1, gn, states, dy)


def _glu(a, b):
    return a * jax.nn.sigmoid(b)


def _conv_post(cv, ln_g, ln_b):
    mu = jnp.mean(cv, axis=-1, keepdims=True)
    d = cv - mu
    var = jnp.mean(d * d, axis=-1, keepdims=True)
    un = d * lax.rsqrt(var + EPS) * ln_g + ln_b
    return un * jax.nn.sigmoid(un)


def _causal_conv(uh, w_ref, width, halo):
    acc = None
    for j in range(width):
        term = _down(uh, width - 1 - j) * w_ref[pl.ds(j, 1), :]
        acc = term if acc is None else acc + term
    return acc[halo:]


def _conf_fwd(p, cw, cb, lg, lb, nreal, real_end, name):
    T = p.shape[0]

    def body(a_ref, b_ref, w_ref, cb_ref, lg_ref, lb_ref, y_ref):
        _zero_pad_rows(y_ref, CH, CH * (1 + nreal), T)

        def chunk(c, carry):
            start = pl.multiple_of(c * CH, CH)
            ext = pl.ds(start - 32, CH + 32)
            cv = _causal_conv(_glu(a_ref[ext, :], b_ref[ext, :]), w_ref, CONV_W, 32) + cb_ref[...]
            y = _conv_post(cv, lg_ref[...], lb_ref[...])
            y_ref[pl.ds(start, CH), :] = jnp.where(start + _rows(CH) < real_end, y, 0.0)
            return carry

        _pairs_loop(nreal, lambda c, carry: chunk(c + 1, carry), 0)

    return pl.pallas_call(
        body, name=name, grid=(4,), out_shape=jax.ShapeDtypeStruct((T, 512), F32),
        in_specs=[_colblock(T, 0), _colblock(T, 4), _vecblock(32), _vecblock(), _vecblock(), _vecblock()],
        out_specs=_colblock(T, 0), compiler_params=_cparams(("parallel",)),
    )(p, p, cw, cb, lg, lb)


def _conf_bwd(p, cw, cb, lg, lb, dy, nreal, real_end, name):
    T = p.shape[0]

    def body(a_ref, b_ref, w_ref, cb_ref, lg_ref, lb_ref, dy_ref, da_ref, db_ref, dw_ref, dcb_ref, dlg_ref, dlb_ref):
        _zero_pad_rows(da_ref, CH, CH * (1 + nreal), T)
        _zero_pad_rows(db_ref, CH, CH * (1 + nreal), T)
        for r in (dw_ref, dcb_ref, dlg_ref, dlb_ref):
            r[...] = jnp.zeros_like(r)

        def chunk(c, carry):
            start = pl.multiple_of(c * CH, CH)
            ext = pl.ds(start - 32, CH + 64)
            ue = _glu(a_ref[ext, :], b_ref[ext, :])
            cv = _causal_conv(ue, w_ref, CONV_W, 32) + cb_ref[...]
            dye = jnp.where(start + _rows(CH + 32) < real_end, dy_ref[pl.ds(start, CH + 32), :], 0.0)
            _, vjp_cur = jax.vjp(_conv_post, cv[:CH], lg_ref[...], lb_ref[...])
            dc_cur, dlg, dlb = vjp_cur(dye[:CH])
            _, vjp_halo = jax.vjp(_conv_post, cv[CH:], lg_ref[...], lb_ref[...])
            dce = jnp.concatenate([dc_cur, vjp_halo(dye[CH:])[0]], axis=0)
            dlg_ref[...] += dlg
            dlb_ref[...] += dlb
            dcb_ref[...] += jnp.sum(dc_cur, axis=0, keepdims=True)
            du = None
            for j in range(CONV_W):
                w_j = w_ref[pl.ds(j, 1), :]
                term = _up(dce, CONV_W - 1 - j)[:CH] * w_j
                du = term if du is None else du + term
                dw_ref[pl.ds(j, 1), :] += jnp.sum(dc_cur * _up(ue, 2 + j)[:CH], axis=0, keepdims=True)
            cur = pl.ds(start, CH)
            _, vjp_glu = jax.vjp(_glu, a_ref[cur, :], b_ref[cur, :])
            da, db = vjp_glu(du)
            da_ref[cur, :] = da
            db_ref[cur, :] = db
            return carry

        _pairs_loop(nreal, lambda c, carry: chunk(c + 1, carry), 0)

    big = jax.ShapeDtypeStruct((T, 512), F32)
    vec = jax.ShapeDtypeStruct((1, 512), F32)
    return pl.pallas_call(
        body, name=name, grid=(4,), out_shape=(big, big, jax.ShapeDtypeStruct((32, 512), F32), vec, vec, vec),
        in_specs=[_colblock(T, 0), _colblock(T, 4), _vecblock(32), _vecblock(), _vecblock(), _vecblock(),
                  _colblock(T, 0)],
        out_specs=(_colblock(T, 0), _colblock(T, 0), _vecblock(32), _vecblock(), _vecblock(), _vecblock()),
        compiler_params=_cparams(("parallel",)),
    )(p, p, cw, cb, lg, lb, dy)


def _softplus_neg(lam):
    e = jnp.exp(-lam)
    small = e * (1.0 - e * (0.5 - e * (1.0 / 3.0 - e * 0.25)))
    return jnp.where(e < 0.02, small, jnp.log(1.0 + e))


def _one_minus_exp(x):
    series = -x * (1.0 + x * (0.5 + x * (1.0 / 6.0 + x * (1.0 / 24.0 + x * (1.0 / 120.0)))))
    return jnp.where(x > -0.05, series, 1.0 - jnp.exp(x))


def _lru_pre(u, wa, wx, ba, bx, lam, first):
    r = jax.nn.sigmoid(_dot(u, wa) + ba)
    i = jax.nn.sigmoid(_dot(u, wx) + bx)
    log_a = -LRU_C * r * _softplus_neg(lam)
    a = jnp.exp(log_a)
    mult = jnp.sqrt(_one_minus_exp(2.0 * log_a))
    return a, jnp.where(first, 1.0, mult) * (i * u)


def _gelu_gate(gate, h):
    inner = math.sqrt(2.0 / math.pi) * (gate + 0.044715 * (gate * gate * gate))
    return 0.5 * gate * (1.0 + jnp.tanh(inner)) * h


def _lru_specs(T):
    mat = pl.BlockSpec((1, 128, 128), lambda j: (j, 0, 0))
    return [_colblock(T, 8), _colblock(T, 12), _vecblock(8), _vecblock(), mat, mat, _vecblock(), _vecblock(),
            _vecblock()]


def _lru_fwd(p, cw, cb, wa, wx, ba, bx, lam, nreal, real_end, name):
    T = p.shape[0]

    def body(x_ref, g_ref, w_ref, cb_ref, wa_ref, wx_ref, ba_ref, bx_ref, lam_ref, y_ref, h_ref):
        _zero_pad_rows(y_ref, CH, CH * (1 + nreal), T)
        _zero_pad_rows(h_ref, CH, CH * (1 + nreal), T)
        rows = _rows(CH)

        def chunk(c, hprev):
            start = pl.multiple_of(c * CH, CH)
            u = _causal_conv(x_ref[pl.ds(start - 8, CH + 8), :], w_ref, LRU_W, 8) + cb_ref[...]
            A, B = _lru_pre(u, wa_ref[0], wx_ref[0], ba_ref[...], bx_ref[...], lam_ref[...], start + rows == CH)
            s = 1
            while s < CH:
                B = A * jnp.where(rows >= s, _down(B, s), 0.0) + B
                A = A * jnp.where(rows >= s, _down(A, s), 1.0)
                s *= 2
            h = B + A * hprev
            cur = pl.ds(start, CH)
            h_ref[cur, :] = h
            y_ref[cur, :] = jnp.where(start + rows < real_end, _gelu_gate(g_ref[cur, :], h), 0.0)
            return jnp.sum(jnp.where(rows == CH - 1, h, 0.0), axis=0, keepdims=True)

        _pairs_loop(nreal, lambda c, carry: chunk(c + 1, carry), jnp.zeros((1, 128), F32))

    big = jax.ShapeDtypeStruct((T, 512), F32)
    return pl.pallas_call(
        body, name=name, grid=(4,), out_shape=(big, big), in_specs=_lru_specs(T),
        out_specs=(_colblock(T, 0), _colblock(T, 0)), compiler_params=_cparams(("parallel",)),
    )(p, p, cw, cb, wa, wx, ba, bx, lam)


def _lru_bwd(p, cw, cb, wa, wx, ba, bx, lam, hs, dy, nreal, real_end, name):
    T = p.shape[0]

    def body(x_ref, g_ref, w_ref, cb_ref, wa_ref, wx_ref, ba_ref, bx_ref, lam_ref, h_ref, dy_ref,
             dx_ref, dgate_ref, dw_ref, dcb_ref, dwa_ref, dwx_ref, dba_ref, dbx_ref, dlam_ref):
        _zero_pad_rows(dx_ref, CH, CH * (1 + nreal), T)
        _zero_pad_rows(dgate_ref, CH, CH * (1 + nreal), T)
        for r in (dw_ref, dcb_ref, dwa_ref, dwx_ref, dba_ref, dbx_ref, dlam_ref):
            r[...] = jnp.zeros_like(r)
        rows = _rows(CH)

        def chunk(k, carry):
            cdh, du_head = carry
            c = nreal - k
            start = pl.multiple_of(c * CH, CH)
            cur = pl.ds(start, CH)
            xe = x_ref[pl.ds(start - 8, CH + 8), :]
            u = _causal_conv(xe, w_ref, LRU_W, 8) + cb_ref[...]
            first = start + rows == CH
            (a, _), vjp_pre = jax.vjp(lambda uu, m1, m2, b1, b2, ll: _lru_pre(uu, m1, m2, b1, b2, ll, first),
                                      u, wa_ref[0], wx_ref[0], ba_ref[...], bx_ref[...], lam_ref[...])
            h = h_ref[cur, :]
            hm1 = _down(h_ref[pl.ds(start - 8, CH + 8), :], 1)[8:]
            _, vjp_post = jax.vjp(_gelu_gate, g_ref[cur, :], h)
            dgate, D = vjp_post(jnp.where(start + rows < real_end, dy_ref[cur, :], 0.0))
            dgate_ref[cur, :] = dgate
            D = D + jnp.where(rows == CH - 1, cdh, 0.0)
            C = jnp.where(rows < CH - 1, _up(a, 1), 0.0)
            s = 1
            while s < CH:
                D = D + C * jnp.where(rows + s < CH, _up(D, s), 0.0)
                C = C * jnp.where(rows + s < CH, _up(C, s), 1.0)
                s *= 2
            du, dwa, dwx, dba, dbx, dlam = vjp_pre((D * hm1, D))
            dwa_ref[0] += dwa
            dwx_ref[0] += dwx
            dba_ref[...] += dba
            dbx_ref[...] += dbx
            dlam_ref[...] += dlam
            dcb_ref[...] += jnp.sum(du, axis=0, keepdims=True)
            due = jnp.concatenate([du, du_head], axis=0)
            dx = None
            for j in range(LRU_W):
                term = _up(due, LRU_W - 1 - j)[:CH] * w_ref[pl.ds(j, 1), :]
                dx = term if dx is None else dx + term
                dw_ref[pl.ds(j, 1), :] += jnp.sum(du * _up(xe, 8 - (LRU_W - 1) + j)[:CH], axis=0, keepdims=True)
            dx_ref[cur, :] = dx
            return jnp.sum(jnp.where(rows == 0, a * D, 0.0), axis=0, keepdims=True), du[:8]

        _pairs_loop(nreal, chunk, (jnp.zeros((1, 128), F32), jnp.zeros((8, 128), F32)))

    big = jax.ShapeDtypeStruct((T, 512), F32)
    vec = jax.ShapeDtypeStruct((1, 512), F32)
    mat = jax.ShapeDtypeStruct((4, 128, 128), F32)
    matspec = pl.BlockSpec((1, 128, 128), lambda j: (j, 0, 0))
    return pl.pallas_call(
        body, name=name, grid=(4,),
        out_shape=(big, big, jax.ShapeDtypeStruct((8, 512), F32), vec, mat, mat, vec, vec, vec),
        in_specs=_lru_specs(T) + [_colblock(T, 0), _colblock(T, 4)],
        out_specs=(_colblock(T, 0), _colblock(T, 0), _vecblock(8), _vecblock(), matspec, matspec, _vecblock(),
                   _vecblock(), _vecblock()),
        compiler_params=_cparams(("parallel",)),
    )(p, p, cw, cb, wa, wx, ba, bx, lam, hs, dy)


def _ffn_forward(h, gamma, wg, wu, wd, tag):
    xn = _rms_fwd(h, gamma, f"rms_fwd_{tag}")
    g, u, a = _ffn_up(xn, wg, wu, f"ffn_up_{tag}")
    if callable(wd):
        wd = wd(a)
    out = _mm([(a, wd)], "NN", f"ffn_down_{tag}", res=h, res_scale=0.5)
    return out, (h, xn, g, u, a)


def _after(w, tok):
    return w if tok is None else w + tok.astype(w.dtype)


def _ffn_backward(saved, gamma, wg, wu, wd, dout, tok, tag, emit_one=None):
    h, xn, p, q, a = saved
    dout, dout16 = dout
    wd = _after(wd, tok)
    dwd = _mm([(a, dout16)], "TN", f"ffn_dwd_{tag}", res_scale=0.5, out_dtype=MXU)
    if emit_one is not None:
        wd = _after(wd, emit_one('wd', dwd))
    dg, du = _ffn_dact(dout16, wd, p, q, 0.5, f"ffn_dact_{tag}")
    dwg = _mm([(dg, xn)], "TN", f"ffn_dwg_{tag}", out_dtype=MXU)
    if emit_one is not None:
        wg = _after(wg, emit_one('wg', dwg))
    dwu = _mm([(du, xn)], "TN", f"ffn_dwu_{tag}", out_dtype=MXU)
    if emit_one is not None:
        wu = _after(wu, emit_one('wu', dwu))
    dxn = _mm([(dg, wg), (du, wu)], "NN", f"ffn_dxn_{tag}")
    dh, dh16, dgamma = _rms_bwd(h, gamma, dxn, dout, f"rms_bwd_{tag}")
    return (dh, dh16), dgamma, dwg, dwu, dwd


def _blockdiag(w, per):
    n, k, _ = w.shape
    out = jnp.zeros((n // per, per * k, per * k), w.dtype)
    for i in range(per):
        out = out.at[:, i * k:(i + 1) * k, i * k:(i + 1) * k].set(w[i::per])
    return out


def _blockdiag_grad(g, per, k):
    parts = [g[:, i * k:(i + 1) * k, i * k:(i + 1) * k] for i in range(per)]
    return jnp.stack(parts, axis=1).reshape(-1, k, k)


def _local_step(x, tgt, W, fetch, emit):
    fetch(0, x)
    seq, D = x.shape
    lr = N_META + seq
    nreal = -(-lr // CH)
    T = CH * (nreal + 2)
    if T > 640 and T % 640:
        T += 640 - T % 640
    lo, real_end = CH + N_META, CH + lr
    zf = lambda n: jnp.zeros((n, D), F32)
    h0 = jnp.concatenate([zf(CH), W['meta_tokens'], x, zf(T - real_end)], axis=0)
    tgt_p = jnp.concatenate([zf(lo), tgt, zf(T - real_end)], axis=0)
    row = lambda v: v.reshape(1, -1)
    G = {}

    h = h0
    saved = []
    for l in range(2):
        wd1 = W['ffn1_wd', l] if l else (lambda after: (fetch(1, after), W['ffn1_wd', 0])[1])
        h, s1 = _ffn_forward(h, row(W['ffn1_norm'][l]), W['ffn1_wg', l], W['ffn1_wu', l], wd1, f"a{l}")
        hm = h
        fetch(3 * l + 2, hm)
        xn = _rms_fwd(hm, row(W['mix_norm'][l]), f"rms_fwd_mix{l}")
        if l == 0:
            p = _mm([(xn, W['w_in_even'])], "NT", "in_even")
            wbd = _blockdiag(W['pool_w'][0], 4)[0]
            l0, l1 = row(W['hgrn_lb_logits'][0]), row(W['hgrn_lb_logits'][1])
            ya = _pool_fwd(p, wbd, W['pool_scale'], nreal, real_end, "pool_fwd")
            yb, states = _hgrn_fwd(p, l0, l1, W['hgrn_gnorm'], nreal, real_end, "hgrn_fwd")
            wo = W['w_out_even']
            h = _mm([(ya, wo[:256]), (yb, wo[256:])], "NN", "out_even", res=hm)
            sm = (hm, xn, p, wbd, l0, l1, ya, yb, states)
        else:
            p = _mm([(xn, W['w_in_odd'])], "NT", "in_odd")
            cw = jnp.pad(W['conv_w'][0], ((0, 1), (0, 0)))
            lw = jnp.pad(W['lru_conv_w'][0], ((0, 4), (0, 0)))
            wa, wx = _blockdiag(W['lru_wa'][0], 2), _blockdiag(W['lru_wx'][0], 2)
            yc = _conf_fwd(p, cw, W['conv_b'], W['conv_ln_g'], W['conv_ln_b'], nreal, real_end, "conf_fwd")
            yd, hs = _lru_fwd(p, lw, W['lru_conv_b'], wa, wx, W['lru_ba'], W['lru_bx'], W['lru_lambda'], nreal,
                              real_end, "lru_fwd")
            wo = W['w_out_odd']
            h = _mm([(yc, wo[:512]), (yd, wo[512:])], "NN", "out_odd", res=hm)
            sm = (hm, xn, p, cw, lw, wa, wx, yc, yd, hs)
        fetch(3 * l + 3, h)
        h, s2 = _ffn_forward(h, row(W['ffn2_norm'][l]), W['ffn2_wg', l], W['ffn2_wu', l], W['ffn2_wd', l], f"b{l}")
        if l == 0:
            fetch(4, h)
        saved.append((s1, sm, s2))

    loss8, dh, dh16, dfin = _loss_head(h, row(W['final_norm']), tgt_p, lo, real_end, "loss_head")
    dh = (dh, dh16)
    G['final_norm'] = dfin[0]

    per_layer = {k: [None, None] for k in ('ffn1_norm', 'mix_norm', 'ffn2_norm')}
    tok = None
    for l in (1, 0):
        s1, sm, s2 = saved[l]
        dh, dn, dwg, dwu, dwd = _ffn_backward(s2, row(W['ffn2_norm'][l]), W['ffn2_wg', l], W['ffn2_wu', l],
                                              W['ffn2_wd', l], dh, tok, f"b{l}")
        per_layer['ffn2_norm'][l] = dn[0]
        tok = emit(f"ffn2_{l}", [('ffn2_wg', l, dwg), ('ffn2_wu', l, dwu), ('ffn2_wd', l, dwd)])
        if l == 0:
            hm, xn, p, wbd, l0, l1, ya, yb, states = sm
            wo, wi = _after(W['w_out_even'], tok), W['w_in_even']
            dwo = jnp.concatenate([_mm([(ya, dh[1])], "TN", "dwo_even_a", out_dtype=MXU),
                                   _mm([(yb, dh[1])], "TN", "dwo_even_b", out_dtype=MXU)], axis=0)
            dy = _mm([(dh[1], wo)], "NT", "dy_even")
            dpp, dwbd, dsc = _pool_bwd(p, wbd, W['pool_scale'], dy, nreal, real_end, "pool_bwd")
            dq, df, di, dg, dl0, dl1, dgn = _hgrn_bwd(p, l0, l1, W['hgrn_gnorm'], states, dy, nreal, real_end,
                                                      "hgrn_bwd")
            G['pool_w'] = _blockdiag_grad(dwbd[None], 4, 64)[None]
            G['pool_scale'] = dsc
            G['hgrn_lb_logits'] = jnp.concatenate([dl0, dl1], axis=0)
            G['hgrn_gnorm'] = dgn
            parts = [dpp, dq, df, di, dg]
            offs = [0, 256, 1024, 1792, 2560, 3328]
            dwi = jnp.concatenate(
                [_mm([(dpart, xn)], "TN", f"dwi_even_{k}", out_dtype=MXU) for k, dpart in enumerate(parts)], axis=0)
            dxn = _mm([(dpart, wi[offs[k]:offs[k + 1]]) for k, dpart in enumerate(parts)], "NN", "dxn_even")
            tok = emit("even", [('w_in_even', None, dwi), ('w_out_even', None, dwo)])
        else:
            hm, xn, p, cw, lw, wa, wx, yc, yd, hs = sm
            wo, wi = _after(W['w_out_odd'], tok), W['w_in_odd']
            dwo = jnp.concatenate([_mm([(yc, dh[1])], "TN", "dwo_odd_c", out_dtype=MXU),
                                   _mm([(yd, dh[1])], "TN", "dwo_odd_d", out_dtype=MXU)], axis=0)
            dy = _mm([(dh[1], wo)], "NT", "dy_odd")
            da, db, dcw, dcb, dlg, dlb = _conf_bwd(p, cw, W['conv_b'], W['conv_ln_g'], W['conv_ln_b'], dy, nreal,
                                                   real_end, "conf_bwd")
            dx, dgate, dlw, dlcb, dwa, dwx, dba, dbx, dlam = _lru_bwd(
                p, lw, W['lru_conv_b'], wa, wx, W['lru_ba'], W['lru_bx'], W['lru_lambda'], hs, dy, nreal, real_end,
                "lru_bwd")
            G['conv_w'], G['conv_b'], G['conv_ln_g'], G['conv_ln_b'] = dcw[None, :CONV_W], dcb, dlg, dlb
            G['lru_conv_w'], G['lru_conv_b'] = dlw[None, :LRU_W], dlcb
            G['lru_wa'] = _blockdiag_grad(dwa, 2, 64)[None]
            G['lru_wx'] = _blockdiag_grad(dwx, 2, 64)[None]
            G['lru_ba'], G['lru_bx'], G['lru_lambda'] = dba, dbx, dlam
            parts = [da, db, dx, dgate]
            dwi = jnp.concatenate(
                [_mm([(dpart, xn)], "TN", f"dwi_odd_{k}", out_dtype=MXU) for k, dpart in enumerate(parts)], axis=0)
            dxn = _mm([(dpart, wi[512 * k:512 * (k + 1)]) for k, dpart in enumerate(parts)], "NN", "dxn_odd")
            tok = emit("odd", [('w_in_odd', None, dwi), ('w_out_odd', None, dwo)])
        dh, dh16, dn = _rms_bwd(hm, _after(row(W['mix_norm'][l]), tok), dxn, dh[0], f"rms_bwd_mix{l}")
        dh = (dh, dh16)
        per_layer['mix_norm'][l] = dn[0]
        one = None if l == 1 else (lambda sfx, g: emit(f"ffn1_0_{sfx}", [('ffn1_' + sfx, 0, g)]))
        dh, dn, dwg, dwu, dwd = _ffn_backward(s1, row(W['ffn1_norm'][l]), W['ffn1_wg', l], W['ffn1_wu', l],
                                              W['ffn1_wd', l], dh, None, f"a{l}", one)
        per_layer['ffn1_norm'][l] = dn[0]
        if l == 1:
            tok = emit("ffn1_1", [('ffn1_wg', l, dwg), ('ffn1_wu', l, dwu), ('ffn1_wd', l, dwd)])
    for k, v in per_layer.items():
        G[k] = jnp.stack(v, axis=0)
    G['meta_tokens'] = dh[0][CH:lo]
    return loss8[0, 0], dh[0][lo:real_end], G


def _pack(arrs):
    flat = jnp.concatenate([a.reshape(-1).astype(F32) for a in arrs])
    n = flat.shape[0]
    padded = -(-n // 1024) * 1024
    return jnp.pad(flat, (0, padded - n)).reshape(-1, 128)


def _unpack(packed, shapes):
    flat = packed.reshape(-1)
    out, off = [], 0
    for s in shapes:
        n = math.prod(s)
        out.append(flat[off:off + n].reshape(s))
        off += n
    return out


def _pack8(arrs):
    flat = jnp.concatenate([a.reshape(NDEV, -1).astype(F32) for a in arrs], axis=1)
    n = flat.shape[1]
    padded = -(-n // 1024) * 1024
    return jnp.pad(flat, ((0, 0), (0, padded - n))).reshape(NDEV, -1, 128)


def _unpack8(packed, shapes):
    flat = packed.reshape(NDEV, -1)
    out, off = [], 0
    for s in shapes:
        n = math.prod(s)
        out.append(flat[:, off:off + n].reshape((NDEV,) + tuple(s)))
        off += n
    return out


def _to_full(gathered, axis):
    s = gathered.shape[1:]
    return jnp.moveaxis(gathered, 0, axis).reshape(s[:axis] + (NDEV * s[axis],) + s[axis + 1:])


def _to_slots(full, axis):
    s = full.shape
    return jnp.moveaxis(full.reshape(s[:axis] + (NDEV, s[axis] // NDEV) + s[axis + 1:]), axis, 0)


def kernel(x, meta_tokens, ffn1_norm, ffn1_wg, ffn1_wu, ffn1_wd, mix_norm, ffn2_norm, ffn2_wg, ffn2_wu, ffn2_wd, w_in_even, pool_w, pool_scale, hgrn_lb_logits, hgrn_gnorm, w_out_even, w_in_odd, conv_w, conv_b, conv_ln_g, conv_ln_b, lru_conv_w, lru_conv_b, lru_wa, lru_ba, lru_wx, lru_bx, lru_lambda, w_out_odd, final_norm, loss_target, m_meta_tokens, m_ffn1_norm, m_ffn1_wg, m_ffn1_wu, m_ffn1_wd, m_mix_norm, m_ffn2_norm, m_ffn2_wg, m_ffn2_wu, m_ffn2_wd, m_w_in_even, m_pool_w, m_pool_scale, m_hgrn_lb_logits, m_hgrn_gnorm, m_w_out_even, m_w_in_odd, m_conv_w, m_conv_b, m_conv_ln_g, m_conv_ln_b, m_lru_conv_w, m_lru_conv_b, m_lru_wa, m_lru_ba, m_lru_wx, m_lru_bx, m_lru_lambda, m_w_out_odd, m_final_norm, v_meta_tokens, v_ffn1_norm, v_ffn1_wg, v_ffn1_wu, v_ffn1_wd, v_mix_norm, v_ffn2_norm, v_ffn2_wg, v_ffn2_wu, v_ffn2_wd, v_w_in_even, v_pool_w, v_pool_scale, v_hgrn_lb_logits, v_hgrn_gnorm, v_w_out_even, v_w_in_odd, v_conv_w, v_conv_b, v_conv_ln_g, v_conv_ln_b, v_lru_conv_w, v_lru_conv_b, v_lru_wa, v_lru_ba, v_lru_wx, v_lru_bx, v_lru_lambda, v_w_out_odd, v_final_norm):
    args = (meta_tokens, ffn1_norm, ffn1_wg, ffn1_wu, ffn1_wd, mix_norm, ffn2_norm, ffn2_wg, ffn2_wu, ffn2_wd, w_in_even, pool_w, pool_scale, hgrn_lb_logits, hgrn_gnorm, w_out_even, w_in_odd, conv_w, conv_b, conv_ln_g, conv_ln_b, lru_conv_w, lru_conv_b, lru_wa, lru_ba, lru_wx, lru_bx, lru_lambda, w_out_odd, final_norm)
    margs = (m_meta_tokens, m_ffn1_norm, m_ffn1_wg, m_ffn1_wu, m_ffn1_wd, m_mix_norm, m_ffn2_norm, m_ffn2_wg, m_ffn2_wu, m_ffn2_wd, m_w_in_even, m_pool_w, m_pool_scale, m_hgrn_lb_logits, m_hgrn_gnorm, m_w_out_even, m_w_in_odd, m_conv_w, m_conv_b, m_conv_ln_g, m_conv_ln_b, m_lru_conv_w, m_lru_conv_b, m_lru_wa, m_lru_ba, m_lru_wx, m_lru_bx, m_lru_lambda, m_w_out_odd, m_final_norm)
    vargs = (v_meta_tokens, v_ffn1_norm, v_ffn1_wg, v_ffn1_wu, v_ffn1_wd, v_mix_norm, v_ffn2_norm, v_ffn2_wg, v_ffn2_wu, v_ffn2_wd, v_w_in_even, v_pool_w, v_pool_scale, v_hgrn_lb_logits, v_hgrn_gnorm, v_w_out_even, v_w_in_odd, v_conv_w, v_conv_b, v_conv_ln_g, v_conv_ln_b, v_lru_conv_w, v_lru_conv_b, v_lru_wa, v_lru_ba, v_lru_wx, v_lru_bx, v_lru_lambda, v_w_out_odd, v_final_norm)
    Wl = dict(zip(W_NAMES, args))
    Ml = dict(zip(W_NAMES, margs))
    Vl = dict(zip(W_NAMES, vargs))

    small_shapes = [Wl[n].shape for n in SMALL_SHARDED]
    ffn = lambda p, l: [(p + s, l) for s in ('_wg', '_wu', '_wd')]
    mix = lambda p: [('w_in_' + p, None), ('w_out_' + p, None)]
    ggroups = [ffn('ffn1', 0)[:2], ffn('ffn1', 0)[2:], mix('even'), ffn('ffn2', 0), ffn('ffn1', 1), mix('odd'),
               ffn('ffn2', 1)]
    colsharded = lambda n: SHARD_AXIS[n] == 2

    def shard(n, l):
        w = Wl[n][0 if l is None else l].astype(MXU)
        return w.T if colsharded(n) else w

    srcs = [[shard(n, l) for n, l in g] for g in ggroups]
    srcs[0] = [_pack([Wl[n] for n in SMALL_SHARDED])] + srcs[0]
    handles, _ = _exchange_start(srcs, True, "gather_start")
    W = {n: Wl[n] for n in REPLICATED}

    def fetch(k, after):
        lands = _exchange_wait(handles[k], True, after, f"gather_wait_{k}")
        if k == 0:
            for n, g in zip(SMALL_SHARDED, _unpack8(lands[0], small_shapes)):
                W[n] = _to_full(g, SHARD_AXIS[n])
            lands = lands[1:]
        for (n, l), g in zip(ggroups[k], lands):
            W[n if l is None else (n, l)] = g.reshape(-1, g.shape[-1])

    pending = []

    def emit(tag, grads):
        slots = [g.astype(MXU).reshape(NDEV, -1, g.shape[-1]) for _, _, g in grads]
        hs, token = _exchange_start([slots], False, f"scatter_start_{tag}")
        pending.append((tag, hs[0], [(n, l) for n, l, _ in grads]))
        return token[0, 0]

    loss_part, grad_x, G = _local_step(x[0], loss_target[0], W, fetch, emit)

    small_slots = [_to_slots(G[n].astype(F32), SHARD_AXIS[n]) for n in SMALL_SHARDED]
    send = [_pack8(small_slots), _pack([G[n] for n in REPLICATED]),
            jnp.broadcast_to(loss_part, (8, 128))]
    got = _exchange(send, [False, True, True], "scatter_small")
    loss = jnp.sum(got[2][:, 0, 0])
    recv = {}
    for tag, handle, keys in pending:
        for key, r in zip(keys, _exchange_wait(handle, False, got[0], f"scatter_wait_{tag}")):
            recv[key] = r

    outs = {}
    for n in BIG:
        shp = Wl[n].shape
        C = shp[-1]
        rs = [recv[n, None]] if shp[0] == 1 else [recv[n, l] for l in range(shp[0])]
        if colsharded(n):
            rs = [_sum8(r, f"sum8_{n}_{j}").T for j, r in enumerate(rs)]
        res = _adamw(rs, Wl[n].reshape(-1, C), Ml[n].reshape(-1, C), Vl[n].reshape(-1, C), f"adamw_{n}",
                     summed=colsharded(n))
        outs[n] = [o.reshape(shp) for o in res]
    for names, r, tag in ((SMALL_SHARDED, got[0], "small"), (REPLICATED, got[1], "repl")):
        shapes = [Wl[n].shape for n in names]
        res = _adamw([r], _pack([Wl[n] for n in names]), _pack([Ml[n] for n in names]),
                     _pack([Vl[n] for n in names]), f"adamw_{tag}")
        unp = [_unpack(o, shapes) for o in res]
        for k, n in enumerate(names):
            outs[n] = [unp[j][k] for j in range(4)]

    result = [loss, grad_x[None]]
    for j in range(4):
        result += [outs[n][j] for n in W_NAMES]
    return tuple(result)
```

```python
import functools
import math

import jax
import jax.numpy as jnp
from jax import lax
from jax.experimental import pallas as pl
from jax.experimental.pallas import tpu as pltpu

F32 = jnp.float32
MXU = jnp.bfloat16
EPS = 1e-6
CH = 128
HG = 128
N_META = 16
CONV_W = 31
LRU_W = 4
LRU_C = 8.0
VMEM_LIMIT = 48 * 2 ** 20
MM_VMEM_BUDGET = 36 * 2 ** 20
ADAM_LR, ADAM_B1, ADAM_B2, ADAM_EPS, ADAM_WD, ADAM_STEP = 0.001, 0.9, 0.999, 1e-08, 0.01, 10
MESH_AXES = ("x", "y", "c")
NDEV = 8

W_NAMES = ['meta_tokens', 'ffn1_norm', 'ffn1_wg', 'ffn1_wu', 'ffn1_wd', 'mix_norm', 'ffn2_norm', 'ffn2_wg', 'ffn2_wu',
           'ffn2_wd', 'w_in_even', 'pool_w', 'pool_scale', 'hgrn_lb_logits', 'hgrn_gnorm', 'w_out_even', 'w_in_odd',
           'conv_w', 'conv_b', 'conv_ln_g', 'conv_ln_b', 'lru_conv_w', 'lru_conv_b', 'lru_wa', 'lru_ba', 'lru_wx',
           'lru_bx', 'lru_lambda', 'w_out_odd', 'final_norm']
SHARD_AXIS = {'meta_tokens': 1, 'ffn1_wg': 2, 'ffn1_wu': 2, 'ffn1_wd': 1, 'ffn2_wg': 2, 'ffn2_wu': 2, 'ffn2_wd': 1,
              'w_in_even': 2, 'w_out_even': 1, 'w_in_odd': 2, 'conv_w': 2, 'conv_b': 1, 'conv_ln_g': 1,
              'conv_ln_b': 1, 'lru_conv_w': 2, 'lru_conv_b': 1, 'lru_ba': 1, 'lru_bx': 1, 'lru_lambda': 1,
              'w_out_odd': 1}
BIG = ['ffn1_wg', 'ffn1_wu', 'ffn1_wd', 'ffn2_wg', 'ffn2_wu', 'ffn2_wd', 'w_in_even', 'w_out_even', 'w_in_odd',
       'w_out_odd']
SMALL_SHARDED = [n for n in W_NAMES if n in SHARD_AXIS and n not in BIG]
REPLICATED = [n for n in W_NAMES if n not in SHARD_AXIS]


def _cparams(sem=None, vmem=VMEM_LIMIT):
    return pltpu.CompilerParams(dimension_semantics=sem, vmem_limit_bytes=vmem)


def _tiles(n):
    return [c for c in range(128, n + 1, 128) if n % c == 0] or [n]


def _tile(n, cap=1024):
    return max([c for c in _tiles(n) if c <= cap], default=_tiles(n)[0])


def _rowtile(n):
    for c in (256, 352, 128, 64, 32, 16, 8):
        if n % c == 0:
            return c
    return n


def _copies(srcs, lands, bcast, ssem, rsem, lsem):
    x, y, c = lax.axis_index("x"), lax.axis_index("y"), lax.axis_index("c")
    me = 4 * x + 2 * y + c
    locs, sends, recvs = [], [], []
    for a in range(len(srcs)):
        locs.append(pltpu.make_async_copy(srcs[a] if bcast[a] else srcs[a].at[me], lands[a].at[me], lsem.at[a]))
        for m in range(1, NDEV):
            px = 1 - x if (m >> 2) & 1 else x
            py = 1 - y if (m >> 1) & 1 else y
            pc = 1 - c if m & 1 else c
            peer = 4 * px + 2 * py + pc
            src = srcs[a] if bcast[a] else srcs[a].at[peer]
            k = a * NDEV + m
            for dst, out in ((lands[a].at[me], sends), (lands[a].at[peer], recvs)):
                out.append(pltpu.make_async_remote_copy(src_ref=src, dst_ref=dst, send_sem=ssem.at[k],
                                                        recv_sem=rsem.at[k], device_id=(px, py, pc),
                                                        device_id_type=pl.DeviceIdType.MESH))
    return locs, sends, recvs


def _land_shape(arr, bc):
    return (NDEV,) + tuple(arr.shape if bc else arr.shape[1:])


def _exchange(arrays, bcast, name):
    n = len(arrays)

    def body(*refs):
        locs, sends, recvs = _copies(refs[:n], refs[n:2 * n], bcast, refs[2 * n], refs[2 * n + 1], refs[2 * n + 2])
        for d in locs + sends:
            d.start()
        for r in recvs:
            r.wait_recv()
        for s in sends:
            s.wait_send()
        for loc in locs:
            loc.wait()

    out_shape = tuple(jax.ShapeDtypeStruct(_land_shape(arr, bc), arr.dtype) for arr, bc in zip(arrays, bcast))
    any_spec = pl.BlockSpec(memory_space=pl.ANY)
    return pl.pallas_call(
        body, name=name, out_shape=out_shape, in_specs=[any_spec] * n, out_specs=tuple([any_spec] * n),
        scratch_shapes=[pltpu.SemaphoreType.DMA((n * NDEV,)), pltpu.SemaphoreType.DMA((n * NDEV,)),
                        pltpu.SemaphoreType.DMA((n,))],
    )(*arrays)


_HBM = pl.BlockSpec(memory_space=pltpu.HBM)
_SEM = pl.BlockSpec(memory_space=pltpu.SEMAPHORE)
_EFFECT = pltpu.SideEffectType.DATAFLOW_SIDE_EFFECTING


def _exchange_start(groups, bcast, name):
    sizes = [len(g) for g in groups]
    srcs = [a for g in groups for a in g]
    n, ng = len(srcs), len(groups)
    lands = [lax.empty(_land_shape(a, bcast), a.dtype) for a in srcs]

    def body(*refs):
        off = 0
        for gi, sz in enumerate(sizes):
            sem = refs[2 * n + 3 * gi:2 * n + 3 * gi + 3]
            locs, sends, _ = _copies(refs[off:off + sz], refs[n + off:n + off + sz], [bcast] * sz, *sem)
            for d in locs + sends:
                d.start()
            off += sz
        refs[-1][...] = jnp.zeros((8, 128), F32)

    sems = []
    for sz in sizes:
        sems += [pltpu.SemaphoreType.DMA((sz * NDEV,)), pltpu.SemaphoreType.DMA((sz * NDEV,)),
                 pltpu.SemaphoreType.DMA((sz,))]
    thru = [pltpu.HBM(a.shape, a.dtype) for a in srcs + lands]
    outs = pl.pallas_call(
        body, name=name, out_shape=tuple(sems + thru + [jax.ShapeDtypeStruct((8, 128), F32)]),
        in_specs=[_HBM] * (2 * n),
        out_specs=tuple([_SEM] * (3 * ng) + [_HBM] * (2 * n) + [pl.BlockSpec(memory_space=pltpu.VMEM)]),
        input_output_aliases={i: 3 * ng + i for i in range(2 * n)},
        compiler_params=pltpu.CompilerParams(has_side_effects=_EFFECT),
    )(*[pltpu.with_memory_space_constraint(a, pltpu.HBM) for a in srcs + lands])
    handles, off = [], 0
    for gi, sz in enumerate(sizes):
        handles.append((outs[3 * gi:3 * gi + 3], outs[3 * ng + off:3 * ng + off + sz],
                        outs[3 * ng + n + off:3 * ng + n + off + sz]))
        off += sz
    return handles, outs[-1]


def _exchange_wait(handle, bcast, after, name):
    sems, srcs, lands = handle
    n = len(srcs)

    def body(*refs):
        locs, sends, recvs = _copies(refs[:n], refs[n:2 * n], [bcast] * n, *refs[2 * n:2 * n + 3])
        for r in recvs:
            r.wait_recv()
        for s in sends:
            s.wait_send()
        for loc in locs:
            loc.wait()

    outs = pl.pallas_call(
        body, name=name, out_shape=tuple(pltpu.HBM(a.shape, a.dtype) for a in list(srcs) + list(lands)),
        in_specs=[_HBM] * (2 * n) + [_SEM] * 3 + [pl.BlockSpec(memory_space=pl.ANY)],
        out_specs=tuple([_HBM] * (2 * n)), input_output_aliases={i: i for i in range(2 * n)},
        compiler_params=pltpu.CompilerParams(has_side_effects=_EFFECT),
    )(*srcs, *lands, *sems, after)
    return outs[n:]


def _sum8(recv, name):
    _, R, C = recv.shape
    br = _rowtile(R)

    def body(r_ref, o_ref):
        s = r_ref[0].astype(F32)
        for k in range(1, NDEV):
            s = s + r_ref[k].astype(F32)
        o_ref[...] = s

    return pl.pallas_call(
        body, name=name, grid=(R // br,), out_shape=jax.ShapeDtypeStruct((R, C), F32),
        in_specs=[pl.BlockSpec((NDEV, br, C), lambda i: (0, i, 0))], out_specs=pl.BlockSpec((br, C), lambda i: (i, 0)),
        compiler_params=_cparams(("parallel",)),
    )(recv)


def _adamw(recvs, w, m, v, name, summed=False):
    R, C = w.shape
    nr = len(recvs)
    br = _rowtile(R // nr)
    nb0 = R // nr // br

    def body(*refs):
        w_ref, m_ref, v_ref, g_o, d_o, m_o, v_o = refs[nr:]
        g = None
        for j in range(nr):
            if summed:
                s = refs[j][...]
            else:
                s = refs[j][0].astype(F32)
                for k in range(1, NDEV):
                    s = s + refs[j][k].astype(F32)
            g = s if g is None else jnp.where(pl.program_id(0) >= j * nb0, s, g)
        mn = ADAM_B1 * m_ref[...] + (1.0 - ADAM_B1) * g
        vn = ADAM_B2 * v_ref[...] + (1.0 - ADAM_B2) * (g * g)
        m_hat = mn / (1.0 - ADAM_B1 ** ADAM_STEP)
        v_hat = vn / (1.0 - ADAM_B2 ** ADAM_STEP)
        g_o[...] = g
        d_o[...] = -ADAM_LR * (m_hat / (jnp.sqrt(v_hat) + ADAM_EPS) + ADAM_WD * w_ref[...])
        m_o[...] = mn
        v_o[...] = vn

    def rspec(j):
        if summed:
            return pl.BlockSpec((br, C), lambda i: (jnp.clip(i - j * nb0, 0, nb0 - 1), 0))
        return pl.BlockSpec((NDEV, br, C), lambda i: (0, jnp.clip(i - j * nb0, 0, nb0 - 1), 0))

    blk = pl.BlockSpec((br, C), lambda i: (i, 0))
    sds = jax.ShapeDtypeStruct((R, C), F32)
    return pl.pallas_call(
        body, name=name, grid=(R // br,), out_shape=(sds, sds, sds, sds),
        in_specs=[rspec(j) for j in range(nr)] + [blk, blk, blk], out_specs=(blk, blk, blk, blk),
        compiler_params=_cparams(("arbitrary",)),
    )(*recvs, w, m, v)


_DIMS = {"NN": ((1,), (0,)), "NT": ((1,), (1,)), "TN": ((0,), (0,))}


def _dot(a, b, mode="NN"):
    return lax.dot_general(a.astype(MXU), b.astype(MXU), (_DIMS[mode], ((), ())), preferred_element_type=F32)


def _dotf(a, b, mode="NN"):
    return lax.dot_general(a, b, (_DIMS[mode], ((), ())), precision=lax.Precision.HIGH,
                           preferred_element_type=F32)


def _mm(pairs, mode, name, res=None, res_scale=1.0, out_dtype=F32, dep=None):
    a0, b0 = pairs[0]
    M = a0.shape[1] if mode == "TN" else a0.shape[0]
    N = b0.shape[0] if mode == "NT" else b0.shape[1]
    npairs = len(pairs)

    def vmem_bytes(tm, tn):
        total = tm * tn * 4 * (2 + (2 if res is not None else 0) + 2)
        for a, b in pairs:
            ka = a.shape[0] if mode == "TN" else a.shape[1]
            kb = b.shape[1] if mode == "NT" else b.shape[0]
            for k, t, arr in ((ka, tm, a), (kb, tn, b)):
                total += k * t * (2 * arr.dtype.itemsize + (2 if arr.dtype == F32 else 0))
        return total

    tm, tn = max(((tm, tn) for tm in _tiles(M) for tn in _tiles(N) if vmem_bytes(tm, tn) <= MM_VMEM_BUDGET),
                 key=lambda t: (t[0] * t[1], t[1]), default=(_tiles(M)[0], _tiles(N)[0]))

    def body(*refs):
        acc = None
        for p in range(npairs):
            d = _dot(refs[2 * p][...], refs[2 * p + 1][...], mode)
            acc = d if acc is None else acc + d
        if res_scale != 1.0:
            acc = res_scale * acc
        if res is not None:
            acc = refs[2 * npairs][...] + acc
        refs[-1][...] = acc.astype(out_dtype)

    in_specs, args = [], []
    for a, b in pairs:
        if mode == "TN":
            in_specs.append(pl.BlockSpec((a.shape[0], tm), lambda i, j: (0, i)))
        else:
            in_specs.append(pl.BlockSpec((tm, a.shape[1]), lambda i, j: (i, 0)))
        if mode == "NT":
            in_specs.append(pl.BlockSpec((tn, b.shape[1]), lambda i, j: (j, 0)))
        else:
            in_specs.append(pl.BlockSpec((b.shape[0], tn), lambda i, j: (0, j)))
        args += [a, b]
    if res is not None:
        in_specs.append(pl.BlockSpec((tm, tn), lambda i, j: (i, j)))
        args.append(res)
    if dep is not None:
        in_specs.append(pl.BlockSpec(memory_space=pl.ANY))
        args.append(dep)
    return pl.pallas_call(
        body, name=name, grid=(M // tm, N // tn), out_shape=jax.ShapeDtypeStruct((M, N), out_dtype),
        in_specs=in_specs, out_specs=pl.BlockSpec((tm, tn), lambda i, j: (i, j)),
        compiler_params=_cparams(("parallel", "parallel")),
    )(*args)


def _rms_fwd(h, gamma, name):
    T, D = h.shape
    tm = _tile(T)

    def body(h_ref, g_ref, o_ref):
        x = h_ref[...]
        r = lax.rsqrt(jnp.mean(x * x, axis=-1, keepdims=True) + EPS)
        o_ref[...] = (x * r * g_ref[...]).astype(MXU)

    return pl.pallas_call(
        body, name=name, grid=(T // tm,), out_shape=jax.ShapeDtypeStruct((T, D), MXU),
        in_specs=[pl.BlockSpec((tm, D), lambda i: (i, 0)), pl.BlockSpec((1, D), lambda i: (0, 0))],
        out_specs=pl.BlockSpec((tm, D), lambda i: (i, 0)), compiler_params=_cparams(("parallel",)),
    )(h, gamma)


def _rms_bwd_math(x, gamma, dy):
    r = lax.rsqrt(jnp.mean(x * x, axis=-1, keepdims=True) + EPS)
    z = dy * gamma
    dx = r * z - x * (r * r * r) * jnp.mean(z * x, axis=-1, keepdims=True)
    dgamma = jnp.sum(dy * x * r, axis=0, keepdims=True)
    return dx, dgamma


def _rms_bwd(h, gamma, dxn, dres, name):
    T, D = h.shape
    tm = _tile(T)

    def body(h_ref, g_ref, dxn_ref, dres_ref, dh_ref, dh16_ref, dg_ref):
        dx, dgamma = _rms_bwd_math(h_ref[...], g_ref[...], dxn_ref[...])
        dh = dres_ref[...] + dx
        dh_ref[...] = dh
        dh16_ref[...] = dh.astype(MXU)

        @pl.when(pl.program_id(0) == 0)
        def _():
            dg_ref[...] = jnp.zeros_like(dg_ref)

        dg_ref[...] += dgamma

    row = pl.BlockSpec((tm, D), lambda i: (i, 0))
    vec = pl.BlockSpec((1, D), lambda i: (0, 0))
    return pl.pallas_call(
        body, name=name, grid=(T // tm,),
        out_shape=(jax.ShapeDtypeStruct((T, D), F32), jax.ShapeDtypeStruct((T, D), MXU),
                   jax.ShapeDtypeStruct((1, D), F32)),
        in_specs=[row, vec, row, row], out_specs=(row, row, vec), compiler_params=_cparams(("arbitrary",)),
    )(h, gamma, dxn, dres)


def _loss_head(h, gamma, tgt, lo, hi, name):
    T, D = h.shape
    tm = _tile(T)

    def body(h_ref, g_ref, t_ref, loss_ref, dh_ref, dh16_ref, dg_ref):
        i = pl.program_id(0)
        x = h_ref[...]
        r = lax.rsqrt(jnp.mean(x * x, axis=-1, keepdims=True) + EPS)
        y = x * r * g_ref[...]
        rows = i * tm + lax.broadcasted_iota(jnp.int32, (tm, 1), 0)
        valid = jnp.logical_and(rows >= lo, rows < hi)
        diff = jnp.where(valid, y - t_ref[...], 0.0)
        part = 0.5 * jnp.sum(jnp.sum(diff * diff, axis=-1, keepdims=True) / D, axis=0, keepdims=True)
        dx, dgamma = _rms_bwd_math(x, g_ref[...], diff / D)
        dh_ref[...] = dx
        dh16_ref[...] = dx.astype(MXU)

        @pl.when(i == 0)
        def _():
            dg_ref[...] = jnp.zeros_like(dg_ref)
            loss_ref[...] = jnp.zeros_like(loss_ref)

        dg_ref[...] += dgamma
        loss_ref[...] += jnp.broadcast_to(part, loss_ref.shape)

    row = pl.BlockSpec((tm, D), lambda i: (i, 0))
    vec = pl.BlockSpec((1, D), lambda i: (0, 0))
    lsp = pl.BlockSpec((8, 128), lambda i: (0, 0))
    return pl.pallas_call(
        body, name=name, grid=(T // tm,),
        out_shape=(jax.ShapeDtypeStruct((8, 128), F32), jax.ShapeDtypeStruct((T, D), F32),
                   jax.ShapeDtypeStruct((T, D), MXU), jax.ShapeDtypeStruct((1, D), F32)),
        in_specs=[row, vec, row], out_specs=(lsp, row, row, vec), compiler_params=_cparams(("arbitrary",)),
    )(h, gamma, tgt)


def _ffn_tiles(T, Fd):
    return (448 if T % 448 == 0 else _tile(T)), max(c for c in _tiles(Fd) if c <= 1536)


def _ffn_up(xn, wg, wu, name):
    T, D = xn.shape
    Fd = wg.shape[0]
    tm, tn = _ffn_tiles(T, Fd)

    def body(x_ref, wg_ref, wu_ref, p_ref, q_ref, a_ref):
        x = x_ref[...]
        g = _dot(x, wg_ref[...], "NT")
        u = _dot(x, wu_ref[...], "NT")
        sg = jax.nn.sigmoid(g)
        q = g * sg
        p_ref[...] = (u * (sg + q * (1.0 - sg))).astype(MXU)
        q_ref[...] = q.astype(MXU)
        a_ref[...] = (q * u).astype(MXU)

    wsp = pl.BlockSpec((tn, D), lambda i, j: (j, 0))
    osp = pl.BlockSpec((tm, tn), lambda i, j: (i, j))
    sds = jax.ShapeDtypeStruct((T, Fd), MXU)
    return pl.pallas_call(
        body, name=name, grid=(T // tm, Fd // tn), out_shape=(sds, sds, sds),
        in_specs=[pl.BlockSpec((tm, D), lambda i, j: (i, 0)), wsp, wsp], out_specs=(osp, osp, osp),
        compiler_params=_cparams(("parallel", "parallel")),
    )(xn, wg, wu)


def _ffn_dact(dy, wd, p, q, scale, name, dep=None):
    T, D = dy.shape
    Fd = wd.shape[0]
    tm, tn = _ffn_tiles(T, Fd)

    def body(dy_ref, wd_ref, p_ref, q_ref, *rest):
        dg_ref, du_ref = rest[-2:]
        da = scale * _dot(dy_ref[...], wd_ref[...], "NT")
        dg_ref[...] = (da * p_ref[...].astype(F32)).astype(MXU)
        du_ref[...] = (da * q_ref[...].astype(F32)).astype(MXU)

    osp = pl.BlockSpec((tm, tn), lambda i, j: (i, j))
    sds = jax.ShapeDtypeStruct((T, Fd), MXU)
    return pl.pallas_call(
        body, name=name, grid=(T // tm, Fd // tn), out_shape=(sds, sds),
        in_specs=[pl.BlockSpec((tm, D), lambda i, j: (i, 0)), pl.BlockSpec((tn, D), lambda i, j: (j, 0)), osp, osp]
        + ([] if dep is None else [pl.BlockSpec(memory_space=pl.ANY)]),
        out_specs=(osp, osp), compiler_params=_cparams(("parallel", "parallel")),
    )(dy, wd, p, q, *([] if dep is None else [dep]))


def _down(v, s):
    return v if s == 0 else pltpu.roll(v, s, 0)


def _up(v, s):
    return v if s == 0 else pltpu.roll(v, v.shape[0] - s, 0)


def _rows(n):
    return lax.broadcasted_iota(jnp.int32, (n, 1), 0)


def _zero_pad_rows(ref, lo_end, hi_start, T):
    ref[pl.ds(0, lo_end), :] = jnp.zeros((lo_end, ref.shape[1]), ref.dtype)
    if T > hi_start:
        ref[pl.ds(hi_start, T - hi_start), :] = jnp.zeros((T - hi_start, ref.shape[1]), ref.dtype)


def _colblock(T, off):
    return pl.BlockSpec((T, 128), lambda j: (0, off + j))


def _vecblock(rows=1):
    return pl.BlockSpec((rows, 128), lambda j: (0, j))


def _pool_lane_consts(n):
    lane = lax.broadcasted_iota(jnp.int32, (n, 256), 1)
    win = jnp.where(lane < 64, 2.0, jnp.where(lane < 128, 4.0, jnp.where(lane < 192, 8.0, 16.0)))
    return lane, win


def _pool_select(lane, s2, s4, s8, s16):
    return jnp.where(lane < 64, s2, jnp.where(lane < 128, s4, jnp.where(lane < 192, s8, s16)))


def _pool_mixed(xh, start):
    s2 = xh + _down(xh, 1)
    s4 = s2 + _down(s2, 2)
    s8 = s4 + _down(s4, 4)
    s16 = s8 + _down(s8, 8)
    n = xh.shape[0] - 16
    lane, _ = _pool_lane_consts(n + 16)
    _, win = _pool_lane_consts(n)
    t1 = (start - CH + 1 + _rows(n)).astype(F32)
    cnt = jnp.minimum(jnp.maximum(t1, 1.0), win)
    return _pool_select(lane, s2, s4, s8, s16)[16:] / cnt - xh[16:]


def _pool_fwd(p, wbd, scale, nreal, real_end, name):
    T = p.shape[0]

    def body(p_ref, w_ref, s_ref, y_ref):
        _zero_pad_rows(y_ref, CH, CH * (1 + nreal), T)

        def chunk(c, carry):
            start = pl.multiple_of(c * CH, CH)
            mixed = _pool_mixed(p_ref[pl.ds(start - 16, CH + 16), :], start)
            y = _dot(mixed, w_ref[...]) * s_ref[...]
            y_ref[pl.ds(start, CH), :] = jnp.where(start + _rows(CH) < real_end, y, 0.0)
            return carry

        _pairs_loop(nreal, lambda c, carry: chunk(c + 1, carry), 0)

    return pl.pallas_call(
        body, name=name, grid=(1,), out_shape=jax.ShapeDtypeStruct((T, 256), F32),
        in_specs=[pl.BlockSpec((T, 256), lambda j: (0, 0)), pl.BlockSpec((256, 256), lambda j: (0, 0)),
                  pl.BlockSpec((1, 256), lambda j: (0, 0))],
        out_specs=pl.BlockSpec((T, 256), lambda j: (0, 0)), compiler_params=_cparams(("arbitrary",)),
    )(p, wbd, scale)


def _pool_bwd(p, wbd, scale, dy, nreal, real_end, name):
    T = p.shape[0]

    def body(p_ref, w_ref, s_ref, dy_ref, dp_ref, dw_ref, ds_ref):
        _zero_pad_rows(dp_ref, CH, CH * (1 + nreal), T)
        dw_ref[...] = jnp.zeros_like(dw_ref)
        ds_ref[...] = jnp.zeros_like(ds_ref)

        def chunk(c, carry):
            start = pl.multiple_of(c * CH, CH)
            mixed = _pool_mixed(p_ref[pl.ds(start - 16, CH + 16), :], start)
            ypre = _dot(mixed, w_ref[...])
            n = CH + 16
            dye = jnp.where(start + _rows(n) < real_end, dy_ref[pl.ds(start, n), :], 0.0)
            dys = dye * s_ref[...]
            ds_ref[...] += jnp.sum(dye[:CH] * ypre, axis=0, keepdims=True)
            dw_ref[...] += _dot(mixed, dys[:CH], "TN")
            dmix = _dot(dys, w_ref[...], "NT")
            lane, win = _pool_lane_consts(n)
            t1 = (start - CH + 1 + _rows(n)).astype(F32)
            z = dmix / jnp.minimum(jnp.maximum(t1, 1.0), win)
            r2 = z + _up(z, 1)
            r4 = r2 + _up(r2, 2)
            r8 = r4 + _up(r4, 4)
            r16 = r8 + _up(r8, 8)
            dp_ref[pl.ds(start, CH), :] = (_pool_select(lane, r2, r4, r8, r16) - dmix)[:CH]
            return carry

        _pairs_loop(nreal, lambda c, carry: chunk(c + 1, carry), 0)

    full = lambda r, c: pl.BlockSpec((r, c), lambda j: (0, 0))
    return pl.pallas_call(
        body, name=name, grid=(1,),
        out_shape=(jax.ShapeDtypeStruct((T, 256), F32), jax.ShapeDtypeStruct((256, 256), F32),
                   jax.ShapeDtypeStruct((1, 256), F32)),
        in_specs=[full(T, 256), full(256, 256), full(1, 256), full(T, 256)],
        out_specs=(full(T, 256), full(256, 256), full(1, 256)), compiler_params=_cparams(("arbitrary",)),
    )(p, wbd, scale, dy)


def _hgrn_chunk(St, qr, fr, ir, gr, l0, l1, gn):
    rows = lax.broadcasted_iota(jnp.int32, (HG, HG), 0)
    cols = lax.broadcasted_iota(jnp.int32, (HG, HG), 1)
    causal = rows >= cols
    ltri = causal.astype(F32)
    lb = jax.nn.sigmoid(l0 - l1)
    sg = jax.nn.sigmoid(fr)
    logf = jnp.log(lb + (1.0 - lb) * sg)
    kk = (1.0 - lb) * (1.0 - sg)
    q = qr * jax.nn.sigmoid(qr)
    b = jnp.dot(ltri, logf, precision=lax.Precision.HIGH, preferred_element_type=F32)
    bl = jnp.sum(logf, axis=0, keepdims=True)
    bm = jnp.sum(jnp.where(_rows(HG) <= HG // 2, logf, 0.0), axis=0, keepdims=True)
    o = _dotf(q * jnp.exp(b), St, "NT")
    A = _dotf(q * jnp.exp(b - bm), kk * jnp.exp(bm - b), "NT")
    o = o + _dotf(jnp.where(causal, A, 0.0), ir)
    St_new = St * jnp.exp(bl) + _dotf(ir, kk * jnp.exp(bl - b), "TN")
    on = o * lax.rsqrt(jnp.mean(o * o, axis=-1, keepdims=True) + EPS) * gn
    return St_new, on * (gr * jax.nn.sigmoid(gr))


def _pairs_loop(n, step, init):
    u = 3 if n % 3 == 0 else 2 if n % 2 == 0 else 1

    def body(i, carry):
        for j in range(u):
            carry = step(u * i + j, carry)
        return carry

    return lax.fori_loop(0, n // u, body, init)


def _hgrn_specs(T):
    return [_colblock(T, 2), _colblock(T, 8), _colblock(T, 14), _colblock(T, 20), _vecblock(), _vecblock(),
            pl.BlockSpec((1, 128), lambda j: (0, 0))]


def _hgrn_fwd(p, l0, l1, gn, nreal, real_end, name):
    T = p.shape[0]
    nch = nreal * (CH // HG)

    def body(q_ref, f_ref, i_ref, g_ref, l0_ref, l1_ref, gn_ref, y_ref, s_ref):
        _zero_pad_rows(y_ref, CH, CH * (1 + nreal), T)

        def chunk(c, St):
            start = pl.multiple_of(CH + c * HG, HG)
            sl = pl.ds(start, HG)
            s_ref[0, c] = St
            St_new, y = _hgrn_chunk(St, q_ref[sl, :], f_ref[sl, :], i_ref[sl, :], g_ref[sl, :], l0_ref[...],
                                    l1_ref[...], gn_ref[...])
            y_ref[sl, :] = jnp.where(start + _rows(HG) < real_end, y, 0.0)
            return St_new

        _pairs_loop(nch, chunk, jnp.zeros((128, 128), F32))

    return pl.pallas_call(
        body, name=name, grid=(6,),
        out_shape=(jax.ShapeDtypeStruct((T, 768), F32), jax.ShapeDtypeStruct((6, nch, 128, 128), F32)),
        in_specs=_hgrn_specs(T),
        out_specs=(_colblock(T, 0), pl.BlockSpec((1, nch, 128, 128), lambda j: (j, 0, 0, 0))),
        compiler_params=_cparams(("parallel",)),
    )(p, p, p, p, l0, l1, gn)


def _hgrn_bwd(p, l0, l1, gn, states, dy, nreal, real_end, name):
    T = p.shape[0]
    nch = nreal * (CH // HG)

    def body(q_ref, f_ref, i_ref, g_ref, l0_ref, l1_ref, gn_ref, s_ref, dy_ref,
             dq_ref, df_ref, di_ref, dg_ref, dl0_ref, dl1_ref, dgn_ref):
        for r in (dq_ref, df_ref, di_ref, dg_ref):
            _zero_pad_rows(r, CH, CH * (1 + nreal), T)

        def chunk(k, carry):
            dSt, a0, a1, agn = carry
            c = nch - 1 - k
            start = pl.multiple_of(CH + c * HG, HG)
            sl = pl.ds(start, HG)
            _, vjp = jax.vjp(_hgrn_chunk, s_ref[0, c], q_ref[sl, :], f_ref[sl, :], i_ref[sl, :], g_ref[sl, :],
                             l0_ref[...], l1_ref[...], gn_ref[...])
            dyc = jnp.where(start + _rows(HG) < real_end, dy_ref[sl, :], 0.0)
            dS, dq, df, di, dg, d0, d1, dgn = vjp((dSt, dyc))
            dq_ref[sl, :] = dq
            df_ref[sl, :] = df
            di_ref[sl, :] = di
            dg_ref[sl, :] = dg
            return dS, a0 + d0, a1 + d1, agn + dgn

        z = jnp.zeros((1, 128), F32)
        _, a0, a1, agn = _pairs_loop(nch, chunk, (jnp.zeros((128, 128), F32), z, z, z))
        dl0_ref[...] = a0
        dl1_ref[...] = a1

        @pl.when(pl.program_id(0) == 0)
        def _():
            dgn_ref[...] = jnp.zeros_like(dgn_ref)

        dgn_ref[...] += agn

    big = jax.ShapeDtypeStruct((T, 768), F32)
    vec = jax.ShapeDtypeStruct((1, 768), F32)
    return pl.pallas_call(
        body, name=name, grid=(6,),
        out_shape=(big, big, big, big, vec, vec, jax.ShapeDtypeStruct((1, 128), F32)),
        in_specs=_hgrn_specs(T) + [pl.BlockSpec((1, nch, 128, 128), lambda j: (j, 0, 0, 0)), _colblock(T, 2)],
        out_specs=(_colblock(T, 0), _colblock(T, 0), _colblock(T, 0), _colblock(T, 0), _vecblock(), _vecblock(),
                   pl.BlockSpec((1, 128), lambda j: (0, 0))),
        compiler_params=_cparams(("arbitrary",), 60 * 2 ** 20),
    )(p, p, p, p, l0, l1, gn, states, dy)


def _glu(a, b):
    return a * jax.nn.sigmoid(b)


def _conv_post(cv, ln_g, ln_b):
    mu = jnp.mean(cv, axis=-1, keepdims=True)
    d = cv - mu
    var = jnp.mean(d * d, axis=-1, keepdims=True)
    un = d * lax.rsqrt(var + EPS) * ln_g + ln_b
    return un * jax.nn.sigmoid(un)


def _causal_conv(uh, w_ref, width, halo):
    acc = None
    for j in range(width):
        term = _down(uh, width - 1 - j) * w_ref[pl.ds(j, 1), :]
        acc = term if acc is None else acc + term
    return acc[halo:]


def _conf_fwd(p, cw, cb, lg, lb, nreal, real_end, name):
    T = p.shape[0]

    def body(a_ref, b_ref, w_ref, cb_ref, lg_ref, lb_ref, y_ref):
        _zero_pad_rows(y_ref, CH, CH * (1 + nreal), T)

        def chunk(c, carry):
            start = pl.multiple_of(c * CH, CH)
            ext = pl.ds(start - 32, CH + 32)
            cv = _causal_conv(_glu(a_ref[ext, :], b_ref[ext, :]), w_ref, CONV_W, 32) + cb_ref[...]
            y = _conv_post(cv, lg_ref[...], lb_ref[...])
            y_ref[pl.ds(start, CH), :] = jnp.where(start + _rows(CH) < real_end, y, 0.0)
            return carry

        _pairs_loop(nreal, lambda c, carry: chunk(c + 1, carry), 0)

    return pl.pallas_call(
        body, name=name, grid=(4,), out_shape=jax.ShapeDtypeStruct((T, 512), F32),
        in_specs=[_colblock(T, 0), _colblock(T, 4), _vecblock(32), _vecblock(), _vecblock(), _vecblock()],
        out_specs=_colblock(T, 0), compiler_params=_cparams(("parallel",)),
    )(p, p, cw, cb, lg, lb)


def _conf_bwd(p, cw, cb, lg, lb, dy, nreal, real_end, name):
    T = p.shape[0]

    def body(a_ref, b_ref, w_ref, cb_ref, lg_ref, lb_ref, dy_ref, da_ref, db_ref, dw_ref, dcb_ref, dlg_ref, dlb_ref):
        _zero_pad_rows(da_ref, CH, CH * (1 + nreal), T)
        _zero_pad_rows(db_ref, CH, CH * (1 + nreal), T)
        for r in (dw_ref, dcb_ref, dlg_ref, dlb_ref):
            r[...] = jnp.zeros_like(r)

        def chunk(c, carry):
            start = pl.multiple_of(c * CH, CH)
            ext = pl.ds(start - 32, CH + 64)
            ue = _glu(a_ref[ext, :], b_ref[ext, :])
            cv = _causal_conv(ue, w_ref, CONV_W, 32) + cb_ref[...]
            dye = jnp.where(start + _rows(CH + 32) < real_end, dy_ref[pl.ds(start, CH + 32), :], 0.0)
            _, vjp_cur = jax.vjp(_conv_post, cv[:CH], lg_ref[...], lb_ref[...])
            dc_cur, dlg, dlb = vjp_cur(dye[:CH])
            _, vjp_halo = jax.vjp(_conv_post, cv[CH:], lg_ref[...], lb_ref[...])
            dce = jnp.concatenate([dc_cur, vjp_halo(dye[CH:])[0]], axis=0)
            dlg_ref[...] += dlg
            dlb_ref[...] += dlb
            dcb_ref[...] += jnp.sum(dc_cur, axis=0, keepdims=True)
            du = None
            for j in range(CONV_W):
                w_j = w_ref[pl.ds(j, 1), :]
                term = _up(dce, CONV_W - 1 - j)[:CH] * w_j
                du = term if du is None else du + term
                dw_ref[pl.ds(j, 1), :] += jnp.sum(dc_cur * _up(ue, 2 + j)[:CH], axis=0, keepdims=True)
            cur = pl.ds(start, CH)
            _, vjp_glu = jax.vjp(_glu, a_ref[cur, :], b_ref[cur, :])
            da, db = vjp_glu(du)
            da_ref[cur, :] = da
            db_ref[cur, :] = db
            return carry

        _pairs_loop(nreal, lambda c, carry: chunk(c + 1, carry), 0)

    big = jax.ShapeDtypeStruct((T, 512), F32)
    vec = jax.ShapeDtypeStruct((1, 512), F32)
    return pl.pallas_call(
        body, name=name, grid=(4,), out_shape=(big, big, jax.ShapeDtypeStruct((32, 512), F32), vec, vec, vec),
        in_specs=[_colblock(T, 0), _colblock(T, 4), _vecblock(32), _vecblock(), _vecblock(), _vecblock(),
                  _colblock(T, 0)],
        out_specs=(_colblock(T, 0), _colblock(T, 0), _vecblock(32), _vecblock(), _vecblock(), _vecblock()),
        compiler_params=_cparams(("parallel",)),
    )(p, p, cw, cb, lg, lb, dy)


def _softplus_neg(lam):
    e = jnp.exp(-lam)
    small = e * (1.0 - e * (0.5 - e * (1.0 / 3.0 - e * 0.25)))
    return jnp.where(e < 0.02, small, jnp.log(1.0 + e))


def _one_minus_exp(x):
    series = -x * (1.0 + x * (0.5 + x * (1.0 / 6.0 + x * (1.0 / 24.0 + x * (1.0 / 120.0)))))
    return jnp.where(x > -0.05, series, 1.0 - jnp.exp(x))


def _lru_pre(u, wa, wx, ba, bx, lam, first):
    r = jax.nn.sigmoid(_dot(u, wa) + ba)
    i = jax.nn.sigmoid(_dot(u, wx) + bx)
    log_a = -LRU_C * r * _softplus_neg(lam)
    a = jnp.exp(log_a)
    mult = jnp.sqrt(_one_minus_exp(2.0 * log_a))
    return a, jnp.where(first, 1.0, mult) * (i * u)


def _gelu_gate(gate, h):
    inner = math.sqrt(2.0 / math.pi) * (gate + 0.044715 * (gate * gate * gate))
    return 0.5 * gate * (1.0 + jnp.tanh(inner)) * h


def _lru_specs(T):
    mat = pl.BlockSpec((1, 128, 128), lambda j: (j, 0, 0))
    return [_colblock(T, 8), _colblock(T, 12), _vecblock(8), _vecblock(), mat, mat, _vecblock(), _vecblock(),
            _vecblock()]


def _lru_fwd(p, cw, cb, wa, wx, ba, bx, lam, nreal, real_end, name):
    T = p.shape[0]

    def body(x_ref, g_ref, w_ref, cb_ref, wa_ref, wx_ref, ba_ref, bx_ref, lam_ref, y_ref, h_ref):
        _zero_pad_rows(y_ref, CH, CH * (1 + nreal), T)
        _zero_pad_rows(h_ref, CH, CH * (1 + nreal), T)
        rows = _rows(CH)

        def chunk(c, hprev):
            start = pl.multiple_of(c * CH, CH)
            u = _causal_conv(x_ref[pl.ds(start - 8, CH + 8), :], w_ref, LRU_W, 8) + cb_ref[...]
            A, B = _lru_pre(u, wa_ref[0], wx_ref[0], ba_ref[...], bx_ref[...], lam_ref[...], start + rows == CH)
            s = 1
            while s < CH:
                B = A * jnp.where(rows >= s, _down(B, s), 0.0) + B
                A = A * jnp.where(rows >= s, _down(A, s), 1.0)
                s *= 2
            h = B + A * hprev
            cur = pl.ds(start, CH)
            h_ref[cur, :] = h
            y_ref[cur, :] = jnp.where(start + rows < real_end, _gelu_gate(g_ref[cur, :], h), 0.0)
            return jnp.sum(jnp.where(rows == CH - 1, h, 0.0), axis=0, keepdims=True)

        _pairs_loop(nreal, lambda c, carry: chunk(c + 1, carry), jnp.zeros((1, 128), F32))

    big = jax.ShapeDtypeStruct((T, 512), F32)
    return pl.pallas_call(
        body, name=name, grid=(4,), out_shape=(big, big), in_specs=_lru_specs(T),
        out_specs=(_colblock(T, 0), _colblock(T, 0)), compiler_params=_cparams(("parallel",)),
    )(p, p, cw, cb, wa, wx, ba, bx, lam)


def _lru_bwd(p, cw, cb, wa, wx, ba, bx, lam, hs, dy, nreal, real_end, name):
    T = p.shape[0]

    def body(x_ref, g_ref, w_ref, cb_ref, wa_ref, wx_ref, ba_ref, bx_ref, lam_ref, h_ref, dy_ref,
             dx_ref, dgate_ref, dw_ref, dcb_ref, dwa_ref, dwx_ref, dba_ref, dbx_ref, dlam_ref):
        _zero_pad_rows(dx_ref, CH, CH * (1 + nreal), T)
        _zero_pad_rows(dgate_ref, CH, CH * (1 + nreal), T)
        for r in (dw_ref, dcb_ref, dwa_ref, dwx_ref, dba_ref, dbx_ref, dlam_ref):
            r[...] = jnp.zeros_like(r)
        rows = _rows(CH)

        def chunk(k, carry):
            cdh, du_head = carry
            c = nreal - k
            start = pl.multiple_of(c * CH, CH)
            cur = pl.ds(start, CH)
            xe = x_ref[pl.ds(start - 8, CH + 8), :]
            u = _causal_conv(xe, w_ref, LRU_W, 8) + cb_ref[...]
            first = start + rows == CH
            (a, _), vjp_pre = jax.vjp(lambda uu, m1, m2, b1, b2, ll: _lru_pre(uu, m1, m2, b1, b2, ll, first),
                                      u, wa_ref[0], wx_ref[0], ba_ref[...], bx_ref[...], lam_ref[...])
            h = h_ref[cur, :]
            hm1 = _down(h_ref[pl.ds(start - 8, CH + 8), :], 1)[8:]
            _, vjp_post = jax.vjp(_gelu_gate, g_ref[cur, :], h)
            dgate, D = vjp_post(jnp.where(start + rows < real_end, dy_ref[cur, :], 0.0))
            dgate_ref[cur, :] = dgate
            D = D + jnp.where(rows == CH - 1, cdh, 0.0)
            C = jnp.where(rows < CH - 1, _up(a, 1), 0.0)
            s = 1
            while s < CH:
                D = D + C * jnp.where(rows + s < CH, _up(D, s), 0.0)
                C = C * jnp.where(rows + s < CH, _up(C, s), 1.0)
                s *= 2
            du, dwa, dwx, dba, dbx, dlam = vjp_pre((D * hm1, D))
            dwa_ref[0] += dwa
            dwx_ref[0] += dwx
            dba_ref[...] += dba
            dbx_ref[...] += dbx
            dlam_ref[...] += dlam
            dcb_ref[...] += jnp.sum(du, axis=0, keepdims=True)
            due = jnp.concatenate([du, du_head], axis=0)
            dx = None
            for j in range(LRU_W):
                term = _up(due, LRU_W - 1 - j)[:CH] * w_ref[pl.ds(j, 1), :]
                dx = term if dx is None else dx + term
                dw_ref[pl.ds(j, 1), :] += jnp.sum(du * _up(xe, 8 - (LRU_W - 1) + j)[:CH], axis=0, keepdims=True)
            dx_ref[cur, :] = dx
            return jnp.sum(jnp.where(rows == 0, a * D, 0.0), axis=0, keepdims=True), du[:8]

        _pairs_loop(nreal, chunk, (jnp.zeros((1, 128), F32), jnp.zeros((8, 128), F32)))

    big = jax.ShapeDtypeStruct((T, 512), F32)
    vec = jax.ShapeDtypeStruct((1, 512), F32)
    mat = jax.ShapeDtypeStruct((4, 128, 128), F32)
    matspec = pl.BlockSpec((1, 128, 128), lambda j: (j, 0, 0))
    return pl.pallas_call(
        body, name=name, grid=(4,),
        out_shape=(big, big, jax.ShapeDtypeStruct((8, 512), F32), vec, mat, mat, vec, vec, vec),
        in_specs=_lru_specs(T) + [_colblock(T, 0), _colblock(T, 4)],
        out_specs=(_colblock(T, 0), _colblock(T, 0), _vecblock(8), _vecblock(), matspec, matspec, _vecblock(),
                   _vecblock(), _vecblock()),
        compiler_params=_cparams(("parallel",)),
    )(p, p, cw, cb, wa, wx, ba, bx, lam, hs, dy)


def _ffn_forward(h, gamma, wg, wu, wd, tag):
    xn = _rms_fwd(h, gamma, f"rms_fwd_{tag}")
    g, u, a = _ffn_up(xn, wg, wu, f"ffn_up_{tag}")
    if callable(wd):
        wd = wd(a)
    out = _mm([(a, wd)], "NN", f"ffn_down_{tag}", res=h, res_scale=0.5)
    return out, (h, xn, g, u, a)


def _after(w, tok):
    return w if tok is None else w + tok[0, 0].astype(w.dtype)


def _ffn_backward(saved, gamma, wg, wu, wd, dout, tok, tag, emit_one=None):
    h, xn, p, q, a = saved
    dout, dout16 = dout
    dwd = _mm([(a, dout16)], "TN", f"ffn_dwd_{tag}", res_scale=0.5, out_dtype=MXU, dep=tok)
    if emit_one is not None:
        tok = emit_one('wd', dwd)
    dg, du = _ffn_dact(dout16, wd, p, q, 0.5, f"ffn_dact_{tag}", dep=tok)
    dwg = _mm([(dg, xn)], "TN", f"ffn_dwg_{tag}", out_dtype=MXU)
    if emit_one is not None:
        tok = emit_one('wg', dwg)
    dwu = _mm([(du, xn)], "TN", f"ffn_dwu_{tag}", out_dtype=MXU, dep=tok if emit_one is not None else None)
    if emit_one is not None:
        tok = emit_one('wu', dwu)
    dxn = _mm([(dg, wg), (du, wu)], "NN", f"ffn_dxn_{tag}", dep=tok if emit_one is not None else None)
    dh, dh16, dgamma = _rms_bwd(h, gamma, dxn, dout, f"rms_bwd_{tag}")
    return (dh, dh16), dgamma, dwg, dwu, dwd


def _blockdiag(w, per):
    n, k, _ = w.shape
    out = jnp.zeros((n // per, per * k, per * k), w.dtype)
    for i in range(per):
        out = out.at[:, i * k:(i + 1) * k, i * k:(i + 1) * k].set(w[i::per])
    return out


def _blockdiag_grad(g, per, k):
    parts = [g[:, i * k:(i + 1) * k, i * k:(i + 1) * k] for i in range(per)]
    return jnp.stack(parts, axis=1).reshape(-1, k, k)


def _local_step(x, tgt, W, fetch, emit):
    fetch(0, x)
    seq, D = x.shape
    lr = N_META + seq
    nreal = -(-lr // CH)
    T = CH * (nreal + 2)
    if T > 640 and T % 640:
        T += 640 - T % 640
    lo, real_end = CH + N_META, CH + lr
    zf = lambda n: jnp.zeros((n, D), F32)
    h0 = jnp.concatenate([zf(CH), W['meta_tokens'], x, zf(T - real_end)], axis=0)
    tgt_p = jnp.concatenate([zf(lo), tgt, zf(T - real_end)], axis=0)
    row = lambda v: v.reshape(1, -1)
    G = {}

    h = h0
    saved = []
    for l in range(2):
        wd1 = W['ffn1_wd', l] if l else (lambda after: (fetch(1, after), W['ffn1_wd', 0])[1])
        h, s1 = _ffn_forward(h, row(W['ffn1_norm'][l]), W['ffn1_wg', l], W['ffn1_wu', l], wd1, f"a{l}")
        hm = h
        fetch(3 * l + 2, hm)
        xn = _rms_fwd(hm, row(W['mix_norm'][l]), f"rms_fwd_mix{l}")
        if l == 0:
            p = _mm([(xn, W['w_in_even'])], "NT", "in_even")
            wbd = _blockdiag(W['pool_w'][0], 4)[0]
            l0, l1 = row(W['hgrn_lb_logits'][0]), row(W['hgrn_lb_logits'][1])
            ya = _pool_fwd(p, wbd, W['pool_scale'], nreal, real_end, "pool_fwd")
            yb, states = _hgrn_fwd(p, l0, l1, W['hgrn_gnorm'], nreal, real_end, "hgrn_fwd")
            wo = W['w_out_even']
            h = _mm([(ya, wo[:256]), (yb, wo[256:])], "NN", "out_even", res=hm)
            sm = (hm, xn, p, wbd, l0, l1, ya, yb, states)
        else:
            p = _mm([(xn, W['w_in_odd'])], "NT", "in_odd")
            cw = jnp.pad(W['conv_w'][0], ((0, 1), (0, 0)))
            lw = jnp.pad(W['lru_conv_w'][0], ((0, 4), (0, 0)))
            wa, wx = _blockdiag(W['lru_wa'][0], 2), _blockdiag(W['lru_wx'][0], 2)
            yc = _conf_fwd(p, cw, W['conv_b'], W['conv_ln_g'], W['conv_ln_b'], nreal, real_end, "conf_fwd")
            yd, hs = _lru_fwd(p, lw, W['lru_conv_b'], wa, wx, W['lru_ba'], W['lru_bx'], W['lru_lambda'], nreal,
                              real_end, "lru_fwd")
            wo = W['w_out_odd']
            h = _mm([(yc, wo[:512]), (yd, wo[512:])], "NN", "out_odd", res=hm)
            sm = (hm, xn, p, cw, lw, wa, wx, yc, yd, hs)
        fetch(3 * l + 3, h)
        h, s2 = _ffn_forward(h, row(W['ffn2_norm'][l]), W['ffn2_wg', l], W['ffn2_wu', l], W['ffn2_wd', l], f"b{l}")
        if l == 0:
            fetch(4, h)
        saved.append((s1, sm, s2))

    loss8, dh, dh16, dfin = _loss_head(h, row(W['final_norm']), tgt_p, lo, real_end, "loss_head")
    dh = (dh, dh16)
    G['final_norm'] = dfin[0]

    per_layer = {k: [None, None] for k in ('ffn1_norm', 'mix_norm', 'ffn2_norm')}
    tok = None
    for l in (1, 0):
        s1, sm, s2 = saved[l]
        dh, dn, dwg, dwu, dwd = _ffn_backward(s2, row(W['ffn2_norm'][l]), W['ffn2_wg', l], W['ffn2_wu', l],
                                              W['ffn2_wd', l], dh, tok, f"b{l}")
        per_layer['ffn2_norm'][l] = dn[0]
        tok = emit(f"ffn2_{l}", [('ffn2_wg', l, dwg), ('ffn2_wu', l, dwu), ('ffn2_wd', l, dwd)])
        if l == 0:
            hm, xn, p, wbd, l0, l1, ya, yb, states = sm
            wo, wi = W['w_out_even'], W['w_in_even']
            dwo = jnp.concatenate([_mm([(ya, dh[1])], "TN", "dwo_even_a", out_dtype=MXU),
                                   _mm([(yb, dh[1])], "TN", "dwo_even_b", out_dtype=MXU)], axis=0)
            dy = _mm([(dh[1], wo)], "NT", "dy_even", dep=tok)
            dpp, dwbd, dsc = _pool_bwd(p, wbd, W['pool_scale'], dy, nreal, real_end, "pool_bwd")
            dq, df, di, dg, dl0, dl1, dgn = _hgrn_bwd(p, l0, l1, W['hgrn_gnorm'], states, dy, nreal, real_end,
                                                      "hgrn_bwd")
            G['pool_w'] = _blockdiag_grad(dwbd[None], 4, 64)[None]
            G['pool_scale'] = dsc
            G['hgrn_lb_logits'] = jnp.concatenate([dl0, dl1], axis=0)
            G['hgrn_gnorm'] = dgn
            parts = [dpp, dq, df, di, dg]
            offs = [0, 256, 1024, 1792, 2560, 3328]
            dwi = jnp.concatenate(
                [_mm([(dpart, xn)], "TN", f"dwi_even_{k}", out_dtype=MXU) for k, dpart in enumerate(parts)], axis=0)
            dxn = _mm([(dpart, wi[offs[k]:offs[k + 1]]) for k, dpart in enumerate(parts)], "NN", "dxn_even")
            tok = emit("even", [('w_in_even', None, dwi), ('w_out_even', None, dwo)])
        else:
            hm, xn, p, cw, lw, wa, wx, yc, yd, hs = sm
            wo, wi = W['w_out_odd'], W['w_in_odd']
            dwo = jnp.concatenate([_mm([(yc, dh[1])], "TN", "dwo_odd_c", out_dtype=MXU),
                                   _mm([(yd, dh[1])], "TN", "dwo_odd_d", out_dtype=MXU)], axis=0)
            dy = _mm([(dh[1], wo)], "NT", "dy_odd", dep=tok)
            da, db, dcw, dcb, dlg, dlb = _conf_bwd(p, cw, W['conv_b'], W['conv_ln_g'], W['conv_ln_b'], dy, nreal,
                                                   real_end, "conf_bwd")
            dx, dgate, dlw, dlcb, dwa, dwx, dba, dbx, dlam = _lru_bwd(
                p, lw, W['lru_conv_b'], wa, wx, W['lru_ba'], W['lru_bx'], W['lru_lambda'], hs, dy, nreal, real_end,
                "lru_bwd")
            G['conv_w'], G['conv_b'], G['conv_ln_g'], G['conv_ln_b'] = dcw[None, :CONV_W], dcb, dlg, dlb
            G['lru_conv_w'], G['lru_conv_b'] = dlw[None, :LRU_W], dlcb
            G['lru_wa'] = _blockdiag_grad(dwa, 2, 64)[None]
            G['lru_wx'] = _blockdiag_grad(dwx, 2, 64)[None]
            G['lru_ba'], G['lru_bx'], G['lru_lambda'] = dba, dbx, dlam
            parts = [da, db, dx, dgate]
            dwi = jnp.concatenate(
                [_mm([(dpart, xn)], "TN", f"dwi_odd_{k}", out_dtype=MXU) for k, dpart in enumerate(parts)], axis=0)
            dxn = _mm([(dpart, wi[512 * k:512 * (k + 1)]) for k, dpart in enumerate(parts)], "NN", "dxn_odd")
            tok = emit("odd", [('w_in_odd', None, dwi), ('w_out_odd', None, dwo)])
        dh, dh16, dn = _rms_bwd(hm, _after(row(W['mix_norm'][l]), tok), dxn, dh[0], f"rms_bwd_mix{l}")
        dh = (dh, dh16)
        per_layer['mix_norm'][l] = dn[0]
        one = None if l == 1 else (lambda sfx, g: emit(f"ffn1_0_{sfx}", [('ffn1_' + sfx, 0, g)]))
        dh, dn, dwg, dwu, dwd = _ffn_backward(s1, row(W['ffn1_norm'][l]), W['ffn1_wg', l], W['ffn1_wu', l],
                                              W['ffn1_wd', l], dh, None, f"a{l}", one)
        per_layer['ffn1_norm'][l] = dn[0]
        if l == 1:
            tok = emit("ffn1_1", [('ffn1_wg', l, dwg), ('ffn1_wu', l, dwu), ('ffn1_wd', l, dwd)])
    for k, v in per_layer.items():
        G[k] = jnp.stack(v, axis=0)
    G['meta_tokens'] = dh[0][CH:lo]
    return loss8[0, 0], dh[0][lo:real_end], G


def _pack(arrs):
    flat = jnp.concatenate([a.reshape(-1).astype(F32) for a in arrs])
    n = flat.shape[0]
    padded = -(-n // 1024) * 1024
    return jnp.pad(flat, (0, padded - n)).reshape(-1, 128)


def _unpack(packed, shapes):
    flat = packed.reshape(-1)
    out, off = [], 0
    for s in shapes:
        n = math.prod(s)
        out.append(flat[off:off + n].reshape(s))
        off += n
    return out


def _pack8(arrs):
    flat = jnp.concatenate([a.reshape(NDEV, -1).astype(F32) for a in arrs], axis=1)
    n = flat.shape[1]
    padded = -(-n // 1024) * 1024
    return jnp.pad(flat, ((0, 0), (0, padded - n))).reshape(NDEV, -1, 128)


def _unpack8(packed, shapes):
    flat = packed.reshape(NDEV, -1)
    out, off = [], 0
    for s in shapes:
        n = math.prod(s)
        out.append(flat[:, off:off + n].reshape((NDEV,) + tuple(s)))
        off += n
    return out


def _to_full(gathered, axis):
    s = gathered.shape[1:]
    return jnp.moveaxis(gathered, 0, axis).reshape(s[:axis] + (NDEV * s[axis],) + s[axis + 1:])


def _to_slots(full, axis):
    s = full.shape
    return jnp.moveaxis(full.reshape(s[:axis] + (NDEV, s[axis] // NDEV) + s[axis + 1:]), axis, 0)


def kernel(x, meta_tokens, ffn1_norm, ffn1_wg, ffn1_wu, ffn1_wd, mix_norm, ffn2_norm, ffn2_wg, ffn2_wu, ffn2_wd, w_in_even, pool_w, pool_scale, hgrn_lb_logits, hgrn_gnorm, w_out_even, w_in_odd, conv_w, conv_b, conv_ln_g, conv_ln_b, lru_conv_w, lru_conv_b, lru_wa, lru_ba, lru_wx, lru_bx, lru_lambda, w_out_odd, final_norm, loss_target, m_meta_tokens, m_ffn1_norm, m_ffn1_wg, m_ffn1_wu, m_ffn1_wd, m_mix_norm, m_ffn2_norm, m_ffn2_wg, m_ffn2_wu, m_ffn2_wd, m_w_in_even, m_pool_w, m_pool_scale, m_hgrn_lb_logits, m_hgrn_gnorm, m_w_out_even, m_w_in_odd, m_conv_w, m_conv_b, m_conv_ln_g, m_conv_ln_b, m_lru_conv_w, m_lru_conv_b, m_lru_wa, m_lru_ba, m_lru_wx, m_lru_bx, m_lru_lambda, m_w_out_odd, m_final_norm, v_meta_tokens, v_ffn1_norm, v_ffn1_wg, v_ffn1_wu, v_ffn1_wd, v_mix_norm, v_ffn2_norm, v_ffn2_wg, v_ffn2_wu, v_ffn2_wd, v_w_in_even, v_pool_w, v_pool_scale, v_hgrn_lb_logits, v_hgrn_gnorm, v_w_out_even, v_w_in_odd, v_conv_w, v_conv_b, v_conv_ln_g, v_conv_ln_b, v_lru_conv_w, v_lru_conv_b, v_lru_wa, v_lru_ba, v_lru_wx, v_lru_bx, v_lru_lambda, v_w_out_odd, v_final_norm):
    args = (meta_tokens, ffn1_norm, ffn1_wg, ffn1_wu, ffn1_wd, mix_norm, ffn2_norm, ffn2_wg, ffn2_wu, ffn2_wd, w_in_even, pool_w, pool_scale, hgrn_lb_logits, hgrn_gnorm, w_out_even, w_in_odd, conv_w, conv_b, conv_ln_g, conv_ln_b, lru_conv_w, lru_conv_b, lru_wa, lru_ba, lru_wx, lru_bx, lru_lambda, w_out_odd, final_norm)
    margs = (m_meta_tokens, m_ffn1_norm, m_ffn1_wg, m_ffn1_wu, m_ffn1_wd, m_mix_norm, m_ffn2_norm, m_ffn2_wg, m_ffn2_wu, m_ffn2_wd, m_w_in_even, m_pool_w, m_pool_scale, m_hgrn_lb_logits, m_hgrn_gnorm, m_w_out_even, m_w_in_odd, m_conv_w, m_conv_b, m_conv_ln_g, m_conv_ln_b, m_lru_conv_w, m_lru_conv_b, m_lru_wa, m_lru_ba, m_lru_wx, m_lru_bx, m_lru_lambda, m_w_out_odd, m_final_norm)
    vargs = (v_meta_tokens, v_ffn1_norm, v_ffn1_wg, v_ffn1_wu, v_ffn1_wd, v_mix_norm, v_ffn2_norm, v_ffn2_wg, v_ffn2_wu, v_ffn2_wd, v_w_in_even, v_pool_w, v_pool_scale, v_hgrn_lb_logits, v_hgrn_gnorm, v_w_out_even, v_w_in_odd, v_conv_w, v_conv_b, v_conv_ln_g, v_conv_ln_b, v_lru_conv_w, v_lru_conv_b, v_lru_wa, v_lru_ba, v_lru_wx, v_lru_bx, v_lru_lambda, v_w_out_odd, v_final_norm)
    Wl = dict(zip(W_NAMES, args))
    Ml = dict(zip(W_NAMES, margs))
    Vl = dict(zip(W_NAMES, vargs))

    small_shapes = [Wl[n].shape for n in SMALL_SHARDED]
    ffn = lambda p, l: [(p + s, l) for s in ('_wg', '_wu', '_wd')]
    mix = lambda p: [('w_in_' + p, None), ('w_out_' + p, None)]
    ggroups = [ffn('ffn1', 0)[:2], ffn('ffn1', 0)[2:], mix('even'), ffn('ffn2', 0), ffn('ffn1', 1), mix('odd'),
               ffn('ffn2', 1)]
    colsharded = lambda n: SHARD_AXIS[n] == 2

    def shard(n, l):
        w = Wl[n][0 if l is None else l].astype(MXU)
        return w.T if colsharded(n) else w

    srcs = [[shard(n, l) for n, l in g] for g in ggroups]
    srcs[0] = [_pack([Wl[n] for n in SMALL_SHARDED])] + srcs[0]
    handles, tok0 = _exchange_start(srcs[:2], True, "gather_start_first")
    rest, _ = _exchange_start([[_after(s, tok0) for s in g] for g in srcs[2:]], True, "gather_start")
    handles = handles + rest
    W = {n: Wl[n] for n in REPLICATED}

    def fetch(k, after):
        lands = _exchange_wait(handles[k], True, after, f"gather_wait_{k}")
        if k == 0:
            for n, g in zip(SMALL_SHARDED, _unpack8(lands[0], small_shapes)):
                W[n] = _to_full(g, SHARD_AXIS[n])
            lands = lands[1:]
        for (n, l), g in zip(ggroups[k], lands):
            W[n if l is None else (n, l)] = g.reshape(-1, g.shape[-1])

    pending = []

    def emit(tag, grads):
        slots = [g.astype(MXU).reshape(NDEV, -1, g.shape[-1]) for _, _, g in grads]
        hs, token = _exchange_start([slots], False, f"scatter_start_{tag}")
        pending.append((tag, hs[0], [(n, l) for n, l, _ in grads]))
        return token

    loss_part, grad_x, G = _local_step(x[0], loss_target[0], W, fetch, emit)

    small_slots = [_to_slots(G[n].astype(F32), SHARD_AXIS[n]) for n in SMALL_SHARDED]
    send = [_pack8(small_slots), _pack([G[n] for n in REPLICATED]),
            jnp.broadcast_to(loss_part, (8, 128))]
    got = _exchange(send, [False, True, True], "scatter_small")
    loss = jnp.sum(got[2][:, 0, 0])
    recv = {}
    for tag, handle, keys in pending:
        for key, r in zip(keys, _exchange_wait(handle, False, got[0], f"scatter_wait_{tag}")):
            recv[key] = r

    outs = {}
    for n in BIG:
        shp = Wl[n].shape
        C = shp[-1]
        rs = [recv[n, None]] if shp[0] == 1 else [recv[n, l] for l in range(shp[0])]
        if colsharded(n):
            rs = [_sum8(r, f"sum8_{n}_{j}").T for j, r in enumerate(rs)]
        res = _adamw(rs, Wl[n].reshape(-1, C), Ml[n].reshape(-1, C), Vl[n].reshape(-1, C), f"adamw_{n}",
                     summed=colsharded(n))
        outs[n] = [o.reshape(shp) for o in res]
    for names, r, tag in ((SMALL_SHARDED, got[0], "small"), (REPLICATED, got[1], "repl")):
        shapes = [Wl[n].shape for n in names]
        res = _adamw([r], _pack([Wl[n] for n in names]), _pack([Ml[n] for n in names]),
                     _pack([Vl[n] for n in names]), f"adamw_{tag}")
        unp = [_unpack(o, shapes) for o in res]
        for k, n in enumerate(names):
            outs[n] = [unp[j][k] for j in range(4)]

    result = [loss, grad_x[None]]
    for j in range(4):
        result += [outs[n][j] for n in W_NAMES]
    return tuple(result)
```

```python
import functools
import math

import jax
import jax.numpy as jnp
from jax import lax
from jax.experimental import pallas as pl
from jax.experimental.pallas import tpu as pltpu

F32 = jnp.float32
MXU = jnp.bfloat16
EPS = 1e-6
CH = 128
HG = 128
N_META = 16
CONV_W = 31
LRU_W = 4
LRU_C = 8.0
VMEM_LIMIT = 48 * 2 ** 20
MM_VMEM_BUDGET = 36 * 2 ** 20
ADAM_LR, ADAM_B1, ADAM_B2, ADAM_EPS, ADAM_WD, ADAM_STEP = 0.001, 0.9, 0.999, 1e-08, 0.01, 10
MESH_AXES = ("x", "y", "c")
NDEV = 8

W_NAMES = ['meta_tokens', 'ffn1_norm', 'ffn1_wg', 'ffn1_wu', 'ffn1_wd', 'mix_norm', 'ffn2_norm', 'ffn2_wg', 'ffn2_wu',
           'ffn2_wd', 'w_in_even', 'pool_w', 'pool_scale', 'hgrn_lb_logits', 'hgrn_gnorm', 'w_out_even', 'w_in_odd',
           'conv_w', 'conv_b', 'conv_ln_g', 'conv_ln_b', 'lru_conv_w', 'lru_conv_b', 'lru_wa', 'lru_ba', 'lru_wx',
           'lru_bx', 'lru_lambda', 'w_out_odd', 'final_norm']
SHARD_AXIS = {'meta_tokens': 1, 'ffn1_wg': 2, 'ffn1_wu': 2, 'ffn1_wd': 1, 'ffn2_wg': 2, 'ffn2_wu': 2, 'ffn2_wd': 1,
              'w_in_even': 2, 'w_out_even': 1, 'w_in_odd': 2, 'conv_w': 2, 'conv_b': 1, 'conv_ln_g': 1,
              'conv_ln_b': 1, 'lru_conv_w': 2, 'lru_conv_b': 1, 'lru_ba': 1, 'lru_bx': 1, 'lru_lambda': 1,
              'w_out_odd': 1}
BIG = ['ffn1_wg', 'ffn1_wu', 'ffn1_wd', 'ffn2_wg', 'ffn2_wu', 'ffn2_wd', 'w_in_even', 'w_out_even', 'w_in_odd',
       'w_out_odd']
SMALL_SHARDED = [n for n in W_NAMES if n in SHARD_AXIS and n not in BIG]
REPLICATED = [n for n in W_NAMES if n not in SHARD_AXIS]


def _cparams(sem=None, vmem=VMEM_LIMIT):
    return pltpu.CompilerParams(dimension_semantics=sem, vmem_limit_bytes=vmem)


def _tiles(n):
    return [c for c in range(128, n + 1, 128) if n % c == 0] or [n]


def _tile(n, cap=1024):
    return max([c for c in _tiles(n) if c <= cap], default=_tiles(n)[0])


def _rowtile(n):
    for c in (256, 352, 128, 64, 32, 16, 8):
        if n % c == 0:
            return c
    return n


def _copies(srcs, lands, bcast, ssem, rsem, lsem):
    x, y, c = lax.axis_index("x"), lax.axis_index("y"), lax.axis_index("c")
    me = 4 * x + 2 * y + c
    locs, sends, recvs = [], [], []
    for a in range(len(srcs)):
        locs.append(pltpu.make_async_copy(srcs[a] if bcast[a] else srcs[a].at[me], lands[a].at[me], lsem.at[a]))
        for m in range(1, NDEV):
            px = 1 - x if (m >> 2) & 1 else x
            py = 1 - y if (m >> 1) & 1 else y
            pc = 1 - c if m & 1 else c
            peer = 4 * px + 2 * py + pc
            src = srcs[a] if bcast[a] else srcs[a].at[peer]
            k = a * NDEV + m
            for dst, out in ((lands[a].at[me], sends), (lands[a].at[peer], recvs)):
                out.append(pltpu.make_async_remote_copy(src_ref=src, dst_ref=dst, send_sem=ssem.at[k],
                                                        recv_sem=rsem.at[k], device_id=(px, py, pc),
                                                        device_id_type=pl.DeviceIdType.MESH))
    return locs, sends, recvs


def _land_shape(arr, bc):
    return (NDEV,) + tuple(arr.shape if bc else arr.shape[1:])


def _exchange(arrays, bcast, name):
    n = len(arrays)

    def body(*refs):
        locs, sends, recvs = _copies(refs[:n], refs[n:2 * n], bcast, refs[2 * n], refs[2 * n + 1], refs[2 * n + 2])
        for d in locs + sends:
            d.start()
        for r in recvs:
            r.wait_recv()
        for s in sends:
            s.wait_send()
        for loc in locs:
            loc.wait()

    out_shape = tuple(jax.ShapeDtypeStruct(_land_shape(arr, bc), arr.dtype) for arr, bc in zip(arrays, bcast))
    any_spec = pl.BlockSpec(memory_space=pl.ANY)
    return pl.pallas_call(
        body, name=name, out_shape=out_shape, in_specs=[any_spec] * n, out_specs=tuple([any_spec] * n),
        scratch_shapes=[pltpu.SemaphoreType.DMA((n * NDEV,)), pltpu.SemaphoreType.DMA((n * NDEV,)),
                        pltpu.SemaphoreType.DMA((n,))],
    )(*arrays)


_HBM = pl.BlockSpec(memory_space=pltpu.HBM)
_SEM = pl.BlockSpec(memory_space=pltpu.SEMAPHORE)
_EFFECT = pltpu.SideEffectType.DATAFLOW_SIDE_EFFECTING


def _exchange_start(groups, bcast, name):
    sizes = [len(g) for g in groups]
    srcs = [a for g in groups for a in g]
    n, ng = len(srcs), len(groups)
    lands = [lax.empty(_land_shape(a, bcast), a.dtype) for a in srcs]

    def body(*refs):
        off = 0
        for gi, sz in enumerate(sizes):
            sem = refs[2 * n + 3 * gi:2 * n + 3 * gi + 3]
            locs, sends, _ = _copies(refs[off:off + sz], refs[n + off:n + off + sz], [bcast] * sz, *sem)
            for d in locs + sends:
                d.start()
            off += sz
        refs[-1][...] = jnp.zeros((8, 128), F32)

    sems = []
    for sz in sizes:
        sems += [pltpu.SemaphoreType.DMA((sz * NDEV,)), pltpu.SemaphoreType.DMA((sz * NDEV,)),
                 pltpu.SemaphoreType.DMA((sz,))]
    thru = [pltpu.HBM(a.shape, a.dtype) for a in srcs + lands]
    outs = pl.pallas_call(
        body, name=name, out_shape=tuple(sems + thru + [jax.ShapeDtypeStruct((8, 128), F32)]),
        in_specs=[_HBM] * (2 * n),
        out_specs=tuple([_SEM] * (3 * ng) + [_HBM] * (2 * n) + [pl.BlockSpec(memory_space=pltpu.VMEM)]),
        input_output_aliases={i: 3 * ng + i for i in range(2 * n)},
        compiler_params=pltpu.CompilerParams(has_side_effects=_EFFECT),
    )(*[pltpu.with_memory_space_constraint(a, pltpu.HBM) for a in srcs + lands])
    handles, off = [], 0
    for gi, sz in enumerate(sizes):
        handles.append((outs[3 * gi:3 * gi + 3], outs[3 * ng + off:3 * ng + off + sz],
                        outs[3 * ng + n + off:3 * ng + n + off + sz]))
        off += sz
    return handles, outs[-1]


def _exchange_wait(handle, bcast, after, name):
    sems, srcs, lands = handle
    n = len(srcs)

    def body(*refs):
        locs, sends, recvs = _copies(refs[:n], refs[n:2 * n], [bcast] * n, *refs[2 * n:2 * n + 3])
        for r in recvs:
            r.wait_recv()
        for s in sends:
            s.wait_send()
        for loc in locs:
            loc.wait()

    outs = pl.pallas_call(
        body, name=name, out_shape=tuple(pltpu.HBM(a.shape, a.dtype) for a in list(srcs) + list(lands)),
        in_specs=[_HBM] * (2 * n) + [_SEM] * 3 + [pl.BlockSpec(memory_space=pl.ANY)],
        out_specs=tuple([_HBM] * (2 * n)), input_output_aliases={i: i for i in range(2 * n)},
        compiler_params=pltpu.CompilerParams(has_side_effects=_EFFECT),
    )(*srcs, *lands, *sems, after)
    return outs[n:]


def _sum8(recv, name):
    _, R, C = recv.shape
    br = _rowtile(R)

    def body(r_ref, o_ref):
        s = r_ref[0].astype(F32)
        for k in range(1, NDEV):
            s = s + r_ref[k].astype(F32)
        o_ref[...] = s

    return pl.pallas_call(
        body, name=name, grid=(R // br,), out_shape=jax.ShapeDtypeStruct((R, C), F32),
        in_specs=[pl.BlockSpec((NDEV, br, C), lambda i: (0, i, 0))], out_specs=pl.BlockSpec((br, C), lambda i: (i, 0)),
        compiler_params=_cparams(("parallel",)),
    )(recv)


def _adamw(recvs, w, m, v, name, summed=False):
    R, C = w.shape
    nr = len(recvs)
    br = _rowtile(R // nr)
    nb0 = R // nr // br

    def body(*refs):
        w_ref, m_ref, v_ref, g_o, d_o, m_o, v_o = refs[nr:]
        g = None
        for j in range(nr):
            if summed:
                s = refs[j][...]
            else:
                s = refs[j][0].astype(F32)
                for k in range(1, NDEV):
                    s = s + refs[j][k].astype(F32)
            g = s if g is None else jnp.where(pl.program_id(0) >= j * nb0, s, g)
        mn = ADAM_B1 * m_ref[...] + (1.0 - ADAM_B1) * g
        vn = ADAM_B2 * v_ref[...] + (1.0 - ADAM_B2) * (g * g)
        m_hat = mn / (1.0 - ADAM_B1 ** ADAM_STEP)
        v_hat = vn / (1.0 - ADAM_B2 ** ADAM_STEP)
        g_o[...] = g
        d_o[...] = -ADAM_LR * (m_hat / (jnp.sqrt(v_hat) + ADAM_EPS) + ADAM_WD * w_ref[...])
        m_o[...] = mn
        v_o[...] = vn

    def rspec(j):
        if summed:
            return pl.BlockSpec((br, C), lambda i: (jnp.clip(i - j * nb0, 0, nb0 - 1), 0))
        return pl.BlockSpec((NDEV, br, C), lambda i: (0, jnp.clip(i - j * nb0, 0, nb0 - 1), 0))

    blk = pl.BlockSpec((br, C), lambda i: (i, 0))
    sds = jax.ShapeDtypeStruct((R, C), F32)
    return pl.pallas_call(
        body, name=name, grid=(R // br,), out_shape=(sds, sds, sds, sds),
        in_specs=[rspec(j) for j in range(nr)] + [blk, blk, blk], out_specs=(blk, blk, blk, blk),
        compiler_params=_cparams(("arbitrary",)),
    )(*recvs, w, m, v)


_DIMS = {"NN": ((1,), (0,)), "NT": ((1,), (1,)), "TN": ((0,), (0,))}


def _dot(a, b, mode="NN"):
    return lax.dot_general(a.astype(MXU), b.astype(MXU), (_DIMS[mode], ((), ())), preferred_element_type=F32)


def _dotf(a, b, mode="NN"):
    return lax.dot_general(a, b, (_DIMS[mode], ((), ())), precision=lax.Precision.HIGH,
                           preferred_element_type=F32)


def _mm(pairs, mode, name, res=None, res_scale=1.0, out_dtype=F32, dep=None):
    a0, b0 = pairs[0]
    M = a0.shape[1] if mode == "TN" else a0.shape[0]
    N = b0.shape[0] if mode == "NT" else b0.shape[1]
    npairs = len(pairs)

    def vmem_bytes(tm, tn):
        total = tm * tn * 4 * (2 + (2 if res is not None else 0) + 2)
        for a, b in pairs:
            ka = a.shape[0] if mode == "TN" else a.shape[1]
            kb = b.shape[1] if mode == "NT" else b.shape[0]
            for k, t, arr in ((ka, tm, a), (kb, tn, b)):
                total += k * t * (2 * arr.dtype.itemsize + (2 if arr.dtype == F32 else 0))
        return total

    tm, tn = max(((tm, tn) for tm in _tiles(M) for tn in _tiles(N) if vmem_bytes(tm, tn) <= MM_VMEM_BUDGET),
                 key=lambda t: (t[0] * t[1], t[1]), default=(_tiles(M)[0], _tiles(N)[0]))

    def body(*refs):
        acc = None
        for p in range(npairs):
            d = _dot(refs[2 * p][...], refs[2 * p + 1][...], mode)
            acc = d if acc is None else acc + d
        if res_scale != 1.0:
            acc = res_scale * acc
        if res is not None:
            acc = refs[2 * npairs][...] + acc
        refs[-1][...] = acc.astype(out_dtype)

    in_specs, args = [], []
    for a, b in pairs:
        if mode == "TN":
            in_specs.append(pl.BlockSpec((a.shape[0], tm), lambda i, j: (0, i)))
        else:
            in_specs.append(pl.BlockSpec((tm, a.shape[1]), lambda i, j: (i, 0)))
        if mode == "NT":
            in_specs.append(pl.BlockSpec((tn, b.shape[1]), lambda i, j: (j, 0)))
        else:
            in_specs.append(pl.BlockSpec((b.shape[0], tn), lambda i, j: (0, j)))
        args += [a, b]
    if res is not None:
        in_specs.append(pl.BlockSpec((tm, tn), lambda i, j: (i, j)))
        args.append(res)
    if dep is not None:
        in_specs.append(pl.BlockSpec(memory_space=pl.ANY))
        args.append(dep)
    return pl.pallas_call(
        body, name=name, grid=(M // tm, N // tn), out_shape=jax.ShapeDtypeStruct((M, N), out_dtype),
        in_specs=in_specs, out_specs=pl.BlockSpec((tm, tn), lambda i, j: (i, j)),
        compiler_params=_cparams(("parallel", "parallel")),
    )(*args)


def _rms_fwd(h, gamma, name):
    T, D = h.shape
    tm = _tile(T)

    def body(h_ref, g_ref, o_ref):
        x = h_ref[...]
        r = lax.rsqrt(jnp.mean(x * x, axis=-1, keepdims=True) + EPS)
        o_ref[...] = (x * r * g_ref[...]).astype(MXU)

    return pl.pallas_call(
        body, name=name, grid=(T // tm,), out_shape=jax.ShapeDtypeStruct((T, D), MXU),
        in_specs=[pl.BlockSpec((tm, D), lambda i: (i, 0)), pl.BlockSpec((1, D), lambda i: (0, 0))],
        out_specs=pl.BlockSpec((tm, D), lambda i: (i, 0)), compiler_params=_cparams(("parallel",)),
    )(h, gamma)


def _rms_bwd_math(x, gamma, dy):
    r = lax.rsqrt(jnp.mean(x * x, axis=-1, keepdims=True) + EPS)
    z = dy * gamma
    dx = r * z - x * (r * r * r) * jnp.mean(z * x, axis=-1, keepdims=True)
    dgamma = jnp.sum(dy * x * r, axis=0, keepdims=True)
    return dx, dgamma


def _rms_bwd(h, gamma, dxn, dres, name):
    T, D = h.shape
    tm = _tile(T)

    def body(h_ref, g_ref, dxn_ref, dres_ref, dh_ref, dh16_ref, dg_ref):
        dx, dgamma = _rms_bwd_math(h_ref[...], g_ref[...], dxn_ref[...])
        dh = dres_ref[...] + dx
        dh_ref[...] = dh
        dh16_ref[...] = dh.astype(MXU)

        @pl.when(pl.program_id(0) == 0)
        def _():
            dg_ref[...] = jnp.zeros_like(dg_ref)

        dg_ref[...] += dgamma

    row = pl.BlockSpec((tm, D), lambda i: (i, 0))
    vec = pl.BlockSpec((1, D), lambda i: (0, 0))
    return pl.pallas_call(
        body, name=name, grid=(T // tm,),
        out_shape=(jax.ShapeDtypeStruct((T, D), F32), jax.ShapeDtypeStruct((T, D), MXU),
                   jax.ShapeDtypeStruct((1, D), F32)),
        in_specs=[row, vec, row, row], out_specs=(row, row, vec), compiler_params=_cparams(("arbitrary",)),
    )(h, gamma, dxn, dres)


def _loss_head(h, gamma, tgt, lo, hi, name):
    T, D = h.shape
    tm = _tile(T)

    def body(h_ref, g_ref, t_ref, loss_ref, dh_ref, dh16_ref, dg_ref):
        i = pl.program_id(0)
        x = h_ref[...]
        r = lax.rsqrt(jnp.mean(x * x, axis=-1, keepdims=True) + EPS)
        y = x * r * g_ref[...]
        rows = i * tm + lax.broadcasted_iota(jnp.int32, (tm, 1), 0)
        valid = jnp.logical_and(rows >= lo, rows < hi)
        diff = jnp.where(valid, y - t_ref[...], 0.0)
        part = 0.5 * jnp.sum(jnp.sum(diff * diff, axis=-1, keepdims=True) / D, axis=0, keepdims=True)
        dx, dgamma = _rms_bwd_math(x, g_ref[...], diff / D)
        dh_ref[...] = dx
        dh16_ref[...] = dx.astype(MXU)

        @pl.when(i == 0)
        def _():
            dg_ref[...] = jnp.zeros_like(dg_ref)
            loss_ref[...] = jnp.zeros_like(loss_ref)

        dg_ref[...] += dgamma
        loss_ref[...] += jnp.broadcast_to(part, loss_ref.shape)

    row = pl.BlockSpec((tm, D), lambda i: (i, 0))
    vec = pl.BlockSpec((1, D), lambda i: (0, 0))
    lsp = pl.BlockSpec((8, 128), lambda i: (0, 0))
    return pl.pallas_call(
        body, name=name, grid=(T // tm,),
        out_shape=(jax.ShapeDtypeStruct((8, 128), F32), jax.ShapeDtypeStruct((T, D), F32),
                   jax.ShapeDtypeStruct((T, D), MXU), jax.ShapeDtypeStruct((1, D), F32)),
        in_specs=[row, vec, row], out_specs=(lsp, row, row, vec), compiler_params=_cparams(("arbitrary",)),
    )(h, gamma, tgt)


def _ffn_tiles(T, Fd):
    return (448 if T % 448 == 0 else _tile(T)), max(c for c in _tiles(Fd) if c <= 1536)


def _ffn_up(xn, wg, wu, name):
    T, D = xn.shape
    Fd = wg.shape[0]
    tm, tn = _ffn_tiles(T, Fd)

    def body(x_ref, wg_ref, wu_ref, p_ref, q_ref, a_ref):
        x = x_ref[...]
        g = _dot(x, wg_ref[...], "NT")
        u = _dot(x, wu_ref[...], "NT")
        sg = jax.nn.sigmoid(g)
        q = g * sg
        p_ref[...] = (u * (sg + q * (1.0 - sg))).astype(MXU)
        q_ref[...] = q.astype(MXU)
        a_ref[...] = (q * u).astype(MXU)

    wsp = pl.BlockSpec((tn, D), lambda i, j: (j, 0))
    osp = pl.BlockSpec((tm, tn), lambda i, j: (i, j))
    sds = jax.ShapeDtypeStruct((T, Fd), MXU)
    return pl.pallas_call(
        body, name=name, grid=(T // tm, Fd // tn), out_shape=(sds, sds, sds),
        in_specs=[pl.BlockSpec((tm, D), lambda i, j: (i, 0)), wsp, wsp], out_specs=(osp, osp, osp),
        compiler_params=_cparams(("parallel", "parallel")),
    )(xn, wg, wu)


def _ffn_dact(dy, wd, p, q, scale, name, dep=None):
    T, D = dy.shape
    Fd = wd.shape[0]
    tm, tn = _ffn_tiles(T, Fd)

    def body(dy_ref, wd_ref, p_ref, q_ref, *rest):
        dg_ref, du_ref = rest[-2:]
        da = scale * _dot(dy_ref[...], wd_ref[...], "NT")
        dg_ref[...] = (da * p_ref[...].astype(F32)).astype(MXU)
        du_ref[...] = (da * q_ref[...].astype(F32)).astype(MXU)

    osp = pl.BlockSpec((tm, tn), lambda i, j: (i, j))
    sds = jax.ShapeDtypeStruct((T, Fd), MXU)
    return pl.pallas_call(
        body, name=name, grid=(T // tm, Fd // tn), out_shape=(sds, sds),
        in_specs=[pl.BlockSpec((tm, D), lambda i, j: (i, 0)), pl.BlockSpec((tn, D), lambda i, j: (j, 0)), osp, osp]
        + ([] if dep is None else [pl.BlockSpec(memory_space=pl.ANY)]),
        out_specs=(osp, osp), compiler_params=_cparams(("parallel", "parallel")),
    )(dy, wd, p, q, *([] if dep is None else [dep]))


def _down(v, s):
    return v if s == 0 else pltpu.roll(v, s, 0)


def _up(v, s):
    return v if s == 0 else pltpu.roll(v, v.shape[0] - s, 0)


def _rows(n):
    return lax.broadcasted_iota(jnp.int32, (n, 1), 0)


def _zero_pad_rows(ref, lo_end, hi_start, T):
    ref[pl.ds(0, lo_end), :] = jnp.zeros((lo_end, ref.shape[1]), ref.dtype)
    if T > hi_start:
        ref[pl.ds(hi_start, T - hi_start), :] = jnp.zeros((T - hi_start, ref.shape[1]), ref.dtype)


def _colblock(T, off):
    return pl.BlockSpec((T, 128), lambda j: (0, off + j))


def _vecblock(rows=1):
    return pl.BlockSpec((rows, 128), lambda j: (0, j))


def _pool_lane_consts(n):
    lane = lax.broadcasted_iota(jnp.int32, (n, 256), 1)
    win = jnp.where(lane < 64, 2.0, jnp.where(lane < 128, 4.0, jnp.where(lane < 192, 8.0, 16.0)))
    return lane, win


def _pool_select(lane, s2, s4, s8, s16):
    return jnp.where(lane < 64, s2, jnp.where(lane < 128, s4, jnp.where(lane < 192, s8, s16)))


def _pool_mixed(xh, start):
    s2 = xh + _down(xh, 1)
    s4 = s2 + _down(s2, 2)
    s8 = s4 + _down(s4, 4)
    s16 = s8 + _down(s8, 8)
    n = xh.shape[0] - 16
    lane, _ = _pool_lane_consts(n + 16)
    _, win = _pool_lane_consts(n)
    t1 = (start - CH + 1 + _rows(n)).astype(F32)
    cnt = jnp.minimum(jnp.maximum(t1, 1.0), win)
    return _pool_select(lane, s2, s4, s8, s16)[16:] / cnt - xh[16:]


def _pool_fwd(p, wbd, scale, nreal, real_end, name):
    T = p.shape[0]

    def body(p_ref, w_ref, s_ref, y_ref):
        _zero_pad_rows(y_ref, CH, CH * (1 + nreal), T)

        def chunk(c, carry):
            start = pl.multiple_of(c * CH, CH)
            mixed = _pool_mixed(p_ref[pl.ds(start - 16, CH + 16), :], start)
            y = _dot(mixed, w_ref[...]) * s_ref[...]
            y_ref[pl.ds(start, CH), :] = jnp.where(start + _rows(CH) < real_end, y, 0.0)
            return carry

        _pairs_loop(nreal, lambda c, carry: chunk(c + 1, carry), 0)

    return pl.pallas_call(
        body, name=name, grid=(1,), out_shape=jax.ShapeDtypeStruct((T, 256), F32),
        in_specs=[pl.BlockSpec((T, 256), lambda j: (0, 0)), pl.BlockSpec((256, 256), lambda j: (0, 0)),
                  pl.BlockSpec((1, 256), lambda j: (0, 0))],
        out_specs=pl.BlockSpec((T, 256), lambda j: (0, 0)), compiler_params=_cparams(("arbitrary",)),
    )(p, wbd, scale)


def _pool_bwd(p, wbd, scale, dy, nreal, real_end, name):
    T = p.shape[0]

    def body(p_ref, w_ref, s_ref, dy_ref, dp_ref, dw_ref, ds_ref):
        _zero_pad_rows(dp_ref, CH, CH * (1 + nreal), T)
        dw_ref[...] = jnp.zeros_like(dw_ref)
        ds_ref[...] = jnp.zeros_like(ds_ref)

        def chunk(c, carry):
            start = pl.multiple_of(c * CH, CH)
            mixed = _pool_mixed(p_ref[pl.ds(start - 16, CH + 16), :], start)
            ypre = _dot(mixed, w_ref[...])
            n = CH + 16
            dye = jnp.where(start + _rows(n) < real_end, dy_ref[pl.ds(start, n), :], 0.0)
            dys = dye * s_ref[...]
            ds_ref[...] += jnp.sum(dye[:CH] * ypre, axis=0, keepdims=True)
            dw_ref[...] += _dot(mixed, dys[:CH], "TN")
            dmix = _dot(dys, w_ref[...], "NT")
            lane, win = _pool_lane_consts(n)
            t1 = (start - CH + 1 + _rows(n)).astype(F32)
            z = dmix / jnp.minimum(jnp.maximum(t1, 1.0), win)
            r2 = z + _up(z, 1)
            r4 = r2 + _up(r2, 2)
            r8 = r4 + _up(r4, 4)
            r16 = r8 + _up(r8, 8)
            dp_ref[pl.ds(start, CH), :] = (_pool_select(lane, r2, r4, r8, r16) - dmix)[:CH]
            return carry

        _pairs_loop(nreal, lambda c, carry: chunk(c + 1, carry), 0)

    full = lambda r, c: pl.BlockSpec((r, c), lambda j: (0, 0))
    return pl.pallas_call(
        body, name=name, grid=(1,),
        out_shape=(jax.ShapeDtypeStruct((T, 256), F32), jax.ShapeDtypeStruct((256, 256), F32),
                   jax.ShapeDtypeStruct((1, 256), F32)),
        in_specs=[full(T, 256), full(256, 256), full(1, 256), full(T, 256)],
        out_specs=(full(T, 256), full(256, 256), full(1, 256)), compiler_params=_cparams(("arbitrary",)),
    )(p, wbd, scale, dy)


def _hgrn_chunk(St, qr, fr, ir, gr, l0, l1, gn):
    rows = lax.broadcasted_iota(jnp.int32, (HG, HG), 0)
    cols = lax.broadcasted_iota(jnp.int32, (HG, HG), 1)
    causal = rows >= cols
    ltri = causal.astype(F32)
    lb = jax.nn.sigmoid(l0 - l1)
    sg = jax.nn.sigmoid(fr)
    logf = jnp.log(lb + (1.0 - lb) * sg)
    kk = (1.0 - lb) * (1.0 - sg)
    q = qr * jax.nn.sigmoid(qr)
    b = jnp.dot(ltri, logf, precision=lax.Precision.HIGH, preferred_element_type=F32)
    bl = jnp.sum(logf, axis=0, keepdims=True)
    bm = jnp.sum(jnp.where(_rows(HG) <= HG // 2, logf, 0.0), axis=0, keepdims=True)
    o = _dotf(q * jnp.exp(b), St, "NT")
    A = _dotf(q * jnp.exp(b - bm), kk * jnp.exp(bm - b), "NT")
    o = o + _dotf(jnp.where(causal, A, 0.0), ir)
    St_new = St * jnp.exp(bl) + _dotf(ir, kk * jnp.exp(bl - b), "TN")
    on = o * lax.rsqrt(jnp.mean(o * o, axis=-1, keepdims=True) + EPS) * gn
    return St_new, on * (gr * jax.nn.sigmoid(gr))


def _pairs_loop(n, step, init):
    u = 3 if n % 3 == 0 else 2 if n % 2 == 0 else 1

    def body(i, carry):
        for j in range(u):
            carry = step(u * i + j, carry)
        return carry

    return lax.fori_loop(0, n // u, body, init)


def _hgrn_specs(T):
    return [_colblock(T, 2), _colblock(T, 8), _colblock(T, 14), _colblock(T, 20), _vecblock(), _vecblock(),
            pl.BlockSpec((1, 128), lambda j: (0, 0))]


def _hgrn_fwd(p, l0, l1, gn, nreal, real_end, name):
    T = p.shape[0]
    nch = nreal * (CH // HG)

    def body(q_ref, f_ref, i_ref, g_ref, l0_ref, l1_ref, gn_ref, y_ref, s_ref):
        _zero_pad_rows(y_ref, CH, CH * (1 + nreal), T)

        def chunk(c, St):
            start = pl.multiple_of(CH + c * HG, HG)
            sl = pl.ds(start, HG)
            s_ref[0, c] = St
            St_new, y = _hgrn_chunk(St, q_ref[sl, :], f_ref[sl, :], i_ref[sl, :], g_ref[sl, :], l0_ref[...],
                                    l1_ref[...], gn_ref[...])
            y_ref[sl, :] = jnp.where(start + _rows(HG) < real_end, y, 0.0)
            return St_new

        _pairs_loop(nch, chunk, jnp.zeros((128, 128), F32))

    return pl.pallas_call(
        body, name=name, grid=(6,),
        out_shape=(jax.ShapeDtypeStruct((T, 768), F32), jax.ShapeDtypeStruct((6, nch, 128, 128), F32)),
        in_specs=_hgrn_specs(T),
        out_specs=(_colblock(T, 0), pl.BlockSpec((1, nch, 128, 128), lambda j: (j, 0, 0, 0))),
        compiler_params=_cparams(("parallel",)),
    )(p, p, p, p, l0, l1, gn)


def _hgrn_bwd(p, l0, l1, gn, states, dy, nreal, real_end, name):
    T = p.shape[0]
    nch = nreal * (CH // HG)

    def body(q_ref, f_ref, i_ref, g_ref, l0_ref, l1_ref, gn_ref, s_ref, dy_ref,
             dq_ref, df_ref, di_ref, dg_ref, dl0_ref, dl1_ref, dgn_ref):
        for r in (dq_ref, df_ref, di_ref, dg_ref):
            _zero_pad_rows(r, CH, CH * (1 + nreal), T)

        def chunk(k, carry):
            dSt, a0, a1, agn = carry
            c = nch - 1 - k
            start = pl.multiple_of(CH + c * HG, HG)
            sl = pl.ds(start, HG)
            _, vjp = jax.vjp(_hgrn_chunk, s_ref[0, c], q_ref[sl, :], f_ref[sl, :], i_ref[sl, :], g_ref[sl, :],
                             l0_ref[...], l1_ref[...], gn_ref[...])
            dyc = jnp.where(start + _rows(HG) < real_end, dy_ref[sl, :], 0.0)
            dS, dq, df, di, dg, d0, d1, dgn = vjp((dSt, dyc))
            dq_ref[sl, :] = dq
            df_ref[sl, :] = df
            di_ref[sl, :] = di
            dg_ref[sl, :] = dg
            return dS, a0 + d0, a1 + d1, agn + dgn

        z = jnp.zeros((1, 128), F32)
        _, a0, a1, agn = _pairs_loop(nch, chunk, (jnp.zeros((128, 128), F32), z, z, z))
        dl0_ref[...] = a0
        dl1_ref[...] = a1

        @pl.when(pl.program_id(0) == 0)
        def _():
            dgn_ref[...] = jnp.zeros_like(dgn_ref)

        dgn_ref[...] += agn

    big = jax.ShapeDtypeStruct((T, 768), F32)
    vec = jax.ShapeDtypeStruct((1, 768), F32)
    return pl.pallas_call(
        body, name=name, grid=(6,),
        out_shape=(big, big, big, big, vec, vec, jax.ShapeDtypeStruct((1, 128), F32)),
        in_specs=_hgrn_specs(T) + [pl.BlockSpec((1, nch, 128, 128), lambda j: (j, 0, 0, 0)), _colblock(T, 2)],
        out_specs=(_colblock(T, 0), _colblock(T, 0), _colblock(T, 0), _colblock(T, 0), _vecblock(), _vecblock(),
                   pl.BlockSpec((1, 128), lambda j: (0, 0))),
        compiler_params=_cparams(("arbitrary",), 60 * 2 ** 20),
    )(p, p, p, p, l0, l1, gn, states, dy)


def _glu(a, b):
    return a * jax.nn.sigmoid(b)


def _conv_post(cv, ln_g, ln_b):
    mu = jnp.mean(cv, axis=-1, keepdims=True)
    d = cv - mu
    var = jnp.mean(d * d, axis=-1, keepdims=True)
    un = d * lax.rsqrt(var + EPS) * ln_g + ln_b
    return un * jax.nn.sigmoid(un)


def _causal_conv(uh, w_ref, width, halo):
    acc = None
    for j in range(width):
        term = _down(uh, width - 1 - j) * w_ref[pl.ds(j, 1), :]
        acc = term if acc is None else acc + term
    return acc[halo:]


def _conf_fwd(p, cw, cb, lg, lb, nreal, real_end, name):
    T = p.shape[0]

    def body(a_ref, b_ref, w_ref, cb_ref, lg_ref, lb_ref, y_ref):
        _zero_pad_rows(y_ref, CH, CH * (1 + nreal), T)

        def chunk(c, carry):
            start = pl.multiple_of(c * CH, CH)
            ext = pl.ds(start - 32, CH + 32)
            cv = _causal_conv(_glu(a_ref[ext, :], b_ref[ext, :]), w_ref, CONV_W, 32) + cb_ref[...]
            y = _conv_post(cv, lg_ref[...], lb_ref[...])
            y_ref[pl.ds(start, CH), :] = jnp.where(start + _rows(CH) < real_end, y, 0.0)
            return carry

        _pairs_loop(nreal, lambda c, carry: chunk(c + 1, carry), 0)

    return pl.pallas_call(
        body, name=name, grid=(4,), out_shape=jax.ShapeDtypeStruct((T, 512), F32),
        in_specs=[_colblock(T, 0), _colblock(T, 4), _vecblock(32), _vecblock(), _vecblock(), _vecblock()],
        out_specs=_colblock(T, 0), compiler_params=_cparams(("parallel",)),
    )(p, p, cw, cb, lg, lb)


def _conf_bwd(p, cw, cb, lg, lb, dy, nreal, real_end, name):
    T = p.shape[0]

    def body(a_ref, b_ref, w_ref, cb_ref, lg_ref, lb_ref, dy_ref, da_ref, db_ref, dw_ref, dcb_ref, dlg_ref, dlb_ref):
        _zero_pad_rows(da_ref, CH, CH * (1 + nreal), T)
        _zero_pad_rows(db_ref, CH, CH * (1 + nreal), T)
        for r in (dw_ref, dcb_ref, dlg_ref, dlb_ref):
            r[...] = jnp.zeros_like(r)

        def chunk(c, carry):
            start = pl.multiple_of(c * CH, CH)
            ext = pl.ds(start - 32, CH + 64)
            ue = _glu(a_ref[ext, :], b_ref[ext, :])
            cv = _causal_conv(ue, w_ref, CONV_W, 32) + cb_ref[...]
            dye = jnp.where(start + _rows(CH + 32) < real_end, dy_ref[pl.ds(start, CH + 32), :], 0.0)
            _, vjp_cur = jax.vjp(_conv_post, cv[:CH], lg_ref[...], lb_ref[...])
            dc_cur, dlg, dlb = vjp_cur(dye[:CH])
            _, vjp_halo = jax.vjp(_conv_post, cv[CH:], lg_ref[...], lb_ref[...])
            dce = jnp.concatenate([dc_cur, vjp_halo(dye[CH:])[0]], axis=0)
            dlg_ref[...] += dlg
            dlb_ref[...] += dlb
            dcb_ref[...] += jnp.sum(dc_cur, axis=0, keepdims=True)
            du = None
            for j in range(CONV_W):
                w_j = w_ref[pl.ds(j, 1), :]
                term = _up(dce, CONV_W - 1 - j)[:CH] * w_j
                du = term if du is None else du + term
                dw_ref[pl.ds(j, 1), :] += jnp.sum(dc_cur * _up(ue, 2 + j)[:CH], axis=0, keepdims=True)
            cur = pl.ds(start, CH)
            _, vjp_glu = jax.vjp(_glu, a_ref[cur, :], b_ref[cur, :])
            da, db = vjp_glu(du)
            da_ref[cur, :] = da
            db_ref[cur, :] = db
            return carry

        _pairs_loop(nreal, lambda c, carry: chunk(c + 1, carry), 0)

    big = jax.ShapeDtypeStruct((T, 512), F32)
    vec = jax.ShapeDtypeStruct((1, 512), F32)
    return pl.pallas_call(
        body, name=name, grid=(4,), out_shape=(big, big, jax.ShapeDtypeStruct((32, 512), F32), vec, vec, vec),
        in_specs=[_colblock(T, 0), _colblock(T, 4), _vecblock(32), _vecblock(), _vecblock(), _vecblock(),
                  _colblock(T, 0)],
        out_specs=(_colblock(T, 0), _colblock(T, 0), _vecblock(32), _vecblock(), _vecblock(), _vecblock()),
        compiler_params=_cparams(("parallel",)),
    )(p, p, cw, cb, lg, lb, dy)


def _softplus_neg(lam):
    e = jnp.exp(-lam)
    small = e * (1.0 - e * (0.5 - e * (1.0 / 3.0 - e * 0.25)))
    return jnp.where(e < 0.02, small, jnp.log(1.0 + e))


def _one_minus_exp(x):
    series = -x * (1.0 + x * (0.5 + x * (1.0 / 6.0 + x * (1.0 / 24.0 + x * (1.0 / 120.0)))))
    return jnp.where(x > -0.05, series, 1.0 - jnp.exp(x))


def _lru_pre(u, wa, wx, ba, bx, lam, first):
    r = jax.nn.sigmoid(_dot(u, wa) + ba)
    i = jax.nn.sigmoid(_dot(u, wx) + bx)
    log_a = -LRU_C * r * _softplus_neg(lam)
    a = jnp.exp(log_a)
    mult = jnp.sqrt(_one_minus_exp(2.0 * log_a))
    return a, jnp.where(first, 1.0, mult) * (i * u)


def _gelu_gate(gate, h):
    inner = math.sqrt(2.0 / math.pi) * (gate + 0.044715 * (gate * gate * gate))
    return 0.5 * gate * (1.0 + jnp.tanh(inner)) * h


def _lru_specs(T):
    mat = pl.BlockSpec((1, 128, 128), lambda j: (j, 0, 0))
    return [_colblock(T, 8), _colblock(T, 12), _vecblock(8), _vecblock(), mat, mat, _vecblock(), _vecblock(),
            _vecblock()]


def _lru_fwd(p, cw, cb, wa, wx, ba, bx, lam, nreal, real_end, name):
    T = p.shape[0]

    def body(x_ref, g_ref, w_ref, cb_ref, wa_ref, wx_ref, ba_ref, bx_ref, lam_ref, y_ref, h_ref):
        _zero_pad_rows(y_ref, CH, CH * (1 + nreal), T)
        _zero_pad_rows(h_ref, CH, CH * (1 + nreal), T)
        rows = _rows(CH)

        def chunk(c, hprev):
            start = pl.multiple_of(c * CH, CH)
            u = _causal_conv(x_ref[pl.ds(start - 8, CH + 8), :], w_ref, LRU_W, 8) + cb_ref[...]
            A, B = _lru_pre(u, wa_ref[0], wx_ref[0], ba_ref[...], bx_ref[...], lam_ref[...], start + rows == CH)
            s = 1
            while s < CH:
                B = A * jnp.where(rows >= s, _down(B, s), 0.0) + B
                A = A * jnp.where(rows >= s, _down(A, s), 1.0)
                s *= 2
            h = B + A * hprev
            cur = pl.ds(start, CH)
            h_ref[cur, :] = h
            y_ref[cur, :] = jnp.where(start + rows < real_end, _gelu_gate(g_ref[cur, :], h), 0.0)
            return jnp.sum(jnp.where(rows == CH - 1, h, 0.0), axis=0, keepdims=True)

        _pairs_loop(nreal, lambda c, carry: chunk(c + 1, carry), jnp.zeros((1, 128), F32))

    big = jax.ShapeDtypeStruct((T, 512), F32)
    return pl.pallas_call(
        body, name=name, grid=(4,), out_shape=(big, big), in_specs=_lru_specs(T),
        out_specs=(_colblock(T, 0), _colblock(T, 0)), compiler_params=_cparams(("parallel",)),
    )(p, p, cw, cb, wa, wx, ba, bx, lam)


def _lru_bwd(p, cw, cb, wa, wx, ba, bx, lam, hs, dy, nreal, real_end, name):
    T = p.shape[0]

    def body(x_ref, g_ref, w_ref, cb_ref, wa_ref, wx_ref, ba_ref, bx_ref, lam_ref, h_ref, dy_ref,
             dx_ref, dgate_ref, dw_ref, dcb_ref, dwa_ref, dwx_ref, dba_ref, dbx_ref, dlam_ref):
        _zero_pad_rows(dx_ref, CH, CH * (1 + nreal), T)
        _zero_pad_rows(dgate_ref, CH, CH * (1 + nreal), T)
        for r in (dw_ref, dcb_ref, dwa_ref, dwx_ref, dba_ref, dbx_ref, dlam_ref):
            r[...] = jnp.zeros_like(r)
        rows = _rows(CH)

        def chunk(k, carry):
            cdh, du_head = carry
            c = nreal - k
            start = pl.multiple_of(c * CH, CH)
            cur = pl.ds(start, CH)
            xe = x_ref[pl.ds(start - 8, CH + 8), :]
            u = _causal_conv(xe, w_ref, LRU_W, 8) + cb_ref[...]
            first = start + rows == CH
            (a, _), vjp_pre = jax.vjp(lambda uu, m1, m2, b1, b2, ll: _lru_pre(uu, m1, m2, b1, b2, ll, first),
                                      u, wa_ref[0], wx_ref[0], ba_ref[...], bx_ref[...], lam_ref[...])
            h = h_ref[cur, :]
            hm1 = _down(h_ref[pl.ds(start - 8, CH + 8), :], 1)[8:]
            _, vjp_post = jax.vjp(_gelu_gate, g_ref[cur, :], h)
            dgate, D = vjp_post(jnp.where(start + rows < real_end, dy_ref[cur, :], 0.0))
            dgate_ref[cur, :] = dgate
            D = D + jnp.where(rows == CH - 1, cdh, 0.0)
            C = jnp.where(rows < CH - 1, _up(a, 1), 0.0)
            s = 1
            while s < CH:
                D = D + C * jnp.where(rows + s < CH, _up(D, s), 0.0)
                C = C * jnp.where(rows + s < CH, _up(C, s), 1.0)
                s *= 2
            du, dwa, dwx, dba, dbx, dlam = vjp_pre((D * hm1, D))
            dwa_ref[0] += dwa
            dwx_ref[0] += dwx
            dba_ref[...] += dba
            dbx_ref[...] += dbx
            dlam_ref[...] += dlam
            dcb_ref[...] += jnp.sum(du, axis=0, keepdims=True)
            due = jnp.concatenate([du, du_head], axis=0)
            dx = None
            for j in range(LRU_W):
                term = _up(due, LRU_W - 1 - j)[:CH] * w_ref[pl.ds(j, 1), :]
                dx = term if dx is None else dx + term
                dw_ref[pl.ds(j, 1), :] += jnp.sum(du * _up(xe, 8 - (LRU_W - 1) + j)[:CH], axis=0, keepdims=True)
            dx_ref[cur, :] = dx
            return jnp.sum(jnp.where(rows == 0, a * D, 0.0), axis=0, keepdims=True), du[:8]

        _pairs_loop(nreal, chunk, (jnp.zeros((1, 128), F32), jnp.zeros((8, 128), F32)))

    big = jax.ShapeDtypeStruct((T, 512), F32)
    vec = jax.ShapeDtypeStruct((1, 512), F32)
    mat = jax.ShapeDtypeStruct((4, 128, 128), F32)
    matspec = pl.BlockSpec((1, 128, 128), lambda j: (j, 0, 0))
    return pl.pallas_call(
        body, name=name, grid=(4,),
        out_shape=(big, big, jax.ShapeDtypeStruct((8, 512), F32), vec, mat, mat, vec, vec, vec),
        in_specs=_lru_specs(T) + [_colblock(T, 0), _colblock(T, 4)],
        out_specs=(_colblock(T, 0), _colblock(T, 0), _vecblock(8), _vecblock(), matspec, matspec, _vecblock(),
                   _vecblock(), _vecblock()),
        compiler_params=_cparams(("parallel",)),
    )(p, p, cw, cb, wa, wx, ba, bx, lam, hs, dy)


def _ffn_forward(h, gamma, wg, wu, wd, tag):
    xn = _rms_fwd(h, gamma, f"rms_fwd_{tag}")
    g, u, a = _ffn_up(xn, wg, wu, f"ffn_up_{tag}")
    if callable(wd):
        wd = wd(a)
    out = _mm([(a, wd)], "NN", f"ffn_down_{tag}", res=h, res_scale=0.5)
    return out, (h, xn, g, u, a)


def _after(w, tok):
    return w if tok is None else w + tok[0, 0].astype(w.dtype)


def _ffn_backward(saved, gamma, wg, wu, wd, dout, tok, tag, emit_one=None):
    h, xn, p, q, a = saved
    dout, dout16 = dout
    dwd = _mm([(a, dout16)], "TN", f"ffn_dwd_{tag}", res_scale=0.5, out_dtype=MXU, dep=tok)
    if emit_one is not None:
        tok = emit_one('wd', dwd)
    dg, du = _ffn_dact(dout16, wd, p, q, 0.5, f"ffn_dact_{tag}", dep=tok)
    dwg = _mm([(dg, xn)], "TN", f"ffn_dwg_{tag}", out_dtype=MXU)
    if emit_one is not None:
        tok = emit_one('wg', dwg)
    dwu = _mm([(du, xn)], "TN", f"ffn_dwu_{tag}", out_dtype=MXU, dep=tok if emit_one is not None else None)
    if emit_one is not None:
        tok = emit_one('wu', dwu)
    dxn = _mm([(dg, wg), (du, wu)], "NN", f"ffn_dxn_{tag}", dep=tok if emit_one is not None else None)
    dh, dh16, dgamma = _rms_bwd(h, gamma, dxn, dout, f"rms_bwd_{tag}")
    return (dh, dh16), dgamma, dwg, dwu, dwd


def _blockdiag(w, per):
    n, k, _ = w.shape
    out = jnp.zeros((n // per, per * k, per * k), w.dtype)
    for i in range(per):
        out = out.at[:, i * k:(i + 1) * k, i * k:(i + 1) * k].set(w[i::per])
    return out


def _blockdiag_grad(g, per, k):
    parts = [g[:, i * k:(i + 1) * k, i * k:(i + 1) * k] for i in range(per)]
    return jnp.stack(parts, axis=1).reshape(-1, k, k)


def _local_step(x, tgt, W, fetch, emit):
    fetch(0, x)
    seq, D = x.shape
    lr = N_META + seq
    nreal = -(-lr // CH)
    T = CH * (nreal + 2)
    if T > 640 and T % 640:
        T += 640 - T % 640
    lo, real_end = CH + N_META, CH + lr
    zf = lambda n: jnp.zeros((n, D), F32)
    h0 = jnp.concatenate([zf(CH), W['meta_tokens'], x, zf(T - real_end)], axis=0)
    tgt_p = jnp.concatenate([zf(lo), tgt, zf(T - real_end)], axis=0)
    row = lambda v: v.reshape(1, -1)
    G = {}

    h = h0
    saved = []
    for l in range(2):
        wd1 = W['ffn1_wd', l] if l else (lambda after: (fetch(1, after), W['ffn1_wd', 0])[1])
        h, s1 = _ffn_forward(h, row(W['ffn1_norm'][l]), W['ffn1_wg', l], W['ffn1_wu', l], wd1, f"a{l}")
        hm = h
        fetch(3 * l + 2, hm)
        xn = _rms_fwd(hm, row(W['mix_norm'][l]), f"rms_fwd_mix{l}")
        if l == 0:
            p = _mm([(xn, W['w_in_even'])], "NT", "in_even")
            wbd = _blockdiag(W['pool_w'][0], 4)[0]
            l0, l1 = row(W['hgrn_lb_logits'][0]), row(W['hgrn_lb_logits'][1])
            ya = _pool_fwd(p, wbd, W['pool_scale'], nreal, real_end, "pool_fwd")
            yb, states = _hgrn_fwd(p, l0, l1, W['hgrn_gnorm'], nreal, real_end, "hgrn_fwd")
            wo = W['w_out_even']
            h = _mm([(ya, wo[:256]), (yb, wo[256:])], "NN", "out_even", res=hm)
            sm = (hm, xn, p, wbd, l0, l1, ya, yb, states)
        else:
            p = _mm([(xn, W['w_in_odd'])], "NT", "in_odd")
            cw = jnp.pad(W['conv_w'][0], ((0, 1), (0, 0)))
            lw = jnp.pad(W['lru_conv_w'][0], ((0, 4), (0, 0)))
            wa, wx = _blockdiag(W['lru_wa'][0], 2), _blockdiag(W['lru_wx'][0], 2)
            yc = _conf_fwd(p, cw, W['conv_b'], W['conv_ln_g'], W['conv_ln_b'], nreal, real_end, "conf_fwd")
            yd, hs = _lru_fwd(p, lw, W['lru_conv_b'], wa, wx, W['lru_ba'], W['lru_bx'], W['lru_lambda'], nreal,
                              real_end, "lru_fwd")
            wo = W['w_out_odd']
            h = _mm([(yc, wo[:512]), (yd, wo[512:])], "NN", "out_odd", res=hm)
            sm = (hm, xn, p, cw, lw, wa, wx, yc, yd, hs)
        fetch(3 * l + 3, h)
        h, s2 = _ffn_forward(h, row(W['ffn2_norm'][l]), W['ffn2_wg', l], W['ffn2_wu', l], W['ffn2_wd', l], f"b{l}")
        if l == 0:
            fetch(4, h)
        saved.append((s1, sm, s2))

    loss8, dh, dh16, dfin = _loss_head(h, row(W['final_norm']), tgt_p, lo, real_end, "loss_head")
    dh = (dh, dh16)
    G['final_norm'] = dfin[0]

    per_layer = {k: [None, None] for k in ('ffn1_norm', 'mix_norm', 'ffn2_norm')}
    tok = None
    for l in (1, 0):
        s1, sm, s2 = saved[l]
        dh, dn, dwg, dwu, dwd = _ffn_backward(s2, row(W['ffn2_norm'][l]), W['ffn2_wg', l], W['ffn2_wu', l],
                                              W['ffn2_wd', l], dh, tok, f"b{l}")
        per_layer['ffn2_norm'][l] = dn[0]
        tok = emit(f"ffn2_{l}", [('ffn2_wg', l, dwg), ('ffn2_wu', l, dwu), ('ffn2_wd', l, dwd)])
        if l == 0:
            hm, xn, p, wbd, l0, l1, ya, yb, states = sm
            wo, wi = W['w_out_even'], W['w_in_even']
            dwo = jnp.concatenate([_mm([(ya, dh[1])], "TN", "dwo_even_a", out_dtype=MXU),
                                   _mm([(yb, dh[1])], "TN", "dwo_even_b", out_dtype=MXU)], axis=0)
            dy = _mm([(dh[1], wo)], "NT", "dy_even", dep=tok)
            dpp, dwbd, dsc = _pool_bwd(p, wbd, W['pool_scale'], dy, nreal, real_end, "pool_bwd")
            dq, df, di, dg, dl0, dl1, dgn = _hgrn_bwd(p, l0, l1, W['hgrn_gnorm'], states, dy, nreal, real_end,
                                                      "hgrn_bwd")
            G['pool_w'] = _blockdiag_grad(dwbd[None], 4, 64)[None]
            G['pool_scale'] = dsc
            G['hgrn_lb_logits'] = jnp.concatenate([dl0, dl1], axis=0)
            G['hgrn_gnorm'] = dgn
            parts = [dpp, dq, df, di, dg]
            offs = [0, 256, 1024, 1792, 2560, 3328]
            dwi = jnp.concatenate(
                [_mm([(dpart, xn)], "TN", f"dwi_even_{k}", out_dtype=MXU) for k, dpart in enumerate(parts)], axis=0)
            dxn = _mm([(dpart, wi[offs[k]:offs[k + 1]]) for k, dpart in enumerate(parts)], "NN", "dxn_even")
            tok = emit("even", [('w_in_even', None, dwi), ('w_out_even', None, dwo)])
        else:
            hm, xn, p, cw, lw, wa, wx, yc, yd, hs = sm
            wo, wi = W['w_out_odd'], W['w_in_odd']
            dwo = jnp.concatenate([_mm([(yc, dh[1])], "TN", "dwo_odd_c", out_dtype=MXU),
                                   _mm([(yd, dh[1])], "TN", "dwo_odd_d", out_dtype=MXU)], axis=0)
            dy = _mm([(dh[1], wo)], "NT", "dy_odd", dep=tok)
            da, db, dcw, dcb, dlg, dlb = _conf_bwd(p, cw, W['conv_b'], W['conv_ln_g'], W['conv_ln_b'], dy, nreal,
                                                   real_end, "conf_bwd")
            dx, dgate, dlw, dlcb, dwa, dwx, dba, dbx, dlam = _lru_bwd(
                p, lw, W['lru_conv_b'], wa, wx, W['lru_ba'], W['lru_bx'], W['lru_lambda'], hs, dy, nreal, real_end,
                "lru_bwd")
            G['conv_w'], G['conv_b'], G['conv_ln_g'], G['conv_ln_b'] = dcw[None, :CONV_W], dcb, dlg, dlb
            G['lru_conv_w'], G['lru_conv_b'] = dlw[None, :LRU_W], dlcb
            G['lru_wa'] = _blockdiag_grad(dwa, 2, 64)[None]
            G['lru_wx'] = _blockdiag_grad(dwx, 2, 64)[None]
            G['lru_ba'], G['lru_bx'], G['lru_lambda'] = dba, dbx, dlam
            parts = [da, db, dx, dgate]
            dwi = jnp.concatenate(
                [_mm([(dpart, xn)], "TN", f"dwi_odd_{k}", out_dtype=MXU) for k, dpart in enumerate(parts)], axis=0)
            dxn = _mm([(dpart, wi[512 * k:512 * (k + 1)]) for k, dpart in enumerate(parts)], "NN", "dxn_odd")
            tok = emit("odd", [('w_in_odd', None, dwi), ('w_out_odd', None, dwo)])
        dh, dh16, dn = _rms_bwd(hm, _after(row(W['mix_norm'][l]), tok), dxn, dh[0], f"rms_bwd_mix{l}")
        dh = (dh, dh16)
        per_layer['mix_norm'][l] = dn[0]
        one = None if l == 1 else (lambda sfx, g: emit(f"ffn1_0_{sfx}", [('ffn1_' + sfx, 0, g)]))
        dh, dn, dwg, dwu, dwd = _ffn_backward(s1, row(W['ffn1_norm'][l]), W['ffn1_wg', l], W['ffn1_wu', l],
                                              W['ffn1_wd', l], dh, None, f"a{l}", one)
        per_layer['ffn1_norm'][l] = dn[0]
        if l == 1:
            tok = emit("ffn1_1", [('ffn1_wg', l, dwg), ('ffn1_wu', l, dwu), ('ffn1_wd', l, dwd)])
    for k, v in per_layer.items():
        G[k] = jnp.stack(v, axis=0)
    G['meta_tokens'] = dh[0][CH:lo]
    return loss8[0, 0], dh[0][lo:real_end], G


def _pack(arrs):
    flat = jnp.concatenate([a.reshape(-1).astype(F32) for a in arrs])
    n = flat.shape[0]
    padded = -(-n // 1024) * 1024
    return jnp.pad(flat, (0, padded - n)).reshape(-1, 128)


def _unpack(packed, shapes):
    flat = packed.reshape(-1)
    out, off = [], 0
    for s in shapes:
        n = math.prod(s)
        out.append(flat[off:off + n].reshape(s))
        off += n
    return out


def _pack8(arrs):
    flat = jnp.concatenate([a.reshape(NDEV, -1).astype(F32) for a in arrs], axis=1)
    n = flat.shape[1]
    padded = -(-n // 1024) * 1024
    return jnp.pad(flat, ((0, 0), (0, padded - n))).reshape(NDEV, -1, 128)


def _unpack8(packed, shapes):
    flat = packed.reshape(NDEV, -1)
    out, off = [], 0
    for s in shapes:
        n = math.prod(s)
        out.append(flat[:, off:off + n].reshape((NDEV,) + tuple(s)))
        off += n
    return out


def _to_full(gathered, axis):
    s = gathered.shape[1:]
    return jnp.moveaxis(gathered, 0, axis).reshape(s[:axis] + (NDEV * s[axis],) + s[axis + 1:])


def _to_slots(full, axis):
    s = full.shape
    return jnp.moveaxis(full.reshape(s[:axis] + (NDEV, s[axis] // NDEV) + s[axis + 1:]), axis, 0)


def kernel(x, meta_tokens, ffn1_norm, ffn1_wg, ffn1_wu, ffn1_wd, mix_norm, ffn2_norm, ffn2_wg, ffn2_wu, ffn2_wd, w_in_even, pool_w, pool_scale, hgrn_lb_logits, hgrn_gnorm, w_out_even, w_in_odd, conv_w, conv_b, conv_ln_g, conv_ln_b, lru_conv_w, lru_conv_b, lru_wa, lru_ba, lru_wx, lru_bx, lru_lambda, w_out_odd, final_norm, loss_target, m_meta_tokens, m_ffn1_norm, m_ffn1_wg, m_ffn1_wu, m_ffn1_wd, m_mix_norm, m_ffn2_norm, m_ffn2_wg, m_ffn2_wu, m_ffn2_wd, m_w_in_even, m_pool_w, m_pool_scale, m_hgrn_lb_logits, m_hgrn_gnorm, m_w_out_even, m_w_in_odd, m_conv_w, m_conv_b, m_conv_ln_g, m_conv_ln_b, m_lru_conv_w, m_lru_conv_b, m_lru_wa, m_lru_ba, m_lru_wx, m_lru_bx, m_lru_lambda, m_w_out_odd, m_final_norm, v_meta_tokens, v_ffn1_norm, v_ffn1_wg, v_ffn1_wu, v_ffn1_wd, v_mix_norm, v_ffn2_norm, v_ffn2_wg, v_ffn2_wu, v_ffn2_wd, v_w_in_even, v_pool_w, v_pool_scale, v_hgrn_lb_logits, v_hgrn_gnorm, v_w_out_even, v_w_in_odd, v_conv_w, v_conv_b, v_conv_ln_g, v_conv_ln_b, v_lru_conv_w, v_lru_conv_b, v_lru_wa, v_lru_ba, v_lru_wx, v_lru_bx, v_lru_lambda, v_w_out_odd, v_final_norm):
    args = (meta_tokens, ffn1_norm, ffn1_wg, ffn1_wu, ffn1_wd, mix_norm, ffn2_norm, ffn2_wg, ffn2_wu, ffn2_wd, w_in_even, pool_w, pool_scale, hgrn_lb_logits, hgrn_gnorm, w_out_even, w_in_odd, conv_w, conv_b, conv_ln_g, conv_ln_b, lru_conv_w, lru_conv_b, lru_wa, lru_ba, lru_wx, lru_bx, lru_lambda, w_out_odd, final_norm)
    margs = (m_meta_tokens, m_ffn1_norm, m_ffn1_wg, m_ffn1_wu, m_ffn1_wd, m_mix_norm, m_ffn2_norm, m_ffn2_wg, m_ffn2_wu, m_ffn2_wd, m_w_in_even, m_pool_w, m_pool_scale, m_hgrn_lb_logits, m_hgrn_gnorm, m_w_out_even, m_w_in_odd, m_conv_w, m_conv_b, m_conv_ln_g, m_conv_ln_b, m_lru_conv_w, m_lru_conv_b, m_lru_wa, m_lru_ba, m_lru_wx, m_lru_bx, m_lru_lambda, m_w_out_odd, m_final_norm)
    vargs = (v_meta_tokens, v_ffn1_norm, v_ffn1_wg, v_ffn1_wu, v_ffn1_wd, v_mix_norm, v_ffn2_norm, v_ffn2_wg, v_ffn2_wu, v_ffn2_wd, v_w_in_even, v_pool_w, v_pool_scale, v_hgrn_lb_logits, v_hgrn_gnorm, v_w_out_even, v_w_in_odd, v_conv_w, v_conv_b, v_conv_ln_g, v_conv_ln_b, v_lru_conv_w, v_lru_conv_b, v_lru_wa, v_lru_ba, v_lru_wx, v_lru_bx, v_lru_lambda, v_w_out_odd, v_final_norm)
    Wl = dict(zip(W_NAMES, args))
    Ml = dict(zip(W_NAMES, margs))
    Vl = dict(zip(W_NAMES, vargs))

    small_shapes = [Wl[n].shape for n in SMALL_SHARDED]
    ffn = lambda p, l: [(p + s, l) for s in ('_wg', '_wu', '_wd')]
    mix = lambda p: [('w_in_' + p, None), ('w_out_' + p, None)]
    ggroups = [ffn('ffn1', 0)[:2], ffn('ffn1', 0)[2:], mix('even'), ffn('ffn2', 0), ffn('ffn1', 1), mix('odd'),
               ffn('ffn2', 1)]
    colsharded = lambda n: SHARD_AXIS[n] == 2

    def shard(n, l):
        w = Wl[n][0 if l is None else l].astype(MXU)
        return w.T if colsharded(n) else w

    srcs = [[shard(n, l) for n, l in g] for g in ggroups]
    srcs[0] = [_pack([Wl[n] for n in SMALL_SHARDED])] + srcs[0]
    handles, _ = _exchange_start(srcs, True, "gather_start")
    W = {n: Wl[n] for n in REPLICATED}

    def fetch(k, after):
        lands = _exchange_wait(handles[k], True, after, f"gather_wait_{k}")
        if k == 0:
            for n, g in zip(SMALL_SHARDED, _unpack8(lands[0], small_shapes)):
                W[n] = _to_full(g, SHARD_AXIS[n])
            lands = lands[1:]
        for (n, l), g in zip(ggroups[k], lands):
            W[n if l is None else (n, l)] = g.reshape(-1, g.shape[-1])

    pending = []

    def emit(tag, grads):
        slots = [g.astype(MXU).reshape(NDEV, -1, g.shape[-1]) for _, _, g in grads]
        hs, token = _exchange_start([slots], False, f"scatter_start_{tag}")
        pending.append((tag, hs[0], [(n, l) for n, l, _ in grads]))
        return token

    loss_part, grad_x, G = _local_step(x[0], loss_target[0], W, fetch, emit)

    small_slots = [_to_slots(G[n].astype(F32), SHARD_AXIS[n]) for n in SMALL_SHARDED]
    send = [_pack8(small_slots), _pack([G[n] for n in REPLICATED]),
            jnp.broadcast_to(loss_part, (8, 128))]
    got = _exchange(send, [False, True, True], "scatter_small")
    loss = jnp.sum(got[2][:, 0, 0])
    recv = {}
    for tag, handle, keys in pending:
        for key, r in zip(keys, _exchange_wait(handle, False, got[0], f"scatter_wait_{tag}")):
            recv[key] = r

    outs = {}
    for n in BIG:
        shp = Wl[n].shape
        C = shp[-1]
        rs = [recv[n, None]] if shp[0] == 1 else [recv[n, l] for l in range(shp[0])]
        if colsharded(n):
            rs = [_sum8(r, f"sum8_{n}_{j}").T for j, r in enumerate(rs)]
        res = _adamw(rs, Wl[n].reshape(-1, C), Ml[n].reshape(-1, C), Vl[n].reshape(-1, C), f"adamw_{n}",
                     summed=colsharded(n))
        outs[n] = [o.reshape(shp) for o in res]
    for names, r, tag in ((SMALL_SHARDED, got[0], "small"), (REPLICATED, got[1], "repl")):
        shapes = [Wl[n].shape for n in names]
        res = _adamw([r], _pack([Wl[n] for n in names]), _pack([Ml[n] for n in names]),
                     _pack([Vl[n] for n in names]), f"adamw_{tag}")
        unp = [_unpack(o, shapes) for o in res]
        for k, n in enumerate(names):
            outs[n] = [unp[j][k] for j in range(4)]

    result = [loss, grad_x[None]]
    for j in range(4):
        result += [outs[n][j] for n in W_NAMES]
    return tuple(result)
```

```python
import functools
import math

import jax
import jax.numpy as jnp
from jax import lax
from jax.experimental import pallas as pl
from jax.experimental.pallas import tpu as pltpu

F32 = jnp.float32
MXU = jnp.bfloat16
EPS = 1e-6
CH = 128
HG = 128
N_META = 16
CONV_W = 31
LRU_W = 4
LRU_C = 8.0
VMEM_LIMIT = 48 * 2 ** 20
MM_VMEM_BUDGET = 36 * 2 ** 20
ADAM_LR, ADAM_B1, ADAM_B2, ADAM_EPS, ADAM_WD, ADAM_STEP = 0.001, 0.9, 0.999, 1e-08, 0.01, 10
MESH_AXES = ("x", "y", "c")
NDEV = 8

W_NAMES = ['meta_tokens', 'ffn1_norm', 'ffn1_wg', 'ffn1_wu', 'ffn1_wd', 'mix_norm', 'ffn2_norm', 'ffn2_wg', 'ffn2_wu',
           'ffn2_wd', 'w_in_even', 'pool_w', 'pool_scale', 'hgrn_lb_logits', 'hgrn_gnorm', 'w_out_even', 'w_in_odd',
           'conv_w', 'conv_b', 'conv_ln_g', 'conv_ln_b', 'lru_conv_w', 'lru_conv_b', 'lru_wa', 'lru_ba', 'lru_wx',
           'lru_bx', 'lru_lambda', 'w_out_odd', 'final_norm']
SHARD_AXIS = {'meta_tokens': 1, 'ffn1_wg': 2, 'ffn1_wu': 2, 'ffn1_wd': 1, 'ffn2_wg': 2, 'ffn2_wu': 2, 'ffn2_wd': 1,
              'w_in_even': 2, 'w_out_even': 1, 'w_in_odd': 2, 'conv_w': 2, 'conv_b': 1, 'conv_ln_g': 1,
              'conv_ln_b': 1, 'lru_conv_w': 2, 'lru_conv_b': 1, 'lru_ba': 1, 'lru_bx': 1, 'lru_lambda': 1,
              'w_out_odd': 1}
BIG = ['ffn1_wg', 'ffn1_wu', 'ffn1_wd', 'ffn2_wg', 'ffn2_wu', 'ffn2_wd', 'w_in_even', 'w_out_even', 'w_in_odd',
       'w_out_odd']
SMALL_SHARDED = [n for n in W_NAMES if n in SHARD_AXIS and n not in BIG]
REPLICATED = [n for n in W_NAMES if n not in SHARD_AXIS]


def _cparams(sem=None, vmem=VMEM_LIMIT):
    return pltpu.CompilerParams(dimension_semantics=sem, vmem_limit_bytes=vmem)


def _tiles(n):
    return [c for c in range(128, n + 1, 128) if n % c == 0] or [n]


def _tile(n, cap=1024):
    return max([c for c in _tiles(n) if c <= cap], default=_tiles(n)[0])


def _rowtile(n):
    for c in (256, 352, 128, 64, 32, 16, 8):
        if n % c == 0:
            return c
    return n


def _copies(srcs, lands, bcast, ssem, rsem, lsem):
    x, y, c = lax.axis_index("x"), lax.axis_index("y"), lax.axis_index("c")
    me = 4 * x + 2 * y + c
    locs, sends, recvs = [], [], []
    for a in range(len(srcs)):
        locs.append(pltpu.make_async_copy(srcs[a] if bcast[a] else srcs[a].at[me], lands[a].at[me], lsem.at[a]))
        for m in range(1, NDEV):
            px = 1 - x if (m >> 2) & 1 else x
            py = 1 - y if (m >> 1) & 1 else y
            pc = 1 - c if m & 1 else c
            peer = 4 * px + 2 * py + pc
            src = srcs[a] if bcast[a] else srcs[a].at[peer]
            k = a * NDEV + m
            for dst, out in ((lands[a].at[me], sends), (lands[a].at[peer], recvs)):
                out.append(pltpu.make_async_remote_copy(src_ref=src, dst_ref=dst, send_sem=ssem.at[k],
                                                        recv_sem=rsem.at[k], device_id=(px, py, pc),
                                                        device_id_type=pl.DeviceIdType.MESH))
    return locs, sends, recvs


def _land_shape(arr, bc):
    return (NDEV,) + tuple(arr.shape if bc else arr.shape[1:])


def _exchange(arrays, bcast, name):
    n = len(arrays)

    def body(*refs):
        locs, sends, recvs = _copies(refs[:n], refs[n:2 * n], bcast, refs[2 * n], refs[2 * n + 1], refs[2 * n + 2])
        for d in locs + sends:
            d.start()
        for r in recvs:
            r.wait_recv()
        for s in sends:
            s.wait_send()
        for loc in locs:
            loc.wait()

    out_shape = tuple(jax.ShapeDtypeStruct(_land_shape(arr, bc), arr.dtype) for arr, bc in zip(arrays, bcast))
    any_spec = pl.BlockSpec(memory_space=pl.ANY)
    return pl.pallas_call(
        body, name=name, out_shape=out_shape, in_specs=[any_spec] * n, out_specs=tuple([any_spec] * n),
        scratch_shapes=[pltpu.SemaphoreType.DMA((n * NDEV,)), pltpu.SemaphoreType.DMA((n * NDEV,)),
                        pltpu.SemaphoreType.DMA((n,))],
    )(*arrays)


_HBM = pl.BlockSpec(memory_space=pltpu.HBM)
_SEM = pl.BlockSpec(memory_space=pltpu.SEMAPHORE)
_EFFECT = pltpu.SideEffectType.DATAFLOW_SIDE_EFFECTING


def _exchange_start(groups, bcast, name):
    sizes = [len(g) for g in groups]
    srcs = [a for g in groups for a in g]
    n, ng = len(srcs), len(groups)
    lands = [lax.empty(_land_shape(a, bcast), a.dtype) for a in srcs]

    def body(*refs):
        off = 0
        for gi, sz in enumerate(sizes):
            sem = refs[2 * n + 3 * gi:2 * n + 3 * gi + 3]
            locs, sends, _ = _copies(refs[off:off + sz], refs[n + off:n + off + sz], [bcast] * sz, *sem)
            for d in locs + sends:
                d.start()
            off += sz
        refs[-1][...] = jnp.zeros((8, 128), F32)

    sems = []
    for sz in sizes:
        sems += [pltpu.SemaphoreType.DMA((sz * NDEV,)), pltpu.SemaphoreType.DMA((sz * NDEV,)),
                 pltpu.SemaphoreType.DMA((sz,))]
    thru = [pltpu.HBM(a.shape, a.dtype) for a in srcs + lands]
    outs = pl.pallas_call(
        body, name=name, out_shape=tuple(sems + thru + [jax.ShapeDtypeStruct((8, 128), F32)]),
        in_specs=[_HBM] * (2 * n),
        out_specs=tuple([_SEM] * (3 * ng) + [_HBM] * (2 * n) + [pl.BlockSpec(memory_space=pltpu.VMEM)]),
        input_output_aliases={i: 3 * ng + i for i in range(2 * n)},
        compiler_params=pltpu.CompilerParams(has_side_effects=_EFFECT),
    )(*[pltpu.with_memory_space_constraint(a, pltpu.HBM) for a in srcs + lands])
    handles, off = [], 0
    for gi, sz in enumerate(sizes):
        handles.append((outs[3 * gi:3 * gi + 3], outs[3 * ng + off:3 * ng + off + sz],
                        outs[3 * ng + n + off:3 * ng + n + off + sz]))
        off += sz
    return handles, outs[-1]


def _exchange_wait(handle, bcast, after, name):
    sems, srcs, lands = handle
    n = len(srcs)

    def body(*refs):
        locs, sends, recvs = _copies(refs[:n], refs[n:2 * n], [bcast] * n, *refs[2 * n:2 * n + 3])
        for r in recvs:
            r.wait_recv()
        for s in sends:
            s.wait_send()
        for loc in locs:
            loc.wait()

    outs = pl.pallas_call(
        body, name=name, out_shape=tuple(pltpu.HBM(a.shape, a.dtype) for a in list(srcs) + list(lands)),
        in_specs=[_HBM] * (2 * n) + [_SEM] * 3 + [pl.BlockSpec(memory_space=pl.ANY)],
        out_specs=tuple([_HBM] * (2 * n)), input_output_aliases={i: i for i in range(2 * n)},
        compiler_params=pltpu.CompilerParams(has_side_effects=_EFFECT),
    )(*srcs, *lands, *sems, after)
    return outs[n:]


def _sum8(recv, name):
    _, R, C = recv.shape
    br = _rowtile(R)

    def body(r_ref, o_ref):
        s = r_ref[0].astype(F32)
        for k in range(1, NDEV):
            s = s + r_ref[k].astype(F32)
        o_ref[...] = s

    return pl.pallas_call(
        body, name=name, grid=(R // br,), out_shape=jax.ShapeDtypeStruct((R, C), F32),
        in_specs=[pl.BlockSpec((NDEV, br, C), lambda i: (0, i, 0))], out_specs=pl.BlockSpec((br, C), lambda i: (i, 0)),
        compiler_params=_cparams(("parallel",)),
    )(recv)


def _adamw(recvs, w, m, v, name, summed=False):
    R, C = w.shape
    nr = len(recvs)
    br = _rowtile(R // nr)
    nb0 = R // nr // br

    def body(*refs):
        w_ref, m_ref, v_ref, g_o, d_o, m_o, v_o = refs[nr:]
        g = None
        for j in range(nr):
            if summed:
                s = refs[j][...]
            else:
                s = refs[j][0].astype(F32)
                for k in range(1, NDEV):
                    s = s + refs[j][k].astype(F32)
            g = s if g is None else jnp.where(pl.program_id(0) >= j * nb0, s, g)
        mn = ADAM_B1 * m_ref[...] + (1.0 - ADAM_B1) * g
        vn = ADAM_B2 * v_ref[...] + (1.0 - ADAM_B2) * (g * g)
        m_hat = mn / (1.0 - ADAM_B1 ** ADAM_STEP)
        v_hat = vn / (1.0 - ADAM_B2 ** ADAM_STEP)
        g_o[...] = g
        d_o[...] = -ADAM_LR * (m_hat / (jnp.sqrt(v_hat) + ADAM_EPS) + ADAM_WD * w_ref[...])
        m_o[...] = mn
        v_o[...] = vn

    def rspec(j):
        if summed:
            return pl.BlockSpec((br, C), lambda i: (jnp.clip(i - j * nb0, 0, nb0 - 1), 0))
        return pl.BlockSpec((NDEV, br, C), lambda i: (0, jnp.clip(i - j * nb0, 0, nb0 - 1), 0))

    blk = pl.BlockSpec((br, C), lambda i: (i, 0))
    sds = jax.ShapeDtypeStruct((R, C), F32)
    return pl.pallas_call(
        body, name=name, grid=(R // br,), out_shape=(sds, sds, sds, sds),
        in_specs=[rspec(j) for j in range(nr)] + [blk, blk, blk], out_specs=(blk, blk, blk, blk),
        compiler_params=_cparams(("arbitrary",)),
    )(*recvs, w, m, v)


_DIMS = {"NN": ((1,), (0,)), "NT": ((1,), (1,)), "TN": ((0,), (0,))}


def _dot(a, b, mode="NN"):
    return lax.dot_general(a.astype(MXU), b.astype(MXU), (_DIMS[mode], ((), ())), preferred_element_type=F32)


def _dotf(a, b, mode="NN"):
    return lax.dot_general(a, b, (_DIMS[mode], ((), ())), precision=lax.Precision.HIGH,
                           preferred_element_type=F32)


def _mm(pairs, mode, name, res=None, res_scale=1.0, out_dtype=F32, dep=None):
    a0, b0 = pairs[0]
    M = a0.shape[1] if mode == "TN" else a0.shape[0]
    N = b0.shape[0] if mode == "NT" else b0.shape[1]
    npairs = len(pairs)

    def vmem_bytes(tm, tn):
        total = tm * tn * 4 * (2 + (2 if res is not None else 0) + 2)
        for a, b in pairs:
            ka = a.shape[0] if mode == "TN" else a.shape[1]
            kb = b.shape[1] if mode == "NT" else b.shape[0]
            for k, t, arr in ((ka, tm, a), (kb, tn, b)):
                total += k * t * (2 * arr.dtype.itemsize + (2 if arr.dtype == F32 else 0))
        return total

    tm, tn = max(((tm, tn) for tm in _tiles(M) for tn in _tiles(N) if vmem_bytes(tm, tn) <= MM_VMEM_BUDGET),
                 key=lambda t: (t[0] * t[1], t[1]), default=(_tiles(M)[0], _tiles(N)[0]))

    def body(*refs):
        acc = None
        for p in range(npairs):
            d = _dot(refs[2 * p][...], refs[2 * p + 1][...], mode)
            acc = d if acc is None else acc + d
        if res_scale != 1.0:
            acc = res_scale * acc
        if res is not None:
            acc = refs[2 * npairs][...] + acc
        refs[-1][...] = acc.astype(out_dtype)

    in_specs, args = [], []
    for a, b in pairs:
        if mode == "TN":
            in_specs.append(pl.BlockSpec((a.shape[0], tm), lambda i, j: (0, i)))
        else:
            in_specs.append(pl.BlockSpec((tm, a.shape[1]), lambda i, j: (i, 0)))
        if mode == "NT":
            in_specs.append(pl.BlockSpec((tn, b.shape[1]), lambda i, j: (j, 0)))
        else:
            in_specs.append(pl.BlockSpec((b.shape[0], tn), lambda i, j: (0, j)))
        args += [a, b]
    if res is not None:
        in_specs.append(pl.BlockSpec((tm, tn), lambda i, j: (i, j)))
        args.append(res)
    if dep is not None:
        in_specs.append(pl.BlockSpec(memory_space=pl.ANY))
        args.append(dep)
    return pl.pallas_call(
        body, name=name, grid=(M // tm, N // tn), out_shape=jax.ShapeDtypeStruct((M, N), out_dtype),
        in_specs=in_specs, out_specs=pl.BlockSpec((tm, tn), lambda i, j: (i, j)),
        compiler_params=_cparams(("parallel", "parallel")),
    )(*args)


def _rms_fwd(h, gamma, name):
    T, D = h.shape
    tm = _tile(T)

    def body(h_ref, g_ref, o_ref):
        x = h_ref[...]
        r = lax.rsqrt(jnp.mean(x * x, axis=-1, keepdims=True) + EPS)
        o_ref[...] = (x * r * g_ref[...]).astype(MXU)

    return pl.pallas_call(
        body, name=name, grid=(T // tm,), out_shape=jax.ShapeDtypeStruct((T, D), MXU),
        in_specs=[pl.BlockSpec((tm, D), lambda i: (i, 0)), pl.BlockSpec((1, D), lambda i: (0, 0))],
        out_specs=pl.BlockSpec((tm, D), lambda i: (i, 0)), compiler_params=_cparams(("parallel",)),
    )(h, gamma)


def _rms_bwd_math(x, gamma, dy):
    r = lax.rsqrt(jnp.mean(x * x, axis=-1, keepdims=True) + EPS)
    z = dy * gamma
    dx = r * z - x * (r * r * r) * jnp.mean(z * x, axis=-1, keepdims=True)
    dgamma = jnp.sum(dy * x * r, axis=0, keepdims=True)
    return dx, dgamma


def _rms_bwd(h, gamma, dxn, dres, name):
    T, D = h.shape
    tm = _tile(T)

    def body(h_ref, g_ref, dxn_ref, dres_ref, dh_ref, dh16_ref, dg_ref):
        dx, dgamma = _rms_bwd_math(h_ref[...], g_ref[...], dxn_ref[...])
        dh = dres_ref[...] + dx
        dh_ref[...] = dh
        dh16_ref[...] = dh.astype(MXU)

        @pl.when(pl.program_id(0) == 0)
        def _():
            dg_ref[...] = jnp.zeros_like(dg_ref)

        dg_ref[...] += dgamma

    row = pl.BlockSpec((tm, D), lambda i: (i, 0))
    vec = pl.BlockSpec((1, D), lambda i: (0, 0))
    return pl.pallas_call(
        body, name=name, grid=(T // tm,),
        out_shape=(jax.ShapeDtypeStruct((T, D), F32), jax.ShapeDtypeStruct((T, D), MXU),
                   jax.ShapeDtypeStruct((1, D), F32)),
        in_specs=[row, vec, row, row], out_specs=(row, row, vec), compiler_params=_cparams(("arbitrary",)),
    )(h, gamma, dxn, dres)


def _loss_head(h, gamma, tgt, lo, hi, name):
    T, D = h.shape
    tm = _tile(T)

    def body(h_ref, g_ref, t_ref, loss_ref, dh_ref, dh16_ref, dg_ref):
        i = pl.program_id(0)
        x = h_ref[...]
        r = lax.rsqrt(jnp.mean(x * x, axis=-1, keepdims=True) + EPS)
        y = x * r * g_ref[...]
        rows = i * tm + lax.broadcasted_iota(jnp.int32, (tm, 1), 0)
        valid = jnp.logical_and(rows >= lo, rows < hi)
        diff = jnp.where(valid, y - t_ref[...], 0.0)
        part = 0.5 * jnp.sum(jnp.sum(diff * diff, axis=-1, keepdims=True) / D, axis=0, keepdims=True)
        dx, dgamma = _rms_bwd_math(x, g_ref[...], diff / D)
        dh_ref[...] = dx
        dh16_ref[...] = dx.astype(MXU)

        @pl.when(i == 0)
        def _():
            dg_ref[...] = jnp.zeros_like(dg_ref)
            loss_ref[...] = jnp.zeros_like(loss_ref)

        dg_ref[...] += dgamma
        loss_ref[...] += jnp.broadcast_to(part, loss_ref.shape)

    row = pl.BlockSpec((tm, D), lambda i: (i, 0))
    vec = pl.BlockSpec((1, D), lambda i: (0, 0))
    lsp = pl.BlockSpec((8, 128), lambda i: (0, 0))
    return pl.pallas_call(
        body, name=name, grid=(T // tm,),
        out_shape=(jax.ShapeDtypeStruct((8, 128), F32), jax.ShapeDtypeStruct((T, D), F32),
                   jax.ShapeDtypeStruct((T, D), MXU), jax.ShapeDtypeStruct((1, D), F32)),
        in_specs=[row, vec, row], out_specs=(lsp, row, row, vec), compiler_params=_cparams(("arbitrary",)),
    )(h, gamma, tgt)


def _ffn_tiles(T, Fd):
    return (448 if T % 448 == 0 else _tile(T)), max(c for c in _tiles(Fd) if c <= 1536)


def _col_pieces(tn):
    cuts = list(range(0, tn, 512)) + [tn]
    return [pl.ds(a, b - a) for a, b in zip(cuts[:-1], cuts[1:])]


def _ffn_up(xn, wg, wu, name):
    T, D = xn.shape
    Fd = wg.shape[0]
    tm, tn = _ffn_tiles(T, Fd)

    def body(x_ref, wg_ref, wu_ref, p_ref, q_ref, a_ref):
        x = x_ref[...]
        for c in _col_pieces(tn):
            g = _dot(x, wg_ref[c, :], "NT")
            u = _dot(x, wu_ref[c, :], "NT")
            sg = jax.nn.sigmoid(g)
            q = g * sg
            p_ref[:, c] = (u * (sg + q * (1.0 - sg))).astype(MXU)
            q_ref[:, c] = q.astype(MXU)
            a_ref[:, c] = (q * u).astype(MXU)

    wsp = pl.BlockSpec((tn, D), lambda i, j: (j, 0))
    osp = pl.BlockSpec((tm, tn), lambda i, j: (i, j))
    sds = jax.ShapeDtypeStruct((T, Fd), MXU)
    return pl.pallas_call(
        body, name=name, grid=(T // tm, Fd // tn), out_shape=(sds, sds, sds),
        in_specs=[pl.BlockSpec((tm, D), lambda i, j: (i, 0)), wsp, wsp], out_specs=(osp, osp, osp),
        compiler_params=_cparams(("parallel", "parallel")),
    )(xn, wg, wu)


def _ffn_dact(dy, wd, p, q, scale, name, dep=None):
    T, D = dy.shape
    Fd = wd.shape[0]
    tm, tn = _ffn_tiles(T, Fd)

    def body(dy_ref, wd_ref, p_ref, q_ref, *rest):
        dg_ref, du_ref = rest[-2:]
        dy = dy_ref[...]
        for c in _col_pieces(tn):
            da = scale * _dot(dy, wd_ref[c, :], "NT")
            dg_ref[:, c] = (da * p_ref[:, c].astype(F32)).astype(MXU)
            du_ref[:, c] = (da * q_ref[:, c].astype(F32)).astype(MXU)

    osp = pl.BlockSpec((tm, tn), lambda i, j: (i, j))
    sds = jax.ShapeDtypeStruct((T, Fd), MXU)
    return pl.pallas_call(
        body, name=name, grid=(T // tm, Fd // tn), out_shape=(sds, sds),
        in_specs=[pl.BlockSpec((tm, D), lambda i, j: (i, 0)), pl.BlockSpec((tn, D), lambda i, j: (j, 0)), osp, osp]
        + ([] if dep is None else [pl.BlockSpec(memory_space=pl.ANY)]),
        out_specs=(osp, osp), compiler_params=_cparams(("parallel", "parallel")),
    )(dy, wd, p, q, *([] if dep is None else [dep]))


def _down(v, s):
    return v if s == 0 else pltpu.roll(v, s, 0)


def _up(v, s):
    return v if s == 0 else pltpu.roll(v, v.shape[0] - s, 0)


def _rows(n):
    return lax.broadcasted_iota(jnp.int32, (n, 1), 0)


def _zero_pad_rows(ref, lo_end, hi_start, T):
    ref[pl.ds(0, lo_end), :] = jnp.zeros((lo_end, ref.shape[1]), ref.dtype)
    if T > hi_start:
        ref[pl.ds(hi_start, T - hi_start), :] = jnp.zeros((T - hi_start, ref.shape[1]), ref.dtype)


def _colblock(T, off):
    return pl.BlockSpec((T, 128), lambda j: (0, off + j))


def _vecblock(rows=1):
    return pl.BlockSpec((rows, 128), lambda j: (0, j))


def _pool_lane_consts(n):
    lane = lax.broadcasted_iota(jnp.int32, (n, 256), 1)
    win = jnp.where(lane < 64, 2.0, jnp.where(lane < 128, 4.0, jnp.where(lane < 192, 8.0, 16.0)))
    return lane, win


def _pool_select(lane, s2, s4, s8, s16):
    return jnp.where(lane < 64, s2, jnp.where(lane < 128, s4, jnp.where(lane < 192, s8, s16)))


def _pool_mixed(xh, start):
    s2 = xh + _down(xh, 1)
    s4 = s2 + _down(s2, 2)
    s8 = s4 + _down(s4, 4)
    s16 = s8 + _down(s8, 8)
    n = xh.shape[0] - 16
    lane, _ = _pool_lane_consts(n + 16)
    _, win = _pool_lane_consts(n)
    t1 = (start - CH + 1 + _rows(n)).astype(F32)
    cnt = jnp.minimum(jnp.maximum(t1, 1.0), win)
    return _pool_select(lane, s2, s4, s8, s16)[16:] / cnt - xh[16:]


def _pool_fwd(p, wbd, scale, nreal, real_end, name):
    T = p.shape[0]

    def body(p_ref, w_ref, s_ref, y_ref):
        _zero_pad_rows(y_ref, CH, CH * (1 + nreal), T)

        def chunk(c, carry):
            start = pl.multiple_of(c * CH, CH)
            mixed = _pool_mixed(p_ref[pl.ds(start - 16, CH + 16), :], start)
            y = _dot(mixed, w_ref[...]) * s_ref[...]
            y_ref[pl.ds(start, CH), :] = jnp.where(start + _rows(CH) < real_end, y, 0.0)
            return carry

        _pairs_loop(nreal, lambda c, carry: chunk(c + 1, carry), 0)

    return pl.pallas_call(
        body, name=name, grid=(1,), out_shape=jax.ShapeDtypeStruct((T, 256), F32),
        in_specs=[pl.BlockSpec((T, 256), lambda j: (0, 0)), pl.BlockSpec((256, 256), lambda j: (0, 0)),
                  pl.BlockSpec((1, 256), lambda j: (0, 0))],
        out_specs=pl.BlockSpec((T, 256), lambda j: (0, 0)), compiler_params=_cparams(("arbitrary",)),
    )(p, wbd, scale)


def _pool_bwd(p, wbd, scale, dy, nreal, real_end, name):
    T = p.shape[0]

    def body(p_ref, w_ref, s_ref, dy_ref, dp_ref, dw_ref, ds_ref):
        _zero_pad_rows(dp_ref, CH, CH * (1 + nreal), T)
        dw_ref[...] = jnp.zeros_like(dw_ref)
        ds_ref[...] = jnp.zeros_like(ds_ref)

        def chunk(c, carry):
            start = pl.multiple_of(c * CH, CH)
            mixed = _pool_mixed(p_ref[pl.ds(start - 16, CH + 16), :], start)
            ypre = _dot(mixed, w_ref[...])
            n = CH + 16
            dye = jnp.where(start + _rows(n) < real_end, dy_ref[pl.ds(start, n), :], 0.0)
            dys = dye * s_ref[...]
            ds_ref[...] += jnp.sum(dye[:CH] * ypre, axis=0, keepdims=True)
            dw_ref[...] += _dot(mixed, dys[:CH], "TN")
            dmix = _dot(dys, w_ref[...], "NT")
            lane, win = _pool_lane_consts(n)
            t1 = (start - CH + 1 + _rows(n)).astype(F32)
            z = dmix / jnp.minimum(jnp.maximum(t1, 1.0), win)
            r2 = z + _up(z, 1)
            r4 = r2 + _up(r2, 2)
            r8 = r4 + _up(r4, 4)
            r16 = r8 + _up(r8, 8)
            dp_ref[pl.ds(start, CH), :] = (_pool_select(lane, r2, r4, r8, r16) - dmix)[:CH]
            return carry

        _pairs_loop(nreal, lambda c, carry: chunk(c + 1, carry), 0)

    full = lambda r, c: pl.BlockSpec((r, c), lambda j: (0, 0))
    return pl.pallas_call(
        body, name=name, grid=(1,),
        out_shape=(jax.ShapeDtypeStruct((T, 256), F32), jax.ShapeDtypeStruct((256, 256), F32),
                   jax.ShapeDtypeStruct((1, 256), F32)),
        in_specs=[full(T, 256), full(256, 256), full(1, 256), full(T, 256)],
        out_specs=(full(T, 256), full(256, 256), full(1, 256)), compiler_params=_cparams(("arbitrary",)),
    )(p, wbd, scale, dy)


def _hgrn_chunk(St, qr, fr, ir, gr, l0, l1, gn):
    rows = lax.broadcasted_iota(jnp.int32, (HG, HG), 0)
    cols = lax.broadcasted_iota(jnp.int32, (HG, HG), 1)
    causal = rows >= cols
    ltri = causal.astype(F32)
    lb = jax.nn.sigmoid(l0 - l1)
    sg = jax.nn.sigmoid(fr)
    logf = jnp.log(lb + (1.0 - lb) * sg)
    kk = (1.0 - lb) * (1.0 - sg)
    q = qr * jax.nn.sigmoid(qr)
    b = jnp.dot(ltri, logf, precision=lax.Precision.HIGH, preferred_element_type=F32)
    bl = jnp.sum(logf, axis=0, keepdims=True)
    bm = jnp.sum(jnp.where(_rows(HG) <= HG // 2, logf, 0.0), axis=0, keepdims=True)
    o = _dotf(q * jnp.exp(b), St, "NT")
    A = _dotf(q * jnp.exp(b - bm), kk * jnp.exp(bm - b), "NT")
    o = o + _dotf(jnp.where(causal, A, 0.0), ir)
    St_new = St * jnp.exp(bl) + _dotf(ir, kk * jnp.exp(bl - b), "TN")
    on = o * lax.rsqrt(jnp.mean(o * o, axis=-1, keepdims=True) + EPS) * gn
    return St_new, on * (gr * jax.nn.sigmoid(gr))


def _pairs_loop(n, step, init):
    u = 3 if n % 3 == 0 else 2 if n % 2 == 0 else 1

    def body(i, carry):
        for j in range(u):
            carry = step(u * i + j, carry)
        return carry

    return lax.fori_loop(0, n // u, body, init)


def _hgrn_specs(T):
    return [_colblock(T, 2), _colblock(T, 8), _colblock(T, 14), _colblock(T, 20), _vecblock(), _vecblock(),
            pl.BlockSpec((1, 128), lambda j: (0, 0))]


def _hgrn_fwd(p, l0, l1, gn, nreal, real_end, name):
    T = p.shape[0]
    nch = nreal * (CH // HG)

    def body(q_ref, f_ref, i_ref, g_ref, l0_ref, l1_ref, gn_ref, y_ref, s_ref):
        _zero_pad_rows(y_ref, CH, CH * (1 + nreal), T)

        def chunk(c, St):
            start = pl.multiple_of(CH + c * HG, HG)
            sl = pl.ds(start, HG)
            s_ref[0, c] = St
            St_new, y = _hgrn_chunk(St, q_ref[sl, :], f_ref[sl, :], i_ref[sl, :], g_ref[sl, :], l0_ref[...],
                                    l1_ref[...], gn_ref[...])
            y_ref[sl, :] = jnp.where(start + _rows(HG) < real_end, y, 0.0)
            return St_new

        _pairs_loop(nch, chunk, jnp.zeros((128, 128), F32))

    return pl.pallas_call(
        body, name=name, grid=(6,),
        out_shape=(jax.ShapeDtypeStruct((T, 768), F32), jax.ShapeDtypeStruct((6, nch, 128, 128), F32)),
        in_specs=_hgrn_specs(T),
        out_specs=(_colblock(T, 0), pl.BlockSpec((1, nch, 128, 128), lambda j: (j, 0, 0, 0))),
        compiler_params=_cparams(("parallel",)),
    )(p, p, p, p, l0, l1, gn)


def _hgrn_bwd(p, l0, l1, gn, states, dy, nreal, real_end, name):
    T = p.shape[0]
    nch = nreal * (CH // HG)

    def body(q_ref, f_ref, i_ref, g_ref, l0_ref, l1_ref, gn_ref, s_ref, dy_ref,
             dq_ref, df_ref, di_ref, dg_ref, dl0_ref, dl1_ref, dgn_ref):
        for r in (dq_ref, df_ref, di_ref, dg_ref):
            _zero_pad_rows(r, CH, CH * (1 + nreal), T)

        def chunk(k, carry):
            dSt, a0, a1, agn = carry
            c = nch - 1 - k
            start = pl.multiple_of(CH + c * HG, HG)
            sl = pl.ds(start, HG)
            _, vjp = jax.vjp(_hgrn_chunk, s_ref[0, c], q_ref[sl, :], f_ref[sl, :], i_ref[sl, :], g_ref[sl, :],
                             l0_ref[...], l1_ref[...], gn_ref[...])
            dyc = jnp.where(start + _rows(HG) < real_end, dy_ref[sl, :], 0.0)
            dS, dq, df, di, dg, d0, d1, dgn = vjp((dSt, dyc))
            dq_ref[sl, :] = dq
            df_ref[sl, :] = df
            di_ref[sl, :] = di
            dg_ref[sl, :] = dg
            return dS, a0 + d0, a1 + d1, agn + dgn

        z = jnp.zeros((1, 128), F32)
        _, a0, a1, agn = _pairs_loop(nch, chunk, (jnp.zeros((128, 128), F32), z, z, z))
        dl0_ref[...] = a0
        dl1_ref[...] = a1

        @pl.when(pl.program_id(0) == 0)
        def _():
            dgn_ref[...] = jnp.zeros_like(dgn_ref)

        dgn_ref[...] += agn

    big = jax.ShapeDtypeStruct((T, 768), F32)
    vec = jax.ShapeDtypeStruct((1, 768), F32)
    return pl.pallas_call(
        body, name=name, grid=(6,),
        out_shape=(big, big, big, big, vec, vec, jax.ShapeDtypeStruct((1, 128), F32)),
        in_specs=_hgrn_specs(T) + [pl.BlockSpec((1, nch, 128, 128), lambda j: (j, 0, 0, 0)), _colblock(T, 2)],
        out_specs=(_colblock(T, 0), _colblock(T, 0), _colblock(T, 0), _colblock(T, 0), _vecblock(), _vecblock(),
                   pl.BlockSpec((1, 128), lambda j: (0, 0))),
        compiler_params=_cparams(("arbitrary",), 60 * 2 ** 20),
    )(p, p, p, p, l0, l1, gn, states, dy)


def _glu(a, b):
    return a * jax.nn.sigmoid(b)


def _conv_post(cv, ln_g, ln_b):
    mu = jnp.mean(cv, axis=-1, keepdims=True)
    d = cv - mu
    var = jnp.mean(d * d, axis=-1, keepdims=True)
    un = d * lax.rsqrt(var + EPS) * ln_g + ln_b
    return un * jax.nn.sigmoid(un)


def _causal_conv(uh, w_ref, width, halo):
    acc = None
    for j in range(width):
        term = _down(uh, width - 1 - j) * w_ref[pl.ds(j, 1), :]
        acc = term if acc is None else acc + term
    return acc[halo:]


def _conf_fwd(p, cw, cb, lg, lb, nreal, real_end, name):
    T = p.shape[0]

    def body(a_ref, b_ref, w_ref, cb_ref, lg_ref, lb_ref, y_ref):
        _zero_pad_rows(y_ref, CH, CH * (1 + nreal), T)

        def chunk(c, carry):
            start = pl.multiple_of(c * CH, CH)
            ext = pl.ds(start - 32, CH + 32)
            cv = _causal_conv(_glu(a_ref[ext, :], b_ref[ext, :]), w_ref, CONV_W, 32) + cb_ref[...]
            y = _conv_post(cv, lg_ref[...], lb_ref[...])
            y_ref[pl.ds(start, CH), :] = jnp.where(start + _rows(CH) < real_end, y, 0.0)
            return carry

        _pairs_loop(nreal, lambda c, carry: chunk(c + 1, carry), 0)

    return pl.pallas_call(
        body, name=name, grid=(4,), out_shape=jax.ShapeDtypeStruct((T, 512), F32),
        in_specs=[_colblock(T, 0), _colblock(T, 4), _vecblock(32), _vecblock(), _vecblock(), _vecblock()],
        out_specs=_colblock(T, 0), compiler_params=_cparams(("parallel",)),
    )(p, p, cw, cb, lg, lb)


def _conf_bwd(p, cw, cb, lg, lb, dy, nreal, real_end, name):
    T = p.shape[0]

    def body(a_ref, b_ref, w_ref, cb_ref, lg_ref, lb_ref, dy_ref, da_ref, db_ref, dw_ref, dcb_ref, dlg_ref, dlb_ref):
        _zero_pad_rows(da_ref, CH, CH * (1 + nreal), T)
        _zero_pad_rows(db_ref, CH, CH * (1 + nreal), T)
        for r in (dw_ref, dcb_ref, dlg_ref, dlb_ref):
            r[...] = jnp.zeros_like(r)

        def chunk(c, carry):
            start = pl.multiple_of(c * CH, CH)
            ext = pl.ds(start - 32, CH + 64)
            ue = _glu(a_ref[ext, :], b_ref[ext, :])
            cv = _causal_conv(ue, w_ref, CONV_W, 32) + cb_ref[...]
            dye = jnp.where(start + _rows(CH + 32) < real_end, dy_ref[pl.ds(start, CH + 32), :], 0.0)
            _, vjp_cur = jax.vjp(_conv_post, cv[:CH], lg_ref[...], lb_ref[...])
            dc_cur, dlg, dlb = vjp_cur(dye[:CH])
            _, vjp_halo = jax.vjp(_conv_post, cv[CH:], lg_ref[...], lb_ref[...])
            dce = jnp.concatenate([dc_cur, vjp_halo(dye[CH:])[0]], axis=0)
            dlg_ref[...] += dlg
            dlb_ref[...] += dlb
            dcb_ref[...] += jnp.sum(dc_cur, axis=0, keepdims=True)
            du = None
            for j in range(CONV_W):
                w_j = w_ref[pl.ds(j, 1), :]
                term = _up(dce, CONV_W - 1 - j)[:CH] * w_j
                du = term if du is None else du + term
                dw_ref[pl.ds(j, 1), :] += jnp.sum(dc_cur * _up(ue, 2 + j)[:CH], axis=0, keepdims=True)
            cur = pl.ds(start, CH)
            _, vjp_glu = jax.vjp(_glu, a_ref[cur, :], b_ref[cur, :])
            da, db = vjp_glu(du)
            da_ref[cur, :] = da
            db_ref[cur, :] = db
            return carry

        _pairs_loop(nreal, lambda c, carry: chunk(c + 1, carry), 0)

    big = jax.ShapeDtypeStruct((T, 512), F32)
    vec = jax.ShapeDtypeStruct((1, 512), F32)
    return pl.pallas_call(
        body, name=name, grid=(4,), out_shape=(big, big, jax.ShapeDtypeStruct((32, 512), F32), vec, vec, vec),
        in_specs=[_colblock(T, 0), _colblock(T, 4), _vecblock(32), _vecblock(), _vecblock(), _vecblock(),
                  _colblock(T, 0)],
        out_specs=(_colblock(T, 0), _colblock(T, 0), _vecblock(32), _vecblock(), _vecblock(), _vecblock()),
        compiler_params=_cparams(("parallel",)),
    )(p, p, cw, cb, lg, lb, dy)


def _softplus_neg(lam):
    e = jnp.exp(-lam)
    small = e * (1.0 - e * (0.5 - e * (1.0 / 3.0 - e * 0.25)))
    return jnp.where(e < 0.02, small, jnp.log(1.0 + e))


def _one_minus_exp(x):
    series = -x * (1.0 + x * (0.5 + x * (1.0 / 6.0 + x * (1.0 / 24.0 + x * (1.0 / 120.0)))))
    return jnp.where(x > -0.05, series, 1.0 - jnp.exp(x))


def _lru_pre(u, wa, wx, ba, bx, lam, first):
    r = jax.nn.sigmoid(_dot(u, wa) + ba)
    i = jax.nn.sigmoid(_dot(u, wx) + bx)
    log_a = -LRU_C * r * _softplus_neg(lam)
    a = jnp.exp(log_a)
    mult = jnp.sqrt(_one_minus_exp(2.0 * log_a))
    return a, jnp.where(first, 1.0, mult) * (i * u)


def _gelu_gate(gate, h):
    inner = math.sqrt(2.0 / math.pi) * (gate + 0.044715 * (gate * gate * gate))
    return 0.5 * gate * (1.0 + jnp.tanh(inner)) * h


def _lru_specs(T):
    mat = pl.BlockSpec((1, 128, 128), lambda j: (j, 0, 0))
    return [_colblock(T, 8), _colblock(T, 12), _vecblock(8), _vecblock(), mat, mat, _vecblock(), _vecblock(),
            _vecblock()]


def _lru_fwd(p, cw, cb, wa, wx, ba, bx, lam, nreal, real_end, name):
    T = p.shape[0]

    def body(x_ref, g_ref, w_ref, cb_ref, wa_ref, wx_ref, ba_ref, bx_ref, lam_ref, y_ref, h_ref):
        _zero_pad_rows(y_ref, CH, CH * (1 + nreal), T)
        _zero_pad_rows(h_ref, CH, CH * (1 + nreal), T)
        rows = _rows(CH)

        def chunk(c, hprev):
            start = pl.multiple_of(c * CH, CH)
            u = _causal_conv(x_ref[pl.ds(start - 8, CH + 8), :], w_ref, LRU_W, 8) + cb_ref[...]
            A, B = _lru_pre(u, wa_ref[0], wx_ref[0], ba_ref[...], bx_ref[...], lam_ref[...], start + rows == CH)
            s = 1
            while s < CH:
                B = A * jnp.where(rows >= s, _down(B, s), 0.0) + B
                A = A * jnp.where(rows >= s, _down(A, s), 1.0)
                s *= 2
            h = B + A * hprev
            cur = pl.ds(start, CH)
            h_ref[cur, :] = h
            y_ref[cur, :] = jnp.where(start + rows < real_end, _gelu_gate(g_ref[cur, :], h), 0.0)
            return jnp.sum(jnp.where(rows == CH - 1, h, 0.0), axis=0, keepdims=True)

        _pairs_loop(nreal, lambda c, carry: chunk(c + 1, carry), jnp.zeros((1, 128), F32))

    big = jax.ShapeDtypeStruct((T, 512), F32)
    return pl.pallas_call(
        body, name=name, grid=(4,), out_shape=(big, big), in_specs=_lru_specs(T),
        out_specs=(_colblock(T, 0), _colblock(T, 0)), compiler_params=_cparams(("parallel",)),
    )(p, p, cw, cb, wa, wx, ba, bx, lam)


def _lru_bwd(p, cw, cb, wa, wx, ba, bx, lam, hs, dy, nreal, real_end, name):
    T = p.shape[0]

    def body(x_ref, g_ref, w_ref, cb_ref, wa_ref, wx_ref, ba_ref, bx_ref, lam_ref, h_ref, dy_ref,
             dx_ref, dgate_ref, dw_ref, dcb_ref, dwa_ref, dwx_ref, dba_ref, dbx_ref, dlam_ref):
        _zero_pad_rows(dx_ref, CH, CH * (1 + nreal), T)
        _zero_pad_rows(dgate_ref, CH, CH * (1 + nreal), T)
        for r in (dw_ref, dcb_ref, dwa_ref, dwx_ref, dba_ref, dbx_ref, dlam_ref):
            r[...] = jnp.zeros_like(r)
        rows = _rows(CH)

        def chunk(k, carry):
            cdh, du_head = carry
            c = nreal - k
            start = pl.multiple_of(c * CH, CH)
            cur = pl.ds(start, CH)
            xe = x_ref[pl.ds(start - 8, CH + 8), :]
            u = _causal_conv(xe, w_ref, LRU_W, 8) + cb_ref[...]
            first = start + rows == CH
            (a, _), vjp_pre = jax.vjp(lambda uu, m1, m2, b1, b2, ll: _lru_pre(uu, m1, m2, b1, b2, ll, first),
                                      u, wa_ref[0], wx_ref[0], ba_ref[...], bx_ref[...], lam_ref[...])
            h = h_ref[cur, :]
            hm1 = _down(h_ref[pl.ds(start - 8, CH + 8), :], 1)[8:]
            _, vjp_post = jax.vjp(_gelu_gate, g_ref[cur, :], h)
            dgate, D = vjp_post(jnp.where(start + rows < real_end, dy_ref[cur, :], 0.0))
            dgate_ref[cur, :] = dgate
            D = D + jnp.where(rows == CH - 1, cdh, 0.0)
            C = jnp.where(rows < CH - 1, _up(a, 1), 0.0)
            s = 1
            while s < CH:
                D = D + C * jnp.where(rows + s < CH, _up(D, s), 0.0)
                C = C * jnp.where(rows + s < CH, _up(C, s), 1.0)
                s *= 2
            du, dwa, dwx, dba, dbx, dlam = vjp_pre((D * hm1, D))
            dwa_ref[0] += dwa
            dwx_ref[0] += dwx
            dba_ref[...] += dba
            dbx_ref[...] += dbx
            dlam_ref[...] += dlam
            dcb_ref[...] += jnp.sum(du, axis=0, keepdims=True)
            due = jnp.concatenate([du, du_head], axis=0)
            dx = None
            for j in range(LRU_W):
                term = _up(due, LRU_W - 1 - j)[:CH] * w_ref[pl.ds(j, 1), :]
                dx = term if dx is None else dx + term
                dw_ref[pl.ds(j, 1), :] += jnp.sum(du * _up(xe, 8 - (LRU_W - 1) + j)[:CH], axis=0, keepdims=True)
            dx_ref[cur, :] = dx
            return jnp.sum(jnp.where(rows == 0, a * D, 0.0), axis=0, keepdims=True), du[:8]

        _pairs_loop(nreal, chunk, (jnp.zeros((1, 128), F32), jnp.zeros((8, 128), F32)))

    big = jax.ShapeDtypeStruct((T, 512), F32)
    vec = jax.ShapeDtypeStruct((1, 512), F32)
    mat = jax.ShapeDtypeStruct((4, 128, 128), F32)
    matspec = pl.BlockSpec((1, 128, 128), lambda j: (j, 0, 0))
    return pl.pallas_call(
        body, name=name, grid=(4,),
        out_shape=(big, big, jax.ShapeDtypeStruct((8, 512), F32), vec, mat, mat, vec, vec, vec),
        in_specs=_lru_specs(T) + [_colblock(T, 0), _colblock(T, 4)],
        out_specs=(_colblock(T, 0), _colblock(T, 0), _vecblock(8), _vecblock(), matspec, matspec, _vecblock(),
                   _vecblock(), _vecblock()),
        compiler_params=_cparams(("parallel",)),
    )(p, p, cw, cb, wa, wx, ba, bx, lam, hs, dy)


def _ffn_forward(h, gamma, wg, wu, wd, tag):
    xn = _rms_fwd(h, gamma, f"rms_fwd_{tag}")
    g, u, a = _ffn_up(xn, wg, wu, f"ffn_up_{tag}")
    if callable(wd):
        wd = wd(a)
    out = _mm([(a, wd)], "NN", f"ffn_down_{tag}", res=h, res_scale=0.5)
    return out, (h, xn, g, u, a)


def _after(w, tok):
    return w if tok is None else w + tok[0, 0].astype(w.dtype)


def _ffn_backward(saved, gamma, wg, wu, wd, dout, tok, tag, emit_one=None):
    h, xn, p, q, a = saved
    dout, dout16 = dout
    dwd = _mm([(a, dout16)], "TN", f"ffn_dwd_{tag}", res_scale=0.5, out_dtype=MXU, dep=tok)
    if emit_one is not None:
        tok = emit_one('wd', dwd)
    dg, du = _ffn_dact(dout16, wd, p, q, 0.5, f"ffn_dact_{tag}", dep=tok)
    dwg = _mm([(dg, xn)], "TN", f"ffn_dwg_{tag}", out_dtype=MXU)
    if emit_one is not None:
        tok = emit_one('wg', dwg)
    dwu = _mm([(du, xn)], "TN", f"ffn_dwu_{tag}", out_dtype=MXU, dep=tok if emit_one is not None else None)
    if emit_one is not None:
        tok = emit_one('wu', dwu)
    dxn = _mm([(dg, wg), (du, wu)], "NN", f"ffn_dxn_{tag}", dep=tok if emit_one is not None else None)
    dh, dh16, dgamma = _rms_bwd(h, gamma, dxn, dout, f"rms_bwd_{tag}")
    return (dh, dh16), dgamma, dwg, dwu, dwd


def _blockdiag(w, per):
    n, k, _ = w.shape
    out = jnp.zeros((n // per, per * k, per * k), w.dtype)
    for i in range(per):
        out = out.at[:, i * k:(i + 1) * k, i * k:(i + 1) * k].set(w[i::per])
    return out


def _blockdiag_grad(g, per, k):
    parts = [g[:, i * k:(i + 1) * k, i * k:(i + 1) * k] for i in range(per)]
    return jnp.stack(parts, axis=1).reshape(-1, k, k)


def _local_step(x, tgt, W, fetch, emit):
    fetch(0, x)
    seq, D = x.shape
    lr = N_META + seq
    nreal = -(-lr // CH)
    T = CH * (nreal + 2)
    if T > 640 and T % 640:
        T += 640 - T % 640
    lo, real_end = CH + N_META, CH + lr
    zf = lambda n: jnp.zeros((n, D), F32)
    h0 = jnp.concatenate([zf(CH), W['meta_tokens'], x, zf(T - real_end)], axis=0)
    tgt_p = jnp.concatenate([zf(lo), tgt, zf(T - real_end)], axis=0)
    row = lambda v: v.reshape(1, -1)
    G = {}

    h = h0
    saved = []
    for l in range(2):
        wd1 = W['ffn1_wd', l] if l else (lambda after: (fetch(1, after), W['ffn1_wd', 0])[1])
        h, s1 = _ffn_forward(h, row(W['ffn1_norm'][l]), W['ffn1_wg', l], W['ffn1_wu', l], wd1, f"a{l}")
        hm = h
        fetch(3 * l + 2, hm)
        xn = _rms_fwd(hm, row(W['mix_norm'][l]), f"rms_fwd_mix{l}")
        if l == 0:
            p = _mm([(xn, W['w_in_even'])], "NT", "in_even")
            wbd = _blockdiag(W['pool_w'][0], 4)[0]
            l0, l1 = row(W['hgrn_lb_logits'][0]), row(W['hgrn_lb_logits'][1])
            ya = _pool_fwd(p, wbd, W['pool_scale'], nreal, real_end, "pool_fwd")
            yb, states = _hgrn_fwd(p, l0, l1, W['hgrn_gnorm'], nreal, real_end, "hgrn_fwd")
            wo = W['w_out_even']
            h = _mm([(ya, wo[:256]), (yb, wo[256:])], "NN", "out_even", res=hm)
            sm = (hm, xn, p, wbd, l0, l1, ya, yb, states)
        else:
            p = _mm([(xn, W['w_in_odd'])], "NT", "in_odd")
            cw = jnp.pad(W['conv_w'][0], ((0, 1), (0, 0)))
            lw = jnp.pad(W['lru_conv_w'][0], ((0, 4), (0, 0)))
            wa, wx = _blockdiag(W['lru_wa'][0], 2), _blockdiag(W['lru_wx'][0], 2)
            yc = _conf_fwd(p, cw, W['conv_b'], W['conv_ln_g'], W['conv_ln_b'], nreal, real_end, "conf_fwd")
            yd, hs = _lru_fwd(p, lw, W['lru_conv_b'], wa, wx, W['lru_ba'], W['lru_bx'], W['lru_lambda'], nreal,
                              real_end, "lru_fwd")
            wo = W['w_out_odd']
            h = _mm([(yc, wo[:512]), (yd, wo[512:])], "NN", "out_odd", res=hm)
            sm = (hm, xn, p, cw, lw, wa, wx, yc, yd, hs)
        fetch(3 * l + 3, h)
        h, s2 = _ffn_forward(h, row(W['ffn2_norm'][l]), W['ffn2_wg', l], W['ffn2_wu', l], W['ffn2_wd', l], f"b{l}")
        if l == 0:
            fetch(4, h)
        saved.append((s1, sm, s2))

    loss8, dh, dh16, dfin = _loss_head(h, row(W['final_norm']), tgt_p, lo, real_end, "loss_head")
    dh = (dh, dh16)
    G['final_norm'] = dfin[0]

    per_layer = {k: [None, None] for k in ('ffn1_norm', 'mix_norm', 'ffn2_norm')}
    tok = None
    for l in (1, 0):
        s1, sm, s2 = saved[l]
        dh, dn, dwg, dwu, dwd = _ffn_backward(s2, row(W['ffn2_norm'][l]), W['ffn2_wg', l], W['ffn2_wu', l],
                                              W['ffn2_wd', l], dh, tok, f"b{l}")
        per_layer['ffn2_norm'][l] = dn[0]
        tok = emit(f"ffn2_{l}", [('ffn2_wg', l, dwg), ('ffn2_wu', l, dwu), ('ffn2_wd', l, dwd)])
        if l == 0:
            hm, xn, p, wbd, l0, l1, ya, yb, states = sm
            wo, wi = W['w_out_even'], W['w_in_even']
            dwo = jnp.concatenate([_mm([(ya, dh[1])], "TN", "dwo_even_a", out_dtype=MXU),
                                   _mm([(yb, dh[1])], "TN", "dwo_even_b", out_dtype=MXU)], axis=0)
            dy = _mm([(dh[1], wo)], "NT", "dy_even", dep=tok)
            dpp, dwbd, dsc = _pool_bwd(p, wbd, W['pool_scale'], dy, nreal, real_end, "pool_bwd")
            dq, df, di, dg, dl0, dl1, dgn = _hgrn_bwd(p, l0, l1, W['hgrn_gnorm'], states, dy, nreal, real_end,
                                                      "hgrn_bwd")
            G['pool_w'] = _blockdiag_grad(dwbd[None], 4, 64)[None]
            G['pool_scale'] = dsc
            G['hgrn_lb_logits'] = jnp.concatenate([dl0, dl1], axis=0)
            G['hgrn_gnorm'] = dgn
            parts = [dpp, dq, df, di, dg]
            offs = [0, 256, 1024, 1792, 2560, 3328]
            dwi = jnp.concatenate(
                [_mm([(dpart, xn)], "TN", f"dwi_even_{k}", out_dtype=MXU) for k, dpart in enumerate(parts)], axis=0)
            dxn = _mm([(dpart, wi[offs[k]:offs[k + 1]]) for k, dpart in enumerate(parts)], "NN", "dxn_even")
            tok = emit("even", [('w_in_even', None, dwi), ('w_out_even', None, dwo)])
        else:
            hm, xn, p, cw, lw, wa, wx, yc, yd, hs = sm
            wo, wi = W['w_out_odd'], W['w_in_odd']
            dwo = jnp.concatenate([_mm([(yc, dh[1])], "TN", "dwo_odd_c", out_dtype=MXU),
                                   _mm([(yd, dh[1])], "TN", "dwo_odd_d", out_dtype=MXU)], axis=0)
            dy = _mm([(dh[1], wo)], "NT", "dy_odd", dep=tok)
            da, db, dcw, dcb, dlg, dlb = _conf_bwd(p, cw, W['conv_b'], W['conv_ln_g'], W['conv_ln_b'], dy, nreal,
                                                   real_end, "conf_bwd")
            dx, dgate, dlw, dlcb, dwa, dwx, dba, dbx, dlam = _lru_bwd(
                p, lw, W['lru_conv_b'], wa, wx, W['lru_ba'], W['lru_bx'], W['lru_lambda'], hs, dy, nreal, real_end,
                "lru_bwd")
            G['conv_w'], G['conv_b'], G['conv_ln_g'], G['conv_ln_b'] = dcw[None, :CONV_W], dcb, dlg, dlb
            G['lru_conv_w'], G['lru_conv_b'] = dlw[None, :LRU_W], dlcb
            G['lru_wa'] = _blockdiag_grad(dwa, 2, 64)[None]
            G['lru_wx'] = _blockdiag_grad(dwx, 2, 64)[None]
            G['lru_ba'], G['lru_bx'], G['lru_lambda'] = dba, dbx, dlam
            parts = [da, db, dx, dgate]
            dwi = jnp.concatenate(
                [_mm([(dpart, xn)], "TN", f"dwi_odd_{k}", out_dtype=MXU) for k, dpart in enumerate(parts)], axis=0)
            dxn = _mm([(dpart, wi[512 * k:512 * (k + 1)]) for k, dpart in enumerate(parts)], "NN", "dxn_odd")
            tok = emit("odd", [('w_in_odd', None, dwi), ('w_out_odd', None, dwo)])
        dh, dh16, dn = _rms_bwd(hm, _after(row(W['mix_norm'][l]), tok), dxn, dh[0], f"rms_bwd_mix{l}")
        dh = (dh, dh16)
        per_layer['mix_norm'][l] = dn[0]
        one = None if l == 1 else (lambda sfx, g: emit(f"ffn1_0_{sfx}", [('ffn1_' + sfx, 0, g)]))
        dh, dn, dwg, dwu, dwd = _ffn_backward(s1, row(W['ffn1_norm'][l]), W['ffn1_wg', l], W['ffn1_wu', l],
                                              W['ffn1_wd', l], dh, None, f"a{l}", one)
        per_layer['ffn1_norm'][l] = dn[0]
        if l == 1:
            tok = emit("ffn1_1", [('ffn1_wg', l, dwg), ('ffn1_wu', l, dwu), ('ffn1_wd', l, dwd)])
    for k, v in per_layer.items():
        G[k] = jnp.stack(v, axis=0)
    G['meta_tokens'] = dh[0][CH:lo]
    return loss8[0, 0], dh[0][lo:real_end], G


def _pack(arrs):
    flat = jnp.concatenate([a.reshape(-1).astype(F32) for a in arrs])
    n = flat.shape[0]
    padded = -(-n // 1024) * 1024
    return jnp.pad(flat, (0, padded - n)).reshape(-1, 128)


def _unpack(packed, shapes):
    flat = packed.reshape(-1)
    out, off = [], 0
    for s in shapes:
        n = math.prod(s)
        out.append(flat[off:off + n].reshape(s))
        off += n
    return out


def _pack8(arrs):
    flat = jnp.concatenate([a.reshape(NDEV, -1).astype(F32) for a in arrs], axis=1)
    n = flat.shape[1]
    padded = -(-n // 1024) * 1024
    return jnp.pad(flat, ((0, 0), (0, padded - n))).reshape(NDEV, -1, 128)


def _unpack8(packed, shapes):
    flat = packed.reshape(NDEV, -1)
    out, off = [], 0
    for s in shapes:
        n = math.prod(s)
        out.append(flat[:, off:off + n].reshape((NDEV,) + tuple(s)))
        off += n
    return out


def _to_full(gathered, axis):
    s = gathered.shape[1:]
    return jnp.moveaxis(gathered, 0, axis).reshape(s[:axis] + (NDEV * s[axis],) + s[axis + 1:])


def _to_slots(full, axis):
    s = full.shape
    return jnp.moveaxis(full.reshape(s[:axis] + (NDEV, s[axis] // NDEV) + s[axis + 1:]), axis, 0)


def kernel(x, meta_tokens, ffn1_norm, ffn1_wg, ffn1_wu, ffn1_wd, mix_norm, ffn2_norm, ffn2_wg, ffn2_wu, ffn2_wd, w_in_even, pool_w, pool_scale, hgrn_lb_logits, hgrn_gnorm, w_out_even, w_in_odd, conv_w, conv_b, conv_ln_g, conv_ln_b, lru_conv_w, lru_conv_b, lru_wa, lru_ba, lru_wx, lru_bx, lru_lambda, w_out_odd, final_norm, loss_target, m_meta_tokens, m_ffn1_norm, m_ffn1_wg, m_ffn1_wu, m_ffn1_wd, m_mix_norm, m_ffn2_norm, m_ffn2_wg, m_ffn2_wu, m_ffn2_wd, m_w_in_even, m_pool_w, m_pool_scale, m_hgrn_lb_logits, m_hgrn_gnorm, m_w_out_even, m_w_in_odd, m_conv_w, m_conv_b, m_conv_ln_g, m_conv_ln_b, m_lru_conv_w, m_lru_conv_b, m_lru_wa, m_lru_ba, m_lru_wx, m_lru_bx, m_lru_lambda, m_w_out_odd, m_final_norm, v_meta_tokens, v_ffn1_norm, v_ffn1_wg, v_ffn1_wu, v_ffn1_wd, v_mix_norm, v_ffn2_norm, v_ffn2_wg, v_ffn2_wu, v_ffn2_wd, v_w_in_even, v_pool_w, v_pool_scale, v_hgrn_lb_logits, v_hgrn_gnorm, v_w_out_even, v_w_in_odd, v_conv_w, v_conv_b, v_conv_ln_g, v_conv_ln_b, v_lru_conv_w, v_lru_conv_b, v_lru_wa, v_lru_ba, v_lru_wx, v_lru_bx, v_lru_lambda, v_w_out_odd, v_final_norm):
    args = (meta_tokens, ffn1_norm, ffn1_wg, ffn1_wu, ffn1_wd, mix_norm, ffn2_norm, ffn2_wg, ffn2_wu, ffn2_wd, w_in_even, pool_w, pool_scale, hgrn_lb_logits, hgrn_gnorm, w_out_even, w_in_odd, conv_w, conv_b, conv_ln_g, conv_ln_b, lru_conv_w, lru_conv_b, lru_wa, lru_ba, lru_wx, lru_bx, lru_lambda, w_out_odd, final_norm)
    margs = (m_meta_tokens, m_ffn1_norm, m_ffn1_wg, m_ffn1_wu, m_ffn1_wd, m_mix_norm, m_ffn2_norm, m_ffn2_wg, m_ffn2_wu, m_ffn2_wd, m_w_in_even, m_pool_w, m_pool_scale, m_hgrn_lb_logits, m_hgrn_gnorm, m_w_out_even, m_w_in_odd, m_conv_w, m_conv_b, m_conv_ln_g, m_conv_ln_b, m_lru_conv_w, m_lru_conv_b, m_lru_wa, m_lru_ba, m_lru_wx, m_lru_bx, m_lru_lambda, m_w_out_odd, m_final_norm)
    vargs = (v_meta_tokens, v_ffn1_norm, v_ffn1_wg, v_ffn1_wu, v_ffn1_wd, v_mix_norm, v_ffn2_norm, v_ffn2_wg, v_ffn2_wu, v_ffn2_wd, v_w_in_even, v_pool_w, v_pool_scale, v_hgrn_lb_logits, v_hgrn_gnorm, v_w_out_even, v_w_in_odd, v_conv_w, v_conv_b, v_conv_ln_g, v_conv_ln_b, v_lru_conv_w, v_lru_conv_b, v_lru_wa, v_lru_ba, v_lru_wx, v_lru_bx, v_lru_lambda, v_w_out_odd, v_final_norm)
    Wl = dict(zip(W_NAMES, args))
    Ml = dict(zip(W_NAMES, margs))
    Vl = dict(zip(W_NAMES, vargs))

    small_shapes = [Wl[n].shape for n in SMALL_SHARDED]
    ffn = lambda p, l: [(p + s, l) for s in ('_wg', '_wu', '_wd')]
    mix = lambda p: [('w_in_' + p, None), ('w_out_' + p, None)]
    ggroups = [ffn('ffn1', 0)[:2], ffn('ffn1', 0)[2:], mix('even'), ffn('ffn2', 0), ffn('ffn1', 1), mix('odd'),
               ffn('ffn2', 1)]
    colsharded = lambda n: SHARD_AXIS[n] == 2

    def shard(n, l):
        w = Wl[n][0 if l is None else l].astype(MXU)
        return w.T if colsharded(n) else w

    srcs = [[shard(n, l) for n, l in g] for g in ggroups]
    srcs[0] = [_pack([Wl[n] for n in SMALL_SHARDED])] + srcs[0]
    handles, _ = _exchange_start(srcs, True, "gather_start")
    W = {n: Wl[n] for n in REPLICATED}

    def fetch(k, after):
        lands = _exchange_wait(handles[k], True, after, f"gather_wait_{k}")
        if k == 0:
            for n, g in zip(SMALL_SHARDED, _unpack8(lands[0], small_shapes)):
                W[n] = _to_full(g, SHARD_AXIS[n])
            lands = lands[1:]
        for (n, l), g in zip(ggroups[k], lands):
            W[n if l is None else (n, l)] = g.reshape(-1, g.shape[-1])

    pending = []

    def emit(tag, grads):
        slots = [g.astype(MXU).reshape(NDEV, -1, g.shape[-1]) for _, _, g in grads]
        hs, token = _exchange_start([slots], False, f"scatter_start_{tag}")
        pending.append((tag, hs[0], [(n, l) for n, l, _ in grads]))
        return token

    loss_part, grad_x, G = _local_step(x[0], loss_target[0], W, fetch, emit)

    small_slots = [_to_slots(G[n].astype(F32), SHARD_AXIS[n]) for n in SMALL_SHARDED]
    send = [_pack8(small_slots), _pack([G[n] for n in REPLICATED]),
            jnp.broadcast_to(loss_part, (8, 128))]
    got = _exchange(send, [False, True, True], "scatter_small")
    loss = jnp.sum(got[2][:, 0, 0])
    recv = {}
    for tag, handle, keys in pending:
        for key, r in zip(keys, _exchange_wait(handle, False, got[0], f"scatter_wait_{tag}")):
            recv[key] = r

    outs = {}
    for n in BIG:
        shp = Wl[n].shape
        C = shp[-1]
        rs = [recv[n, None]] if shp[0] == 1 else [recv[n, l] for l in range(shp[0])]
        if colsharded(n):
            rs = [_sum8(r, f"sum8_{n}_{j}").T for j, r in enumerate(rs)]
        res = _adamw(rs, Wl[n].reshape(-1, C), Ml[n].reshape(-1, C), Vl[n].reshape(-1, C), f"adamw_{n}",
                     summed=colsharded(n))
        outs[n] = [o.reshape(shp) for o in res]
    for names, r, tag in ((SMALL_SHARDED, got[0], "small"), (REPLICATED, got[1], "repl")):
        shapes = [Wl[n].shape for n in names]
        res = _adamw([r], _pack([Wl[n] for n in names]), _pack([Ml[n] for n in names]),
                     _pack([Vl[n] for n in names]), f"adamw_{tag}")
        unp = [_unpack(o, shapes) for o in res]
        for k, n in enumerate(names):
            outs[n] = [unp[j][k] for j in range(4)]

    result = [loss, grad_x[None]]
    for j in range(4):
        result += [outs[n][j] for n in W_NAMES]
    return tuple(result)
```

```python
import functools
import math

import jax
import jax.numpy as jnp
from jax import lax
from jax.experimental import pallas as pl
from jax.experimental.pallas import tpu as pltpu

F32 = jnp.float32
MXU = jnp.bfloat16
EPS = 1e-6
CH = 128
HG = 128
N_META = 16
CONV_W = 31
LRU_W = 4
LRU_C = 8.0
VMEM_LIMIT = 48 * 2 ** 20
MM_VMEM_BUDGET = 36 * 2 ** 20
ADAM_LR, ADAM_B1, ADAM_B2, ADAM_EPS, ADAM_WD, ADAM_STEP = 0.001, 0.9, 0.999, 1e-08, 0.01, 10
MESH_AXES = ("x", "y", "c")
NDEV = 8

W_NAMES = ['meta_tokens', 'ffn1_norm', 'ffn1_wg', 'ffn1_wu', 'ffn1_wd', 'mix_norm', 'ffn2_norm', 'ffn2_wg', 'ffn2_wu',
           'ffn2_wd', 'w_in_even', 'pool_w', 'pool_scale', 'hgrn_lb_logits', 'hgrn_gnorm', 'w_out_even', 'w_in_odd',
           'conv_w', 'conv_b', 'conv_ln_g', 'conv_ln_b', 'lru_conv_w', 'lru_conv_b', 'lru_wa', 'lru_ba', 'lru_wx',
           'lru_bx', 'lru_lambda', 'w_out_odd', 'final_norm']
SHARD_AXIS = {'meta_tokens': 1, 'ffn1_wg': 2, 'ffn1_wu': 2, 'ffn1_wd': 1, 'ffn2_wg': 2, 'ffn2_wu': 2, 'ffn2_wd': 1,
              'w_in_even': 2, 'w_out_even': 1, 'w_in_odd': 2, 'conv_w': 2, 'conv_b': 1, 'conv_ln_g': 1,
              'conv_ln_b': 1, 'lru_conv_w': 2, 'lru_conv_b': 1, 'lru_ba': 1, 'lru_bx': 1, 'lru_lambda': 1,
              'w_out_odd': 1}
BIG = ['ffn1_wg', 'ffn1_wu', 'ffn1_wd', 'ffn2_wg', 'ffn2_wu', 'ffn2_wd', 'w_in_even', 'w_out_even', 'w_in_odd',
       'w_out_odd']
SMALL_SHARDED = [n for n in W_NAMES if n in SHARD_AXIS and n not in BIG]
REPLICATED = [n for n in W_NAMES if n not in SHARD_AXIS]


def _cparams(sem=None, vmem=VMEM_LIMIT):
    return pltpu.CompilerParams(dimension_semantics=sem, vmem_limit_bytes=vmem)


def _tiles(n):
    return [c for c in range(128, n + 1, 128) if n % c == 0] or [n]


def _tile(n, cap=1024):
    return max([c for c in _tiles(n) if c <= cap], default=_tiles(n)[0])


def _rowtile(n):
    for c in (256, 352, 128, 64, 32, 16, 8):
        if n % c == 0:
            return c
    return n


def _copies(srcs, lands, bcast, ssem, rsem, lsem):
    x, y, c = lax.axis_index("x"), lax.axis_index("y"), lax.axis_index("c")
    me = 4 * x + 2 * y + c
    locs, sends, recvs = [], [], []
    for a in range(len(srcs)):
        locs.append(pltpu.make_async_copy(srcs[a] if bcast[a] else srcs[a].at[me], lands[a].at[me], lsem.at[a]))
        for m in range(1, NDEV):
            px = 1 - x if (m >> 2) & 1 else x
            py = 1 - y if (m >> 1) & 1 else y
            pc = 1 - c if m & 1 else c
            peer = 4 * px + 2 * py + pc
            src = srcs[a] if bcast[a] else srcs[a].at[peer]
            k = a * NDEV + m
            for dst, out in ((lands[a].at[me], sends), (lands[a].at[peer], recvs)):
                out.append(pltpu.make_async_remote_copy(src_ref=src, dst_ref=dst, send_sem=ssem.at[k],
                                                        recv_sem=rsem.at[k], device_id=(px, py, pc),
                                                        device_id_type=pl.DeviceIdType.MESH))
    return locs, sends, recvs


def _land_shape(arr, bc):
    return (NDEV,) + tuple(arr.shape if bc else arr.shape[1:])


def _exchange(arrays, bcast, name):
    n = len(arrays)

    def body(*refs):
        locs, sends, recvs = _copies(refs[:n], refs[n:2 * n], bcast, refs[2 * n], refs[2 * n + 1], refs[2 * n + 2])
        for d in locs + sends:
            d.start()
        for r in recvs:
            r.wait_recv()
        for s in sends:
            s.wait_send()
        for loc in locs:
            loc.wait()

    out_shape = tuple(jax.ShapeDtypeStruct(_land_shape(arr, bc), arr.dtype) for arr, bc in zip(arrays, bcast))
    any_spec = pl.BlockSpec(memory_space=pl.ANY)
    return pl.pallas_call(
        body, name=name, out_shape=out_shape, in_specs=[any_spec] * n, out_specs=tuple([any_spec] * n),
        scratch_shapes=[pltpu.SemaphoreType.DMA((n * NDEV,)), pltpu.SemaphoreType.DMA((n * NDEV,)),
                        pltpu.SemaphoreType.DMA((n,))],
    )(*arrays)


_HBM = pl.BlockSpec(memory_space=pltpu.HBM)
_SEM = pl.BlockSpec(memory_space=pltpu.SEMAPHORE)
_EFFECT = pltpu.SideEffectType.DATAFLOW_SIDE_EFFECTING


def _exchange_start(groups, bcast, name):
    sizes = [len(g) for g in groups]
    srcs = [a for g in groups for a in g]
    n, ng = len(srcs), len(groups)
    lands = [lax.empty(_land_shape(a, bcast), a.dtype) for a in srcs]

    def body(*refs):
        off = 0
        for gi, sz in enumerate(sizes):
            sem = refs[2 * n + 3 * gi:2 * n + 3 * gi + 3]
            locs, sends, _ = _copies(refs[off:off + sz], refs[n + off:n + off + sz], [bcast] * sz, *sem)
            for d in locs + sends:
                d.start()
            off += sz
        refs[-1][...] = jnp.zeros((8, 128), F32)

    sems = []
    for sz in sizes:
        sems += [pltpu.SemaphoreType.DMA((sz * NDEV,)), pltpu.SemaphoreType.DMA((sz * NDEV,)),
                 pltpu.SemaphoreType.DMA((sz,))]
    thru = [pltpu.HBM(a.shape, a.dtype) for a in srcs + lands]
    outs = pl.pallas_call(
        body, name=name, out_shape=tuple(sems + thru + [jax.ShapeDtypeStruct((8, 128), F32)]),
        in_specs=[_HBM] * (2 * n),
        out_specs=tuple([_SEM] * (3 * ng) + [_HBM] * (2 * n) + [pl.BlockSpec(memory_space=pltpu.VMEM)]),
        input_output_aliases={i: 3 * ng + i for i in range(2 * n)},
        compiler_params=pltpu.CompilerParams(has_side_effects=_EFFECT),
    )(*[pltpu.with_memory_space_constraint(a, pltpu.HBM) for a in srcs + lands])
    handles, off = [], 0
    for gi, sz in enumerate(sizes):
        handles.append((outs[3 * gi:3 * gi + 3], outs[3 * ng + off:3 * ng + off + sz],
                        outs[3 * ng + n + off:3 * ng + n + off + sz]))
        off += sz
    return handles, outs[-1]


def _exchange_wait(handle, bcast, after, name):
    sems, srcs, lands = handle
    n = len(srcs)

    def body(*refs):
        locs, sends, recvs = _copies(refs[:n], refs[n:2 * n], [bcast] * n, *refs[2 * n:2 * n + 3])
        for r in recvs:
            r.wait_recv()
        for s in sends:
            s.wait_send()
        for loc in locs:
            loc.wait()

    outs = pl.pallas_call(
        body, name=name, out_shape=tuple(pltpu.HBM(a.shape, a.dtype) for a in list(srcs) + list(lands)),
        in_specs=[_HBM] * (2 * n) + [_SEM] * 3 + [pl.BlockSpec(memory_space=pl.ANY)],
        out_specs=tuple([_HBM] * (2 * n)), input_output_aliases={i: i for i in range(2 * n)},
        compiler_params=pltpu.CompilerParams(has_side_effects=_EFFECT),
    )(*srcs, *lands, *sems, after)
    return outs[n:]


def _sum8(recv, name):
    _, R, C = recv.shape
    br = _rowtile(R)

    def body(r_ref, o_ref):
        s = r_ref[0].astype(F32)
        for k in range(1, NDEV):
            s = s + r_ref[k].astype(F32)
        o_ref[...] = s

    return pl.pallas_call(
        body, name=name, grid=(R // br,), out_shape=jax.ShapeDtypeStruct((R, C), F32),
        in_specs=[pl.BlockSpec((NDEV, br, C), lambda i: (0, i, 0))], out_specs=pl.BlockSpec((br, C), lambda i: (i, 0)),
        compiler_params=_cparams(("parallel",)),
    )(recv)


def _adamw(recvs, w, m, v, name, summed=False):
    R, C = w.shape
    nr = len(recvs)
    br = _rowtile(R // nr)
    nb0 = R // nr // br

    def body(*refs):
        w_ref, m_ref, v_ref, g_o, d_o, m_o, v_o = refs[nr:]
        g = None
        for j in range(nr):
            if summed:
                s = refs[j][...]
            else:
                s = refs[j][0].astype(F32)
                for k in range(1, NDEV):
                    s = s + refs[j][k].astype(F32)
            g = s if g is None else jnp.where(pl.program_id(0) >= j * nb0, s, g)
        mn = ADAM_B1 * m_ref[...] + (1.0 - ADAM_B1) * g
        vn = ADAM_B2 * v_ref[...] + (1.0 - ADAM_B2) * (g * g)
        m_hat = mn / (1.0 - ADAM_B1 ** ADAM_STEP)
        v_hat = vn / (1.0 - ADAM_B2 ** ADAM_STEP)
        g_o[...] = g
        d_o[...] = -ADAM_LR * (m_hat / (jnp.sqrt(v_hat) + ADAM_EPS) + ADAM_WD * w_ref[...])
        m_o[...] = mn
        v_o[...] = vn

    def rspec(j):
        if summed:
            return pl.BlockSpec((br, C), lambda i: (jnp.clip(i - j * nb0, 0, nb0 - 1), 0))
        return pl.BlockSpec((NDEV, br, C), lambda i: (0, jnp.clip(i - j * nb0, 0, nb0 - 1), 0))

    blk = pl.BlockSpec((br, C), lambda i: (i, 0))
    sds = jax.ShapeDtypeStruct((R, C), F32)
    return pl.pallas_call(
        body, name=name, grid=(R // br,), out_shape=(sds, sds, sds, sds),
        in_specs=[rspec(j) for j in range(nr)] + [blk, blk, blk], out_specs=(blk, blk, blk, blk),
        compiler_params=_cparams(("arbitrary",)),
    )(*recvs, w, m, v)


_DIMS = {"NN": ((1,), (0,)), "NT": ((1,), (1,)), "TN": ((0,), (0,))}


def _dot(a, b, mode="NN"):
    return lax.dot_general(a.astype(MXU), b.astype(MXU), (_DIMS[mode], ((), ())), preferred_element_type=F32)


def _dotf(a, b, mode="NN"):
    return lax.dot_general(a, b, (_DIMS[mode], ((), ())), precision=lax.Precision.HIGH,
                           preferred_element_type=F32)


def _mm(pairs, mode, name, res=None, res_scale=1.0, out_dtype=F32, dep=None):
    a0, b0 = pairs[0]
    M = a0.shape[1] if mode == "TN" else a0.shape[0]
    N = b0.shape[0] if mode == "NT" else b0.shape[1]
    npairs = len(pairs)

    def vmem_bytes(tm, tn):
        total = tm * tn * 4 * (2 + (2 if res is not None else 0) + 2)
        for a, b in pairs:
            ka = a.shape[0] if mode == "TN" else a.shape[1]
            kb = b.shape[1] if mode == "NT" else b.shape[0]
            for k, t, arr in ((ka, tm, a), (kb, tn, b)):
                total += k * t * (2 * arr.dtype.itemsize + (2 if arr.dtype == F32 else 0))
        return total

    tm, tn = max(((tm, tn) for tm in _tiles(M) for tn in _tiles(N) if vmem_bytes(tm, tn) <= MM_VMEM_BUDGET),
                 key=lambda t: (t[0] * t[1], t[1]), default=(_tiles(M)[0], _tiles(N)[0]))

    def body(*refs):
        acc = None
        for p in range(npairs):
            d = _dot(refs[2 * p][...], refs[2 * p + 1][...], mode)
            acc = d if acc is None else acc + d
        if res_scale != 1.0:
            acc = res_scale * acc
        if res is not None:
            acc = refs[2 * npairs][...] + acc
        refs[-1][...] = acc.astype(out_dtype)

    in_specs, args = [], []
    for a, b in pairs:
        if mode == "TN":
            in_specs.append(pl.BlockSpec((a.shape[0], tm), lambda i, j: (0, i)))
        else:
            in_specs.append(pl.BlockSpec((tm, a.shape[1]), lambda i, j: (i, 0)))
        if mode == "NT":
            in_specs.append(pl.BlockSpec((tn, b.shape[1]), lambda i, j: (j, 0)))
        else:
            in_specs.append(pl.BlockSpec((b.shape[0], tn), lambda i, j: (0, j)))
        args += [a, b]
    if res is not None:
        in_specs.append(pl.BlockSpec((tm, tn), lambda i, j: (i, j)))
        args.append(res)
    if dep is not None:
        in_specs.append(pl.BlockSpec(memory_space=pl.ANY))
        args.append(dep)
    return pl.pallas_call(
        body, name=name, grid=(M // tm, N // tn), out_shape=jax.ShapeDtypeStruct((M, N), out_dtype),
        in_specs=in_specs, out_specs=pl.BlockSpec((tm, tn), lambda i, j: (i, j)),
        compiler_params=_cparams(("parallel", "parallel")),
    )(*args)


def _rms_fwd(h, gamma, name):
    T, D = h.shape
    tm = _tile(T)

    def body(h_ref, g_ref, o_ref):
        x = h_ref[...]
        r = lax.rsqrt(jnp.mean(x * x, axis=-1, keepdims=True) + EPS)
        o_ref[...] = (x * r * g_ref[...]).astype(MXU)

    return pl.pallas_call(
        body, name=name, grid=(T // tm,), out_shape=jax.ShapeDtypeStruct((T, D), MXU),
        in_specs=[pl.BlockSpec((tm, D), lambda i: (i, 0)), pl.BlockSpec((1, D), lambda i: (0, 0))],
        out_specs=pl.BlockSpec((tm, D), lambda i: (i, 0)), compiler_params=_cparams(("parallel",)),
    )(h, gamma)


def _rms_bwd_math(x, gamma, dy):
    r = lax.rsqrt(jnp.mean(x * x, axis=-1, keepdims=True) + EPS)
    z = dy * gamma
    dx = r * z - x * (r * r * r) * jnp.mean(z * x, axis=-1, keepdims=True)
    dgamma = jnp.sum(dy * x * r, axis=0, keepdims=True)
    return dx, dgamma


def _rms_bwd(h, gamma, dxn, dres, name):
    T, D = h.shape
    tm = _tile(T)

    def body(h_ref, g_ref, dxn_ref, dres_ref, dh_ref, dh16_ref, dg_ref):
        dx, dgamma = _rms_bwd_math(h_ref[...], g_ref[...], dxn_ref[...])
        dh = dres_ref[...] + dx
        dh_ref[...] = dh
        dh16_ref[...] = dh.astype(MXU)

        @pl.when(pl.program_id(0) == 0)
        def _():
            dg_ref[...] = jnp.zeros_like(dg_ref)

        dg_ref[...] += dgamma

    row = pl.BlockSpec((tm, D), lambda i: (i, 0))
    vec = pl.BlockSpec((1, D), lambda i: (0, 0))
    return pl.pallas_call(
        body, name=name, grid=(T // tm,),
        out_shape=(jax.ShapeDtypeStruct((T, D), F32), jax.ShapeDtypeStruct((T, D), MXU),
                   jax.ShapeDtypeStruct((1, D), F32)),
        in_specs=[row, vec, row, row], out_specs=(row, row, vec), compiler_params=_cparams(("arbitrary",)),
    )(h, gamma, dxn, dres)


def _loss_head(h, gamma, tgt, lo, hi, name):
    T, D = h.shape
    tm = _tile(T)

    def body(h_ref, g_ref, t_ref, loss_ref, dh_ref, dh16_ref, dg_ref):
        i = pl.program_id(0)
        x = h_ref[...]
        r = lax.rsqrt(jnp.mean(x * x, axis=-1, keepdims=True) + EPS)
        y = x * r * g_ref[...]
        rows = i * tm + lax.broadcasted_iota(jnp.int32, (tm, 1), 0)
        valid = jnp.logical_and(rows >= lo, rows < hi)
        diff = jnp.where(valid, y - t_ref[...], 0.0)
        part = 0.5 * jnp.sum(jnp.sum(diff * diff, axis=-1, keepdims=True) / D, axis=0, keepdims=True)
        dx, dgamma = _rms_bwd_math(x, g_ref[...], diff / D)
        dh_ref[...] = dx
        dh16_ref[...] = dx.astype(MXU)

        @pl.when(i == 0)
        def _():
            dg_ref[...] = jnp.zeros_like(dg_ref)
            loss_ref[...] = jnp.zeros_like(loss_ref)

        dg_ref[...] += dgamma
        loss_ref[...] += jnp.broadcast_to(part, loss_ref.shape)

    row = pl.BlockSpec((tm, D), lambda i: (i, 0))
    vec = pl.BlockSpec((1, D), lambda i: (0, 0))
    lsp = pl.BlockSpec((8, 128), lambda i: (0, 0))
    return pl.pallas_call(
        body, name=name, grid=(T // tm,),
        out_shape=(jax.ShapeDtypeStruct((8, 128), F32), jax.ShapeDtypeStruct((T, D), F32),
                   jax.ShapeDtypeStruct((T, D), MXU), jax.ShapeDtypeStruct((1, D), F32)),
        in_specs=[row, vec, row], out_specs=(lsp, row, row, vec), compiler_params=_cparams(("arbitrary",)),
    )(h, gamma, tgt)


def _ffn_tiles(T, Fd):
    return (448 if T % 448 == 0 else _tile(T)), max(c for c in _tiles(Fd) if c <= 1536)


def _ffn_up(xn, wg, wu, name):
    T, D = xn.shape
    Fd = wg.shape[0]
    tm, tn = _ffn_tiles(T, Fd)

    def body(x_ref, wg_ref, wu_ref, p_ref, q_ref, a_ref):
        x = x_ref[...]
        g = _dot(x, wg_ref[...], "NT")
        u = _dot(x, wu_ref[...], "NT")
        sg = jax.nn.sigmoid(g)
        q = g * sg
        p_ref[...] = (u * (sg + q * (1.0 - sg))).astype(MXU)
        q_ref[...] = q.astype(MXU)
        a_ref[...] = (q * u).astype(MXU)

    wsp = pl.BlockSpec((tn, D), lambda j, i: (j, 0))
    osp = pl.BlockSpec((tm, tn), lambda j, i: (i, j))
    sds = jax.ShapeDtypeStruct((T, Fd), MXU)
    return pl.pallas_call(
        body, name=name, grid=(Fd // tn, T // tm), out_shape=(sds, sds, sds),
        in_specs=[pl.BlockSpec((tm, D), lambda j, i: (i, 0)), wsp, wsp], out_specs=(osp, osp, osp),
        compiler_params=_cparams(("parallel", "parallel")),
    )(xn, wg, wu)


def _ffn_dact(dy, wd, p, q, scale, name, dep=None):
    T, D = dy.shape
    Fd = wd.shape[0]
    tm, tn = _ffn_tiles(T, Fd)

    def body(dy_ref, wd_ref, p_ref, q_ref, *rest):
        dg_ref, du_ref = rest[-2:]
        da = scale * _dot(dy_ref[...], wd_ref[...], "NT")
        dg_ref[...] = (da * p_ref[...].astype(F32)).astype(MXU)
        du_ref[...] = (da * q_ref[...].astype(F32)).astype(MXU)

    osp = pl.BlockSpec((tm, tn), lambda j, i: (i, j))
    sds = jax.ShapeDtypeStruct((T, Fd), MXU)
    return pl.pallas_call(
        body, name=name, grid=(Fd // tn, T // tm), out_shape=(sds, sds),
        in_specs=[pl.BlockSpec((tm, D), lambda j, i: (i, 0)), pl.BlockSpec((tn, D), lambda j, i: (j, 0)), osp, osp]
        + ([] if dep is None else [pl.BlockSpec(memory_space=pl.ANY)]),
        out_specs=(osp, osp), compiler_params=_cparams(("parallel", "parallel")),
    )(dy, wd, p, q, *([] if dep is None else [dep]))


def _down(v, s):
    return v if s == 0 else pltpu.roll(v, s, 0)


def _up(v, s):
    return v if s == 0 else pltpu.roll(v, v.shape[0] - s, 0)


def _rows(n):
    return lax.broadcasted_iota(jnp.int32, (n, 1), 0)


def _zero_pad_rows(ref, lo_end, hi_start, T):
    ref[pl.ds(0, lo_end), :] = jnp.zeros((lo_end, ref.shape[1]), ref.dtype)
    if T > hi_start:
        ref[pl.ds(hi_start, T - hi_start), :] = jnp.zeros((T - hi_start, ref.shape[1]), ref.dtype)


def _colblock(T, off):
    return pl.BlockSpec((T, 128), lambda j: (0, off + j))


def _vecblock(rows=1):
    return pl.BlockSpec((rows, 128), lambda j: (0, j))


def _pool_lane_consts(n):
    lane = lax.broadcasted_iota(jnp.int32, (n, 256), 1)
    win = jnp.where(lane < 64, 2.0, jnp.where(lane < 128, 4.0, jnp.where(lane < 192, 8.0, 16.0)))
    return lane, win


def _pool_select(lane, s2, s4, s8, s16):
    return jnp.where(lane < 64, s2, jnp.where(lane < 128, s4, jnp.where(lane < 192, s8, s16)))


def _pool_mixed(xh, start):
    s2 = xh + _down(xh, 1)
    s4 = s2 + _down(s2, 2)
    s8 = s4 + _down(s4, 4)
    s16 = s8 + _down(s8, 8)
    n = xh.shape[0] - 16
    lane, _ = _pool_lane_consts(n + 16)
    _, win = _pool_lane_consts(n)
    t1 = (start - CH + 1 + _rows(n)).astype(F32)
    cnt = jnp.minimum(jnp.maximum(t1, 1.0), win)
    return _pool_select(lane, s2, s4, s8, s16)[16:] / cnt - xh[16:]


def _pool_fwd(p, wbd, scale, nreal, real_end, name):
    T = p.shape[0]

    def body(p_ref, w_ref, s_ref, y_ref):
        _zero_pad_rows(y_ref, CH, CH * (1 + nreal), T)

        def chunk(c, carry):
            start = pl.multiple_of(c * CH, CH)
            mixed = _pool_mixed(p_ref[pl.ds(start - 16, CH + 16), :], start)
            y = _dot(mixed, w_ref[...]) * s_ref[...]
            y_ref[pl.ds(start, CH), :] = jnp.where(start + _rows(CH) < real_end, y, 0.0)
            return carry

        _pairs_loop(nreal, lambda c, carry: chunk(c + 1, carry), 0)

    return pl.pallas_call(
        body, name=name, grid=(1,), out_shape=jax.ShapeDtypeStruct((T, 256), F32),
        in_specs=[pl.BlockSpec((T, 256), lambda j: (0, 0)), pl.BlockSpec((256, 256), lambda j: (0, 0)),
                  pl.BlockSpec((1, 256), lambda j: (0, 0))],
        out_specs=pl.BlockSpec((T, 256), lambda j: (0, 0)), compiler_params=_cparams(("arbitrary",)),
    )(p, wbd, scale)


def _pool_bwd(p, wbd, scale, dy, nreal, real_end, name):
    T = p.shape[0]

    def body(p_ref, w_ref, s_ref, dy_ref, dp_ref, dw_ref, ds_ref):
        _zero_pad_rows(dp_ref, CH, CH * (1 + nreal), T)
        dw_ref[...] = jnp.zeros_like(dw_ref)
        ds_ref[...] = jnp.zeros_like(ds_ref)

        def chunk(c, carry):
            start = pl.multiple_of(c * CH, CH)
            mixed = _pool_mixed(p_ref[pl.ds(start - 16, CH + 16), :], start)
            ypre = _dot(mixed, w_ref[...])
            n = CH + 16
            dye = jnp.where(start + _rows(n) < real_end, dy_ref[pl.ds(start, n), :], 0.0)
            dys = dye * s_ref[...]
            ds_ref[...] += jnp.sum(dye[:CH] * ypre, axis=0, keepdims=True)
            dw_ref[...] += _dot(mixed, dys[:CH], "TN")
            dmix = _dot(dys, w_ref[...], "NT")
            lane, win = _pool_lane_consts(n)
            t1 = (start - CH + 1 + _rows(n)).astype(F32)
            z = dmix / jnp.minimum(jnp.maximum(t1, 1.0), win)
            r2 = z + _up(z, 1)
            r4 = r2 + _up(r2, 2)
            r8 = r4 + _up(r4, 4)
            r16 = r8 + _up(r8, 8)
            dp_ref[pl.ds(start, CH), :] = (_pool_select(lane, r2, r4, r8, r16) - dmix)[:CH]
            return carry

        _pairs_loop(nreal, lambda c, carry: chunk(c + 1, carry), 0)

    full = lambda r, c: pl.BlockSpec((r, c), lambda j: (0, 0))
    return pl.pallas_call(
        body, name=name, grid=(1,),
        out_shape=(jax.ShapeDtypeStruct((T, 256), F32), jax.ShapeDtypeStruct((256, 256), F32),
                   jax.ShapeDtypeStruct((1, 256), F32)),
        in_specs=[full(T, 256), full(256, 256), full(1, 256), full(T, 256)],
        out_specs=(full(T, 256), full(256, 256), full(1, 256)), compiler_params=_cparams(("arbitrary",)),
    )(p, wbd, scale, dy)


def _hgrn_chunk(St, qr, fr, ir, gr, l0, l1, gn):
    rows = lax.broadcasted_iota(jnp.int32, (HG, HG), 0)
    cols = lax.broadcasted_iota(jnp.int32, (HG, HG), 1)
    causal = rows >= cols
    ltri = causal.astype(F32)
    lb = jax.nn.sigmoid(l0 - l1)
    sg = jax.nn.sigmoid(fr)
    logf = jnp.log(lb + (1.0 - lb) * sg)
    kk = (1.0 - lb) * (1.0 - sg)
    q = qr * jax.nn.sigmoid(qr)
    b = jnp.dot(ltri, logf, precision=lax.Precision.HIGH, preferred_element_type=F32)
    bl = jnp.sum(logf, axis=0, keepdims=True)
    bm = jnp.sum(jnp.where(_rows(HG) <= HG // 2, logf, 0.0), axis=0, keepdims=True)
    o = _dotf(q * jnp.exp(b), St, "NT")
    A = _dotf(q * jnp.exp(b - bm), kk * jnp.exp(bm - b), "NT")
    o = o + _dotf(jnp.where(causal, A, 0.0), ir)
    St_new = St * jnp.exp(bl) + _dotf(ir, kk * jnp.exp(bl - b), "TN")
    on = o * lax.rsqrt(jnp.mean(o * o, axis=-1, keepdims=True) + EPS) * gn
    return St_new, on * (gr * jax.nn.sigmoid(gr))


def _pairs_loop(n, step, init):
    u = 3 if n % 3 == 0 else 2 if n % 2 == 0 else 1

    def body(i, carry):
        for j in range(u):
            carry = step(u * i + j, carry)
        return carry

    return lax.fori_loop(0, n // u, body, init)


def _hgrn_specs(T):
    return [_colblock(T, 2), _colblock(T, 8), _colblock(T, 14), _colblock(T, 20), _vecblock(), _vecblock(),
            pl.BlockSpec((1, 128), lambda j: (0, 0))]


def _hgrn_fwd(p, l0, l1, gn, nreal, real_end, name):
    T = p.shape[0]
    nch = nreal * (CH // HG)

    def body(q_ref, f_ref, i_ref, g_ref, l0_ref, l1_ref, gn_ref, y_ref, s_ref):
        _zero_pad_rows(y_ref, CH, CH * (1 + nreal), T)

        def chunk(c, St):
            start = pl.multiple_of(CH + c * HG, HG)
            sl = pl.ds(start, HG)
            s_ref[0, c] = St
            St_new, y = _hgrn_chunk(St, q_ref[sl, :], f_ref[sl, :], i_ref[sl, :], g_ref[sl, :], l0_ref[...],
                                    l1_ref[...], gn_ref[...])
            y_ref[sl, :] = jnp.where(start + _rows(HG) < real_end, y, 0.0)
            return St_new

        _pairs_loop(nch, chunk, jnp.zeros((128, 128), F32))

    return pl.pallas_call(
        body, name=name, grid=(6,),
        out_shape=(jax.ShapeDtypeStruct((T, 768), F32), jax.ShapeDtypeStruct((6, nch, 128, 128), F32)),
        in_specs=_hgrn_specs(T),
        out_specs=(_colblock(T, 0), pl.BlockSpec((1, nch, 128, 128), lambda j: (j, 0, 0, 0))),
        compiler_params=_cparams(("parallel",)),
    )(p, p, p, p, l0, l1, gn)


def _hgrn_bwd(p, l0, l1, gn, states, dy, nreal, real_end, name):
    T = p.shape[0]
    nch = nreal * (CH // HG)

    def body(q_ref, f_ref, i_ref, g_ref, l0_ref, l1_ref, gn_ref, s_ref, dy_ref,
             dq_ref, df_ref, di_ref, dg_ref, dl0_ref, dl1_ref, dgn_ref):
        for r in (dq_ref, df_ref, di_ref, dg_ref):
            _zero_pad_rows(r, CH, CH * (1 + nreal), T)

        def chunk(k, carry):
            dSt, a0, a1, agn = carry
            c = nch - 1 - k
            start = pl.multiple_of(CH + c * HG, HG)
            sl = pl.ds(start, HG)
            _, vjp = jax.vjp(_hgrn_chunk, s_ref[0, c], q_ref[sl, :], f_ref[sl, :], i_ref[sl, :], g_ref[sl, :],
                             l0_ref[...], l1_ref[...], gn_ref[...])
            dyc = jnp.where(start + _rows(HG) < real_end, dy_ref[sl, :], 0.0)
            dS, dq, df, di, dg, d0, d1, dgn = vjp((dSt, dyc))
            dq_ref[sl, :] = dq
            df_ref[sl, :] = df
            di_ref[sl, :] = di
            dg_ref[sl, :] = dg
            return dS, a0 + d0, a1 + d1, agn + dgn

        z = jnp.zeros((1, 128), F32)
        _, a0, a1, agn = _pairs_loop(nch, chunk, (jnp.zeros((128, 128), F32), z, z, z))
        dl0_ref[...] = a0
        dl1_ref[...] = a1

        @pl.when(pl.program_id(0) == 0)
        def _():
            dgn_ref[...] = jnp.zeros_like(dgn_ref)

        dgn_ref[...] += agn

    big = jax.ShapeDtypeStruct((T, 768), F32)
    vec = jax.ShapeDtypeStruct((1, 768), F32)
    return pl.pallas_call(
        body, name=name, grid=(6,),
        out_shape=(big, big, big, big, vec, vec, jax.ShapeDtypeStruct((1, 128), F32)),
        in_specs=_hgrn_specs(T) + [pl.BlockSpec((1, nch, 128, 128), lambda j: (j, 0, 0, 0)), _colblock(T, 2)],
        out_specs=(_colblock(T, 0), _colblock(T, 0), _colblock(T, 0), _colblock(T, 0), _vecblock(), _vecblock(),
                   pl.BlockSpec((1, 128), lambda j: (0, 0))),
        compiler_params=_cparams(("arbitrary",), 60 * 2 ** 20),
    )(p, p, p, p, l0, l1, gn, states, dy)


def _glu(a, b):
    return a * jax.nn.sigmoid(b)


def _conv_post(cv, ln_g, ln_b):
    mu = jnp.mean(cv, axis=-1, keepdims=True)
    d = cv - mu
    var = jnp.mean(d * d, axis=-1, keepdims=True)
    un = d * lax.rsqrt(var + EPS) * ln_g + ln_b
    return un * jax.nn.sigmoid(un)


def _causal_conv(uh, w_ref, width, halo):
    acc = None
    for j in range(width):
        term = _down(uh, width - 1 - j) * w_ref[pl.ds(j, 1), :]
        acc = term if acc is None else acc + term
    return acc[halo:]


def _conf_fwd(p, cw, cb, lg, lb, nreal, real_end, name):
    T = p.shape[0]

    def body(a_ref, b_ref, w_ref, cb_ref, lg_ref, lb_ref, y_ref):
        _zero_pad_rows(y_ref, CH, CH * (1 + nreal), T)

        def chunk(c, carry):
            start = pl.multiple_of(c * CH, CH)
            ext = pl.ds(start - 32, CH + 32)
            cv = _causal_conv(_glu(a_ref[ext, :], b_ref[ext, :]), w_ref, CONV_W, 32) + cb_ref[...]
            y = _conv_post(cv, lg_ref[...], lb_ref[...])
            y_ref[pl.ds(start, CH), :] = jnp.where(start + _rows(CH) < real_end, y, 0.0)
            return carry

        _pairs_loop(nreal, lambda c, carry: chunk(c + 1, carry), 0)

    return pl.pallas_call(
        body, name=name, grid=(4,), out_shape=jax.ShapeDtypeStruct((T, 512), F32),
        in_specs=[_colblock(T, 0), _colblock(T, 4), _vecblock(32), _vecblock(), _vecblock(), _vecblock()],
        out_specs=_colblock(T, 0), compiler_params=_cparams(("parallel",)),
    )(p, p, cw, cb, lg, lb)


def _conf_bwd(p, cw, cb, lg, lb, dy, nreal, real_end, name):
    T = p.shape[0]

    def body(a_ref, b_ref, w_ref, cb_ref, lg_ref, lb_ref, dy_ref, da_ref, db_ref, dw_ref, dcb_ref, dlg_ref, dlb_ref):
        _zero_pad_rows(da_ref, CH, CH * (1 + nreal), T)
        _zero_pad_rows(db_ref, CH, CH * (1 + nreal), T)
        for r in (dw_ref, dcb_ref, dlg_ref, dlb_ref):
            r[...] = jnp.zeros_like(r)

        def chunk(c, carry):
            start = pl.multiple_of(c * CH, CH)
            ext = pl.ds(start - 32, CH + 64)
            ue = _glu(a_ref[ext, :], b_ref[ext, :])
            cv = _causal_conv(ue, w_ref, CONV_W, 32) + cb_ref[...]
            dye = jnp.where(start + _rows(CH + 32) < real_end, dy_ref[pl.ds(start, CH + 32), :], 0.0)
            _, vjp_cur = jax.vjp(_conv_post, cv[:CH], lg_ref[...], lb_ref[...])
            dc_cur, dlg, dlb = vjp_cur(dye[:CH])
            _, vjp_halo = jax.vjp(_conv_post, cv[CH:], lg_ref[...], lb_ref[...])
            dce = jnp.concatenate([dc_cur, vjp_halo(dye[CH:])[0]], axis=0)
            dlg_ref[...] += dlg
            dlb_ref[...] += dlb
            dcb_ref[...] += jnp.sum(dc_cur, axis=0, keepdims=True)
            du = None
            for j in range(CONV_W):
                w_j = w_ref[pl.ds(j, 1), :]
                term = _up(dce, CONV_W - 1 - j)[:CH] * w_j
                du = term if du is None else du + term
                dw_ref[pl.ds(j, 1), :] += jnp.sum(dc_cur * _up(ue, 2 + j)[:CH], axis=0, keepdims=True)
            cur = pl.ds(start, CH)
            _, vjp_glu = jax.vjp(_glu, a_ref[cur, :], b_ref[cur, :])
            da, db = vjp_glu(du)
            da_ref[cur, :] = da
            db_ref[cur, :] = db
            return carry

        _pairs_loop(nreal, lambda c, carry: chunk(c + 1, carry), 0)

    big = jax.ShapeDtypeStruct((T, 512), F32)
    vec = jax.ShapeDtypeStruct((1, 512), F32)
    return pl.pallas_call(
        body, name=name, grid=(4,), out_shape=(big, big, jax.ShapeDtypeStruct((32, 512), F32), vec, vec, vec),
        in_specs=[_colblock(T, 0), _colblock(T, 4), _vecblock(32), _vecblock(), _vecblock(), _vecblock(),
                  _colblock(T, 0)],
        out_specs=(_colblock(T, 0), _colblock(T, 0), _vecblock(32), _vecblock(), _vecblock(), _vecblock()),
        compiler_params=_cparams(("parallel",)),
    )(p, p, cw, cb, lg, lb, dy)


def _softplus_neg(lam):
    e = jnp.exp(-lam)
    small = e * (1.0 - e * (0.5 - e * (1.0 / 3.0 - e * 0.25)))
    return jnp.where(e < 0.02, small, jnp.log(1.0 + e))


def _one_minus_exp(x):
    series = -x * (1.0 + x * (0.5 + x * (1.0 / 6.0 + x * (1.0 / 24.0 + x * (1.0 / 120.0)))))
    return jnp.where(x > -0.05, series, 1.0 - jnp.exp(x))


def _lru_pre(u, wa, wx, ba, bx, lam, first):
    r = jax.nn.sigmoid(_dot(u, wa) + ba)
    i = jax.nn.sigmoid(_dot(u, wx) + bx)
    log_a = -LRU_C * r * _softplus_neg(lam)
    a = jnp.exp(log_a)
    mult = jnp.sqrt(_one_minus_exp(2.0 * log_a))
    return a, jnp.where(first, 1.0, mult) * (i * u)


def _gelu_gate(gate, h):
    inner = math.sqrt(2.0 / math.pi) * (gate + 0.044715 * (gate * gate * gate))
    return 0.5 * gate * (1.0 + jnp.tanh(inner)) * h


def _lru_specs(T):
    mat = pl.BlockSpec((1, 128, 128), lambda j: (j, 0, 0))
    return [_colblock(T, 8), _colblock(T, 12), _vecblock(8), _vecblock(), mat, mat, _vecblock(), _vecblock(),
            _vecblock()]


def _lru_fwd(p, cw, cb, wa, wx, ba, bx, lam, nreal, real_end, name):
    T = p.shape[0]

    def body(x_ref, g_ref, w_ref, cb_ref, wa_ref, wx_ref, ba_ref, bx_ref, lam_ref, y_ref, h_ref):
        _zero_pad_rows(y_ref, CH, CH * (1 + nreal), T)
        _zero_pad_rows(h_ref, CH, CH * (1 + nreal), T)
        rows = _rows(CH)

        def chunk(c, hprev):
            start = pl.multiple_of(c * CH, CH)
            u = _causal_conv(x_ref[pl.ds(start - 8, CH + 8), :], w_ref, LRU_W, 8) + cb_ref[...]
            A, B = _lru_pre(u, wa_ref[0], wx_ref[0], ba_ref[...], bx_ref[...], lam_ref[...], start + rows == CH)
            s = 1
            while s < CH:
                B = A * jnp.where(rows >= s, _down(B, s), 0.0) + B
                A = A * jnp.where(rows >= s, _down(A, s), 1.0)
                s *= 2
            h = B + A * hprev
            cur = pl.ds(start, CH)
            h_ref[cur, :] = h
            y_ref[cur, :] = jnp.where(start + rows < real_end, _gelu_gate(g_ref[cur, :], h), 0.0)
            return jnp.sum(jnp.where(rows == CH - 1, h, 0.0), axis=0, keepdims=True)

        _pairs_loop(nreal, lambda c, carry: chunk(c + 1, carry), jnp.zeros((1, 128), F32))

    big = jax.ShapeDtypeStruct((T, 512), F32)
    return pl.pallas_call(
        body, name=name, grid=(4,), out_shape=(big, big), in_specs=_lru_specs(T),
        out_specs=(_colblock(T, 0), _colblock(T, 0)), compiler_params=_cparams(("parallel",)),
    )(p, p, cw, cb, wa, wx, ba, bx, lam)


def _lru_bwd(p, cw, cb, wa, wx, ba, bx, lam, hs, dy, nreal, real_end, name):
    T = p.shape[0]

    def body(x_ref, g_ref, w_ref, cb_ref, wa_ref, wx_ref, ba_ref, bx_ref, lam_ref, h_ref, dy_ref,
             dx_ref, dgate_ref, dw_ref, dcb_ref, dwa_ref, dwx_ref, dba_ref, dbx_ref, dlam_ref):
        _zero_pad_rows(dx_ref, CH, CH * (1 + nreal), T)
        _zero_pad_rows(dgate_ref, CH, CH * (1 + nreal), T)
        for r in (dw_ref, dcb_ref, dwa_ref, dwx_ref, dba_ref, dbx_ref, dlam_ref):
            r[...] = jnp.zeros_like(r)
        rows = _rows(CH)

        def chunk(k, carry):
            cdh, du_head = carry
            c = nreal - k
            start = pl.multiple_of(c * CH, CH)
            cur = pl.ds(start, CH)
            xe = x_ref[pl.ds(start - 8, CH + 8), :]
            u = _causal_conv(xe, w_ref, LRU_W, 8) + cb_ref[...]
            first = start + rows == CH
            (a, _), vjp_pre = jax.vjp(lambda uu, m1, m2, b1, b2, ll: _lru_pre(uu, m1, m2, b1, b2, ll, first),
                                      u, wa_ref[0], wx_ref[0], ba_ref[...], bx_ref[...], lam_ref[...])
            h = h_ref[cur, :]
            hm1 = _down(h_ref[pl.ds(start - 8, CH + 8), :], 1)[8:]
            _, vjp_post = jax.vjp(_gelu_gate, g_ref[cur, :], h)
            dgate, D = vjp_post(jnp.where(start + rows < real_end, dy_ref[cur, :], 0.0))
            dgate_ref[cur, :] = dgate
            D = D + jnp.where(rows == CH - 1, cdh, 0.0)
            C = jnp.where(rows < CH - 1, _up(a, 1), 0.0)
            s = 1
            while s < CH:
                D = D + C * jnp.where(rows + s < CH, _up(D, s), 0.0)
                C = C * jnp.where(rows + s < CH, _up(C, s), 1.0)
                s *= 2
            du, dwa, dwx, dba, dbx, dlam = vjp_pre((D * hm1, D))
            dwa_ref[0] += dwa
            dwx_ref[0] += dwx
            dba_ref[...] += dba
            dbx_ref[...] += dbx
            dlam_ref[...] += dlam
            dcb_ref[...] += jnp.sum(du, axis=0, keepdims=True)
            due = jnp.concatenate([du, du_head], axis=0)
            dx = None
            for j in range(LRU_W):
                term = _up(due, LRU_W - 1 - j)[:CH] * w_ref[pl.ds(j, 1), :]
                dx = term if dx is None else dx + term
                dw_ref[pl.ds(j, 1), :] += jnp.sum(du * _up(xe, 8 - (LRU_W - 1) + j)[:CH], axis=0, keepdims=True)
            dx_ref[cur, :] = dx
            return jnp.sum(jnp.where(rows == 0, a * D, 0.0), axis=0, keepdims=True), du[:8]

        _pairs_loop(nreal, chunk, (jnp.zeros((1, 128), F32), jnp.zeros((8, 128), F32)))

    big = jax.ShapeDtypeStruct((T, 512), F32)
    vec = jax.ShapeDtypeStruct((1, 512), F32)
    mat = jax.ShapeDtypeStruct((4, 128, 128), F32)
    matspec = pl.BlockSpec((1, 128, 128), lambda j: (j, 0, 0))
    return pl.pallas_call(
        body, name=name, grid=(4,),
        out_shape=(big, big, jax.ShapeDtypeStruct((8, 512), F32), vec, mat, mat, vec, vec, vec),
        in_specs=_lru_specs(T) + [_colblock(T, 0), _colblock(T, 4)],
        out_specs=(_colblock(T, 0), _colblock(T, 0), _vecblock(8), _vecblock(), matspec, matspec, _vecblock(),
                   _vecblock(), _vecblock()),
        compiler_params=_cparams(("parallel",)),
    )(p, p, cw, cb, wa, wx, ba, bx, lam, hs, dy)


def _ffn_forward(h, gamma, wg, wu, wd, tag):
    xn = _rms_fwd(h, gamma, f"rms_fwd_{tag}")
    g, u, a = _ffn_up(xn, wg, wu, f"ffn_up_{tag}")
    if callable(wd):
        wd = wd(a)
    out = _mm([(a, wd)], "NN", f"ffn_down_{tag}", res=h, res_scale=0.5)
    return out, (h, xn, g, u, a)


def _after(w, tok):
    return w if tok is None else w + tok[0, 0].astype(w.dtype)


def _ffn_backward(saved, gamma, wg, wu, wd, dout, tok, tag, emit_one=None):
    h, xn, p, q, a = saved
    dout, dout16 = dout
    dwd = _mm([(a, dout16)], "TN", f"ffn_dwd_{tag}", res_scale=0.5, out_dtype=MXU, dep=tok)
    if emit_one is not None:
        tok = emit_one('wd', dwd)
    dg, du = _ffn_dact(dout16, wd, p, q, 0.5, f"ffn_dact_{tag}", dep=tok)
    dwg = _mm([(dg, xn)], "TN", f"ffn_dwg_{tag}", out_dtype=MXU)
    if emit_one is not None:
        tok = emit_one('wg', dwg)
    dwu = _mm([(du, xn)], "TN", f"ffn_dwu_{tag}", out_dtype=MXU, dep=tok if emit_one is not None else None)
    if emit_one is not None:
        tok = emit_one('wu', dwu)
    dxn = _mm([(dg, wg), (du, wu)], "NN", f"ffn_dxn_{tag}", dep=tok if emit_one is not None else None)
    dh, dh16, dgamma = _rms_bwd(h, gamma, dxn, dout, f"rms_bwd_{tag}")
    return (dh, dh16), dgamma, dwg, dwu, dwd


def _blockdiag(w, per):
    n, k, _ = w.shape
    out = jnp.zeros((n // per, per * k, per * k), w.dtype)
    for i in range(per):
        out = out.at[:, i * k:(i + 1) * k, i * k:(i + 1) * k].set(w[i::per])
    return out


def _blockdiag_grad(g, per, k):
    parts = [g[:, i * k:(i + 1) * k, i * k:(i + 1) * k] for i in range(per)]
    return jnp.stack(parts, axis=1).reshape(-1, k, k)


def _local_step(x, tgt, W, fetch, emit):
    fetch(0, x)
    seq, D = x.shape
    lr = N_META + seq
    nreal = -(-lr // CH)
    T = CH * (nreal + 2)
    if T > 640 and T % 640:
        T += 640 - T % 640
    lo, real_end = CH + N_META, CH + lr
    zf = lambda n: jnp.zeros((n, D), F32)
    h0 = jnp.concatenate([zf(CH), W['meta_tokens'], x, zf(T - real_end)], axis=0)
    tgt_p = jnp.concatenate([zf(lo), tgt, zf(T - real_end)], axis=0)
    row = lambda v: v.reshape(1, -1)
    G = {}

    h = h0
    saved = []
    for l in range(2):
        wd1 = W['ffn1_wd', l] if l else (lambda after: (fetch(1, after), W['ffn1_wd', 0])[1])
        h, s1 = _ffn_forward(h, row(W['ffn1_norm'][l]), W['ffn1_wg', l], W['ffn1_wu', l], wd1, f"a{l}")
        hm = h
        fetch(3 * l + 2, hm)
        xn = _rms_fwd(hm, row(W['mix_norm'][l]), f"rms_fwd_mix{l}")
        if l == 0:
            p = _mm([(xn, W['w_in_even'])], "NT", "in_even")
            wbd = _blockdiag(W['pool_w'][0], 4)[0]
            l0, l1 = row(W['hgrn_lb_logits'][0]), row(W['hgrn_lb_logits'][1])
            ya = _pool_fwd(p, wbd, W['pool_scale'], nreal, real_end, "pool_fwd")
            yb, states = _hgrn_fwd(p, l0, l1, W['hgrn_gnorm'], nreal, real_end, "hgrn_fwd")
            wo = W['w_out_even']
            h = _mm([(ya, wo[:256]), (yb, wo[256:])], "NN", "out_even", res=hm)
            sm = (hm, xn, p, wbd, l0, l1, ya, yb, states)
        else:
            p = _mm([(xn, W['w_in_odd'])], "NT", "in_odd")
            cw = jnp.pad(W['conv_w'][0], ((0, 1), (0, 0)))
            lw = jnp.pad(W['lru_conv_w'][0], ((0, 4), (0, 0)))
            wa, wx = _blockdiag(W['lru_wa'][0], 2), _blockdiag(W['lru_wx'][0], 2)
            yc = _conf_fwd(p, cw, W['conv_b'], W['conv_ln_g'], W['conv_ln_b'], nreal, real_end, "conf_fwd")
            yd, hs = _lru_fwd(p, lw, W['lru_conv_b'], wa, wx, W['lru_ba'], W['lru_bx'], W['lru_lambda'], nreal,
                              real_end, "lru_fwd")
            wo = W['w_out_odd']
            h = _mm([(yc, wo[:512]), (yd, wo[512:])], "NN", "out_odd", res=hm)
            sm = (hm, xn, p, cw, lw, wa, wx, yc, yd, hs)
        fetch(3 * l + 3, h)
        h, s2 = _ffn_forward(h, row(W['ffn2_norm'][l]), W['ffn2_wg', l], W['ffn2_wu', l], W['ffn2_wd', l], f"b{l}")
        if l == 0:
            fetch(4, h)
        saved.append((s1, sm, s2))

    loss8, dh, dh16, dfin = _loss_head(h, row(W['final_norm']), tgt_p, lo, real_end, "loss_head")
    dh = (dh, dh16)
    G['final_norm'] = dfin[0]

    per_layer = {k: [None, None] for k in ('ffn1_norm', 'mix_norm', 'ffn2_norm')}
    tok = None
    for l in (1, 0):
        s1, sm, s2 = saved[l]
        dh, dn, dwg, dwu, dwd = _ffn_backward(s2, row(W['ffn2_norm'][l]), W['ffn2_wg', l], W['ffn2_wu', l],
                                              W['ffn2_wd', l], dh, tok, f"b{l}")
        per_layer['ffn2_norm'][l] = dn[0]
        tok = emit(f"ffn2_{l}", [('ffn2_wg', l, dwg), ('ffn2_wu', l, dwu), ('ffn2_wd', l, dwd)])
        if l == 0:
            hm, xn, p, wbd, l0, l1, ya, yb, states = sm
            wo, wi = W['w_out_even'], W['w_in_even']
            dwo = jnp.concatenate([_mm([(ya, dh[1])], "TN", "dwo_even_a", out_dtype=MXU),
                                   _mm([(yb, dh[1])], "TN", "dwo_even_b", out_dtype=MXU)], axis=0)
            dy = _mm([(dh[1], wo)], "NT", "dy_even", dep=tok)
            dpp, dwbd, dsc = _pool_bwd(p, wbd, W['pool_scale'], dy, nreal, real_end, "pool_bwd")
            dq, df, di, dg, dl0, dl1, dgn = _hgrn_bwd(p, l0, l1, W['hgrn_gnorm'], states, dy, nreal, real_end,
                                                      "hgrn_bwd")
            G['pool_w'] = _blockdiag_grad(dwbd[None], 4, 64)[None]
            G['pool_scale'] = dsc
            G['hgrn_lb_logits'] = jnp.concatenate([dl0, dl1], axis=0)
            G['hgrn_gnorm'] = dgn
            parts = [dpp, dq, df, di, dg]
            offs = [0, 256, 1024, 1792, 2560, 3328]
            dwi = jnp.concatenate(
                [_mm([(dpart, xn)], "TN", f"dwi_even_{k}", out_dtype=MXU) for k, dpart in enumerate(parts)], axis=0)
            dxn = _mm([(dpart, wi[offs[k]:offs[k + 1]]) for k, dpart in enumerate(parts)], "NN", "dxn_even")
            tok = emit("even", [('w_in_even', None, dwi), ('w_out_even', None, dwo)])
        else:
            hm, xn, p, cw, lw, wa, wx, yc, yd, hs = sm
            wo, wi = W['w_out_odd'], W['w_in_odd']
            dwo = jnp.concatenate([_mm([(yc, dh[1])], "TN", "dwo_odd_c", out_dtype=MXU),
                                   _mm([(yd, dh[1])], "TN", "dwo_odd_d", out_dtype=MXU)], axis=0)
            dy = _mm([(dh[1], wo)], "NT", "dy_odd", dep=tok)
            da, db, dcw, dcb, dlg, dlb = _conf_bwd(p, cw, W['conv_b'], W['conv_ln_g'], W['conv_ln_b'], dy, nreal,
                                                   real_end, "conf_bwd")
            dx, dgate, dlw, dlcb, dwa, dwx, dba, dbx, dlam = _lru_bwd(
                p, lw, W['lru_conv_b'], wa, wx, W['lru_ba'], W['lru_bx'], W['lru_lambda'], hs, dy, nreal, real_end,
                "lru_bwd")
            G['conv_w'], G['conv_b'], G['conv_ln_g'], G['conv_ln_b'] = dcw[None, :CONV_W], dcb, dlg, dlb
            G['lru_conv_w'], G['lru_conv_b'] = dlw[None, :LRU_W], dlcb
            G['lru_wa'] = _blockdiag_grad(dwa, 2, 64)[None]
            G['lru_wx'] = _blockdiag_grad(dwx, 2, 64)[None]
            G['lru_ba'], G['lru_bx'], G['lru_lambda'] = dba, dbx, dlam
            parts = [da, db, dx, dgate]
            dwi = jnp.concatenate(
                [_mm([(dpart, xn)], "TN", f"dwi_odd_{k}", out_dtype=MXU) for k, dpart in enumerate(parts)], axis=0)
            dxn = _mm([(dpart, wi[512 * k:512 * (k + 1)]) for k, dpart in enumerate(parts)], "NN", "dxn_odd")
            tok = emit("odd", [('w_in_odd', None, dwi), ('w_out_odd', None, dwo)])
        dh, dh16, dn = _rms_bwd(hm, _after(row(W['mix_norm'][l]), tok), dxn, dh[0], f"rms_bwd_mix{l}")
        dh = (dh, dh16)
        per_layer['mix_norm'][l] = dn[0]
        one = None if l == 1 else (lambda sfx, g: emit(f"ffn1_0_{sfx}", [('ffn1_' + sfx, 0, g)]))
        dh, dn, dwg, dwu, dwd = _ffn_backward(s1, row(W['ffn1_norm'][l]), W['ffn1_wg', l], W['ffn1_wu', l],
                                              W['ffn1_wd', l], dh, None, f"a{l}", one)
        per_layer['ffn1_norm'][l] = dn[0]
        if l == 1:
            tok = emit("ffn1_1", [('ffn1_wg', l, dwg), ('ffn1_wu', l, dwu), ('ffn1_wd', l, dwd)])
    for k, v in per_layer.items():
        G[k] = jnp.stack(v, axis=0)
    G['meta_tokens'] = dh[0][CH:lo]
    return loss8[0, 0], dh[0][lo:real_end], G


def _pack(arrs):
    flat = jnp.concatenate([a.reshape(-1).astype(F32) for a in arrs])
    n = flat.shape[0]
    padded = -(-n // 1024) * 1024
    return jnp.pad(flat, (0, padded - n)).reshape(-1, 128)


def _unpack(packed, shapes):
    flat = packed.reshape(-1)
    out, off = [], 0
    for s in shapes:
        n = math.prod(s)
        out.append(flat[off:off + n].reshape(s))
        off += n
    return out


def _pack8(arrs):
    flat = jnp.concatenate([a.reshape(NDEV, -1).astype(F32) for a in arrs], axis=1)
    n = flat.shape[1]
    padded = -(-n // 1024) * 1024
    return jnp.pad(flat, ((0, 0), (0, padded - n))).reshape(NDEV, -1, 128)


def _unpack8(packed, shapes):
    flat = packed.reshape(NDEV, -1)
    out, off = [], 0
    for s in shapes:
        n = math.prod(s)
        out.append(flat[:, off:off + n].reshape((NDEV,) + tuple(s)))
        off += n
    return out


def _to_full(gathered, axis):
    s = gathered.shape[1:]
    return jnp.moveaxis(gathered, 0, axis).reshape(s[:axis] + (NDEV * s[axis],) + s[axis + 1:])


def _to_slots(full, axis):
    s = full.shape
    return jnp.moveaxis(full.reshape(s[:axis] + (NDEV, s[axis] // NDEV) + s[axis + 1:]), axis, 0)


def kernel(x, meta_tokens, ffn1_norm, ffn1_wg, ffn1_wu, ffn1_wd, mix_norm, ffn2_norm, ffn2_wg, ffn2_wu, ffn2_wd, w_in_even, pool_w, pool_scale, hgrn_lb_logits, hgrn_gnorm, w_out_even, w_in_odd, conv_w, conv_b, conv_ln_g, conv_ln_b, lru_conv_w, lru_conv_b, lru_wa, lru_ba, lru_wx, lru_bx, lru_lambda, w_out_odd, final_norm, loss_target, m_meta_tokens, m_ffn1_norm, m_ffn1_wg, m_ffn1_wu, m_ffn1_wd, m_mix_norm, m_ffn2_norm, m_ffn2_wg, m_ffn2_wu, m_ffn2_wd, m_w_in_even, m_pool_w, m_pool_scale, m_hgrn_lb_logits, m_hgrn_gnorm, m_w_out_even, m_w_in_odd, m_conv_w, m_conv_b, m_conv_ln_g, m_conv_ln_b, m_lru_conv_w, m_lru_conv_b, m_lru_wa, m_lru_ba, m_lru_wx, m_lru_bx, m_lru_lambda, m_w_out_odd, m_final_norm, v_meta_tokens, v_ffn1_norm, v_ffn1_wg, v_ffn1_wu, v_ffn1_wd, v_mix_norm, v_ffn2_norm, v_ffn2_wg, v_ffn2_wu, v_ffn2_wd, v_w_in_even, v_pool_w, v_pool_scale, v_hgrn_lb_logits, v_hgrn_gnorm, v_w_out_even, v_w_in_odd, v_conv_w, v_conv_b, v_conv_ln_g, v_conv_ln_b, v_lru_conv_w, v_lru_conv_b, v_lru_wa, v_lru_ba, v_lru_wx, v_lru_bx, v_lru_lambda, v_w_out_odd, v_final_norm):
    args = (meta_tokens, ffn1_norm, ffn1_wg, ffn1_wu, ffn1_wd, mix_norm, ffn2_norm, ffn2_wg, ffn2_wu, ffn2_wd, w_in_even, pool_w, pool_scale, hgrn_lb_logits, hgrn_gnorm, w_out_even, w_in_odd, conv_w, conv_b, conv_ln_g, conv_ln_b, lru_conv_w, lru_conv_b, lru_wa, lru_ba, lru_wx, lru_bx, lru_lambda, w_out_odd, final_norm)
    margs = (m_meta_tokens, m_ffn1_norm, m_ffn1_wg, m_ffn1_wu, m_ffn1_wd, m_mix_norm, m_ffn2_norm, m_ffn2_wg, m_ffn2_wu, m_ffn2_wd, m_w_in_even, m_pool_w, m_pool_scale, m_hgrn_lb_logits, m_hgrn_gnorm, m_w_out_even, m_w_in_odd, m_conv_w, m_conv_b, m_conv_ln_g, m_conv_ln_b, m_lru_conv_w, m_lru_conv_b, m_lru_wa, m_lru_ba, m_lru_wx, m_lru_bx, m_lru_lambda, m_w_out_odd, m_final_norm)
    vargs = (v_meta_tokens, v_ffn1_norm, v_ffn1_wg, v_ffn1_wu, v_ffn1_wd, v_mix_norm, v_ffn2_norm, v_ffn2_wg, v_ffn2_wu, v_ffn2_wd, v_w_in_even, v_pool_w, v_pool_scale, v_hgrn_lb_logits, v_hgrn_gnorm, v_w_out_even, v_w_in_odd, v_conv_w, v_conv_b, v_conv_ln_g, v_conv_ln_b, v_lru_conv_w, v_lru_conv_b, v_lru_wa, v_lru_ba, v_lru_wx, v_lru_bx, v_lru_lambda, v_w_out_odd, v_final_norm)
    Wl = dict(zip(W_NAMES, args))
    Ml = dict(zip(W_NAMES, margs))
    Vl = dict(zip(W_NAMES, vargs))

    small_shapes = [Wl[n].shape for n in SMALL_SHARDED]
    ffn = lambda p, l: [(p + s, l) for s in ('_wg', '_wu', '_wd')]
    mix = lambda p: [('w_in_' + p, None), ('w_out_' + p, None)]
    ggroups = [ffn('ffn1', 0)[:2], ffn('ffn1', 0)[2:], mix('even'), ffn('ffn2', 0), ffn('ffn1', 1), mix('odd'),
               ffn('ffn2', 1)]
    colsharded = lambda n: SHARD_AXIS[n] == 2

    def shard(n, l):
        w = Wl[n][0 if l is None else l].astype(MXU)
        return w.T if colsharded(n) else w

    srcs = [[shard(n, l) for n, l in g] for g in ggroups]
    srcs[0] = [_pack([Wl[n] for n in SMALL_SHARDED])] + srcs[0]
    handles, _ = _exchange_start(srcs, True, "gather_start")
    W = {n: Wl[n] for n in REPLICATED}

    def fetch(k, after):
        lands = _exchange_wait(handles[k], True, after, f"gather_wait_{k}")
        if k == 0:
            for n, g in zip(SMALL_SHARDED, _unpack8(lands[0], small_shapes)):
                W[n] = _to_full(g, SHARD_AXIS[n])
            lands = lands[1:]
        for (n, l), g in zip(ggroups[k], lands):
            W[n if l is None else (n, l)] = g.reshape(-1, g.shape[-1])

    pending = []

    def emit(tag, grads):
        slots = [g.astype(MXU).reshape(NDEV, -1, g.shape[-1]) for _, _, g in grads]
        hs, token = _exchange_start([slots], False, f"scatter_start_{tag}")
        pending.append((tag, hs[0], [(n, l) for n, l, _ in grads]))
        return token

    loss_part, grad_x, G = _local_step(x[0], loss_target[0], W, fetch, emit)

    small_slots = [_to_slots(G[n].astype(F32), SHARD_AXIS[n]) for n in SMALL_SHARDED]
    send = [_pack8(small_slots), _pack([G[n] for n in REPLICATED]),
            jnp.broadcast_to(loss_part, (8, 128))]
    got = _exchange(send, [False, True, True], "scatter_small")
    loss = jnp.sum(got[2][:, 0, 0])
    recv = {}
    for tag, handle, keys in pending:
        for key, r in zip(keys, _exchange_wait(handle, False, got[0], f"scatter_wait_{tag}")):
            recv[key] = r

    outs = {}
    for n in BIG:
        shp = Wl[n].shape
        C = shp[-1]
        rs = [recv[n, None]] if shp[0] == 1 else [recv[n, l] for l in range(shp[0])]
        if colsharded(n):
            rs = [_sum8(r, f"sum8_{n}_{j}").T for j, r in enumerate(rs)]
        res = _adamw(rs, Wl[n].reshape(-1, C), Ml[n].reshape(-1, C), Vl[n].reshape(-1, C), f"adamw_{n}",
                     summed=colsharded(n))
        outs[n] = [o.reshape(shp) for o in res]
    for names, r, tag in ((SMALL_SHARDED, got[0], "small"), (REPLICATED, got[1], "repl")):
        shapes = [Wl[n].shape for n in names]
        res = _adamw([r], _pack([Wl[n] for n in names]), _pack([Ml[n] for n in names]),
                     _pack([Vl[n] for n in names]), f"adamw_{tag}")
        unp = [_unpack(o, shapes) for o in res]
        for k, n in enumerate(names):
            outs[n] = [unp[j][k] for j in range(4)]

    result = [loss, grad_x[None]]
    for j in range(4):
        result += [outs[n][j] for n in W_NAMES]
    return tuple(result)
```

```python
import functools
import math

import jax
import jax.numpy as jnp
from jax import lax
from jax.experimental import pallas as pl
from jax.experimental.pallas import tpu as pltpu

F32 = jnp.float32
MXU = jnp.bfloat16
EPS = 1e-6
CH = 128
HG = 128
N_META = 16
CONV_W = 31
LRU_W = 4
LRU_C = 8.0
VMEM_LIMIT = 48 * 2 ** 20
MM_VMEM_BUDGET = 36 * 2 ** 20
ADAM_LR, ADAM_B1, ADAM_B2, ADAM_EPS, ADAM_WD, ADAM_STEP = 0.001, 0.9, 0.999, 1e-08, 0.01, 10
MESH_AXES = ("x", "y", "c")
NDEV = 8

W_NAMES = ['meta_tokens', 'ffn1_norm', 'ffn1_wg', 'ffn1_wu', 'ffn1_wd', 'mix_norm', 'ffn2_norm', 'ffn2_wg', 'ffn2_wu',
           'ffn2_wd', 'w_in_even', 'pool_w', 'pool_scale', 'hgrn_lb_logits', 'hgrn_gnorm', 'w_out_even', 'w_in_odd',
           'conv_w', 'conv_b', 'conv_ln_g', 'conv_ln_b', 'lru_conv_w', 'lru_conv_b', 'lru_wa', 'lru_ba', 'lru_wx',
           'lru_bx', 'lru_lambda', 'w_out_odd', 'final_norm']
SHARD_AXIS = {'meta_tokens': 1, 'ffn1_wg': 2, 'ffn1_wu': 2, 'ffn1_wd': 1, 'ffn2_wg': 2, 'ffn2_wu': 2, 'ffn2_wd': 1,
              'w_in_even': 2, 'w_out_even': 1, 'w_in_odd': 2, 'conv_w': 2, 'conv_b': 1, 'conv_ln_g': 1,
              'conv_ln_b': 1, 'lru_conv_w': 2, 'lru_conv_b': 1, 'lru_ba': 1, 'lru_bx': 1, 'lru_lambda': 1,
              'w_out_odd': 1}
BIG = ['ffn1_wg', 'ffn1_wu', 'ffn1_wd', 'ffn2_wg', 'ffn2_wu', 'ffn2_wd', 'w_in_even', 'w_out_even', 'w_in_odd',
       'w_out_odd']
SMALL_SHARDED = [n for n in W_NAMES if n in SHARD_AXIS and n not in BIG]
REPLICATED = [n for n in W_NAMES if n not in SHARD_AXIS]


def _cparams(sem=None, vmem=VMEM_LIMIT):
    return pltpu.CompilerParams(dimension_semantics=sem, vmem_limit_bytes=vmem)


def _tiles(n):
    return [c for c in range(128, n + 1, 128) if n % c == 0] or [n]


def _tile(n, cap=1024):
    return max([c for c in _tiles(n) if c <= cap], default=_tiles(n)[0])


def _rowtile(n):
    for c in (256, 352, 128, 64, 32, 16, 8):
        if n % c == 0:
            return c
    return n


def _copies(srcs, lands, bcast, ssem, rsem, lsem):
    x, y, c = lax.axis_index("x"), lax.axis_index("y"), lax.axis_index("c")
    me = 4 * x + 2 * y + c
    locs, sends, recvs = [], [], []
    for a in range(len(srcs)):
        locs.append(pltpu.make_async_copy(srcs[a] if bcast[a] else srcs[a].at[me], lands[a].at[me], lsem.at[a]))
        for m in range(1, NDEV):
            px = 1 - x if (m >> 2) & 1 else x
            py = 1 - y if (m >> 1) & 1 else y
            pc = 1 - c if m & 1 else c
            peer = 4 * px + 2 * py + pc
            src = srcs[a] if bcast[a] else srcs[a].at[peer]
            k = a * NDEV + m
            for dst, out in ((lands[a].at[me], sends), (lands[a].at[peer], recvs)):
                out.append(pltpu.make_async_remote_copy(src_ref=src, dst_ref=dst, send_sem=ssem.at[k],
                                                        recv_sem=rsem.at[k], device_id=(px, py, pc),
                                                        device_id_type=pl.DeviceIdType.MESH))
    return locs, sends, recvs


def _land_shape(arr, bc):
    return (NDEV,) + tuple(arr.shape if bc else arr.shape[1:])


def _exchange(arrays, bcast, name):
    n = len(arrays)

    def body(*refs):
        locs, sends, recvs = _copies(refs[:n], refs[n:2 * n], bcast, refs[2 * n], refs[2 * n + 1], refs[2 * n + 2])
        for d in locs + sends:
            d.start()
        for r in recvs:
            r.wait_recv()
        for s in sends:
            s.wait_send()
        for loc in locs:
            loc.wait()

    out_shape = tuple(jax.ShapeDtypeStruct(_land_shape(arr, bc), arr.dtype) for arr, bc in zip(arrays, bcast))
    any_spec = pl.BlockSpec(memory_space=pl.ANY)
    return pl.pallas_call(
        body, name=name, out_shape=out_shape, in_specs=[any_spec] * n, out_specs=tuple([any_spec] * n),
        scratch_shapes=[pltpu.SemaphoreType.DMA((n * NDEV,)), pltpu.SemaphoreType.DMA((n * NDEV,)),
                        pltpu.SemaphoreType.DMA((n,))],
    )(*arrays)


def _gather_two_level(arrays, name):
    n = len(arrays)

    def body(*refs):
        ins, outs, ssem, rsem, lsem = refs[:n], refs[n:2 * n], refs[2 * n], refs[2 * n + 1], refs[2 * n + 2]
        x, y, c = lax.axis_index("x"), lax.axis_index("y"), lax.axis_index("c")
        me, sibling = (x, y, c), (x, y, 1 - c)
        chips = [(1 - x, y), (x, 1 - y), (1 - x, 1 - y)]
        flat = lambda p: 4 * p[0] + 2 * p[1] + p[2]

        def copy(a, k, block, to, src=None):
            slot = outs[a].at[flat(block)]
            return pltpu.make_async_remote_copy(src_ref=slot if src is None else src, dst_ref=slot,
                                                send_sem=ssem.at[a * 7 + k], recv_sem=rsem.at[a * 7 + k],
                                                device_id=to, device_id_type=pl.DeviceIdType.MESH)

        mine = [pltpu.make_async_copy(ins[a], outs[a].at[flat(me)], lsem.at[a]) for a in range(n)]
        first = []
        for a in range(n):
            mine[a].start()
            first.append(copy(a, 0, me, sibling, src=ins[a]))
            first += [copy(a, 1 + j, me, (*chip, c), src=ins[a]) for j, chip in enumerate(chips)]
        for cp in first:
            cp.start()
        passed = []
        for j, chip in enumerate(chips):
            for a in range(n):
                copy(a, 1 + j, (*chip, c), me).wait_recv()
                fwd = copy(a, 4 + j, (*chip, c), sibling)
                fwd.start()
                passed.append(fwd)
        for a in range(n):
            copy(a, 0, sibling, me).wait_recv()
            for j, chip in enumerate(chips):
                copy(a, 4 + j, (*chip, 1 - c), me).wait_recv()
        for cp in first + passed:
            cp.wait_send()
        for m in mine:
            m.wait()

    any_spec = pl.BlockSpec(memory_space=pl.ANY)
    return pl.pallas_call(
        body, name=name, out_shape=tuple(jax.ShapeDtypeStruct((NDEV,) + a.shape, a.dtype) for a in arrays),
        in_specs=[any_spec] * n, out_specs=tuple([any_spec] * n),
        scratch_shapes=[pltpu.SemaphoreType.DMA((n * 7,)), pltpu.SemaphoreType.DMA((n * 7,)),
                        pltpu.SemaphoreType.DMA((n,))],
    )(*arrays)


_HBM = pl.BlockSpec(memory_space=pltpu.HBM)
_SEM = pl.BlockSpec(memory_space=pltpu.SEMAPHORE)
_EFFECT = pltpu.SideEffectType.DATAFLOW_SIDE_EFFECTING


def _exchange_start(groups, bcast, name):
    sizes = [len(g) for g in groups]
    srcs = [a for g in groups for a in g]
    n, ng = len(srcs), len(groups)
    lands = [lax.empty(_land_shape(a, bcast), a.dtype) for a in srcs]

    def body(*refs):
        off = 0
        for gi, sz in enumerate(sizes):
            sem = refs[2 * n + 3 * gi:2 * n + 3 * gi + 3]
            locs, sends, _ = _copies(refs[off:off + sz], refs[n + off:n + off + sz], [bcast] * sz, *sem)
            for d in locs + sends:
                d.start()
            off += sz
        refs[-1][...] = jnp.zeros((8, 128), F32)

    sems = []
    for sz in sizes:
        sems += [pltpu.SemaphoreType.DMA((sz * NDEV,)), pltpu.SemaphoreType.DMA((sz * NDEV,)),
                 pltpu.SemaphoreType.DMA((sz,))]
    thru = [pltpu.HBM(a.shape, a.dtype) for a in srcs + lands]
    outs = pl.pallas_call(
        body, name=name, out_shape=tuple(sems + thru + [jax.ShapeDtypeStruct((8, 128), F32)]),
        in_specs=[_HBM] * (2 * n),
        out_specs=tuple([_SEM] * (3 * ng) + [_HBM] * (2 * n) + [pl.BlockSpec(memory_space=pltpu.VMEM)]),
        input_output_aliases={i: 3 * ng + i for i in range(2 * n)},
        compiler_params=pltpu.CompilerParams(has_side_effects=_EFFECT),
    )(*[pltpu.with_memory_space_constraint(a, pltpu.HBM) for a in srcs + lands])
    handles, off = [], 0
    for gi, sz in enumerate(sizes):
        handles.append((outs[3 * gi:3 * gi + 3], outs[3 * ng + off:3 * ng + off + sz],
                        outs[3 * ng + n + off:3 * ng + n + off + sz]))
        off += sz
    return handles, outs[-1]


def _exchange_wait(handle, bcast, after, name):
    sems, srcs, lands = handle
    n = len(srcs)

    def body(*refs):
        locs, sends, recvs = _copies(refs[:n], refs[n:2 * n], [bcast] * n, *refs[2 * n:2 * n + 3])
        for r in recvs:
            r.wait_recv()
        for s in sends:
            s.wait_send()
        for loc in locs:
            loc.wait()

    outs = pl.pallas_call(
        body, name=name, out_shape=tuple(pltpu.HBM(a.shape, a.dtype) for a in list(srcs) + list(lands)),
        in_specs=[_HBM] * (2 * n) + [_SEM] * 3 + [pl.BlockSpec(memory_space=pl.ANY)],
        out_specs=tuple([_HBM] * (2 * n)), input_output_aliases={i: i for i in range(2 * n)},
        compiler_params=pltpu.CompilerParams(has_side_effects=_EFFECT),
    )(*srcs, *lands, *sems, after)
    return outs[n:]


def _sum8(recv, name):
    _, R, C = recv.shape
    br = _rowtile(R)

    def body(r_ref, o_ref):
        s = r_ref[0].astype(F32)
        for k in range(1, NDEV):
            s = s + r_ref[k].astype(F32)
        o_ref[...] = s

    return pl.pallas_call(
        body, name=name, grid=(R // br,), out_shape=jax.ShapeDtypeStruct((R, C), F32),
        in_specs=[pl.BlockSpec((NDEV, br, C), lambda i: (0, i, 0))], out_specs=pl.BlockSpec((br, C), lambda i: (i, 0)),
        compiler_params=_cparams(("parallel",)),
    )(recv)


def _adamw(recvs, w, m, v, name, summed=False):
    R, C = w.shape
    nr = len(recvs)
    br = _rowtile(R // nr)
    nb0 = R // nr // br

    def body(*refs):
        w_ref, m_ref, v_ref, g_o, d_o, m_o, v_o = refs[nr:]
        g = None
        for j in range(nr):
            if summed:
                s = refs[j][...]
            else:
                s = refs[j][0].astype(F32)
                for k in range(1, NDEV):
                    s = s + refs[j][k].astype(F32)
            g = s if g is None else jnp.where(pl.program_id(0) >= j * nb0, s, g)
        mn = ADAM_B1 * m_ref[...] + (1.0 - ADAM_B1) * g
        vn = ADAM_B2 * v_ref[...] + (1.0 - ADAM_B2) * (g * g)
        m_hat = mn / (1.0 - ADAM_B1 ** ADAM_STEP)
        v_hat = vn / (1.0 - ADAM_B2 ** ADAM_STEP)
        g_o[...] = g
        d_o[...] = -ADAM_LR * (m_hat / (jnp.sqrt(v_hat) + ADAM_EPS) + ADAM_WD * w_ref[...])
        m_o[...] = mn
        v_o[...] = vn

    def rspec(j):
        if summed:
            return pl.BlockSpec((br, C), lambda i: (jnp.clip(i - j * nb0, 0, nb0 - 1), 0))
        return pl.BlockSpec((NDEV, br, C), lambda i: (0, jnp.clip(i - j * nb0, 0, nb0 - 1), 0))

    blk = pl.BlockSpec((br, C), lambda i: (i, 0))
    sds = jax.ShapeDtypeStruct((R, C), F32)
    return pl.pallas_call(
        body, name=name, grid=(R // br,), out_shape=(sds, sds, sds, sds),
        in_specs=[rspec(j) for j in range(nr)] + [blk, blk, blk], out_specs=(blk, blk, blk, blk),
        compiler_params=_cparams(("arbitrary",)),
    )(*recvs, w, m, v)


_DIMS = {"NN": ((1,), (0,)), "NT": ((1,), (1,)), "TN": ((0,), (0,))}


def _dot(a, b, mode="NN"):
    return lax.dot_general(a.astype(MXU), b.astype(MXU), (_DIMS[mode], ((), ())), preferred_element_type=F32)


def _dotf(a, b, mode="NN"):
    return lax.dot_general(a, b, (_DIMS[mode], ((), ())), precision=lax.Precision.HIGH,
                           preferred_element_type=F32)


def _mm(pairs, mode, name, res=None, res_scale=1.0, out_dtype=F32, dep=None):
    a0, b0 = pairs[0]
    M = a0.shape[1] if mode == "TN" else a0.shape[0]
    N = b0.shape[0] if mode == "NT" else b0.shape[1]
    npairs = len(pairs)

    def vmem_bytes(tm, tn):
        total = tm * tn * 4 * (2 + (2 if res is not None else 0) + 2)
        for a, b in pairs:
            ka = a.shape[0] if mode == "TN" else a.shape[1]
            kb = b.shape[1] if mode == "NT" else b.shape[0]
            for k, t, arr in ((ka, tm, a), (kb, tn, b)):
                total += k * t * (2 * arr.dtype.itemsize + (2 if arr.dtype == F32 else 0))
        return total

    tm, tn = max(((tm, tn) for tm in _tiles(M) for tn in _tiles(N) if vmem_bytes(tm, tn) <= MM_VMEM_BUDGET),
                 key=lambda t: (t[0] * t[1], t[1]), default=(_tiles(M)[0], _tiles(N)[0]))

    def body(*refs):
        acc = None
        for p in range(npairs):
            d = _dot(refs[2 * p][...], refs[2 * p + 1][...], mode)
            acc = d if acc is None else acc + d
        if res_scale != 1.0:
            acc = res_scale * acc
        if res is not None:
            acc = refs[2 * npairs][...] + acc
        refs[-1][...] = acc.astype(out_dtype)

    in_specs, args = [], []
    for a, b in pairs:
        if mode == "TN":
            in_specs.append(pl.BlockSpec((a.shape[0], tm), lambda i, j: (0, i)))
        else:
            in_specs.append(pl.BlockSpec((tm, a.shape[1]), lambda i, j: (i, 0)))
        if mode == "NT":
            in_specs.append(pl.BlockSpec((tn, b.shape[1]), lambda i, j: (j, 0)))
        else:
            in_specs.append(pl.BlockSpec((b.shape[0], tn), lambda i, j: (0, j)))
        args += [a, b]
    if res is not None:
        in_specs.append(pl.BlockSpec((tm, tn), lambda i, j: (i, j)))
        args.append(res)
    if dep is not None:
        in_specs.append(pl.BlockSpec(memory_space=pl.ANY))
        args.append(dep)
    return pl.pallas_call(
        body, name=name, grid=(M // tm, N // tn), out_shape=jax.ShapeDtypeStruct((M, N), out_dtype),
        in_specs=in_specs, out_specs=pl.BlockSpec((tm, tn), lambda i, j: (i, j)),
        compiler_params=_cparams(("parallel", "parallel")),
    )(*args)


def _rms_fwd(h, gamma, name):
    T, D = h.shape
    tm = _tile(T)

    def body(h_ref, g_ref, o_ref):
        x = h_ref[...]
        r = lax.rsqrt(jnp.mean(x * x, axis=-1, keepdims=True) + EPS)
        o_ref[...] = (x * r * g_ref[...]).astype(MXU)

    return pl.pallas_call(
        body, name=name, grid=(T // tm,), out_shape=jax.ShapeDtypeStruct((T, D), MXU),
        in_specs=[pl.BlockSpec((tm, D), lambda i: (i, 0)), pl.BlockSpec((1, D), lambda i: (0, 0))],
        out_specs=pl.BlockSpec((tm, D), lambda i: (i, 0)), compiler_params=_cparams(("parallel",)),
    )(h, gamma)


def _rms_bwd_math(x, gamma, dy):
    r = lax.rsqrt(jnp.mean(x * x, axis=-1, keepdims=True) + EPS)
    z = dy * gamma
    dx = r * z - x * (r * r * r) * jnp.mean(z * x, axis=-1, keepdims=True)
    dgamma = jnp.sum(dy * x * r, axis=0, keepdims=True)
    return dx, dgamma


def _rms_bwd(h, gamma, dxn, dres, name):
    T, D = h.shape
    tm = _tile(T)

    def body(h_ref, g_ref, dxn_ref, dres_ref, dh_ref, dh16_ref, dg_ref):
        dx, dgamma = _rms_bwd_math(h_ref[...], g_ref[...], dxn_ref[...])
        dh = dres_ref[...] + dx
        dh_ref[...] = dh
        dh16_ref[...] = dh.astype(MXU)

        @pl.when(pl.program_id(0) == 0)
        def _():
            dg_ref[...] = jnp.zeros_like(dg_ref)

        dg_ref[...] += dgamma

    row = pl.BlockSpec((tm, D), lambda i: (i, 0))
    vec = pl.BlockSpec((1, D), lambda i: (0, 0))
    return pl.pallas_call(
        body, name=name, grid=(T // tm,),
        out_shape=(jax.ShapeDtypeStruct((T, D), F32), jax.ShapeDtypeStruct((T, D), MXU),
                   jax.ShapeDtypeStruct((1, D), F32)),
        in_specs=[row, vec, row, row], out_specs=(row, row, vec), compiler_params=_cparams(("arbitrary",)),
    )(h, gamma, dxn, dres)


def _loss_head(h, gamma, tgt, lo, hi, name):
    T, D = h.shape
    tm = _tile(T)

    def body(h_ref, g_ref, t_ref, loss_ref, dh_ref, dh16_ref, dg_ref):
        i = pl.program_id(0)
        x = h_ref[...]
        r = lax.rsqrt(jnp.mean(x * x, axis=-1, keepdims=True) + EPS)
        y = x * r * g_ref[...]
        rows = i * tm + lax.broadcasted_iota(jnp.int32, (tm, 1), 0)
        valid = jnp.logical_and(rows >= lo, rows < hi)
        diff = jnp.where(valid, y - t_ref[...], 0.0)
        part = 0.5 * jnp.sum(jnp.sum(diff * diff, axis=-1, keepdims=True) / D, axis=0, keepdims=True)
        dx, dgamma = _rms_bwd_math(x, g_ref[...], diff / D)
        dh_ref[...] = dx
        dh16_ref[...] = dx.astype(MXU)

        @pl.when(i == 0)
        def _():
            dg_ref[...] = jnp.zeros_like(dg_ref)
            loss_ref[...] = jnp.zeros_like(loss_ref)

        dg_ref[...] += dgamma
        loss_ref[...] += jnp.broadcast_to(part, loss_ref.shape)

    row = pl.BlockSpec((tm, D), lambda i: (i, 0))
    vec = pl.BlockSpec((1, D), lambda i: (0, 0))
    lsp = pl.BlockSpec((8, 128), lambda i: (0, 0))
    return pl.pallas_call(
        body, name=name, grid=(T // tm,),
        out_shape=(jax.ShapeDtypeStruct((8, 128), F32), jax.ShapeDtypeStruct((T, D), F32),
                   jax.ShapeDtypeStruct((T, D), MXU), jax.ShapeDtypeStruct((1, D), F32)),
        in_specs=[row, vec, row], out_specs=(lsp, row, row, vec), compiler_params=_cparams(("arbitrary",)),
    )(h, gamma, tgt)


def _ffn_tiles(T, Fd):
    return (448 if T % 448 == 0 else _tile(T)), max(c for c in _tiles(Fd) if c <= 1536)


def _ffn_up(xn, wg, wu, name):
    T, D = xn.shape
    Fd = wg.shape[0]
    tm, tn = _ffn_tiles(T, Fd)

    def body(x_ref, wg_ref, wu_ref, p_ref, q_ref, a_ref):
        x = x_ref[...]
        g = _dot(x, wg_ref[...], "NT")
        u = _dot(x, wu_ref[...], "NT")
        sg = jax.nn.sigmoid(g)
        q = g * sg
        p_ref[...] = (u * (sg + q * (1.0 - sg))).astype(MXU)
        q_ref[...] = q.astype(MXU)
        a_ref[...] = (q * u).astype(MXU)

    wsp = pl.BlockSpec((tn, D), lambda j, i: (j, 0))
    osp = pl.BlockSpec((tm, tn), lambda j, i: (i, j))
    sds = jax.ShapeDtypeStruct((T, Fd), MXU)
    return pl.pallas_call(
        body, name=name, grid=(Fd // tn, T // tm), out_shape=(sds, sds, sds),
        in_specs=[pl.BlockSpec((tm, D), lambda j, i: (i, 0)), wsp, wsp], out_specs=(osp, osp, osp),
        compiler_params=_cparams(("parallel", "parallel")),
    )(xn, wg, wu)


def _ffn_dact(dy, wd, p, q, scale, name, dep=None):
    T, D = dy.shape
    Fd = wd.shape[0]
    tm, tn = _ffn_tiles(T, Fd)

    def body(dy_ref, wd_ref, p_ref, q_ref, *rest):
        dg_ref, du_ref = rest[-2:]
        da = scale * _dot(dy_ref[...], wd_ref[...], "NT")
        dg_ref[...] = (da * p_ref[...].astype(F32)).astype(MXU)
        du_ref[...] = (da * q_ref[...].astype(F32)).astype(MXU)

    osp = pl.BlockSpec((tm, tn), lambda j, i: (i, j))
    sds = jax.ShapeDtypeStruct((T, Fd), MXU)
    return pl.pallas_call(
        body, name=name, grid=(Fd // tn, T // tm), out_shape=(sds, sds),
        in_specs=[pl.BlockSpec((tm, D), lambda j, i: (i, 0)), pl.BlockSpec((tn, D), lambda j, i: (j, 0)), osp, osp]
        + ([] if dep is None else [pl.BlockSpec(memory_space=pl.ANY)]),
        out_specs=(osp, osp), compiler_params=_cparams(("parallel", "parallel")),
    )(dy, wd, p, q, *([] if dep is None else [dep]))


def _down(v, s):
    return v if s == 0 else pltpu.roll(v, s, 0)


def _up(v, s):
    return v if s == 0 else pltpu.roll(v, v.shape[0] - s, 0)


def _rows(n):
    return lax.broadcasted_iota(jnp.int32, (n, 1), 0)


def _zero_pad_rows(ref, lo_end, hi_start, T):
    ref[pl.ds(0, lo_end), :] = jnp.zeros((lo_end, ref.shape[1]), ref.dtype)
    if T > hi_start:
        ref[pl.ds(hi_start, T - hi_start), :] = jnp.zeros((T - hi_start, ref.shape[1]), ref.dtype)


def _colblock(T, off):
    return pl.BlockSpec((T, 128), lambda j: (0, off + j))


def _vecblock(rows=1):
    return pl.BlockSpec((rows, 128), lambda j: (0, j))


def _pool_lane_consts(n):
    lane = lax.broadcasted_iota(jnp.int32, (n, 256), 1)
    win = jnp.where(lane < 64, 2.0, jnp.where(lane < 128, 4.0, jnp.where(lane < 192, 8.0, 16.0)))
    return lane, win


def _pool_select(lane, s2, s4, s8, s16):
    return jnp.where(lane < 64, s2, jnp.where(lane < 128, s4, jnp.where(lane < 192, s8, s16)))


def _pool_mixed(xh, start):
    s2 = xh + _down(xh, 1)
    s4 = s2 + _down(s2, 2)
    s8 = s4 + _down(s4, 4)
    s16 = s8 + _down(s8, 8)
    n = xh.shape[0] - 16
    lane, _ = _pool_lane_consts(n + 16)
    _, win = _pool_lane_consts(n)
    t1 = (start - CH + 1 + _rows(n)).astype(F32)
    cnt = jnp.minimum(jnp.maximum(t1, 1.0), win)
    return _pool_select(lane, s2, s4, s8, s16)[16:] / cnt - xh[16:]


def _pool_fwd(p, wbd, scale, nreal, real_end, name):
    T = p.shape[0]

    def body(p_ref, w_ref, s_ref, y_ref):
        _zero_pad_rows(y_ref, CH, CH * (1 + nreal), T)

        def chunk(c, carry):
            start = pl.multiple_of(c * CH, CH)
            mixed = _pool_mixed(p_ref[pl.ds(start - 16, CH + 16), :], start)
            y = _dot(mixed, w_ref[...]) * s_ref[...]
            y_ref[pl.ds(start, CH), :] = jnp.where(start + _rows(CH) < real_end, y, 0.0)
            return carry

        _pairs_loop(nreal, lambda c, carry: chunk(c + 1, carry), 0)

    return pl.pallas_call(
        body, name=name, grid=(1,), out_shape=jax.ShapeDtypeStruct((T, 256), F32),
        in_specs=[pl.BlockSpec((T, 256), lambda j: (0, 0)), pl.BlockSpec((256, 256), lambda j: (0, 0)),
                  pl.BlockSpec((1, 256), lambda j: (0, 0))],
        out_specs=pl.BlockSpec((T, 256), lambda j: (0, 0)), compiler_params=_cparams(("arbitrary",)),
    )(p, wbd, scale)


def _pool_bwd(p, wbd, scale, dy, nreal, real_end, name):
    T = p.shape[0]

    def body(p_ref, w_ref, s_ref, dy_ref, dp_ref, dw_ref, ds_ref):
        _zero_pad_rows(dp_ref, CH, CH * (1 + nreal), T)
        dw_ref[...] = jnp.zeros_like(dw_ref)
        ds_ref[...] = jnp.zeros_like(ds_ref)

        def chunk(c, carry):
            start = pl.multiple_of(c * CH, CH)
            mixed = _pool_mixed(p_ref[pl.ds(start - 16, CH + 16), :], start)
            ypre = _dot(mixed, w_ref[...])
            n = CH + 16
            dye = jnp.where(start + _rows(n) < real_end, dy_ref[pl.ds(start, n), :], 0.0)
            dys = dye * s_ref[...]
            ds_ref[...] += jnp.sum(dye[:CH] * ypre, axis=0, keepdims=True)
            dw_ref[...] += _dot(mixed, dys[:CH], "TN")
            dmix = _dot(dys, w_ref[...], "NT")
            lane, win = _pool_lane_consts(n)
            t1 = (start - CH + 1 + _rows(n)).astype(F32)
            z = dmix / jnp.minimum(jnp.maximum(t1, 1.0), win)
            r2 = z + _up(z, 1)
            r4 = r2 + _up(r2, 2)
            r8 = r4 + _up(r4, 4)
            r16 = r8 + _up(r8, 8)
            dp_ref[pl.ds(start, CH), :] = (_pool_select(lane, r2, r4, r8, r16) - dmix)[:CH]
            return carry

        _pairs_loop(nreal, lambda c, carry: chunk(c + 1, carry), 0)

    full = lambda r, c: pl.BlockSpec((r, c), lambda j: (0, 0))
    return pl.pallas_call(
        body, name=name, grid=(1,),
        out_shape=(jax.ShapeDtypeStruct((T, 256), F32), jax.ShapeDtypeStruct((256, 256), F32),
                   jax.ShapeDtypeStruct((1, 256), F32)),
        in_specs=[full(T, 256), full(256, 256), full(1, 256), full(T, 256)],
        out_specs=(full(T, 256), full(256, 256), full(1, 256)), compiler_params=_cparams(("arbitrary",)),
    )(p, wbd, scale, dy)


def _hgrn_chunk(St, qr, fr, ir, gr, l0, l1, gn):
    rows = lax.broadcasted_iota(jnp.int32, (HG, HG), 0)
    cols = lax.broadcasted_iota(jnp.int32, (HG, HG), 1)
    causal = rows >= cols
    ltri = causal.astype(F32)
    lb = jax.nn.sigmoid(l0 - l1)
    sg = jax.nn.sigmoid(fr)
    logf = jnp.log(lb + (1.0 - lb) * sg)
    kk = (1.0 - lb) * (1.0 - sg)
    q = qr * jax.nn.sigmoid(qr)
    b = jnp.dot(ltri, logf, precision=lax.Precision.HIGH, preferred_element_type=F32)
    bl = jnp.sum(logf, axis=0, keepdims=True)
    bm = jnp.sum(jnp.where(_rows(HG) <= HG // 2, logf, 0.0), axis=0, keepdims=True)
    o = _dotf(q * jnp.exp(b), St, "NT")
    A = _dotf(q * jnp.exp(b - bm), kk * jnp.exp(bm - b), "NT")
    o = o + _dotf(jnp.where(causal, A, 0.0), ir)
    St_new = St * jnp.exp(bl) + _dotf(ir, kk * jnp.exp(bl - b), "TN")
    on = o * lax.rsqrt(jnp.mean(o * o, axis=-1, keepdims=True) + EPS) * gn
    return St_new, on * (gr * jax.nn.sigmoid(gr))


def _pairs_loop(n, step, init):
    u = 3 if n % 3 == 0 else 2 if n % 2 == 0 else 1

    def body(i, carry):
        for j in range(u):
            carry = step(u * i + j, carry)
        return carry

    return lax.fori_loop(0, n // u, body, init)


def _hgrn_specs(T):
    return [_colblock(T, 2), _colblock(T, 8), _colblock(T, 14), _colblock(T, 20), _vecblock(), _vecblock(),
            pl.BlockSpec((1, 128), lambda j: (0, 0))]


def _hgrn_fwd(p, l0, l1, gn, nreal, real_end, name):
    T = p.shape[0]
    nch = nreal * (CH // HG)

    def body(q_ref, f_ref, i_ref, g_ref, l0_ref, l1_ref, gn_ref, y_ref, s_ref):
        _zero_pad_rows(y_ref, CH, CH * (1 + nreal), T)

        def chunk(c, St):
            start = pl.multiple_of(CH + c * HG, HG)
            sl = pl.ds(start, HG)
            s_ref[0, c] = St
            St_new, y = _hgrn_chunk(St, q_ref[sl, :], f_ref[sl, :], i_ref[sl, :], g_ref[sl, :], l0_ref[...],
                                    l1_ref[...], gn_ref[...])
            y_ref[sl, :] = jnp.where(start + _rows(HG) < real_end, y, 0.0)
            return St_new

        _pairs_loop(nch, chunk, jnp.zeros((128, 128), F32))

    return pl.pallas_call(
        body, name=name, grid=(6,),
        out_shape=(jax.ShapeDtypeStruct((T, 768), F32), jax.ShapeDtypeStruct((6, nch, 128, 128), F32)),
        in_specs=_hgrn_specs(T),
        out_specs=(_colblock(T, 0), pl.BlockSpec((1, nch, 128, 128), lambda j: (j, 0, 0, 0))),
        compiler_params=_cparams(("parallel",)),
    )(p, p, p, p, l0, l1, gn)


def _hgrn_bwd(p, l0, l1, gn, states, dy, nreal, real_end, name):
    T = p.shape[0]
    nch = nreal * (CH // HG)

    def body(q_ref, f_ref, i_ref, g_ref, l0_ref, l1_ref, gn_ref, s_ref, dy_ref,
             dq_ref, df_ref, di_ref, dg_ref, dl0_ref, dl1_ref, dgn_ref):
        for r in (dq_ref, df_ref, di_ref, dg_ref):
            _zero_pad_rows(r, CH, CH * (1 + nreal), T)

        def chunk(k, carry):
            dSt, a0, a1, agn = carry
            c = nch - 1 - k
            start = pl.multiple_of(CH + c * HG, HG)
            sl = pl.ds(start, HG)
            _, vjp = jax.vjp(_hgrn_chunk, s_ref[0, c], q_ref[sl, :], f_ref[sl, :], i_ref[sl, :], g_ref[sl, :],
                             l0_ref[...], l1_ref[...], gn_ref[...])
            dyc = jnp.where(start + _rows(HG) < real_end, dy_ref[sl, :], 0.0)
            dS, dq, df, di, dg, d0, d1, dgn = vjp((dSt, dyc))
            dq_ref[sl, :] = dq
            df_ref[sl, :] = df
            di_ref[sl, :] = di
            dg_ref[sl, :] = dg
            return dS, a0 + d0, a1 + d1, agn + dgn

        z = jnp.zeros((1, 128), F32)
        _, a0, a1, agn = _pairs_loop(nch, chunk, (jnp.zeros((128, 128), F32), z, z, z))
        dl0_ref[...] = a0
        dl1_ref[...] = a1

        @pl.when(pl.program_id(0) == 0)
        def _():
            dgn_ref[...] = jnp.zeros_like(dgn_ref)

        dgn_ref[...] += agn

    big = jax.ShapeDtypeStruct((T, 768), F32)
    vec = jax.ShapeDtypeStruct((1, 768), F32)
    return pl.pallas_call(
        body, name=name, grid=(6,),
        out_shape=(big, big, big, big, vec, vec, jax.ShapeDtypeStruct((1, 128), F32)),
        in_specs=_hgrn_specs(T) + [pl.BlockSpec((1, nch, 128, 128), lambda j: (j, 0, 0, 0)), _colblock(T, 2)],
        out_specs=(_colblock(T, 0), _colblock(T, 0), _colblock(T, 0), _colblock(T, 0), _vecblock(), _vecblock(),
                   pl.BlockSpec((1, 128), lambda j: (0, 0))),
        compiler_params=_cparams(("arbitrary",), 60 * 2 ** 20),
    )(p, p, p, p, l0, l1, gn, states, dy)


def _glu(a, b):
    return a * jax.nn.sigmoid(b)


def _conv_post(cv, ln_g, ln_b):
    mu = jnp.mean(cv, axis=-1, keepdims=True)
    d = cv - mu
    var = jnp.mean(d * d, axis=-1, keepdims=True)
    un = d * lax.rsqrt(var + EPS) * ln_g + ln_b
    return un * jax.nn.sigmoid(un)


def _causal_conv(uh, w_ref, width, halo):
    acc = None
    for j in range(width):
        term = _down(uh, width - 1 - j) * w_ref[pl.ds(j, 1), :]
        acc = term if acc is None else acc + term
    return acc[halo:]


def _conf_fwd(p, cw, cb, lg, lb, nreal, real_end, name):
    T = p.shape[0]

    def body(a_ref, b_ref, w_ref, cb_ref, lg_ref, lb_ref, y_ref):
        _zero_pad_rows(y_ref, CH, CH * (1 + nreal), T)

        def chunk(c, carry):
            start = pl.multiple_of(c * CH, CH)
            ext = pl.ds(start - 32, CH + 32)
            cv = _causal_conv(_glu(a_ref[ext, :], b_ref[ext, :]), w_ref, CONV_W, 32) + cb_ref[...]
            y = _conv_post(cv, lg_ref[...], lb_ref[...])
            y_ref[pl.ds(start, CH), :] = jnp.where(start + _rows(CH) < real_end, y, 0.0)
            return carry

        _pairs_loop(nreal, lambda c, carry: chunk(c + 1, carry), 0)

    return pl.pallas_call(
        body, name=name, grid=(4,), out_shape=jax.ShapeDtypeStruct((T, 512), F32),
        in_specs=[_colblock(T, 0), _colblock(T, 4), _vecblock(32), _vecblock(), _vecblock(), _vecblock()],
        out_specs=_colblock(T, 0), compiler_params=_cparams(("parallel",)),
    )(p, p, cw, cb, lg, lb)


def _conf_bwd(p, cw, cb, lg, lb, dy, nreal, real_end, name):
    T = p.shape[0]

    def body(a_ref, b_ref, w_ref, cb_ref, lg_ref, lb_ref, dy_ref, da_ref, db_ref, dw_ref, dcb_ref, dlg_ref, dlb_ref):
        _zero_pad_rows(da_ref, CH, CH * (1 + nreal), T)
        _zero_pad_rows(db_ref, CH, CH * (1 + nreal), T)
        for r in (dw_ref, dcb_ref, dlg_ref, dlb_ref):
            r[...] = jnp.zeros_like(r)

        def chunk(c, carry):
            start = pl.multiple_of(c * CH, CH)
            ext = pl.ds(start - 32, CH + 64)
            ue = _glu(a_ref[ext, :], b_ref[ext, :])
            cv = _causal_conv(ue, w_ref, CONV_W, 32) + cb_ref[...]
            dye = jnp.where(start + _rows(CH + 32) < real_end, dy_ref[pl.ds(start, CH + 32), :], 0.0)
            _, vjp_cur = jax.vjp(_conv_post, cv[:CH], lg_ref[...], lb_ref[...])
            dc_cur, dlg, dlb = vjp_cur(dye[:CH])
            _, vjp_halo = jax.vjp(_conv_post, cv[CH:], lg_ref[...], lb_ref[...])
            dce = jnp.concatenate([dc_cur, vjp_halo(dye[CH:])[0]], axis=0)
            dlg_ref[...] += dlg
            dlb_ref[...] += dlb
            dcb_ref[...] += jnp.sum(dc_cur, axis=0, keepdims=True)
            du = None
            for j in range(CONV_W):
                w_j = w_ref[pl.ds(j, 1), :]
                term = _up(dce, CONV_W - 1 - j)[:CH] * w_j
                du = term if du is None else du + term
                dw_ref[pl.ds(j, 1), :] += jnp.sum(dc_cur * _up(ue, 2 + j)[:CH], axis=0, keepdims=True)
            cur = pl.ds(start, CH)
            _, vjp_glu = jax.vjp(_glu, a_ref[cur, :], b_ref[cur, :])
            da, db = vjp_glu(du)
            da_ref[cur, :] = da
            db_ref[cur, :] = db
            return carry

        _pairs_loop(nreal, lambda c, carry: chunk(c + 1, carry), 0)

    big = jax.ShapeDtypeStruct((T, 512), F32)
    vec = jax.ShapeDtypeStruct((1, 512), F32)
    return pl.pallas_call(
        body, name=name, grid=(4,), out_shape=(big, big, jax.ShapeDtypeStruct((32, 512), F32), vec, vec, vec),
        in_specs=[_colblock(T, 0), _colblock(T, 4), _vecblock(32), _vecblock(), _vecblock(), _vecblock(),
                  _colblock(T, 0)],
        out_specs=(_colblock(T, 0), _colblock(T, 0), _vecblock(32), _vecblock(), _vecblock(), _vecblock()),
        compiler_params=_cparams(("parallel",)),
    )(p, p, cw, cb, lg, lb, dy)


def _softplus_neg(lam):
    e = jnp.exp(-lam)
    small = e * (1.0 - e * (0.5 - e * (1.0 / 3.0 - e * 0.25)))
    return jnp.where(e < 0.02, small, jnp.log(1.0 + e))


def _one_minus_exp(x):
    series = -x * (1.0 + x * (0.5 + x * (1.0 / 6.0 + x * (1.0 / 24.0 + x * (1.0 / 120.0)))))
    return jnp.where(x > -0.05, series, 1.0 - jnp.exp(x))


def _lru_pre(u, wa, wx, ba, bx, lam, first):
    r = jax.nn.sigmoid(_dot(u, wa) + ba)
    i = jax.nn.sigmoid(_dot(u, wx) + bx)
    log_a = -LRU_C * r * _softplus_neg(lam)
    a = jnp.exp(log_a)
    mult = jnp.sqrt(_one_minus_exp(2.0 * log_a))
    return a, jnp.where(first, 1.0, mult) * (i * u)


def _gelu_gate(gate, h):
    inner = math.sqrt(2.0 / math.pi) * (gate + 0.044715 * (gate * gate * gate))
    return 0.5 * gate * (1.0 + jnp.tanh(inner)) * h


def _lru_specs(T):
    mat = pl.BlockSpec((1, 128, 128), lambda j: (j, 0, 0))
    return [_colblock(T, 8), _colblock(T, 12), _vecblock(8), _vecblock(), mat, mat, _vecblock(), _vecblock(),
            _vecblock()]


def _lru_fwd(p, cw, cb, wa, wx, ba, bx, lam, nreal, real_end, name):
    T = p.shape[0]

    def body(x_ref, g_ref, w_ref, cb_ref, wa_ref, wx_ref, ba_ref, bx_ref, lam_ref, y_ref, h_ref):
        _zero_pad_rows(y_ref, CH, CH * (1 + nreal), T)
        _zero_pad_rows(h_ref, CH, CH * (1 + nreal), T)
        rows = _rows(CH)

        def chunk(c, hprev):
            start = pl.multiple_of(c * CH, CH)
            u = _causal_conv(x_ref[pl.ds(start - 8, CH + 8), :], w_ref, LRU_W, 8) + cb_ref[...]
            A, B = _lru_pre(u, wa_ref[0], wx_ref[0], ba_ref[...], bx_ref[...], lam_ref[...], start + rows == CH)
            s = 1
            while s < CH:
                B = A * jnp.where(rows >= s, _down(B, s), 0.0) + B
                A = A * jnp.where(rows >= s, _down(A, s), 1.0)
                s *= 2
            h = B + A * hprev
            cur = pl.ds(start, CH)
            h_ref[cur, :] = h
            y_ref[cur, :] = jnp.where(start + rows < real_end, _gelu_gate(g_ref[cur, :], h), 0.0)
            return jnp.sum(jnp.where(rows == CH - 1, h, 0.0), axis=0, keepdims=True)

        _pairs_loop(nreal, lambda c, carry: chunk(c + 1, carry), jnp.zeros((1, 128), F32))

    big = jax.ShapeDtypeStruct((T, 512), F32)
    return pl.pallas_call(
        body, name=name, grid=(4,), out_shape=(big, big), in_specs=_lru_specs(T),
        out_specs=(_colblock(T, 0), _colblock(T, 0)), compiler_params=_cparams(("parallel",)),
    )(p, p, cw, cb, wa, wx, ba, bx, lam)


def _lru_bwd(p, cw, cb, wa, wx, ba, bx, lam, hs, dy, nreal, real_end, name):
    T = p.shape[0]

    def body(x_ref, g_ref, w_ref, cb_ref, wa_ref, wx_ref, ba_ref, bx_ref, lam_ref, h_ref, dy_ref,
             dx_ref, dgate_ref, dw_ref, dcb_ref, dwa_ref, dwx_ref, dba_ref, dbx_ref, dlam_ref):
        _zero_pad_rows(dx_ref, CH, CH * (1 + nreal), T)
        _zero_pad_rows(dgate_ref, CH, CH * (1 + nreal), T)
        for r in (dw_ref, dcb_ref, dwa_ref, dwx_ref, dba_ref, dbx_ref, dlam_ref):
            r[...] = jnp.zeros_like(r)
        rows = _rows(CH)

        def chunk(k, carry):
            cdh, du_head = carry
            c = nreal - k
            start = pl.multiple_of(c * CH, CH)
            cur = pl.ds(start, CH)
            xe = x_ref[pl.ds(start - 8, CH + 8), :]
            u = _causal_conv(xe, w_ref, LRU_W, 8) + cb_ref[...]
            first = start + rows == CH
            (a, _), vjp_pre = jax.vjp(lambda uu, m1, m2, b1, b2, ll: _lru_pre(uu, m1, m2, b1, b2, ll, first),
                                      u, wa_ref[0], wx_ref[0], ba_ref[...], bx_ref[...], lam_ref[...])
            h = h_ref[cur, :]
            hm1 = _down(h_ref[pl.ds(start - 8, CH + 8), :], 1)[8:]
            _, vjp_post = jax.vjp(_gelu_gate, g_ref[cur, :], h)
            dgate, D = vjp_post(jnp.where(start + rows < real_end, dy_ref[cur, :], 0.0))
            dgate_ref[cur, :] = dgate
            D = D + jnp.where(rows == CH - 1, cdh, 0.0)
            C = jnp.where(rows < CH - 1, _up(a, 1), 0.0)
            s = 1
            while s < CH:
                D = D + C * jnp.where(rows + s < CH, _up(D, s), 0.0)
                C = C * jnp.where(rows + s < CH, _up(C, s), 1.0)
                s *= 2
            du, dwa, dwx, dba, dbx, dlam = vjp_pre((D * hm1, D))
            dwa_ref[0] += dwa
            dwx_ref[0] += dwx
            dba_ref[...] += dba
            dbx_ref[...] += dbx
            dlam_ref[...] += dlam
            dcb_ref[...] += jnp.sum(du, axis=0, keepdims=True)
            due = jnp.concatenate([du, du_head], axis=0)
            dx = None
            for j in range(LRU_W):
                term = _up(due, LRU_W - 1 - j)[:CH] * w_ref[pl.ds(j, 1), :]
                dx = term if dx is None else dx + term
                dw_ref[pl.ds(j, 1), :] += jnp.sum(du * _up(xe, 8 - (LRU_W - 1) + j)[:CH], axis=0, keepdims=True)
            dx_ref[cur, :] = dx
            return jnp.sum(jnp.where(rows == 0, a * D, 0.0), axis=0, keepdims=True), du[:8]

        _pairs_loop(nreal, chunk, (jnp.zeros((1, 128), F32), jnp.zeros((8, 128), F32)))

    big = jax.ShapeDtypeStruct((T, 512), F32)
    vec = jax.ShapeDtypeStruct((1, 512), F32)
    mat = jax.ShapeDtypeStruct((4, 128, 128), F32)
    matspec = pl.BlockSpec((1, 128, 128), lambda j: (j, 0, 0))
    return pl.pallas_call(
        body, name=name, grid=(4,),
        out_shape=(big, big, jax.ShapeDtypeStruct((8, 512), F32), vec, mat, mat, vec, vec, vec),
        in_specs=_lru_specs(T) + [_colblock(T, 0), _colblock(T, 4)],
        out_specs=(_colblock(T, 0), _colblock(T, 0), _vecblock(8), _vecblock(), matspec, matspec, _vecblock(),
                   _vecblock(), _vecblock()),
        compiler_params=_cparams(("parallel",)),
    )(p, p, cw, cb, wa, wx, ba, bx, lam, hs, dy)


def _ffn_forward(h, gamma, wg, wu, wd, tag):
    xn = _rms_fwd(h, gamma, f"rms_fwd_{tag}")
    g, u, a = _ffn_up(xn, wg, wu, f"ffn_up_{tag}")
    if callable(wd):
        wd = wd(a)
    out = _mm([(a, wd)], "NN", f"ffn_down_{tag}", res=h, res_scale=0.5)
    return out, (h, xn, g, u, a)


def _after(w, tok):
    return w if tok is None else w + tok[0, 0].astype(w.dtype)


def _ffn_backward(saved, gamma, wg, wu, wd, dout, tok, tag, emit_one=None):
    h, xn, p, q, a = saved
    dout, dout16 = dout
    dwd = _mm([(a, dout16)], "TN", f"ffn_dwd_{tag}", res_scale=0.5, out_dtype=MXU, dep=tok)
    if emit_one is not None:
        tok = emit_one('wd', dwd)
    dg, du = _ffn_dact(dout16, wd, p, q, 0.5, f"ffn_dact_{tag}", dep=tok)
    dwg = _mm([(dg, xn)], "TN", f"ffn_dwg_{tag}", out_dtype=MXU)
    if emit_one is not None:
        tok = emit_one('wg', dwg)
    dwu = _mm([(du, xn)], "TN", f"ffn_dwu_{tag}", out_dtype=MXU, dep=tok if emit_one is not None else None)
    if emit_one is not None:
        tok = emit_one('wu', dwu)
    dxn = _mm([(dg, wg), (du, wu)], "NN", f"ffn_dxn_{tag}", dep=tok if emit_one is not None else None)
    dh, dh16, dgamma = _rms_bwd(h, gamma, dxn, dout, f"rms_bwd_{tag}")
    return (dh, dh16), dgamma, dwg, dwu, dwd


def _blockdiag(w, per):
    n, k, _ = w.shape
    out = jnp.zeros((n // per, per * k, per * k), w.dtype)
    for i in range(per):
        out = out.at[:, i * k:(i + 1) * k, i * k:(i + 1) * k].set(w[i::per])
    return out


def _blockdiag_grad(g, per, k):
    parts = [g[:, i * k:(i + 1) * k, i * k:(i + 1) * k] for i in range(per)]
    return jnp.stack(parts, axis=1).reshape(-1, k, k)


def _local_step(x, tgt, W, fetch, emit):
    fetch(0, x)
    seq, D = x.shape
    lr = N_META + seq
    nreal = -(-lr // CH)
    T = CH * (nreal + 2)
    if T > 640 and T % 640:
        T += 640 - T % 640
    lo, real_end = CH + N_META, CH + lr
    zf = lambda n: jnp.zeros((n, D), F32)
    h0 = jnp.concatenate([zf(CH), W['meta_tokens'], x, zf(T - real_end)], axis=0)
    tgt_p = jnp.concatenate([zf(lo), tgt, zf(T - real_end)], axis=0)
    row = lambda v: v.reshape(1, -1)
    G = {}

    h = h0
    saved = []
    for l in range(2):
        wd1 = W['ffn1_wd', l] if l else (lambda after: (fetch(1, after), W['ffn1_wd', 0])[1])
        h, s1 = _ffn_forward(h, row(W['ffn1_norm'][l]), W['ffn1_wg', l], W['ffn1_wu', l], wd1, f"a{l}")
        hm = h
        fetch(3 * l + 2, hm)
        xn = _rms_fwd(hm, row(W['mix_norm'][l]), f"rms_fwd_mix{l}")
        if l == 0:
            p = _mm([(xn, W['w_in_even'])], "NT", "in_even")
            wbd = _blockdiag(W['pool_w'][0], 4)[0]
            l0, l1 = row(W['hgrn_lb_logits'][0]), row(W['hgrn_lb_logits'][1])
            ya = _pool_fwd(p, wbd, W['pool_scale'], nreal, real_end, "pool_fwd")
            yb, states = _hgrn_fwd(p, l0, l1, W['hgrn_gnorm'], nreal, real_end, "hgrn_fwd")
            wo = W['w_out_even']
            h = _mm([(ya, wo[:256]), (yb, wo[256:])], "NN", "out_even", res=hm)
            sm = (hm, xn, p, wbd, l0, l1, ya, yb, states)
        else:
            p = _mm([(xn, W['w_in_odd'])], "NT", "in_odd")
            cw = jnp.pad(W['conv_w'][0], ((0, 1), (0, 0)))
            lw = jnp.pad(W['lru_conv_w'][0], ((0, 4), (0, 0)))
            wa, wx = _blockdiag(W['lru_wa'][0], 2), _blockdiag(W['lru_wx'][0], 2)
            yc = _conf_fwd(p, cw, W['conv_b'], W['conv_ln_g'], W['conv_ln_b'], nreal, real_end, "conf_fwd")
            yd, hs = _lru_fwd(p, lw, W['lru_conv_b'], wa, wx, W['lru_ba'], W['lru_bx'], W['lru_lambda'], nreal,
                              real_end, "lru_fwd")
            wo = W['w_out_odd']
            h = _mm([(yc, wo[:512]), (yd, wo[512:])], "NN", "out_odd", res=hm)
            sm = (hm, xn, p, cw, lw, wa, wx, yc, yd, hs)
        fetch(3 * l + 3, h)
        h, s2 = _ffn_forward(h, row(W['ffn2_norm'][l]), W['ffn2_wg', l], W['ffn2_wu', l], W['ffn2_wd', l], f"b{l}")
        if l == 0:
            fetch(4, h)
        saved.append((s1, sm, s2))

    loss8, dh, dh16, dfin = _loss_head(h, row(W['final_norm']), tgt_p, lo, real_end, "loss_head")
    dh = (dh, dh16)
    G['final_norm'] = dfin[0]

    per_layer = {k: [None, None] for k in ('ffn1_norm', 'mix_norm', 'ffn2_norm')}
    tok = None
    for l in (1, 0):
        s1, sm, s2 = saved[l]
        dh, dn, dwg, dwu, dwd = _ffn_backward(s2, row(W['ffn2_norm'][l]), W['ffn2_wg', l], W['ffn2_wu', l],
                                              W['ffn2_wd', l], dh, tok, f"b{l}")
        per_layer['ffn2_norm'][l] = dn[0]
        tok = emit(f"ffn2_{l}", [('ffn2_wg', l, dwg), ('ffn2_wu', l, dwu), ('ffn2_wd', l, dwd)])
        if l == 0:
            hm, xn, p, wbd, l0, l1, ya, yb, states = sm
            wo, wi = W['w_out_even'], W['w_in_even']
            dwo = jnp.concatenate([_mm([(ya, dh[1])], "TN", "dwo_even_a", out_dtype=MXU),
                                   _mm([(yb, dh[1])], "TN", "dwo_even_b", out_dtype=MXU)], axis=0)
            dy = _mm([(dh[1], wo)], "NT", "dy_even", dep=tok)
            dpp, dwbd, dsc = _pool_bwd(p, wbd, W['pool_scale'], dy, nreal, real_end, "pool_bwd")
            dq, df, di, dg, dl0, dl1, dgn = _hgrn_bwd(p, l0, l1, W['hgrn_gnorm'], states, dy, nreal, real_end,
                                                      "hgrn_bwd")
            G['pool_w'] = _blockdiag_grad(dwbd[None], 4, 64)[None]
            G['pool_scale'] = dsc
            G['hgrn_lb_logits'] = jnp.concatenate([dl0, dl1], axis=0)
            G['hgrn_gnorm'] = dgn
            parts = [dpp, dq, df, di, dg]
            offs = [0, 256, 1024, 1792, 2560, 3328]
            dwi = jnp.concatenate(
                [_mm([(dpart, xn)], "TN", f"dwi_even_{k}", out_dtype=MXU) for k, dpart in enumerate(parts)], axis=0)
            dxn = _mm([(dpart, wi[offs[k]:offs[k + 1]]) for k, dpart in enumerate(parts)], "NN", "dxn_even")
            tok = emit("even", [('w_in_even', None, dwi), ('w_out_even', None, dwo)])
        else:
            hm, xn, p, cw, lw, wa, wx, yc, yd, hs = sm
            wo, wi = W['w_out_odd'], W['w_in_odd']
            dwo = jnp.concatenate([_mm([(yc, dh[1])], "TN", "dwo_odd_c", out_dtype=MXU),
                                   _mm([(yd, dh[1])], "TN", "dwo_odd_d", out_dtype=MXU)], axis=0)
            dy = _mm([(dh[1], wo)], "NT", "dy_odd", dep=tok)
            da, db, dcw, dcb, dlg, dlb = _conf_bwd(p, cw, W['conv_b'], W['conv_ln_g'], W['conv_ln_b'], dy, nreal,
                                                   real_end, "conf_bwd")
            dx, dgate, dlw, dlcb, dwa, dwx, dba, dbx, dlam = _lru_bwd(
                p, lw, W['lru_conv_b'], wa, wx, W['lru_ba'], W['lru_bx'], W['lru_lambda'], hs, dy, nreal, real_end,
                "lru_bwd")
            G['conv_w'], G['conv_b'], G['conv_ln_g'], G['conv_ln_b'] = dcw[None, :CONV_W], dcb, dlg, dlb
            G['lru_conv_w'], G['lru_conv_b'] = dlw[None, :LRU_W], dlcb
            G['lru_wa'] = _blockdiag_grad(dwa, 2, 64)[None]
            G['lru_wx'] = _blockdiag_grad(dwx, 2, 64)[None]
            G['lru_ba'], G['lru_bx'], G['lru_lambda'] = dba, dbx, dlam
            parts = [da, db, dx, dgate]
            dwi = jnp.concatenate(
                [_mm([(dpart, xn)], "TN", f"dwi_odd_{k}", out_dtype=MXU) for k, dpart in enumerate(parts)], axis=0)
            dxn = _mm([(dpart, wi[512 * k:512 * (k + 1)]) for k, dpart in enumerate(parts)], "NN", "dxn_odd")
            tok = emit("odd", [('w_in_odd', None, dwi), ('w_out_odd', None, dwo)])
        dh, dh16, dn = _rms_bwd(hm, _after(row(W['mix_norm'][l]), tok), dxn, dh[0], f"rms_bwd_mix{l}")
        dh = (dh, dh16)
        per_layer['mix_norm'][l] = dn[0]
        one = None if l == 1 else (lambda sfx, g: emit(f"ffn1_0_{sfx}", [('ffn1_' + sfx, 0, g)]))
        dh, dn, dwg, dwu, dwd = _ffn_backward(s1, row(W['ffn1_norm'][l]), W['ffn1_wg', l], W['ffn1_wu', l],
                                              W['ffn1_wd', l], dh, None, f"a{l}", one)
        per_layer['ffn1_norm'][l] = dn[0]
        if l == 1:
            tok = emit("ffn1_1", [('ffn1_wg', l, dwg), ('ffn1_wu', l, dwu), ('ffn1_wd', l, dwd)])
    for k, v in per_layer.items():
        G[k] = jnp.stack(v, axis=0)
    G['meta_tokens'] = dh[0][CH:lo]
    return loss8[0, 0], dh[0][lo:real_end], G


def _pack(arrs):
    flat = jnp.concatenate([a.reshape(-1).astype(F32) for a in arrs])
    n = flat.shape[0]
    padded = -(-n // 1024) * 1024
    return jnp.pad(flat, (0, padded - n)).reshape(-1, 128)


def _unpack(packed, shapes):
    flat = packed.reshape(-1)
    out, off = [], 0
    for s in shapes:
        n = math.prod(s)
        out.append(flat[off:off + n].reshape(s))
        off += n
    return out


def _pack8(arrs):
    flat = jnp.concatenate([a.reshape(NDEV, -1).astype(F32) for a in arrs], axis=1)
    n = flat.shape[1]
    padded = -(-n // 1024) * 1024
    return jnp.pad(flat, ((0, 0), (0, padded - n))).reshape(NDEV, -1, 128)


def _unpack8(packed, shapes):
    flat = packed.reshape(NDEV, -1)
    out, off = [], 0
    for s in shapes:
        n = math.prod(s)
        out.append(flat[:, off:off + n].reshape((NDEV,) + tuple(s)))
        off += n
    return out


def _to_full(gathered, axis):
    s = gathered.shape[1:]
    return jnp.moveaxis(gathered, 0, axis).reshape(s[:axis] + (NDEV * s[axis],) + s[axis + 1:])


def _to_slots(full, axis):
    s = full.shape
    return jnp.moveaxis(full.reshape(s[:axis] + (NDEV, s[axis] // NDEV) + s[axis + 1:]), axis, 0)


def kernel(x, meta_tokens, ffn1_norm, ffn1_wg, ffn1_wu, ffn1_wd, mix_norm, ffn2_norm, ffn2_wg, ffn2_wu, ffn2_wd, w_in_even, pool_w, pool_scale, hgrn_lb_logits, hgrn_gnorm, w_out_even, w_in_odd, conv_w, conv_b, conv_ln_g, conv_ln_b, lru_conv_w, lru_conv_b, lru_wa, lru_ba, lru_wx, lru_bx, lru_lambda, w_out_odd, final_norm, loss_target, m_meta_tokens, m_ffn1_norm, m_ffn1_wg, m_ffn1_wu, m_ffn1_wd, m_mix_norm, m_ffn2_norm, m_ffn2_wg, m_ffn2_wu, m_ffn2_wd, m_w_in_even, m_pool_w, m_pool_scale, m_hgrn_lb_logits, m_hgrn_gnorm, m_w_out_even, m_w_in_odd, m_conv_w, m_conv_b, m_conv_ln_g, m_conv_ln_b, m_lru_conv_w, m_lru_conv_b, m_lru_wa, m_lru_ba, m_lru_wx, m_lru_bx, m_lru_lambda, m_w_out_odd, m_final_norm, v_meta_tokens, v_ffn1_norm, v_ffn1_wg, v_ffn1_wu, v_ffn1_wd, v_mix_norm, v_ffn2_norm, v_ffn2_wg, v_ffn2_wu, v_ffn2_wd, v_w_in_even, v_pool_w, v_pool_scale, v_hgrn_lb_logits, v_hgrn_gnorm, v_w_out_even, v_w_in_odd, v_conv_w, v_conv_b, v_conv_ln_g, v_conv_ln_b, v_lru_conv_w, v_lru_conv_b, v_lru_wa, v_lru_ba, v_lru_wx, v_lru_bx, v_lru_lambda, v_w_out_odd, v_final_norm):
    args = (meta_tokens, ffn1_norm, ffn1_wg, ffn1_wu, ffn1_wd, mix_norm, ffn2_norm, ffn2_wg, ffn2_wu, ffn2_wd, w_in_even, pool_w, pool_scale, hgrn_lb_logits, hgrn_gnorm, w_out_even, w_in_odd, conv_w, conv_b, conv_ln_g, conv_ln_b, lru_conv_w, lru_conv_b, lru_wa, lru_ba, lru_wx, lru_bx, lru_lambda, w_out_odd, final_norm)
    margs = (m_meta_tokens, m_ffn1_norm, m_ffn1_wg, m_ffn1_wu, m_ffn1_wd, m_mix_norm, m_ffn2_norm, m_ffn2_wg, m_ffn2_wu, m_ffn2_wd, m_w_in_even, m_pool_w, m_pool_scale, m_hgrn_lb_logits, m_hgrn_gnorm, m_w_out_even, m_w_in_odd, m_conv_w, m_conv_b, m_conv_ln_g, m_conv_ln_b, m_lru_conv_w, m_lru_conv_b, m_lru_wa, m_lru_ba, m_lru_wx, m_lru_bx, m_lru_lambda, m_w_out_odd, m_final_norm)
    vargs = (v_meta_tokens, v_ffn1_norm, v_ffn1_wg, v_ffn1_wu, v_ffn1_wd, v_mix_norm, v_ffn2_norm, v_ffn2_wg, v_ffn2_wu, v_ffn2_wd, v_w_in_even, v_pool_w, v_pool_scale, v_hgrn_lb_logits, v_hgrn_gnorm, v_w_out_even, v_w_in_odd, v_conv_w, v_conv_b, v_conv_ln_g, v_conv_ln_b, v_lru_conv_w, v_lru_conv_b, v_lru_wa, v_lru_ba, v_lru_wx, v_lru_bx, v_lru_lambda, v_w_out_odd, v_final_norm)
    Wl = dict(zip(W_NAMES, args))
    Ml = dict(zip(W_NAMES, margs))
    Vl = dict(zip(W_NAMES, vargs))

    small_shapes = [Wl[n].shape for n in SMALL_SHARDED]
    ffn = lambda p, l: [(p + s, l) for s in ('_wg', '_wu', '_wd')]
    mix = lambda p: [('w_in_' + p, None), ('w_out_' + p, None)]
    ggroups = [ffn('ffn1', 0)[:2], ffn('ffn1', 0)[2:], mix('even'), ffn('ffn2', 0), ffn('ffn1', 1), mix('odd'),
               ffn('ffn2', 1)]
    colsharded = lambda n: SHARD_AXIS[n] == 2

    def shard(n, l):
        w = Wl[n][0 if l is None else l].astype(MXU)
        return w.T if colsharded(n) else w

    srcs = [[shard(n, l) for n, l in g] for g in ggroups]
    srcs[0] = [_pack([Wl[n] for n in SMALL_SHARDED])] + srcs[0]
    first = _gather_two_level(srcs[0], "gather_first")
    zero = first[0][0, :8] * 0.0
    handles, _ = _exchange_start([[_after(s, zero) for s in g] for g in srcs[1:]], True, "gather_start")
    W = {n: Wl[n] for n in REPLICATED}

    def fetch(k, after):
        lands = first if k == 0 else _exchange_wait(handles[k - 1], True, after, f"gather_wait_{k}")
        if k == 0:
            for n, g in zip(SMALL_SHARDED, _unpack8(lands[0], small_shapes)):
                W[n] = _to_full(g, SHARD_AXIS[n])
            lands = lands[1:]
        for (n, l), g in zip(ggroups[k], lands):
            W[n if l is None else (n, l)] = g.reshape(-1, g.shape[-1])

    pending = []

    def emit(tag, grads):
        slots = [g.astype(MXU).reshape(NDEV, -1, g.shape[-1]) for _, _, g in grads]
        hs, token = _exchange_start([slots], False, f"scatter_start_{tag}")
        pending.append((tag, hs[0], [(n, l) for n, l, _ in grads]))
        return token

    loss_part, grad_x, G = _local_step(x[0], loss_target[0], W, fetch, emit)

    small_slots = [_to_slots(G[n].astype(F32), SHARD_AXIS[n]) for n in SMALL_SHARDED]
    send = [_pack8(small_slots), _pack([G[n] for n in REPLICATED]),
            jnp.broadcast_to(loss_part, (8, 128))]
    got = _exchange(send, [False, True, True], "scatter_small")
    loss = jnp.sum(got[2][:, 0, 0])
    recv = {}
    for tag, handle, keys in pending:
        for key, r in zip(keys, _exchange_wait(handle, False, got[0], f"scatter_wait_{tag}")):
            recv[key] = r

    outs = {}
    for n in BIG:
        shp = Wl[n].shape
        C = shp[-1]
        rs = [recv[n, None]] if shp[0] == 1 else [recv[n, l] for l in range(shp[0])]
        if colsharded(n):
            rs = [_sum8(r, f"sum8_{n}_{j}").T for j, r in enumerate(rs)]
        res = _adamw(rs, Wl[n].reshape(-1, C), Ml[n].reshape(-1, C), Vl[n].reshape(-1, C), f"adamw_{n}",
                     summed=colsharded(n))
        outs[n] = [o.reshape(shp) for o in res]
    for names, r, tag in ((SMALL_SHARDED, got[0], "small"), (REPLICATED, got[1], "repl")):
        shapes = [Wl[n].shape for n in names]
        res = _adamw([r], _pack([Wl[n] for n in names]), _pack([Ml[n] for n in names]),
                     _pack([Vl[n] for n in names]), f"adamw_{tag}")
        unp = [_unpack(o, shapes) for o in res]
        for k, n in enumerate(names):
            outs[n] = [unp[j][k] for j in range(4)]

    result = [loss, grad_x[None]]
    for j in range(4):
        result += [outs[n][j] for n in W_NAMES]
    return tuple(result)
```
